```python
import math
import jax, jax.numpy as jnp
from jax import lax
import numpy as np

D_MODEL = 1024
BATCH = 16
SEQ = 256
DEPTH = 2
DEC_BATCH = 4
DEC_SEQ = 1024
PAST_LEN = 512

GRID_W = 64
H_A = 4
DH_A = 64
DV_A = 128
W_A = H_A * DV_A
G_B = 4
DG_B = 128
W_B = G_B * DG_B
H_C = 4
DK_C = 64
DV_C = 128
W_C = H_C * DV_C
CHUNK = 64
N_EXPERTS = 32
TOP_K = 4
D_FF = 1024
SWIGLU_LIMIT = 7.0
SWIGLU_ALPHA = 1.702
ROPE_BASE = 10000.0
Q_BLOCK = 128
EPS = 1e-6
IN_SIZES = (H_A * 2 * DH_A, H_A * 2 * DH_A, H_A * DV_A, W_B, H_C * DK_C, H_C * DK_C, H_C * DV_C, W_C, 3 * D_MODEL)
IN_TOTAL = sum(IN_SIZES)

kernel_name = "hybrid_diffattn_fnet_retention_moe_prefix_step"


def rmsnorm(x, g):
    x32 = x.astype(jnp.float32)
    y = x32 * lax.rsqrt(jnp.mean(x32 * x32, axis=-1, keepdims=True) + EPS)
    return (y * g.astype(jnp.float32)).astype(x.dtype)


def adaln_params(cond, w_mod, b_mod):
    m = jax.nn.silu(cond) @ w_mod + b_mod
    return jnp.split(m[..., None, :], 6, axis=-1)


def split_cols(z):
    offs = np.cumsum(IN_SIZES)[:-1].tolist()
    return jnp.split(z, offs, axis=-1)


def axial_rope_tables(S):
    rows = S // GRID_W
    row = jnp.repeat(jnp.arange(rows, dtype=jnp.float32), GRID_W)
    col = jnp.tile(jnp.arange(GRID_W, dtype=jnp.float32), rows)
    quarter = DH_A // 4
    inv = ROPE_BASE ** (-jnp.arange(quarter, dtype=jnp.float32) / quarter)
    ar = row[:, None] * inv
    ac = col[:, None] * inv
    ang = jnp.concatenate([ar, ar, ac, ac], axis=-1)
    return jnp.cos(ang), jnp.sin(ang)


def axial_rope(x, cos, sin):
    x32 = x.astype(jnp.float32)
    half = DH_A // 2
    quarter = DH_A // 4

    def rot(u):
        return jnp.concatenate([-u[..., quarter:], u[..., :quarter]], axis=-1)

    xr = jnp.concatenate([rot(x32[..., :half]), rot(x32[..., half:])], axis=-1)
    c = cos[:, None, None, :]
    s = sin[:, None, None, :]
    return (x32 * c + xr * s).astype(x.dtype)


def diff_attention(q, k, v, lam):
    B, S, H, _, DH = q.shape
    DV = v.shape[-1]
    nb = S // Q_BLOCK
    qb = jnp.moveaxis(q.reshape(B, nb, Q_BLOCK, H, 2, DH), 1, 0)
    scale = DH ** -0.5

    def one_block(qblk):
        s = jnp.einsum('bqhcd,bkhcd->bhcqk', qblk, k).astype(jnp.float32) * scale
        p = jax.nn.softmax(s, axis=-1)
        a = p[:, :, 0] - lam * p[:, :, 1]
        return jnp.einsum('bhqk,bkhv->bqhv', a.astype(v.dtype), v)

    o = lax.map(one_block, qb)
    return jnp.moveaxis(o, 0, 1).reshape(B, S, H, DV)


def fourier_mix(f):
    B, S, _ = f.shape
    fg = f.astype(jnp.float32).reshape(B, S, G_B, DG_B)
    out = jnp.fft.fft2(fg, axes=(1, 3), norm='ortho').real
    return out.reshape(B, S, W_B).astype(f.dtype)


def retention_scan(q, k, v, log_gamma, s0):
    B, S, H, _ = q.shape
    DV = v.shape[-1]
    n = S // CHUNK

    def chunks(a):
        return a.astype(jnp.float32).reshape(B, n, CHUNK, H, a.shape[-1]).transpose(1, 0, 3, 2, 4)

    qc, kc, vc = chunks(q), chunks(k), chunks(v)
    pos = jnp.arange(CHUNK, dtype=jnp.float32)
    rel = pos[:, None] - pos[None, :]
    decay = jnp.where(rel >= 0, jnp.exp(log_gamma[:, None, None] * jnp.maximum(rel, 0.0)), 0.0)
    xi = jnp.exp(log_gamma[:, None] * (pos + 1.0))[:, :, None]
    zeta = jnp.exp(log_gamma[:, None] * (CHUNK - 1.0 - pos))[:, :, None]
    g_chunk = jnp.exp(log_gamma * CHUNK)[:, None, None]

    def step(state, inp):
        qi, ki, vi = inp
        inner = jnp.einsum('bhid,bhjd->bhij', qi, ki) * decay
        o = jnp.einsum('bhij,bhjv->bhiv', inner, vi) + jnp.einsum('bhid,bhdv->bhiv', qi, state) * xi
        state = g_chunk * state + jnp.einsum('bhjd,bhjv->bhdv', ki * zeta, vi)
        return state, o

    s_final, o = lax.scan(step, s0.astype(jnp.float32), (qc, kc, vc))
    o = o.transpose(1, 0, 3, 2, 4).reshape(B, S, H, DV)
    return o.astype(q.dtype), s_final.astype(q.dtype)


def mixing_block(h, w_in, lam_a, subln_a, ret_decay, ret_norm, w_branch_a, w_branch_b, w_branch_c,
                 w_out, lam_init, ctx):
    B, S, _ = h.shape
    qa, ka, va, fb, qc, kc, vc, gc, gates = split_cols(h @ w_in)
    qa = qa.reshape(B, S, H_A, 2, DH_A)
    ka = ka.reshape(B, S, H_A, 2, DH_A)
    va = va.reshape(B, S, H_A, DV_A)
    qc = qc.reshape(B, S, H_C, DK_C)
    kc = kc.reshape(B, S, H_C, DK_C) * (DK_C ** -0.5)
    vc = vc.reshape(B, S, H_C, DV_C)
    lam = (jnp.exp(jnp.sum(lam_a[0] * lam_a[1])) - jnp.exp(jnp.sum(lam_a[2] * lam_a[3])) + lam_init)
    log_g = jax.nn.log_sigmoid(ret_decay.astype(jnp.float32))

    if ctx is None:
        k_all, v_all = ka, va
        s0f = jnp.zeros((B, H_C, DK_C, DV_C), jnp.float32)
        s0b = jnp.zeros((B, H_C, DK_C, DV_C), jnp.float32)
    else:
        ctx_k, ctx_v, s0f, s0b = ctx
        cos, sin = axial_rope_tables(S)
        qa = axial_rope(qa, cos, sin)
        k_all = jnp.concatenate([axial_rope(ka, cos, sin), ctx_k.astype(ka.dtype)], axis=1)
        v_all = jnp.concatenate([va, ctx_v.astype(va.dtype)], axis=1)

    ya = diff_attention(qa, k_all, v_all, lam)
    ya = (rmsnorm(ya, subln_a) * (1.0 - lam_init)).reshape(B, S, W_A)
    yb = fourier_mix(fb)
    of, sf = retention_scan(qc, kc, vc, log_g[0], s0f)
    ob, sb = retention_scan(jnp.flip(qc, 1), jnp.flip(kc, 1), jnp.flip(vc, 1), log_g[1], s0b)
    yc = rmsnorm(of + jnp.flip(ob, 1), ret_norm).reshape(B, S, W_C) * jax.nn.silu(gc)
    g_a, g_b, g_c = jnp.split(jax.nn.sigmoid(gates), 3, axis=-1)
    merged = g_a * (ya @ w_branch_a) + g_b * (yb @ w_branch_b) + g_c * (yc @ w_branch_c)
    new_ctx = (ka, va, sf, sb) if ctx is None else None
    return merged @ w_out, new_ctx


def moe(h, router_w, router_b, w_gate_up, b_gate_up, w_down, b_down):
    B, S, D = h.shape
    t = h.reshape(B * S, D)
    logits = (t @ router_w + router_b).astype(jnp.float32)
    top_v, top_i = lax.top_k(logits, TOP_K)
    top_w = jax.nn.softmax(top_v, axis=-1)
    combine = jnp.sum(jax.nn.one_hot(top_i, N_EXPERTS, dtype=jnp.float32) * top_w[..., None], axis=1)

    def expert(acc, inp):
        wgu, bgu, wdn, bdn, cw = inp
        gu = t @ wgu + bgu
        x_glu = jnp.minimum(gu[:, ::2], SWIGLU_LIMIT)
        x_lin = jnp.clip(gu[:, 1::2], -SWIGLU_LIMIT, SWIGLU_LIMIT)
        a = x_glu * jax.nn.sigmoid(SWIGLU_ALPHA * x_glu) * (x_lin + 1.0)
        out = a @ wdn + bdn
        return acc + cw[:, None].astype(out.dtype) * out, None

    y, _ = lax.scan(expert, jnp.zeros_like(t), (w_gate_up, b_gate_up, w_down, b_down, combine.T))
    return y.reshape(B, S, D)


def setup_inputs(seed: int = 0) -> dict:
    key = jax.random.key(seed)
    ks = jax.random.split(key, 32)
    f32 = jnp.float32
    D, L, E, F = D_MODEL, DEPTH, N_EXPERTS, D_FF

    def nrm(k, shape, s):
        return jax.random.normal(k, shape, f32) * s

    decay_base = jnp.log(2.0 ** (5.0 + jnp.arange(H_C, dtype=f32)) - 1.0)
    return {
        "x_prompt": nrm(ks[0], (BATCH, SEQ, D), 1.0),
        "x_sample": nrm(ks[1], (DEC_BATCH, DEC_SEQ, D), 1.0),
        "cache_k": nrm(ks[2], (DEC_BATCH, L, PAST_LEN, H_A, 2, DH_A), 1.0),
        "cache_v": nrm(ks[3], (DEC_BATCH, L, PAST_LEN, H_A, DV_A), 1.0),
        "state_ret_fwd": nrm(ks[4], (DEC_BATCH, L, H_C, DK_C, DV_C), 1.0),
        "state_ret_bwd": nrm(ks[5], (DEC_BATCH, L, H_C, DK_C, DV_C), 1.0),
        "c": nrm(ks[6], (DEC_BATCH, D), 1.0),
        "c_ctx": nrm(ks[7], (D,), 1.0),
        "norm1_g": 1.0 + nrm(ks[8], (L, D), 0.02),
        "norm2_g": 1.0 + nrm(ks[9], (L, D), 0.02),
        "w_mod": nrm(ks[10], (L, D, 6 * D), 0.5 * D ** -0.5),
        "b_mod": nrm(ks[11], (L, 6 * D), 0.02),
        "w_in": nrm(ks[12], (L, D, IN_TOTAL), D ** -0.5),
        "lam_a": nrm(ks[13], (L, 4, DH_A), 0.1),
        "subln_a": 1.0 + nrm(ks[14], (L, DV_A), 0.02),
        "ret_decay": decay_base + nrm(ks[15], (L, 2, H_C), 0.1),
        "ret_norm": 1.0 + nrm(ks[16], (L, DV_C), 0.02),
        "w_branch_a": nrm(ks[17], (L, W_A, D), W_A ** -0.5),
        "w_branch_b": nrm(ks[18], (L, W_B, D), W_B ** -0.5),
        "w_branch_c": nrm(ks[19], (L, W_C, D), W_C ** -0.5),
        "w_out": nrm(ks[20], (L, D, D), D ** -0.5),
        "router_w": nrm(ks[21], (L, D, E), D ** -0.5),
        "router_b": nrm(ks[22], (L, E), 0.01),
        "w_gate_up": nrm(ks[23], (L, E, D, 2 * F), D ** -0.5),
        "b_gate_up": nrm(ks[24], (L, E, 2 * F), 0.02),
        "w_down": nrm(ks[25], (L, E, F, D), F ** -0.5),
        "b_down": nrm(ks[26], (L, E, D), 0.02),
        "final_g": 1.0 + nrm(ks[27], (D,), 0.02),
    }


def reference(x_prompt, x_sample, cache_k, cache_v, state_ret_fwd, state_ret_bwd, c, c_ctx,
              norm1_g, norm2_g, w_mod, b_mod, w_in, lam_a, subln_a, ret_decay, ret_norm,
              w_branch_a, w_branch_b, w_branch_c, w_out, router_w, router_b,
              w_gate_up, b_gate_up, w_down, b_down, final_g):
    xp, xs = x_prompt, x_sample
    new_k, new_v, new_sf, new_sb = [], [], [], []
    for l in range(DEPTH):
        lam_init = 0.8 - 0.6 * math.exp(-0.3 * l)
        mix_w = (w_in[l], lam_a[l], subln_a[l], ret_decay[l], ret_norm[l],
                 w_branch_a[l], w_branch_b[l], w_branch_c[l], w_out[l], lam_init)
        moe_w = (router_w[l], router_b[l], w_gate_up[l], b_gate_up[l], w_down[l], b_down[l])

        sh1, sc1, g1, sh2, sc2, g2 = adaln_params(c_ctx, w_mod[l], b_mod[l])
        h = rmsnorm(xp, norm1_g[l]) * (1 + sc1) + sh1
        y, (k_l, v_l, sf_l, sb_l) = mixing_block(h, *mix_w, None)
        xp = xp + g1 * y
        h = rmsnorm(xp, norm2_g[l]) * (1 + sc2) + sh2
        xp = xp + g2 * moe(h, *moe_w)
        new_k.append(k_l)
        new_v.append(v_l)
        new_sf.append(sf_l)
        new_sb.append(sb_l)

        sh1, sc1, g1, sh2, sc2, g2 = adaln_params(c, w_mod[l], b_mod[l])
        h = rmsnorm(xs, norm1_g[l]) * (1 + sc1) + sh1
        y, _ = mixing_block(h, *mix_w, (cache_k[:, l], cache_v[:, l], state_ret_fwd[:, l], state_ret_bwd[:, l]))
        xs = xs + g1 * y
        h = rmsnorm(xs, norm2_g[l]) * (1 + sc2) + sh2
        xs = xs + g2 * moe(h, *moe_w)

    return (rmsnorm(xp, final_g), rmsnorm(xs, final_g), jnp.stack(new_k, axis=1), jnp.stack(new_v, axis=1),
            jnp.stack(new_sf, axis=1), jnp.stack(new_sb, axis=1))
```

```python
import functools
import math

import numpy as np
import jax
import jax.numpy as jnp
from jax import lax
from jax.experimental import pallas as pl
from jax.experimental.pallas import tpu as pltpu

F32 = jnp.float32
BF16 = jnp.bfloat16

D_MODEL = 1024
BATCH = 16
SEQ = 256
DEPTH = 2
DEC_BATCH = 4
DEC_SEQ = 1024
PAST_LEN = 512
GRID_W = 64
H_A = 4
DH_A = 64
DV_A = 128
G_B = 4
DG_B = 128
H_C = 4
DK_C = 64
DV_C = 128
N_EXPERTS = 32
TOP_K = 4
D_FF = 1024
SWIGLU_LIMIT = 7.0
SWIGLU_ALPHA = 1.702
ROPE_BASE = 10000.0
EPS = 1e-6

N_CTX_TOK = BATCH * SEQ
N_LAT_TOK = DEC_BATCH * DEC_SEQ
N_TOK = N_CTX_TOK + N_LAT_TOK
IN_TOTAL = 6656
Z_GATES = 0
Z_QA = 3072
Z_KA = 3584
Z_VA = 4096
Z_FB = 4608
Z_QC = 5120
Z_KC = 5376
Z_VC = 5632
Z_GC = 6144
LANES = 128
MIB = 1024 * 1024


def _cparams(semantics, vmem_mib):
    return pltpu.CompilerParams(dimension_semantics=semantics,
                                vmem_limit_bytes=vmem_mib * MIB)


def _dot(a, b):
    return jnp.dot(a, b, preferred_element_type=F32)


def _dot_nt(a, b):
    return lax.dot_general(a, b, (((1,), (1,)), ((), ())), preferred_element_type=F32)


def _dot_tn(a, b):
    return lax.dot_general(a, b, (((0,), (0,)), ((), ())), preferred_element_type=F32)


def _split_bf16(a):
    hi = a.astype(BF16)
    lo = (a - hi.astype(F32)).astype(BF16)
    return hi, lo


def _dot_f32x3(a, b):
    a_hi, a_lo = _split_bf16(a)
    b_hi, b_lo = _split_bf16(b)
    return _dot(a_hi, b_hi) + _dot(a_lo, b_hi) + _dot(a_hi, b_lo)


def _rms(x):
    return x * lax.rsqrt(jnp.mean(x * x, axis=-1, keepdims=True) + EPS)


def _modnorm(x, g, scale, shift):
    return (_rms(x) * g) * (1.0 + scale) + shift


def _sigmoid(x):
    return 1.0 / (1.0 + jnp.exp(-x))


MOD_TN = 1536


def _mod_kernel(cond_ref, w_ref, b_ref, o_ref):
    cnd = cond_ref[...]
    s = cnd * _sigmoid(cnd)
    o_ref[0] = _dot_f32x3(s, w_ref[0]) + b_ref[0]


def _modulation(cond8, w_mod, b_mod):
    n = 6 * D_MODEL
    out = pl.pallas_call(
        _mod_kernel,
        grid=(DEPTH, n // MOD_TN),
        in_specs=[
            pl.BlockSpec((8, D_MODEL), lambda l, j: (0, 0)),
            pl.BlockSpec((1, D_MODEL, MOD_TN), lambda l, j: (l, 0, j)),
            pl.BlockSpec((1, 1, MOD_TN), lambda l, j: (l, 0, j)),
        ],
        out_specs=pl.BlockSpec((1, 8, MOD_TN), lambda l, j: (l, 0, j)),
        out_shape=jax.ShapeDtypeStruct((DEPTH, 8, n), F32),
        compiler_params=_cparams(("parallel", "parallel"), 40),
        name="modulation",
    )(cond8, w_mod, b_mod.reshape(DEPTH, 1, n))
    return out.reshape(DEPTH, 8, 6, D_MODEL)


def _mod_row(tile_idx, tm):
    n_ctx_tiles = N_CTX_TOK // tm
    per_batch = DEC_SEQ // tm
    return jnp.where(tile_idx < n_ctx_tiles, 0, 1 + (tile_idx - n_ctx_tiles) // per_batch)


INP_TM = 512
INP_TN = 1664


def _inproj_kernel(x_ref, mod_ref, g_ref, w_ref, z_ref):
    h = _modnorm(x_ref[...], g_ref[...], mod_ref[0, 1:2, :], mod_ref[0, 0:1, :])
    z_ref[...] = _dot(h.astype(BF16), w_ref[...]).astype(BF16)


def _inproj(x, mod_l, g, w_bf):
    nj, ni = IN_TOTAL // INP_TN, N_TOK // INP_TM
    return pl.pallas_call(
        _inproj_kernel,
        grid=(nj, ni),
        in_specs=[
            pl.BlockSpec((INP_TM, D_MODEL), lambda j, i: (i, 0)),
            pl.BlockSpec((1, 6, D_MODEL), lambda j, i: (_mod_row(i, INP_TM), 0, 0)),
            pl.BlockSpec((1, D_MODEL), lambda j, i: (0, 0)),
            pl.BlockSpec((D_MODEL, INP_TN), lambda j, i: (0, j)),
        ],
        out_specs=pl.BlockSpec((INP_TM, INP_TN), lambda j, i: (i, j)),
        out_shape=jax.ShapeDtypeStruct((N_TOK, IN_TOTAL), BF16),
        compiler_params=_cparams(("parallel", "parallel"), 40),
        name="inproj",
    )(x, mod_l, g.reshape(1, D_MODEL), w_bf)


ATT_TQ = 256
W_A = H_A * DV_A


def _lambda(lam_ref, lam_init):
    la = lam_ref[...]
    l01 = jnp.sum(la[0:1, :] * la[1:2, :], axis=1, keepdims=True)
    l23 = jnp.sum(la[2:3, :] * la[3:4, :], axis=1, keepdims=True)
    return jnp.exp(l01) - jnp.exp(l23) + lam_init


def _softmax_parts(s):
    m = jnp.max(s, axis=-1, keepdims=True)
    e = jnp.exp(s - m)
    return e, 1.0 / jnp.sum(e, axis=-1, keepdims=True)


def _diff_attn_block(q, k_ref, v_ref, lam, subln, lam_init):
    lane = lax.broadcasted_iota(jnp.int32, (1, LANES), 1)
    first = lane < DH_A
    scale = DH_A ** -0.5
    outs = []
    for h in range(H_A):
        sl = slice(h * LANES, (h + 1) * LANES)
        qh = q[:, sl]
        kh = k_ref[:, sl]
        zero = jnp.zeros_like(qh)
        s0 = _dot_nt(jnp.where(first, qh, zero), kh) * scale
        s1 = _dot_nt(jnp.where(first, zero, qh), kh) * scale
        e0, r0 = _softmax_parts(s0)
        e1, r1 = _softmax_parts(s1)
        a = e0 * r0 - (lam * r1) * e1
        o = _dot(a.astype(BF16), v_ref[:, sl])
        outs.append((_rms(o) * subln) * (1.0 - lam_init))
    return outs


def _rope(x, cos, sin):
    lane = lax.broadcasted_iota(jnp.int32, (1, x.shape[1]), 1)
    first = (lane % 32) < 16
    n = x.shape[1]
    xr = jnp.where(first, -pltpu.roll(x, n - 16, 1), pltpu.roll(x, 16, 1))
    return x * cos + xr * sin


def _attn_ctx_kernel(lam_init, q_ref, k_ref, v_ref, lam_ref, subln_ref, o_ref):
    lam = _lambda(lam_ref, lam_init)
    outs = _diff_attn_block(q_ref[...], k_ref, v_ref, lam, subln_ref[...], lam_init)
    for h in range(H_A):
        o_ref[:, h * LANES:(h + 1) * LANES] = outs[h].astype(BF16)


def _attn_ctx(z, lam_a_l, subln_l, lam_init):
    wb = W_A
    return pl.pallas_call(
        functools.partial(_attn_ctx_kernel, lam_init),
        grid=(BATCH,),
        in_specs=[
            pl.BlockSpec((SEQ, wb), lambda b: (b, Z_QA // wb)),
            pl.BlockSpec((SEQ, wb), lambda b: (b, Z_KA // wb)),
            pl.BlockSpec((SEQ, wb), lambda b: (b, Z_VA // wb)),
            pl.BlockSpec((4, DH_A), lambda b: (0, 0)),
            pl.BlockSpec((1, DV_A), lambda b: (0, 0)),
        ],
        out_specs=pl.BlockSpec((SEQ, wb), lambda b: (b, 0)),
        out_shape=jax.ShapeDtypeStruct((N_CTX_TOK, wb), BF16),
        compiler_params=_cparams(("parallel",), 40),
        name="attn_ctx",
    )(z, z, z, lam_a_l, subln_l.reshape(1, DV_A))


def _attn_lat_kernel(lam_init, q_ref, k_ref, v_ref, ck_ref, cv_ref, cos_ref, sin_ref,
                     lam_ref, subln_ref, o_ref, kall, vall):
    lam = _lambda(lam_ref, lam_init)
    subln = subln_ref[...]
    kall[0:DEC_SEQ, :] = _rope(k_ref[...].astype(F32), cos_ref[...], sin_ref[...]).astype(BF16)
    kall[DEC_SEQ:, :] = ck_ref[0, 0].astype(BF16)
    vall[0:DEC_SEQ, :] = v_ref[...]
    vall[DEC_SEQ:, :] = cv_ref[0, 0].astype(BF16)

    def q_block(qb, carry):
        rows = pl.ds(pl.multiple_of(qb * ATT_TQ, ATT_TQ), ATT_TQ)
        q = _rope(q_ref[rows, :].astype(F32), cos_ref[rows, :], sin_ref[rows, :]).astype(BF16)
        outs = _diff_attn_block(q, kall, vall, lam, subln, lam_init)
        for h in range(H_A):
            o_ref[rows, h * LANES:(h + 1) * LANES] = outs[h].astype(BF16)
        return carry

    lax.fori_loop(0, DEC_SEQ // ATT_TQ, q_block, 0)


def _attn_lat(z, cache_k4, cache_v4, layer, cos, sin, lam_a_l, subln_l, lam_init):
    wb = W_A
    rb0 = N_CTX_TOK // DEC_SEQ
    t_all = DEC_SEQ + PAST_LEN
    return pl.pallas_call(
        functools.partial(_attn_lat_kernel, lam_init),
        grid=(DEC_BATCH,),
        in_specs=[
            pl.BlockSpec((DEC_SEQ, wb), lambda b: (rb0 + b, Z_QA // wb)),
            pl.BlockSpec((DEC_SEQ, wb), lambda b: (rb0 + b, Z_KA // wb)),
            pl.BlockSpec((DEC_SEQ, wb), lambda b: (rb0 + b, Z_VA // wb)),
            pl.BlockSpec((1, 1, PAST_LEN, wb), lambda b: (b, layer, 0, 0)),
            pl.BlockSpec((1, 1, PAST_LEN, wb), lambda b: (b, layer, 0, 0)),
            pl.BlockSpec((DEC_SEQ, wb), lambda b: (0, 0)),
            pl.BlockSpec((DEC_SEQ, wb), lambda b: (0, 0)),
            pl.BlockSpec((4, DH_A), lambda b: (0, 0)),
            pl.BlockSpec((1, DV_A), lambda b: (0, 0)),
        ],
        out_specs=pl.BlockSpec((DEC_SEQ, wb), lambda b: (b, 0)),
        out_shape=jax.ShapeDtypeStruct((N_LAT_TOK, wb), BF16),
        scratch_shapes=[pltpu.VMEM((t_all, wb), BF16), pltpu.VMEM((t_all, wb), BF16)],
        compiler_params=_cparams(("parallel",), 48),
        name="attn_lat",
    )(z, z, z, cache_k4, cache_v4, cos, sin, lam_a_l, subln_l.reshape(1, DV_A))


def _rope_tables(s):
    rows = s // GRID_W
    row = jnp.repeat(jnp.arange(rows, dtype=F32), GRID_W)
    col = jnp.tile(jnp.arange(GRID_W, dtype=F32), rows)
    quarter = DH_A // 4
    inv = ROPE_BASE ** (-jnp.arange(quarter, dtype=F32) / quarter)
    ar = row[:, None] * inv
    ac = col[:, None] * inv
    ang = jnp.concatenate([ar, ar, ac, ac], axis=-1)
    reps = (H_A * 2 * DH_A) // DH_A
    return jnp.tile(jnp.cos(ang), (1, reps)), jnp.tile(jnp.sin(ang), (1, reps))


W_B = G_B * DG_B


def _dft_consts(s):
    def cs(n, scale):
        j = np.arange(n, dtype=np.int64)
        ang = 2.0 * np.pi * ((j[:, None] * j[None, :]) % n).astype(np.float64) / n
        return np.cos(ang) * scale, np.sin(ang) * scale
    cc, sc = cs(DG_B, DG_B ** -0.5)
    chan = np.concatenate([cc, sc], axis=1)
    cp, sp = cs(s, s ** -0.5)
    pos = np.concatenate([cp, -sp], axis=1)
    return jnp.asarray(chan, dtype=BF16), jnp.asarray(pos, dtype=BF16)


def _fourier_kernel(s, f_ref, chan_ref, pos_ref, o_ref, p_ref):
    chan = chan_ref[...]
    for g in range(G_B):
        sl = slice(g * DG_B, (g + 1) * DG_B)
        a = _dot(f_ref[:, sl], chan)
        p_ref[0:s, sl] = a[:, :DG_B].astype(BF16)
        p_ref[s:2 * s, sl] = a[:, DG_B:].astype(BF16)
    o_ref[...] = _dot(pos_ref[...], p_ref[...]).astype(BF16)


def _fourier(z, s, n_batch, row_block0, chan, pos):
    return pl.pallas_call(
        functools.partial(_fourier_kernel, s),
        grid=(n_batch,),
        in_specs=[
            pl.BlockSpec((s, W_B), lambda b: (row_block0 + b, Z_FB // W_B)),
            pl.BlockSpec((DG_B, 2 * DG_B), lambda b: (0, 0)),
            pl.BlockSpec((s, 2 * s), lambda b: (0, 0)),
        ],
        out_specs=pl.BlockSpec((s, W_B), lambda b: (b, 0)),
        out_shape=jax.ShapeDtypeStruct((n_batch * s, W_B), BF16),
        scratch_shapes=[pltpu.VMEM((2 * s, W_B), BF16)],
        compiler_params=_cparams(("parallel",), 40),
        name="fourier_%d" % s,
    )(z, chan, pos)


W_C = H_C * DV_C
RET_TQ = 256


def _log_sigmoid(x):
    return jnp.minimum(x, 0.0) - jnp.log1p(jnp.exp(-jnp.abs(x)))


def _head_pair_lanes(vals, pair):
    lane = lax.broadcasted_iota(jnp.int32, (1, LANES), 1)
    return jnp.where(lane < DK_C, vals[:, 2 * pair:2 * pair + 1], vals[:, 2 * pair + 1:2 * pair + 2])


def _retention_rows(s, row0, tq, q, k_ref, v_ref, g, lg, rnorm, s0f_ref, s0b_ref):
    lane = lax.broadcasted_iota(jnp.int32, (1, LANES), 1)
    ii = (row0 + lax.broadcasted_iota(jnp.int32, (tq, s), 0)).astype(F32)
    jj = lax.broadcasted_iota(jnp.int32, (tq, s), 1).astype(F32)
    rel = ii - jj
    pos = (row0 + lax.broadcasted_iota(jnp.int32, (tq, 1), 0)).astype(F32)
    outs = []
    for h in range(H_C):
        pair, half = h // 2, h % 2
        sl = slice(pair * LANES, (pair + 1) * LANES)
        lgf = lg[0:1, h:h + 1]
        lgb = lg[1:2, h:h + 1]
        in_head = (lane < DK_C) if half == 0 else (lane >= DK_C)
        q_pair = q[:, sl]
        qh = jnp.where(in_head, q_pair, jnp.zeros_like(q_pair))
        sc = _dot_nt(qh, k_ref[:, sl]) * (DK_C ** -0.5)
        decay = (jnp.where(rel >= 0, jnp.exp(lgf * jnp.maximum(rel, 0.0)), 0.0)
                 + jnp.where(rel <= 0, jnp.exp(lgb * jnp.maximum(-rel, 0.0)), 0.0))
        o = _dot((sc * decay).astype(BF16), v_ref[:, h * DV_C:(h + 1) * DV_C])
        if s0f_ref is not None:
            qf = qh.astype(F32)
            xf = jnp.exp(lgf * (pos + 1.0))
            xb = jnp.exp(lgb * (s - pos))
            o = o + _dot((qf * xf).astype(BF16), s0f_ref[0, 0, sl, :].astype(BF16))
            o = o + _dot((qf * xb).astype(BF16), s0b_ref[0, 0, sl, :].astype(BF16))
        gh = g[:, h * DV_C:(h + 1) * DV_C].astype(F32)
        outs.append((_rms(o) * rnorm) * (gh * _sigmoid(gh)))
    return outs


def _ret_ctx_kernel(q_ref, k_ref, v_ref, g_ref, dec_ref, rn_ref, o_ref, sf_ref, sb_ref):
    s = SEQ
    lg = _log_sigmoid(dec_ref[...])
    outs = _retention_rows(s, 0, s, q_ref[...], k_ref, v_ref, g_ref[...], lg, rn_ref[...], None, None)
    for h in range(H_C):
        o_ref[:, h * DV_C:(h + 1) * DV_C] = outs[h].astype(BF16)
    pos = lax.broadcasted_iota(jnp.int32, (s, 1), 0).astype(F32)
    v = v_ref[...]
    for pair in range(H_C // 2):
        sl = slice(pair * LANES, (pair + 1) * LANES)
        kf = k_ref[:, sl].astype(F32) * (DK_C ** -0.5)
        zf = jnp.exp(_head_pair_lanes(lg[0:1, :], pair) * (s - 1.0 - pos))
        zb = jnp.exp(_head_pair_lanes(lg[1:2, :], pair) * pos)
        stf = _dot_tn((kf * zf).astype(BF16), v)
        stb = _dot_tn((kf * zb).astype(BF16), v)
        for half in range(2):
            h = 2 * pair + half
            rs = slice(half * DK_C, (half + 1) * DK_C)
            cs = slice(h * DV_C, (h + 1) * DV_C)
            sf_ref[0, 0, h] = stf[rs, cs]
            sb_ref[0, 0, h] = stb[rs, cs]


def _ret_ctx(z, dec_l, rnorm_l):
    wq = H_C * DK_C
    st_shape = jax.ShapeDtypeStruct((BATCH, 1, H_C, DK_C, DV_C), F32)
    st_spec = pl.BlockSpec((1, 1, H_C, DK_C, DV_C), lambda b: (b, 0, 0, 0, 0))
    return pl.pallas_call(
        _ret_ctx_kernel,
        grid=(BATCH,),
        in_specs=[
            pl.BlockSpec((SEQ, wq), lambda b: (b, Z_QC // wq)),
            pl.BlockSpec((SEQ, wq), lambda b: (b, Z_KC // wq)),
            pl.BlockSpec((SEQ, W_C), lambda b: (b, Z_VC // W_C)),
            pl.BlockSpec((SEQ, W_C), lambda b: (b, Z_GC // W_C)),
            pl.BlockSpec((2, H_C), lambda b: (0, 0)),
            pl.BlockSpec((1, DV_C), lambda b: (0, 0)),
        ],
        out_specs=[pl.BlockSpec((SEQ, W_C), lambda b: (b, 0)), st_spec, st_spec],
        out_shape=[jax.ShapeDtypeStruct((N_CTX_TOK, W_C), BF16), st_shape, st_shape],
        compiler_params=_cparams(("parallel",), 40),
        name="ret_ctx",
    )(z, z, z, z, dec_l, rnorm_l.reshape(1, DV_C))


def _ret_lat_kernel(q_ref, k_ref, v_ref, g_ref, dec_ref, rn_ref, s0f_ref, s0b_ref, o_ref):
    lg = _log_sigmoid(dec_ref[...])
    rnorm = rn_ref[...]

    def q_block(qb, carry):
        row0 = pl.multiple_of(qb * RET_TQ, RET_TQ)
        rows = pl.ds(row0, RET_TQ)
        outs = _retention_rows(DEC_SEQ, row0, RET_TQ, q_ref[rows, :], k_ref, v_ref, g_ref[rows, :],
                               lg, rnorm, s0f_ref, s0b_ref)
        for h in range(H_C):
            o_ref[rows, h * DV_C:(h + 1) * DV_C] = outs[h].astype(BF16)
        return carry

    lax.fori_loop(0, DEC_SEQ // RET_TQ, q_block, 0)


def _ret_lat(z, dec_l, rnorm_l, s0f4, s0b4, layer):
    wq = H_C * DK_C
    rb0 = N_CTX_TOK // DEC_SEQ
    st_spec = pl.BlockSpec((1, 1, H_C * DK_C, DV_C), lambda b: (b, layer, 0, 0))
    return pl.pallas_call(
        _ret_lat_kernel,
        grid=(DEC_BATCH,),
        in_specs=[
            pl.BlockSpec((DEC_SEQ, wq), lambda b: (rb0 + b, Z_QC // wq)),
            pl.BlockSpec((DEC_SEQ, wq), lambda b: (rb0 + b, Z_KC // wq)),
            pl.BlockSpec((DEC_SEQ, W_C), lambda b: (rb0 + b, Z_VC // W_C)),
            pl.BlockSpec((DEC_SEQ, W_C), lambda b: (rb0 + b, Z_GC // W_C)),
            pl.BlockSpec((2, H_C), lambda b: (0, 0)),
            pl.BlockSpec((1, DV_C), lambda b: (0, 0)),
            st_spec, st_spec,
        ],
        out_specs=pl.BlockSpec((DEC_SEQ, W_C), lambda b: (b, 0)),
        out_shape=jax.ShapeDtypeStruct((N_LAT_TOK, W_C), BF16),
        compiler_params=_cparams(("parallel",), 40),
        name="ret_lat",
    )(z, z, z, z, dec_l, rnorm_l.reshape(1, DV_C), s0f4, s0b4)


MRG_TM = 256


def _merge_kernel(x_ref, ya_ref, yb_ref, yc_ref, gt_ref, mod_ref, wa_ref, wb_ref, wc_ref, wo_ref,
                  n2_ref, rw_ref, rb_ref, x1_ref, h2_ref, cw_ref):
    d = D_MODEL
    merged = _sigmoid(gt_ref[:, 0:d].astype(F32)) * _dot(ya_ref[...], wa_ref[...])
    merged += _sigmoid(gt_ref[:, d:2 * d].astype(F32)) * _dot(yb_ref[...], wb_ref[...])
    merged += _sigmoid(gt_ref[:, 2 * d:3 * d].astype(F32)) * _dot(yc_ref[...], wc_ref[...])
    y = _dot(merged.astype(BF16), wo_ref[...])
    x1 = x_ref[...] + mod_ref[0, 2:3, :] * y
    x1_ref[...] = x1
    h2 = _modnorm(x1, n2_ref[...], mod_ref[0, 4:5, :], mod_ref[0, 3:4, :])
    h2_ref[...] = h2.astype(BF16)
    logits = _dot_f32x3(h2, rw_ref[...]) + rb_ref[...]
    ids = lax.broadcasted_iota(jnp.int32, logits.shape, 1)
    work = logits
    sels, vals = [], []
    for _ in range(TOP_K):
        m = jnp.max(work, axis=-1, keepdims=True)
        first = jnp.min(jnp.where(work == m, ids, N_EXPERTS), axis=-1, keepdims=True)
        sel = ids == first
        sels.append(sel)
        vals.append(m)
        work = jnp.where(sel, -jnp.inf, work)
    es = [jnp.exp(v - vals[0]) for v in vals]
    inv = 1.0 / (es[0] + es[1] + es[2] + es[3])
    cw = jnp.zeros_like(logits)
    for sel, e in zip(sels, es):
        cw = cw + jnp.where(sel, e * inv, 0.0)
    cw_ref[...] = cw


def _merge(x, ya, yb, yc, z, mod_l, wa, wb, wc, wo, n2, rw, rb):
    tm = MRG_TM
    d = D_MODEL
    full = lambda shape: pl.BlockSpec(shape, lambda i: tuple(0 for _ in shape))
    return pl.pallas_call(
        _merge_kernel,
        grid=(N_TOK // tm,),
        in_specs=[
            pl.BlockSpec((tm, d), lambda i: (i, 0)),
            pl.BlockSpec((tm, W_A), lambda i: (i, 0)),
            pl.BlockSpec((tm, W_B), lambda i: (i, 0)),
            pl.BlockSpec((tm, W_C), lambda i: (i, 0)),
            pl.BlockSpec((tm, 3 * d), lambda i: (i, 0)),
            pl.BlockSpec((1, 6, d), lambda i: (_mod_row(i, tm), 0, 0)),
            full((W_A, d)), full((W_B, d)), full((W_C, d)), full((d, d)),
            full((1, d)), full((d, N_EXPERTS)), full((1, N_EXPERTS)),
        ],
        out_specs=[
            pl.BlockSpec((tm, d), lambda i: (i, 0)),
            pl.BlockSpec((tm, d), lambda i: (i, 0)),
            pl.BlockSpec((tm, N_EXPERTS), lambda i: (i, 0)),
        ],
        out_shape=[
            jax.ShapeDtypeStruct((N_TOK, d), F32),
            jax.ShapeDtypeStruct((N_TOK, d), BF16),
            jax.ShapeDtypeStruct((N_TOK, N_EXPERTS), F32),
        ],
        compiler_params=_cparams(("parallel",), 48),
        name="merge_router",
    )(x, ya, yb, yc, z, mod_l, wa, wb, wc, wo, n2.reshape(1, d), rw, rb.reshape(1, N_EXPERTS))


FFN_ROWS = 256
PAIR_BLOCKS = (2 * D_FF) // (2 * LANES)


def _prep_expert_weights(wgu_ref, wdn_ref, wgu_bf, wdn_perm, wdn_bf):
    wgu_bf[...] = wgu_ref[0, 0].astype(BF16)
    half = LANES // 2
    for cb in range(D_MODEL // LANES):
        cols = slice(cb * LANES, (cb + 1) * LANES)
        for m in range(PAIR_BLOCKS):
            base = m * LANES
            wdn_perm[cb, pl.ds(base, half, stride=2), :] = wdn_ref[0, 0, base:base + half, cols]
            wdn_perm[cb, pl.ds(base + 1, half, stride=2), :] = wdn_ref[0, 0, base + half:base + LANES, cols]
        wdn_bf[:, cols] = wdn_perm[cb].astype(BF16)


def _ffn_rows(x, wgu_bf, bgu, wdn_bf, bdn):
    gu = _dot(x, wgu_bf[...]) + bgu
    lane = lax.broadcasted_iota(jnp.int32, (1, LANES), 1)
    even = (lane % 2) == 0

    def act(block):
        t = gu[:, block * LANES:(block + 1) * LANES]
        glu = jnp.minimum(t, SWIGLU_LIMIT)
        glu = glu * _sigmoid(SWIGLU_ALPHA * glu)
        lin = jnp.clip(t, -SWIGLU_LIMIT, SWIGLU_LIMIT) + 1.0
        return glu * pltpu.roll(lin, LANES - 1, 1)

    parts = []
    for m in range(PAIR_BLOCKS):
        parts.append(jnp.where(even, act(2 * m), pltpu.roll(act(2 * m + 1), 1, 1)))
    hidden = jnp.concatenate(parts, axis=1).astype(BF16)
    return _dot(hidden, wdn_bf[...]) + bdn


MOE_TM = 512


def _moe_dense_kernel(h_ref, cw_ref, x1_ref, mod_ref, wgu_ref, bgu_ref, wdn_ref, bdn_ref, o_ref,
                      acc, wgu_bf, wdn_perm, wdn_bf):
    e = pl.program_id(1)

    @pl.when(e == 0)
    def _():
        acc[...] = jnp.zeros_like(acc)

    _prep_expert_weights(wgu_ref, wdn_ref, wgu_bf, wdn_perm, wdn_bf)
    ids = lax.broadcasted_iota(jnp.int32, (1, N_EXPERTS), 1)
    bgu = bgu_ref[0, 0]
    bdn = bdn_ref[0, 0]
    for rb in range(MOE_TM // FFN_ROWS):
        rows = slice(rb * FFN_ROWS, (rb + 1) * FFN_ROWS)
        c = jnp.sum(jnp.where(ids == e, cw_ref[rows, :], 0.0), axis=-1, keepdims=True)
        acc[rows, :] += c * _ffn_rows(h_ref[rows, :], wgu_bf, bgu, wdn_bf, bdn)

    @pl.when(e == N_EXPERTS - 1)
    def _():
        o_ref[...] = x1_ref[...] + mod_ref[0, 5:6, :] * acc[...]


def _moe_dense(h2, cw, x1, mod_l, w_gate_up, b_gate_up4, w_down, b_down4, layer):
    tm, d, f2 = MOE_TM, D_MODEL, 2 * D_FF
    return pl.pallas_call(
        _moe_dense_kernel,
        grid=(N_TOK // tm, N_EXPERTS),
        in_specs=[
            pl.BlockSpec((tm, d), lambda i, e: (i, 0)),
            pl.BlockSpec((tm, N_EXPERTS), lambda i, e: (i, 0)),
            pl.BlockSpec((tm, d), lambda i, e: (i, 0)),
            pl.BlockSpec((1, 6, d), lambda i, e: (_mod_row(i, tm), 0, 0)),
            pl.BlockSpec((1, 1, d, f2), lambda i, e: (layer, e, 0, 0)),
            pl.BlockSpec((1, 1, 1, f2), lambda i, e: (layer, e, 0, 0)),
            pl.BlockSpec((1, 1, D_FF, d), lambda i, e: (layer, e, 0, 0)),
            pl.BlockSpec((1, 1, 1, d), lambda i, e: (layer, e, 0, 0)),
        ],
        out_specs=pl.BlockSpec((tm, d), lambda i, e: (i, 0)),
        out_shape=jax.ShapeDtypeStruct((N_TOK, d), F32),
        scratch_shapes=[
            pltpu.VMEM((tm, d), F32),
            pltpu.VMEM((d, f2), BF16),
            pltpu.VMEM((d // LANES, D_FF, LANES), F32),
            pltpu.VMEM((D_FF, d), BF16),
        ],
        compiler_params=_cparams(("parallel", "arbitrary"), 56),
        name="moe_dense",
    )(h2, cw, x1, mod_l, w_gate_up, b_gate_up4, w_down, b_down4)


FIN_TM = 512


def _final_kernel(x_ref, g_ref, o_ref):
    o_ref[...] = _rms(x_ref[...]) * g_ref[...]


def _final_norm(x, g):
    return pl.pallas_call(
        _final_kernel,
        grid=(N_TOK // FIN_TM,),
        in_specs=[pl.BlockSpec((FIN_TM, D_MODEL), lambda i: (i, 0)),
                  pl.BlockSpec((1, D_MODEL), lambda i: (0, 0))],
        out_specs=pl.BlockSpec((FIN_TM, D_MODEL), lambda i: (i, 0)),
        out_shape=jax.ShapeDtypeStruct((N_TOK, D_MODEL), F32),
        compiler_params=_cparams(("parallel",), 32),
        name="final_norm",
    )(x, g.reshape(1, D_MODEL))


def _permute_w_in(w):
    split = IN_TOTAL - 3 * D_MODEL
    return jnp.concatenate([w[:, split:], w[:, :split]], axis=1).astype(BF16)


def kernel(x_prompt, x_sample, cache_k, cache_v, state_ret_fwd, state_ret_bwd, c, c_ctx, norm1_g, norm2_g, w_mod, b_mod, w_in, lam_a, subln_a, ret_decay, ret_norm, w_branch_a, w_branch_b, w_branch_c, w_out, router_w, router_b, w_gate_up, b_gate_up, w_down, b_down, final_g):
    d = D_MODEL
    x = jnp.concatenate([x_prompt.reshape(N_CTX_TOK, d), x_sample.reshape(N_LAT_TOK, d)], axis=0)
    cond8 = jnp.concatenate([c_ctx[None, :], c, jnp.zeros((3, d), F32)], axis=0)
    mod = _modulation(cond8, w_mod, b_mod)

    cache_k4 = cache_k.reshape(DEC_BATCH, DEPTH, PAST_LEN, W_A)
    cache_v4 = cache_v.reshape(DEC_BATCH, DEPTH, PAST_LEN, W_A)
    s0f4 = state_ret_fwd.reshape(DEC_BATCH, DEPTH, H_C * DK_C, DV_C)
    s0b4 = state_ret_bwd.reshape(DEC_BATCH, DEPTH, H_C * DK_C, DV_C)
    b_gate_up4 = b_gate_up.reshape(DEPTH, N_EXPERTS, 1, 2 * D_FF)
    b_down4 = b_down.reshape(DEPTH, N_EXPERTS, 1, d)
    cos, sin = _rope_tables(DEC_SEQ)
    chan, pos_ctx = _dft_consts(SEQ)
    _, pos_lat = _dft_consts(DEC_SEQ)

    new_k, new_v, new_sf, new_sb = [], [], [], []
    for l in range(DEPTH):
        lam_init = 0.8 - 0.6 * math.exp(-0.3 * l)
        z = _inproj(x, mod[l], norm1_g[l], _permute_w_in(w_in[l]))
        ya = jnp.concatenate([
            _attn_ctx(z, lam_a[l], subln_a[l], lam_init),
            _attn_lat(z, cache_k4, cache_v4, l, cos, sin, lam_a[l], subln_a[l], lam_init)], axis=0)
        yb = jnp.concatenate([
            _fourier(z, SEQ, BATCH, 0, chan, pos_ctx),
            _fourier(z, DEC_SEQ, DEC_BATCH, N_CTX_TOK // DEC_SEQ, chan, pos_lat)], axis=0)
        yc_ctx, sf, sb = _ret_ctx(z, ret_decay[l], ret_norm[l])
        yc = jnp.concatenate([yc_ctx, _ret_lat(z, ret_decay[l], ret_norm[l], s0f4, s0b4, l)], axis=0)
        x1, h2, cw = _merge(x, ya, yb, yc, z, mod[l],
                            w_branch_a[l].astype(BF16), w_branch_b[l].astype(BF16),
                            w_branch_c[l].astype(BF16), w_out[l].astype(BF16),
                            norm2_g[l], router_w[l], router_b[l])
        x = _moe_dense(h2, cw, x1, mod[l], w_gate_up, b_gate_up4, w_down, b_down4, l)
        new_k.append(z[:N_CTX_TOK, Z_KA:Z_KA + W_A].astype(F32).reshape(BATCH, SEQ, H_A, 2, DH_A))
        new_v.append(z[:N_CTX_TOK, Z_VA:Z_VA + W_A].astype(F32).reshape(BATCH, SEQ, H_A, DV_A))
        new_sf.append(sf)
        new_sb.append(sb)

    y = _final_norm(x, final_g)
    return (y[:N_CTX_TOK].reshape(BATCH, SEQ, d), y[N_CTX_TOK:].reshape(DEC_BATCH, DEC_SEQ, d),
            jnp.stack(new_k, axis=1), jnp.stack(new_v, axis=1),
            jnp.concatenate(new_sf, axis=1), jnp.concatenate(new_sb, axis=1))
```

```python
import functools
import math

import numpy as np
import jax
import jax.numpy as jnp
from jax import lax
from jax.experimental import pallas as pl
from jax.experimental.pallas import tpu as pltpu

F32 = jnp.float32
BF16 = jnp.bfloat16

D_MODEL = 1024
BATCH = 16
SEQ = 256
DEPTH = 2
DEC_BATCH = 4
DEC_SEQ = 1024
PAST_LEN = 512
GRID_W = 64
H_A = 4
DH_A = 64
DV_A = 128
G_B = 4
DG_B = 128
H_C = 4
DK_C = 64
DV_C = 128
N_EXPERTS = 32
TOP_K = 4
D_FF = 1024
SWIGLU_LIMIT = 7.0
SWIGLU_ALPHA = 1.702
ROPE_BASE = 10000.0
EPS = 1e-6

N_CTX_TOK = BATCH * SEQ
N_LAT_TOK = DEC_BATCH * DEC_SEQ
N_TOK = N_CTX_TOK + N_LAT_TOK
IN_TOTAL = 6656
Z_GATES = 0
Z_QA = 3072
Z_KA = 3584
Z_VA = 4096
Z_FB = 4608
Z_QC = 5120
Z_KC = 5376
Z_VC = 5632
Z_GC = 6144
LANES = 128
MIB = 1024 * 1024


def _cparams(semantics, vmem_mib):
    return pltpu.CompilerParams(dimension_semantics=semantics,
                                vmem_limit_bytes=vmem_mib * MIB)


def _dot(a, b):
    return jnp.dot(a, b, preferred_element_type=F32)


def _dot_nt(a, b):
    return lax.dot_general(a, b, (((1,), (1,)), ((), ())), preferred_element_type=F32)


def _dot_tn(a, b):
    return lax.dot_general(a, b, (((0,), (0,)), ((), ())), preferred_element_type=F32)


def _split_bf16(a):
    hi = a.astype(BF16)
    lo = (a - hi.astype(F32)).astype(BF16)
    return hi, lo


def _dot_f32x3(a, b):
    a_hi, a_lo = _split_bf16(a)
    b_hi, b_lo = _split_bf16(b)
    return _dot(a_hi, b_hi) + _dot(a_lo, b_hi) + _dot(a_hi, b_lo)


def _rms(x):
    return x * lax.rsqrt(jnp.mean(x * x, axis=-1, keepdims=True) + EPS)


def _modnorm(x, g, scale, shift):
    return (_rms(x) * g) * (1.0 + scale) + shift


def _sigmoid(x):
    return 1.0 / (1.0 + jnp.exp(-x))


MOD_TN = 1536


def _mod_kernel(cond_ref, w_ref, b_ref, o_ref):
    cnd = cond_ref[...]
    s = cnd * _sigmoid(cnd)
    o_ref[0] = _dot_f32x3(s, w_ref[0]) + b_ref[0]


def _modulation(cond8, w_mod, b_mod):
    n = 6 * D_MODEL
    out = pl.pallas_call(
        _mod_kernel,
        grid=(DEPTH, n // MOD_TN),
        in_specs=[
            pl.BlockSpec((8, D_MODEL), lambda l, j: (0, 0)),
            pl.BlockSpec((1, D_MODEL, MOD_TN), lambda l, j: (l, 0, j)),
            pl.BlockSpec((1, 1, MOD_TN), lambda l, j: (l, 0, j)),
        ],
        out_specs=pl.BlockSpec((1, 8, MOD_TN), lambda l, j: (l, 0, j)),
        out_shape=jax.ShapeDtypeStruct((DEPTH, 8, n), F32),
        compiler_params=_cparams(("parallel", "parallel"), 40),
        name="modulation",
    )(cond8, w_mod, b_mod.reshape(DEPTH, 1, n))
    return out.reshape(DEPTH, 8, 6, D_MODEL)


def _mod_row(tile_idx, tm):
    n_ctx_tiles = N_CTX_TOK // tm
    per_batch = DEC_SEQ // tm
    return jnp.where(tile_idx < n_ctx_tiles, 0, 1 + (tile_idx - n_ctx_tiles) // per_batch)


INP_TM = 1024
INP_TN = 512
INP_NJ = IN_TOTAL // INP_TN
INP_GATE_BLOCKS = (3 * D_MODEL) // INP_TN
CTX_PER_TILE = INP_TM // SEQ


def _inproj_kernel(x_ref, mod_ref, g_ref, w_ref, *rest):
    z_ref, kc_ref, vc_ref, h_scr = rest[-4:]
    i = pl.program_id(0)
    j = pl.program_id(1)

    @pl.when(j == 0)
    def _():
        h = _modnorm(x_ref[...], g_ref[...], mod_ref[0, 1:2, :], mod_ref[0, 0:1, :])
        h_scr[...] = h.astype(BF16)

    acc = _dot(h_scr[...], w_ref[0].astype(BF16))
    z_ref[...] = acc.astype(BF16)
    is_ctx = i < N_CTX_TOK // INP_TM

    @pl.when(jnp.logical_and(is_ctx, j == Z_KA // INP_TN))
    def _():
        for b in range(CTX_PER_TILE):
            kc_ref[b, 0] = acc[b * SEQ:(b + 1) * SEQ, :]

    @pl.when(jnp.logical_and(is_ctx, j == Z_VA // INP_TN))
    def _():
        for b in range(CTX_PER_TILE):
            vc_ref[b, 0] = acc[b * SEQ:(b + 1) * SEQ, :]


def _inproj(x, mod_l, g, w_in, layer, prev_caches):
    ni = N_TOK // INP_TM
    n_ctx_tiles = N_CTX_TOK // INP_TM
    cache_shape = jax.ShapeDtypeStruct((BATCH, DEPTH, SEQ, W_A), F32)
    cache_spec = pl.BlockSpec((CTX_PER_TILE, 1, SEQ, W_A),
                              lambda i, j: (jnp.minimum(i, n_ctx_tiles - 1), layer, 0, 0))
    in_specs = [
        pl.BlockSpec((INP_TM, D_MODEL), lambda i, j: (i, 0)),
        pl.BlockSpec((1, 6, D_MODEL), lambda i, j: (_mod_row(i, INP_TM), 0, 0)),
        pl.BlockSpec((1, D_MODEL), lambda i, j: (0, 0)),
        pl.BlockSpec((1, D_MODEL, INP_TN),
                     lambda i, j: (layer, 0, (j + INP_NJ - INP_GATE_BLOCKS) % INP_NJ)),
    ]
    args = [x, mod_l, g.reshape(1, D_MODEL), w_in]
    aliases = {}
    if prev_caches is not None:
        in_specs += [pl.BlockSpec(memory_space=pl.ANY)] * 2
        args += list(prev_caches)
        aliases = {4: 1, 5: 2}
    return pl.pallas_call(
        _inproj_kernel,
        grid=(ni, INP_NJ),
        in_specs=in_specs,
        out_specs=[pl.BlockSpec((INP_TM, INP_TN), lambda i, j: (i, j)), cache_spec, cache_spec],
        out_shape=[jax.ShapeDtypeStruct((N_TOK, IN_TOTAL), BF16), cache_shape, cache_shape],
        scratch_shapes=[pltpu.VMEM((INP_TM, D_MODEL), BF16)],
        input_output_aliases=aliases,
        compiler_params=_cparams(("arbitrary", "arbitrary"), 40),
        name="inproj",
    )(*args)


ATT_TQ = 256
W_A = H_A * DV_A


def _lambda(lam_ref, lam_init):
    la = lam_ref[...]
    l01 = jnp.sum(la[0:1, :] * la[1:2, :], axis=1, keepdims=True)
    l23 = jnp.sum(la[2:3, :] * la[3:4, :], axis=1, keepdims=True)
    return jnp.exp(l01) - jnp.exp(l23) + lam_init


def _softmax_parts(s):
    m = jnp.max(s, axis=-1, keepdims=True)
    e = jnp.exp(s - m)
    return e, 1.0 / jnp.sum(e, axis=-1, keepdims=True)


def _diff_attn_block(q, k_ref, v_ref, lam, subln, lam_init):
    lane = lax.broadcasted_iota(jnp.int32, (1, LANES), 1)
    first = lane < DH_A
    scale = DH_A ** -0.5
    outs = []
    for h in range(H_A):
        sl = slice(h * LANES, (h + 1) * LANES)
        qh = q[:, sl]
        kh = k_ref[:, sl]
        zero = jnp.zeros_like(qh)
        s0 = _dot_nt(jnp.where(first, qh, zero), kh) * scale
        s1 = _dot_nt(jnp.where(first, zero, qh), kh) * scale
        e0, r0 = _softmax_parts(s0)
        e1, r1 = _softmax_parts(s1)
        a = e0 * r0 - (lam * r1) * e1
        o = _dot(a.astype(BF16), v_ref[:, sl])
        outs.append((_rms(o) * subln) * (1.0 - lam_init))
    return outs


def _rope(x, cos, sin):
    lane = lax.broadcasted_iota(jnp.int32, (1, x.shape[1]), 1)
    first = (lane % 32) < 16
    n = x.shape[1]
    xr = jnp.where(first, -pltpu.roll(x, n - 16, 1), pltpu.roll(x, 16, 1))
    return x * cos + xr * sin


def _attn_ctx_kernel(lam_init, q_ref, k_ref, v_ref, lam_ref, subln_ref, o_ref):
    lam = _lambda(lam_ref, lam_init)
    outs = _diff_attn_block(q_ref[...], k_ref, v_ref, lam, subln_ref[...], lam_init)
    for h in range(H_A):
        o_ref[:, h * LANES:(h + 1) * LANES] = outs[h].astype(BF16)


def _attn_ctx(z, lam_a_l, subln_l, lam_init):
    wb = W_A
    return pl.pallas_call(
        functools.partial(_attn_ctx_kernel, lam_init),
        grid=(BATCH,),
        in_specs=[
            pl.BlockSpec((SEQ, wb), lambda b: (b, Z_QA // wb)),
            pl.BlockSpec((SEQ, wb), lambda b: (b, Z_KA // wb)),
            pl.BlockSpec((SEQ, wb), lambda b: (b, Z_VA // wb)),
            pl.BlockSpec((4, DH_A), lambda b: (0, 0)),
            pl.BlockSpec((1, DV_A), lambda b: (0, 0)),
        ],
        out_specs=pl.BlockSpec((SEQ, wb), lambda b: (b, 0)),
        out_shape=jax.ShapeDtypeStruct((N_TOK, wb), BF16),
        compiler_params=_cparams(("parallel",), 40),
        name="attn_ctx",
    )(z, z, z, lam_a_l, subln_l.reshape(1, DV_A))


def _attn_lat_kernel(lam_init, q_ref, k_ref, v_ref, ck_ref, cv_ref, cos_ref, sin_ref,
                     lam_ref, subln_ref, ya_ctx_ref, o_ref, kall, vall):
    del ya_ctx_ref
    lam = _lambda(lam_ref, lam_init)
    subln = subln_ref[...]
    kall[0:DEC_SEQ, :] = _rope(k_ref[...].astype(F32), cos_ref[...], sin_ref[...]).astype(BF16)
    kall[DEC_SEQ:, :] = ck_ref[0, 0].astype(BF16)
    vall[0:DEC_SEQ, :] = v_ref[...]
    vall[DEC_SEQ:, :] = cv_ref[0, 0].astype(BF16)

    def q_block(qb, carry):
        rows = pl.ds(pl.multiple_of(qb * ATT_TQ, ATT_TQ), ATT_TQ)
        q = _rope(q_ref[rows, :].astype(F32), cos_ref[rows, :], sin_ref[rows, :]).astype(BF16)
        outs = _diff_attn_block(q, kall, vall, lam, subln, lam_init)
        for h in range(H_A):
            o_ref[rows, h * LANES:(h + 1) * LANES] = outs[h].astype(BF16)
        return carry

    lax.fori_loop(0, DEC_SEQ // ATT_TQ, q_block, 0)


def _attn_lat(z, cache_k4, cache_v4, layer, cos, sin, lam_a_l, subln_l, lam_init, ya_ctx):
    wb = W_A
    rb0 = N_CTX_TOK // DEC_SEQ
    t_all = DEC_SEQ + PAST_LEN
    return pl.pallas_call(
        functools.partial(_attn_lat_kernel, lam_init),
        grid=(DEC_BATCH,),
        in_specs=[
            pl.BlockSpec((DEC_SEQ, wb), lambda b: (rb0 + b, Z_QA // wb)),
            pl.BlockSpec((DEC_SEQ, wb), lambda b: (rb0 + b, Z_KA // wb)),
            pl.BlockSpec((DEC_SEQ, wb), lambda b: (rb0 + b, Z_VA // wb)),
            pl.BlockSpec((1, 1, PAST_LEN, wb), lambda b: (b, layer, 0, 0)),
            pl.BlockSpec((1, 1, PAST_LEN, wb), lambda b: (b, layer, 0, 0)),
            pl.BlockSpec((DEC_SEQ, wb), lambda b: (0, 0)),
            pl.BlockSpec((DEC_SEQ, wb), lambda b: (0, 0)),
            pl.BlockSpec((4, DH_A), lambda b: (0, 0)),
            pl.BlockSpec((1, DV_A), lambda b: (0, 0)),
            pl.BlockSpec(memory_space=pl.ANY),
        ],
        out_specs=pl.BlockSpec((DEC_SEQ, wb), lambda b: (rb0 + b, 0)),
        out_shape=jax.ShapeDtypeStruct((N_TOK, wb), BF16),
        scratch_shapes=[pltpu.VMEM((t_all, wb), BF16), pltpu.VMEM((t_all, wb), BF16)],
        input_output_aliases={9: 0},
        compiler_params=_cparams(("parallel",), 48),
        name="attn_lat",
    )(z, z, z, cache_k4, cache_v4, cos, sin, lam_a_l, subln_l.reshape(1, DV_A), ya_ctx)


def _rope_tables(s):
    rows = s // GRID_W
    row = jnp.repeat(jnp.arange(rows, dtype=F32), GRID_W)
    col = jnp.tile(jnp.arange(GRID_W, dtype=F32), rows)
    quarter = DH_A // 4
    inv = ROPE_BASE ** (-jnp.arange(quarter, dtype=F32) / quarter)
    ar = row[:, None] * inv
    ac = col[:, None] * inv
    ang = jnp.concatenate([ar, ar, ac, ac], axis=-1)
    reps = (H_A * 2 * DH_A) // DH_A
    return jnp.tile(jnp.cos(ang), (1, reps)), jnp.tile(jnp.sin(ang), (1, reps))


W_B = G_B * DG_B


def _dft_consts(s):
    def cs(n, scale):
        j = np.arange(n, dtype=np.int64)
        ang = 2.0 * np.pi * ((j[:, None] * j[None, :]) % n).astype(np.float64) / n
        return np.cos(ang) * scale, np.sin(ang) * scale
    cc, sc = cs(DG_B, DG_B ** -0.5)
    chan = np.concatenate([cc, sc], axis=1)
    cp, sp = cs(s, s ** -0.5)
    pos = np.concatenate([cp, -sp], axis=1)
    return jnp.asarray(chan, dtype=F32).astype(BF16), jnp.asarray(pos, dtype=F32).astype(BF16)


def _fourier_kernel(s, f_ref, chan_ref, pos_ref, *rest):
    o_ref, p_ref = rest[-2:]
    chan = chan_ref[...]
    for g in range(G_B):
        sl = slice(g * DG_B, (g + 1) * DG_B)
        a = _dot(f_ref[:, sl], chan)
        p_ref[0:s, sl] = a[:, :DG_B].astype(BF16)
        p_ref[s:2 * s, sl] = a[:, DG_B:].astype(BF16)
    o_ref[...] = _dot(pos_ref[...], p_ref[...]).astype(BF16)


def _fourier(z, s, n_batch, row_block0, chan, pos, prev=None):
    in_specs = [
        pl.BlockSpec((s, W_B), lambda b: (row_block0 + b, Z_FB // W_B)),
        pl.BlockSpec((DG_B, 2 * DG_B), lambda b: (0, 0)),
        pl.BlockSpec((s, 2 * s), lambda b: (0, 0)),
    ]
    args = [z, chan, pos]
    aliases = {}
    if prev is not None:
        in_specs.append(pl.BlockSpec(memory_space=pl.ANY))
        args.append(prev)
        aliases = {3: 0}
    return pl.pallas_call(
        functools.partial(_fourier_kernel, s),
        grid=(n_batch,),
        in_specs=in_specs,
        out_specs=pl.BlockSpec((s, W_B), lambda b: (row_block0 + b, 0)),
        out_shape=jax.ShapeDtypeStruct((N_TOK, W_B), BF16),
        scratch_shapes=[pltpu.VMEM((2 * s, W_B), BF16)],
        input_output_aliases=aliases,
        compiler_params=_cparams(("parallel",), 40),
        name="fourier_%d" % s,
    )(*args)


W_C = H_C * DV_C
RET_TQ = 256


def _log_sigmoid(x):
    return jnp.minimum(x, 0.0) - jnp.log1p(jnp.exp(-jnp.abs(x)))


def _head_pair_lanes(vals, pair):
    lane = lax.broadcasted_iota(jnp.int32, (1, LANES), 1)
    return jnp.where(lane < DK_C, vals[:, 2 * pair:2 * pair + 1], vals[:, 2 * pair + 1:2 * pair + 2])


def _retention_rows(s, row0, tq, q, k_ref, v_ref, g, lg, rnorm, s0f_ref, s0b_ref):
    lane = lax.broadcasted_iota(jnp.int32, (1, LANES), 1)
    ii = (row0 + lax.broadcasted_iota(jnp.int32, (tq, s), 0)).astype(F32)
    jj = lax.broadcasted_iota(jnp.int32, (tq, s), 1).astype(F32)
    rel = ii - jj
    pos = (row0 + lax.broadcasted_iota(jnp.int32, (tq, 1), 0)).astype(F32)
    outs = []
    for h in range(H_C):
        pair, half = h // 2, h % 2
        sl = slice(pair * LANES, (pair + 1) * LANES)
        lgf = lg[0:1, h:h + 1]
        lgb = lg[1:2, h:h + 1]
        in_head = (lane < DK_C) if half == 0 else (lane >= DK_C)
        q_pair = q[:, sl]
        qh = jnp.where(in_head, q_pair, jnp.zeros_like(q_pair))
        sc = _dot_nt(qh, k_ref[:, sl]) * (DK_C ** -0.5)
        decay = (jnp.where(rel >= 0, jnp.exp(lgf * jnp.maximum(rel, 0.0)), 0.0)
                 + jnp.where(rel <= 0, jnp.exp(lgb * jnp.maximum(-rel, 0.0)), 0.0))
        o = _dot((sc * decay).astype(BF16), v_ref[:, h * DV_C:(h + 1) * DV_C])
        if s0f_ref is not None:
            qf = qh.astype(F32)
            xf = jnp.exp(lgf * (pos + 1.0))
            xb = jnp.exp(lgb * (s - pos))
            o = o + _dot((qf * xf).astype(BF16), s0f_ref[0, 0, sl, :].astype(BF16))
            o = o + _dot((qf * xb).astype(BF16), s0b_ref[0, 0, sl, :].astype(BF16))
        gh = g[:, h * DV_C:(h + 1) * DV_C].astype(F32)
        outs.append((_rms(o) * rnorm) * (gh * _sigmoid(gh)))
    return outs


def _ret_ctx_kernel(q_ref, k_ref, v_ref, g_ref, dec_ref, rn_ref, *rest):
    o_ref, sf_ref, sb_ref = rest[-3:]
    s = SEQ
    lg = _log_sigmoid(dec_ref[...])
    outs = _retention_rows(s, 0, s, q_ref[...], k_ref, v_ref, g_ref[...], lg, rn_ref[...], None, None)
    for h in range(H_C):
        o_ref[:, h * DV_C:(h + 1) * DV_C] = outs[h].astype(BF16)
    pos = lax.broadcasted_iota(jnp.int32, (s, 1), 0).astype(F32)
    v = v_ref[...]
    for pair in range(H_C // 2):
        sl = slice(pair * LANES, (pair + 1) * LANES)
        kf = k_ref[:, sl].astype(F32) * (DK_C ** -0.5)
        zf = jnp.exp(_head_pair_lanes(lg[0:1, :], pair) * (s - 1.0 - pos))
        zb = jnp.exp(_head_pair_lanes(lg[1:2, :], pair) * pos)
        stf = _dot_tn((kf * zf).astype(BF16), v)
        stb = _dot_tn((kf * zb).astype(BF16), v)
        for half in range(2):
            h = 2 * pair + half
            rs = slice(half * DK_C, (half + 1) * DK_C)
            cs = slice(h * DV_C, (h + 1) * DV_C)
            sf_ref[0, 0, h] = stf[rs, cs]
            sb_ref[0, 0, h] = stb[rs, cs]


def _ret_ctx(z, dec_l, rnorm_l, layer, prev_states):
    wq = H_C * DK_C
    st_shape = jax.ShapeDtypeStruct((BATCH, DEPTH, H_C, DK_C, DV_C), F32)
    st_spec = pl.BlockSpec((1, 1, H_C, DK_C, DV_C), lambda b: (b, layer, 0, 0, 0))
    in_specs = [
        pl.BlockSpec((SEQ, wq), lambda b: (b, Z_QC // wq)),
        pl.BlockSpec((SEQ, wq), lambda b: (b, Z_KC // wq)),
        pl.BlockSpec((SEQ, W_C), lambda b: (b, Z_VC // W_C)),
        pl.BlockSpec((SEQ, W_C), lambda b: (b, Z_GC // W_C)),
        pl.BlockSpec((2, H_C), lambda b: (0, 0)),
        pl.BlockSpec((1, DV_C), lambda b: (0, 0)),
    ]
    args = [z, z, z, z, dec_l, rnorm_l.reshape(1, DV_C)]
    aliases = {}
    if prev_states is not None:
        in_specs += [pl.BlockSpec(memory_space=pl.ANY)] * 2
        args += list(prev_states)
        aliases = {6: 1, 7: 2}
    return pl.pallas_call(
        _ret_ctx_kernel,
        grid=(BATCH,),
        in_specs=in_specs,
        out_specs=[pl.BlockSpec((SEQ, W_C), lambda b: (b, 0)), st_spec, st_spec],
        out_shape=[jax.ShapeDtypeStruct((N_TOK, W_C), BF16), st_shape, st_shape],
        input_output_aliases=aliases,
        compiler_params=_cparams(("parallel",), 40),
        name="ret_ctx",
    )(*args)


def _ret_lat_kernel(q_ref, k_ref, v_ref, g_ref, dec_ref, rn_ref, s0f_ref, s0b_ref, yc_ctx_ref, o_ref):
    del yc_ctx_ref
    lg = _log_sigmoid(dec_ref[...])
    rnorm = rn_ref[...]

    def q_block(qb, carry):
        row0 = pl.multiple_of(qb * RET_TQ, RET_TQ)
        rows = pl.ds(row0, RET_TQ)
        outs = _retention_rows(DEC_SEQ, row0, RET_TQ, q_ref[rows, :], k_ref, v_ref, g_ref[rows, :],
                               lg, rnorm, s0f_ref, s0b_ref)
        for h in range(H_C):
            o_ref[rows, h * DV_C:(h + 1) * DV_C] = outs[h].astype(BF16)
        return carry

    lax.fori_loop(0, DEC_SEQ // RET_TQ, q_block, 0)


def _ret_lat(z, dec_l, rnorm_l, s0f4, s0b4, layer, yc_ctx):
    wq = H_C * DK_C
    rb0 = N_CTX_TOK // DEC_SEQ
    st_spec = pl.BlockSpec((1, 1, H_C * DK_C, DV_C), lambda b: (b, layer, 0, 0))
    return pl.pallas_call(
        _ret_lat_kernel,
        grid=(DEC_BATCH,),
        in_specs=[
            pl.BlockSpec((DEC_SEQ, wq), lambda b: (rb0 + b, Z_QC // wq)),
            pl.BlockSpec((DEC_SEQ, wq), lambda b: (rb0 + b, Z_KC // wq)),
            pl.BlockSpec((DEC_SEQ, W_C), lambda b: (rb0 + b, Z_VC // W_C)),
            pl.BlockSpec((DEC_SEQ, W_C), lambda b: (rb0 + b, Z_GC // W_C)),
            pl.BlockSpec((2, H_C), lambda b: (0, 0)),
            pl.BlockSpec((1, DV_C), lambda b: (0, 0)),
            st_spec, st_spec,
            pl.BlockSpec(memory_space=pl.ANY),
        ],
        out_specs=pl.BlockSpec((DEC_SEQ, W_C), lambda b: (rb0 + b, 0)),
        out_shape=jax.ShapeDtypeStruct((N_TOK, W_C), BF16),
        input_output_aliases={8: 0},
        compiler_params=_cparams(("parallel",), 40),
        name="ret_lat",
    )(z, z, z, z, dec_l, rnorm_l.reshape(1, DV_C), s0f4, s0b4, yc_ctx)


MRG_TM = 256


def _merge_kernel(x_ref, ya_ref, yb_ref, yc_ref, gt_ref, mod_ref, wa_ref, wb_ref, wc_ref, wo_ref,
                  n2_ref, rw_ref, rb_ref, x1_ref, h2_ref, topi_ref, topw_ref, rank_ref,
                  cnt_row_ref, cnt_col_ref, carry_row, carry_col, wa_bf, wb_bf, wc_bf, wo_bf):
    d = D_MODEL
    tm = x_ref.shape[0]

    @pl.when(pl.program_id(0) == 0)
    def _():
        carry_row[...] = jnp.zeros_like(carry_row)
        carry_col[...] = jnp.zeros_like(carry_col)
        wa_bf[...] = wa_ref[0].astype(BF16)
        wb_bf[...] = wb_ref[0].astype(BF16)
        wc_bf[...] = wc_ref[0].astype(BF16)
        wo_bf[...] = wo_ref[0].astype(BF16)

    merged = _sigmoid(gt_ref[:, 0:d].astype(F32)) * _dot(ya_ref[...], wa_bf[...])
    merged += _sigmoid(gt_ref[:, d:2 * d].astype(F32)) * _dot(yb_ref[...], wb_bf[...])
    merged += _sigmoid(gt_ref[:, 2 * d:3 * d].astype(F32)) * _dot(yc_ref[...], wc_bf[...])
    y = _dot(merged.astype(BF16), wo_bf[...])
    x1 = x_ref[...] + mod_ref[0, 2:3, :] * y
    x1_ref[...] = x1
    h2 = _modnorm(x1, n2_ref[...], mod_ref[0, 4:5, :], mod_ref[0, 3:4, :])
    h2_ref[...] = h2
    logits = _dot_f32x3(h2, rw_ref[...]) + rb_ref[...]
    ids = lax.broadcasted_iota(jnp.int32, logits.shape, 1)
    work = logits
    sels, vals, firsts = [], [], []
    for _ in range(TOP_K):
        m = jnp.max(work, axis=-1, keepdims=True)
        first = jnp.min(jnp.where(work == m, ids, N_EXPERTS), axis=-1, keepdims=True)
        sel = ids == first
        sels.append(sel)
        vals.append(m)
        firsts.append(first)
        work = jnp.where(sel, -jnp.inf, work)
    es = [jnp.exp(v - vals[0]) for v in vals]
    inv = 1.0 / (es[0] + es[1] + es[2] + es[3])
    chosen = jnp.zeros_like(logits)
    for sel in sels:
        chosen = chosen + jnp.where(sel, 1.0, 0.0)
    chosen_bf = chosen.astype(BF16)
    rr = lax.broadcasted_iota(jnp.int32, (tm, tm), 0)
    cc = lax.broadcasted_iota(jnp.int32, (tm, tm), 1)
    before = jnp.where(rr > cc, 1.0, 0.0).astype(BF16)
    prefix = _dot(before, chosen_bf) + carry_row[...]
    ranks = [jnp.sum(jnp.where(sel, prefix, 0.0), axis=-1, keepdims=True) for sel in sels]
    carry_row[...] += jnp.sum(chosen, axis=0, keepdims=True)
    carry_col[...] += _dot_tn(chosen_bf, jnp.ones((tm, LANES), BF16))
    cnt_row_ref[...] = carry_row[...]
    cnt_col_ref[...] = carry_col[...]
    topi_ref[...] = jnp.concatenate(firsts, axis=1)
    topw_ref[...] = jnp.concatenate([e * inv for e in es], axis=1)
    rank_ref[...] = jnp.concatenate(ranks, axis=1).astype(jnp.int32)


def _merge(x, ya, yb, yc, z, mod_l, wa, wb, wc, wo, n2, rw, rb, layer):
    tm = MRG_TM
    d = D_MODEL
    full = lambda shape: pl.BlockSpec(shape, lambda i: tuple(0 for _ in shape))
    tok = lambda w: pl.BlockSpec((tm, w), lambda i: (i, 0))
    per_layer = lambda rows: pl.BlockSpec((1, rows, d), lambda i: (layer, 0, 0))
    return pl.pallas_call(
        _merge_kernel,
        grid=(N_TOK // tm,),
        in_specs=[
            tok(d), tok(W_A), tok(W_B), tok(W_C), tok(3 * d),
            pl.BlockSpec((1, 6, d), lambda i: (_mod_row(i, tm), 0, 0)),
            per_layer(W_A), per_layer(W_B), per_layer(W_C), per_layer(d),
            full((1, d)), full((d, N_EXPERTS)), full((1, N_EXPERTS)),
        ],
        out_specs=[tok(d), tok(d), tok(TOP_K), tok(TOP_K), tok(TOP_K),
                   full((1, N_EXPERTS)), full((N_EXPERTS, LANES))],
        out_shape=[
            jax.ShapeDtypeStruct((N_TOK, d), F32),
            jax.ShapeDtypeStruct((N_TOK, d), F32),
            jax.ShapeDtypeStruct((N_TOK, TOP_K), jnp.int32),
            jax.ShapeDtypeStruct((N_TOK, TOP_K), F32),
            jax.ShapeDtypeStruct((N_TOK, TOP_K), jnp.int32),
            jax.ShapeDtypeStruct((1, N_EXPERTS), F32),
            jax.ShapeDtypeStruct((N_EXPERTS, LANES), F32),
        ],
        scratch_shapes=[pltpu.VMEM((1, N_EXPERTS), F32), pltpu.VMEM((N_EXPERTS, LANES), F32),
                        pltpu.VMEM((W_A, d), BF16), pltpu.VMEM((W_B, d), BF16),
                        pltpu.VMEM((W_C, d), BF16), pltpu.VMEM((d, d), BF16)],
        compiler_params=_cparams(("arbitrary",), 56),
        name="merge_router",
    )(x, ya, yb, yc, z, mod_l, wa, wb, wc, wo, n2.reshape(1, d), rw, rb.reshape(1, N_EXPERTS))


MOE_TM = 256
MOE_TILES = (N_TOK * TOP_K) // MOE_TM + N_EXPERTS
MOE_ROWS = MOE_TILES * MOE_TM
PLAN_LANES = 256
PLAN_TM = 1024


def _plan_kernel(topi_ref, rank_ref, cnt_row_ref, cnt_col_ref, pos_ref, te_ref, nt_ref, last_ref):
    e_r = lax.broadcasted_iota(jnp.int32, (N_EXPERTS, N_EXPERTS), 0)
    e_c = lax.broadcasted_iota(jnp.int32, (N_EXPERTS, N_EXPERTS), 1)
    cnt_row = cnt_row_ref[...]
    tiles_row = jnp.ceil(cnt_row * (1.0 / MOE_TM))
    earlier = jnp.where(e_r < e_c, 1.0, 0.0).astype(BF16)
    start_tile_row = _dot(tiles_row.astype(BF16), earlier)
    off_row = start_tile_row * MOE_TM
    ids = lax.broadcasted_iota(jnp.int32, (PLAN_TM, N_EXPERTS), 1)
    cols = []
    for k in range(TOP_K):
        hit = ids == topi_ref[:, k:k + 1]
        cols.append(jnp.sum(jnp.where(hit, off_row, 0.0), axis=-1, keepdims=True))
    pos_ref[...] = jnp.concatenate(cols, axis=1).astype(jnp.int32) + rank_ref[...]
    tiles_col = jnp.ceil(cnt_col_ref[...] * (1.0 / MOE_TM))
    upto = jnp.where(e_c <= e_r, 1.0, 0.0).astype(BF16)
    end_col = _dot(upto, tiles_col.astype(BF16))
    end_b = jnp.concatenate([end_col] * (PLAN_LANES // LANES), axis=1)
    r = lax.broadcasted_iota(jnp.int32, (N_EXPERTS, PLAN_LANES), 1).astype(F32)
    total = end_b[N_EXPERTS - 1:N_EXPERTS, :]
    te = jnp.sum(jnp.where(end_b <= jnp.minimum(r, total - 1.0), 1.0, 0.0), axis=0, keepdims=True)
    te_ref[...] = te.astype(jnp.int32)
    nt_ref[...] = total[:, :LANES].astype(jnp.int32)
    last = jnp.where(cnt_row > 0.0, (start_tile_row + tiles_row - 1.0) * MOE_TM, -1.0)
    last_ref[...] = last.astype(jnp.int32)


def _plan(topi, rank, cnt_row, cnt_col):
    tok = pl.BlockSpec((PLAN_TM, TOP_K), lambda i: (i, 0))
    full = lambda shape: pl.BlockSpec(shape, lambda i: tuple(0 for _ in shape))
    return pl.pallas_call(
        _plan_kernel,
        grid=(N_TOK // PLAN_TM,),
        in_specs=[tok, tok, full((1, N_EXPERTS)), full((N_EXPERTS, LANES))],
        out_specs=[tok, full((1, PLAN_LANES)), full((1, LANES)), full((1, N_EXPERTS))],
        out_shape=[
            jax.ShapeDtypeStruct((N_TOK, TOP_K), jnp.int32),
            jax.ShapeDtypeStruct((1, PLAN_LANES), jnp.int32),
            jax.ShapeDtypeStruct((1, LANES), jnp.int32),
            jax.ShapeDtypeStruct((1, N_EXPERTS), jnp.int32),
        ],
        compiler_params=_cparams(("arbitrary",), 32),
        name="route_plan",
    )(topi, rank, cnt_row, cnt_col)


DSP_TM = 256


def _dispatch_kernel(pos_ref, last_ref, h_ref, xs_ref, zero_buf, zsem, sem):
    i = pl.program_id(0)

    @pl.when(i == 0)
    def _():
        zero_buf[...] = jnp.zeros_like(zero_buf)

        def zero_copy(e):
            return pltpu.make_async_copy(zero_buf, xs_ref.at[pl.ds(pl.multiple_of(last_ref[0, e], MOE_TM), MOE_TM)], zsem)

        for e in range(N_EXPERTS):
            @pl.when(last_ref[0, e] >= 0)
            def _():
                zero_copy(e).start()
        for e in range(N_EXPERTS):
            @pl.when(last_ref[0, e] >= 0)
            def _():
                zero_copy(e).wait()

    def row_copy(t, k):
        return pltpu.make_async_copy(h_ref.at[pl.ds(t, 1)],
                                     xs_ref.at[pl.ds(pos_ref[0, 0, t * TOP_K + k], 1)], sem)

    def issue(t, carry):
        for k in range(TOP_K):
            row_copy(t, k).start()
        return carry

    lax.fori_loop(0, DSP_TM, issue, 0, unroll=8)

    for k in range(TOP_K):
        pltpu.make_async_copy(h_ref, xs_ref.at[pl.ds(0, DSP_TM)], sem).wait()


def _dispatch(pos, last, h2):
    n_tiles = N_TOK // DSP_TM
    return pl.pallas_call(
        _dispatch_kernel,
        grid=(n_tiles,),
        in_specs=[
            pl.BlockSpec((1, 1, DSP_TM * TOP_K), lambda i: (i, 0, 0), memory_space=pltpu.SMEM),
            pl.BlockSpec((1, N_EXPERTS), lambda i: (0, 0), memory_space=pltpu.SMEM),
            pl.BlockSpec((DSP_TM, D_MODEL), lambda i: (i, 0)),
        ],
        out_specs=pl.BlockSpec(memory_space=pl.ANY),
        out_shape=jax.ShapeDtypeStruct((MOE_ROWS, D_MODEL), F32),
        scratch_shapes=[pltpu.VMEM((MOE_TM, D_MODEL), F32), pltpu.SemaphoreType.DMA,
                        pltpu.SemaphoreType.DMA],
        compiler_params=_cparams(("arbitrary",), 32),
        name="dispatch",
    )(pos.reshape(n_tiles, 1, DSP_TM * TOP_K), last, h2)


FFN_ROWS = 256
PAIR_BLOCKS = (2 * D_FF) // (2 * LANES)


def _prep_expert_weights(wgu_ref, wdn_ref, wgu_bf, wdn_perm, wdn_bf):
    wgu_bf[...] = wgu_ref[0, 0].astype(BF16)
    half = LANES // 2
    for cb in range(D_MODEL // LANES):
        cols = slice(cb * LANES, (cb + 1) * LANES)
        for m in range(PAIR_BLOCKS):
            base = m * LANES
            wdn_perm[cb, pl.ds(base, half, stride=2), :] = wdn_ref[0, 0, base:base + half, cols]
            wdn_perm[cb, pl.ds(base + 1, half, stride=2), :] = wdn_ref[0, 0, base + half:base + LANES, cols]
        wdn_bf[:, cols] = wdn_perm[cb].astype(BF16)


def _ffn_rows(x, wgu_bf, bgu, wdn_bf, bdn):
    gu = _dot(x, wgu_bf[...]) + bgu
    lane = lax.broadcasted_iota(jnp.int32, (1, LANES), 1)
    even = (lane % 2) == 0

    def act(block):
        t = gu[:, block * LANES:(block + 1) * LANES]
        glu = jnp.minimum(t, SWIGLU_LIMIT)
        glu = glu * _sigmoid(SWIGLU_ALPHA * glu)
        lin = jnp.clip(t, -SWIGLU_LIMIT, SWIGLU_LIMIT) + 1.0
        return glu * pltpu.roll(lin, LANES - 1, 1)

    parts = []
    for m in range(PAIR_BLOCKS):
        parts.append(jnp.where(even, act(2 * m), pltpu.roll(act(2 * m + 1), 1, 1)))
    hidden = jnp.concatenate(parts, axis=1).astype(BF16)
    return _dot(hidden, wdn_bf[...]) + bdn


def _moe_kernel(te_ref, nt_ref, x_ref, wgu_ref, bgu_ref, wdn_ref, bdn_ref, o_ref,
                wgu_bf, wdn_perm, wdn_bf):
    r = pl.program_id(0)
    valid = r < nt_ref[0]
    new_expert = jnp.logical_or(r == 0, te_ref[r] != te_ref[jnp.maximum(r - 1, 0)])

    @pl.when(jnp.logical_and(valid, new_expert))
    def _():
        _prep_expert_weights(wgu_ref, wdn_ref, wgu_bf, wdn_perm, wdn_bf)

    @pl.when(valid)
    def _():
        o_ref[...] = _ffn_rows(x_ref[...].astype(BF16), wgu_bf, bgu_ref[0, 0], wdn_bf, bdn_ref[0, 0])


def _moe_grouped(tile_expert, n_tiles, xs, w_gate_up, b_gate_up4, w_down, b_down4, layer):
    d, f2 = D_MODEL, 2 * D_FF
    row_tile = lambda r, te, nt: (jnp.minimum(r, nt[0] - 1), 0)
    expert = lambda r, te, nt: (layer, te[r], 0, 0)
    grid_spec = pltpu.PrefetchScalarGridSpec(
        num_scalar_prefetch=2,
        grid=(MOE_TILES,),
        in_specs=[
            pl.BlockSpec((MOE_TM, d), row_tile),
            pl.BlockSpec((1, 1, d, f2), expert),
            pl.BlockSpec((1, 1, 1, f2), expert),
            pl.BlockSpec((1, 1, D_FF, d), expert),
            pl.BlockSpec((1, 1, 1, d), expert),
        ],
        out_specs=pl.BlockSpec((MOE_TM, d), row_tile),
        scratch_shapes=[
            pltpu.VMEM((d, f2), BF16),
            pltpu.VMEM((d // LANES, D_FF, LANES), F32),
            pltpu.VMEM((D_FF, d), BF16),
        ],
    )
    return pl.pallas_call(
        _moe_kernel,
        grid_spec=grid_spec,
        out_shape=jax.ShapeDtypeStruct((MOE_ROWS, d), F32),
        compiler_params=_cparams(("arbitrary",), 56),
        name="moe_grouped",
    )(tile_expert, n_tiles, xs, w_gate_up, b_gate_up4, w_down, b_down4)


CMB_TM = 256


def _combine_kernel(final, pos_ref, ys_ref, x1_ref, topw_ref, mod_ref, *rest):
    buf, sem = rest[-2:]

    def row_copy(t, k):
        return pltpu.make_async_copy(ys_ref.at[pl.ds(pos_ref[0, 0, t * TOP_K + k], 1)],
                                     buf.at[k, pl.ds(t, 1)], sem)

    def issue(t, carry):
        for k in range(TOP_K):
            row_copy(t, k).start()
        return carry

    lax.fori_loop(0, CMB_TM, issue, 0, unroll=8)
    for k in range(TOP_K):
        pltpu.make_async_copy(ys_ref.at[pl.ds(0, CMB_TM)], buf.at[k], sem).wait()
    y = topw_ref[:, 0:1] * buf[0]
    for k in range(1, TOP_K):
        y = y + topw_ref[:, k:k + 1] * buf[k]
    x = x1_ref[...] + mod_ref[0, 5:6, :] * y
    if not final:
        rest[0][...] = x
    else:
        fg_ref, o_ctx_ref, o_lat_ref = rest[:3]
        out = _rms(x) * fg_ref[...]
        is_ctx = pl.program_id(0) < N_CTX_TOK // CMB_TM

        @pl.when(is_ctx)
        def _():
            o_ctx_ref[...] = out

        @pl.when(jnp.logical_not(is_ctx))
        def _():
            o_lat_ref[...] = out


def _combine(pos, ys, x1, topw, mod_l, final_g=None):
    tm, d = CMB_TM, D_MODEL
    n_tiles = N_TOK // tm
    n_ctx = N_CTX_TOK // tm
    final = final_g is not None
    in_specs = [
        pl.BlockSpec((1, 1, tm * TOP_K), lambda i: (i, 0, 0), memory_space=pltpu.SMEM),
        pl.BlockSpec(memory_space=pl.ANY),
        pl.BlockSpec((tm, d), lambda i: (i, 0)),
        pl.BlockSpec((tm, TOP_K), lambda i: (i, 0)),
        pl.BlockSpec((1, 6, d), lambda i: (_mod_row(i, tm), 0, 0)),
    ]
    args = [pos.reshape(n_tiles, 1, tm * TOP_K), ys, x1, topw, mod_l]
    if final:
        in_specs.append(pl.BlockSpec((1, d), lambda i: (0, 0)))
        args.append(final_g.reshape(1, d))
        out_specs = [pl.BlockSpec((tm, d), lambda i: (jnp.minimum(i, n_ctx - 1), 0)),
                     pl.BlockSpec((tm, d), lambda i: (jnp.maximum(i - n_ctx, 0), 0))]
        out_shape = [jax.ShapeDtypeStruct((N_CTX_TOK, d), F32), jax.ShapeDtypeStruct((N_LAT_TOK, d), F32)]
    else:
        out_specs = pl.BlockSpec((tm, d), lambda i: (i, 0))
        out_shape = jax.ShapeDtypeStruct((N_TOK, d), F32)
    return pl.pallas_call(
        functools.partial(_combine_kernel, final),
        grid=(n_tiles,),
        in_specs=in_specs,
        out_specs=out_specs,
        out_shape=out_shape,
        scratch_shapes=[pltpu.VMEM((TOP_K, tm, d), F32), pltpu.SemaphoreType.DMA],
        compiler_params=_cparams(("arbitrary",), 32),
        name="combine_final" if final else "combine",
    )(*args)


def kernel(x_prompt, x_sample, cache_k, cache_v, state_ret_fwd, state_ret_bwd, c, c_ctx, norm1_g, norm2_g, w_mod, b_mod, w_in, lam_a, subln_a, ret_decay, ret_norm, w_branch_a, w_branch_b, w_branch_c, w_out, router_w, router_b, w_gate_up, b_gate_up, w_down, b_down, final_g):
    d = D_MODEL
    x = jnp.concatenate([x_prompt.reshape(N_CTX_TOK, d), x_sample.reshape(N_LAT_TOK, d)], axis=0)
    cond8 = jnp.concatenate([c_ctx[None, :], c, jnp.zeros((3, d), F32)], axis=0)
    mod = _modulation(cond8, w_mod, b_mod)

    cache_k4 = cache_k.reshape(DEC_BATCH, DEPTH, PAST_LEN, W_A)
    cache_v4 = cache_v.reshape(DEC_BATCH, DEPTH, PAST_LEN, W_A)
    s0f4 = state_ret_fwd.reshape(DEC_BATCH, DEPTH, H_C * DK_C, DV_C)
    s0b4 = state_ret_bwd.reshape(DEC_BATCH, DEPTH, H_C * DK_C, DV_C)
    b_gate_up4 = b_gate_up.reshape(DEPTH, N_EXPERTS, 1, 2 * D_FF)
    b_down4 = b_down.reshape(DEPTH, N_EXPERTS, 1, d)
    cos, sin = _rope_tables(DEC_SEQ)
    chan, pos_ctx = _dft_consts(SEQ)
    _, pos_lat = _dft_consts(DEC_SEQ)

    caches, states = None, None
    for l in range(DEPTH):
        lam_init = 0.8 - 0.6 * math.exp(-0.3 * l)
        z, new_k, new_v = _inproj(x, mod[l], norm1_g[l], w_in, l, caches)
        caches = (new_k, new_v)
        ya = _attn_ctx(z, lam_a[l], subln_a[l], lam_init)
        ya = _attn_lat(z, cache_k4, cache_v4, l, cos, sin, lam_a[l], subln_a[l], lam_init, ya)
        yb = _fourier(z, SEQ, BATCH, 0, chan, pos_ctx)
        yb = _fourier(z, DEC_SEQ, DEC_BATCH, N_CTX_TOK // DEC_SEQ, chan, pos_lat, yb)
        yc, new_sf, new_sb = _ret_ctx(z, ret_decay[l], ret_norm[l], l, states)
        states = (new_sf, new_sb)
        yc = _ret_lat(z, ret_decay[l], ret_norm[l], s0f4, s0b4, l, yc)
        x1, h2, topi, topw, rank, cnt_row, cnt_col = _merge(
            x, ya, yb, yc, z, mod[l], w_branch_a, w_branch_b, w_branch_c, w_out,
            norm2_g[l], router_w[l], router_b[l], l)
        pos, tile_expert, n_tiles, last = _plan(topi, rank, cnt_row, cnt_col)
        xs = _dispatch(pos, last, h2)
        ys = _moe_grouped(tile_expert.reshape(PLAN_LANES), n_tiles[0, :1], xs,
                          w_gate_up, b_gate_up4, w_down, b_down4, l)
        if l < DEPTH - 1:
            x = _combine(pos, ys, x1, topw, mod[l])
        else:
            y_ctx, y_lat = _combine(pos, ys, x1, topw, mod[l], final_g)

    return (y_ctx.reshape(BATCH, SEQ, d), y_lat.reshape(DEC_BATCH, DEC_SEQ, d),
            caches[0].reshape(BATCH, DEPTH, SEQ, H_A, 2, DH_A),
            caches[1].reshape(BATCH, DEPTH, SEQ, H_A, DV_A),
            states[0], states[1])
```

```python
import functools
import math

import numpy as np
import jax
import jax.numpy as jnp
from jax import lax
from jax.experimental import pallas as pl
from jax.experimental.pallas import tpu as pltpu

F32 = jnp.float32
BF16 = jnp.bfloat16

D_MODEL = 1024
BATCH = 16
SEQ = 256
DEPTH = 2
DEC_BATCH = 4
DEC_SEQ = 1024
PAST_LEN = 512
GRID_W = 64
H_A = 4
DH_A = 64
DV_A = 128
G_B = 4
DG_B = 128
H_C = 4
DK_C = 64
DV_C = 128
N_EXPERTS = 32
TOP_K = 4
D_FF = 1024
SWIGLU_LIMIT = 7.0
SWIGLU_ALPHA = 1.702
ROPE_BASE = 10000.0
EPS = 1e-6

N_CTX_TOK = BATCH * SEQ
N_LAT_TOK = DEC_BATCH * DEC_SEQ
N_TOK = N_CTX_TOK + N_LAT_TOK
IN_TOTAL = 6656
Z_GATES = 0
Z_QA = 3072
Z_KA = 3584
Z_VA = 4096
Z_FB = 4608
Z_QC = 5120
Z_KC = 5376
Z_VC = 5632
Z_GC = 6144
LANES = 128
MIB = 1024 * 1024


def _cparams(semantics, vmem_mib):
    return pltpu.CompilerParams(dimension_semantics=semantics,
                                vmem_limit_bytes=vmem_mib * MIB)


def _dot(a, b):
    return jnp.dot(a, b, preferred_element_type=F32)


def _dot_nt(a, b):
    return lax.dot_general(a, b, (((1,), (1,)), ((), ())), preferred_element_type=F32)


def _dot_tn(a, b):
    return lax.dot_general(a, b, (((0,), (0,)), ((), ())), preferred_element_type=F32)


def _split_bf16(a):
    hi = a.astype(BF16)
    lo = (a - hi.astype(F32)).astype(BF16)
    return hi, lo


def _dot_f32x3(a, b):
    a_hi, a_lo = _split_bf16(a)
    b_hi, b_lo = _split_bf16(b)
    return _dot(a_hi, b_hi) + _dot(a_lo, b_hi) + _dot(a_hi, b_lo)


def _rms(x):
    return x * lax.rsqrt(jnp.mean(x * x, axis=-1, keepdims=True) + EPS)


def _modnorm(x, g, scale, shift):
    return (_rms(x) * g) * (1.0 + scale) + shift


def _sigmoid(x):
    return 1.0 / (1.0 + jnp.exp(-x))


MOD_TN = 1536


def _mod_kernel(cond_ref, w_ref, b_ref, o_ref):
    cnd = cond_ref[...]
    s = cnd * _sigmoid(cnd)
    o_ref[0] = _dot_f32x3(s, w_ref[0]) + b_ref[0]


def _modulation(cond8, w_mod, b_mod):
    n = 6 * D_MODEL
    out = pl.pallas_call(
        _mod_kernel,
        grid=(DEPTH, n // MOD_TN),
        in_specs=[
            pl.BlockSpec((8, D_MODEL), lambda l, j: (0, 0)),
            pl.BlockSpec((1, D_MODEL, MOD_TN), lambda l, j: (l, 0, j)),
            pl.BlockSpec((1, 1, MOD_TN), lambda l, j: (l, 0, j)),
        ],
        out_specs=pl.BlockSpec((1, 8, MOD_TN), lambda l, j: (l, 0, j)),
        out_shape=jax.ShapeDtypeStruct((DEPTH, 8, n), F32),
        compiler_params=_cparams(("parallel", "parallel"), 40),
        name="modulation",
    )(cond8, w_mod, b_mod.reshape(DEPTH, 1, n))
    return out.reshape(DEPTH, 8, 6, D_MODEL)


def _mod_row(tile_idx, tm):
    n_ctx_tiles = N_CTX_TOK // tm
    per_batch = DEC_SEQ // tm
    return jnp.where(tile_idx < n_ctx_tiles, 0, 1 + (tile_idx - n_ctx_tiles) // per_batch)


INP_TM = 1024
INP_TN = 512
INP_NJ = IN_TOTAL // INP_TN
INP_GATE_BLOCKS = (3 * D_MODEL) // INP_TN
CTX_PER_TILE = INP_TM // SEQ


def _inproj_kernel(x_ref, mod_ref, g_ref, w_ref, *rest):
    z_ref, kc_ref, vc_ref, h_scr = rest[-4:]
    i = pl.program_id(0)
    j = pl.program_id(1)

    @pl.when(j == 0)
    def _():
        h = _modnorm(x_ref[...], g_ref[...], mod_ref[0, 1:2, :], mod_ref[0, 0:1, :])
        h_scr[...] = h.astype(BF16)

    acc = _dot(h_scr[...], w_ref[0].astype(BF16))
    z_ref[...] = acc.astype(BF16)
    is_ctx = i < N_CTX_TOK // INP_TM

    @pl.when(jnp.logical_and(is_ctx, j == Z_KA // INP_TN))
    def _():
        for b in range(CTX_PER_TILE):
            kc_ref[b, 0] = acc[b * SEQ:(b + 1) * SEQ, :]

    @pl.when(jnp.logical_and(is_ctx, j == Z_VA // INP_TN))
    def _():
        for b in range(CTX_PER_TILE):
            vc_ref[b, 0] = acc[b * SEQ:(b + 1) * SEQ, :]


def _inproj(x, mod_l, g, w_in, layer, prev_caches):
    ni = N_TOK // INP_TM
    n_ctx_tiles = N_CTX_TOK // INP_TM
    cache_shape = jax.ShapeDtypeStruct((BATCH, DEPTH, SEQ, W_A), F32)
    cache_spec = pl.BlockSpec((CTX_PER_TILE, 1, SEQ, W_A),
                              lambda i, j: (jnp.minimum(i, n_ctx_tiles - 1), layer, 0, 0))
    in_specs = [
        pl.BlockSpec((INP_TM, D_MODEL), lambda i, j: (i, 0)),
        pl.BlockSpec((1, 6, D_MODEL), lambda i, j: (_mod_row(i, INP_TM), 0, 0)),
        pl.BlockSpec((1, D_MODEL), lambda i, j: (0, 0)),
        pl.BlockSpec((1, D_MODEL, INP_TN),
                     lambda i, j: (layer, 0, (j + INP_NJ - INP_GATE_BLOCKS) % INP_NJ)),
    ]
    args = [x, mod_l, g.reshape(1, D_MODEL), w_in]
    aliases = {}
    if prev_caches is not None:
        in_specs += [pl.BlockSpec(memory_space=pl.ANY)] * 2
        args += list(prev_caches)
        aliases = {4: 1, 5: 2}
    return pl.pallas_call(
        _inproj_kernel,
        grid=(ni, INP_NJ),
        in_specs=in_specs,
        out_specs=[pl.BlockSpec((INP_TM, INP_TN), lambda i, j: (i, j)), cache_spec, cache_spec],
        out_shape=[jax.ShapeDtypeStruct((N_TOK, IN_TOTAL), BF16), cache_shape, cache_shape],
        scratch_shapes=[pltpu.VMEM((INP_TM, D_MODEL), BF16)],
        input_output_aliases=aliases,
        compiler_params=_cparams(("arbitrary", "arbitrary"), 40),
        name="inproj",
    )(*args)


ATT_TQ = 256
W_A = H_A * DV_A


def _lambda(lam_ref, lam_init):
    la = lam_ref[...]
    l01 = jnp.sum(la[0:1, :] * la[1:2, :], axis=1, keepdims=True)
    l23 = jnp.sum(la[2:3, :] * la[3:4, :], axis=1, keepdims=True)
    return jnp.exp(l01) - jnp.exp(l23) + lam_init


def _softmax_parts(s):
    m = jnp.max(s, axis=-1, keepdims=True)
    e = jnp.exp(s - m)
    return e, 1.0 / jnp.sum(e, axis=-1, keepdims=True)


def _diff_attn_block(q, k_ref, v_ref, lam, subln, lam_init):
    lane = lax.broadcasted_iota(jnp.int32, (1, LANES), 1)
    first = lane < DH_A
    scale = DH_A ** -0.5
    outs = []
    for h in range(H_A):
        sl = slice(h * LANES, (h + 1) * LANES)
        qh = q[:, sl]
        kh = k_ref[:, sl]
        zero = jnp.zeros_like(qh)
        s0 = _dot_nt(jnp.where(first, qh, zero), kh) * scale
        s1 = _dot_nt(jnp.where(first, zero, qh), kh) * scale
        e0, r0 = _softmax_parts(s0)
        e1, r1 = _softmax_parts(s1)
        a = e0 * r0 - (lam * r1) * e1
        o = _dot(a.astype(BF16), v_ref[:, sl])
        outs.append((_rms(o) * subln) * (1.0 - lam_init))
    return outs


def _rope(x, cos, sin):
    lane = lax.broadcasted_iota(jnp.int32, (1, x.shape[1]), 1)
    first = (lane % 32) < 16
    n = x.shape[1]
    xr = jnp.where(first, -pltpu.roll(x, n - 16, 1), pltpu.roll(x, 16, 1))
    return x * cos + xr * sin


def _attn_ctx_kernel(lam_init, q_ref, k_ref, v_ref, lam_ref, subln_ref, o_ref):
    lam = _lambda(lam_ref, lam_init)
    outs = _diff_attn_block(q_ref[...], k_ref, v_ref, lam, subln_ref[...], lam_init)
    for h in range(H_A):
        o_ref[:, h * LANES:(h + 1) * LANES] = outs[h].astype(BF16)


def _attn_ctx(z, lam_a_l, subln_l, lam_init):
    wb = W_A
    return pl.pallas_call(
        functools.partial(_attn_ctx_kernel, lam_init),
        grid=(BATCH,),
        in_specs=[
            pl.BlockSpec((SEQ, wb), lambda b: (b, Z_QA // wb)),
            pl.BlockSpec((SEQ, wb), lambda b: (b, Z_KA // wb)),
            pl.BlockSpec((SEQ, wb), lambda b: (b, Z_VA // wb)),
            pl.BlockSpec((4, DH_A), lambda b: (0, 0)),
            pl.BlockSpec((1, DV_A), lambda b: (0, 0)),
        ],
        out_specs=pl.BlockSpec((SEQ, wb), lambda b: (b, 0)),
        out_shape=jax.ShapeDtypeStruct((N_TOK, wb), BF16),
        compiler_params=_cparams(("parallel",), 40),
        name="attn_ctx",
    )(z, z, z, lam_a_l, subln_l.reshape(1, DV_A))


def _attn_lat_kernel(lam_init, q_ref, k_ref, v_ref, ck_ref, cv_ref, cos_ref, sin_ref,
                     lam_ref, subln_ref, ya_ctx_ref, o_ref, kall, vall):
    del ya_ctx_ref
    lam = _lambda(lam_ref, lam_init)
    subln = subln_ref[...]
    kall[0:DEC_SEQ, :] = _rope(k_ref[...].astype(F32), cos_ref[...], sin_ref[...]).astype(BF16)
    kall[DEC_SEQ:, :] = ck_ref[0, 0].astype(BF16)
    vall[0:DEC_SEQ, :] = v_ref[...]
    vall[DEC_SEQ:, :] = cv_ref[0, 0].astype(BF16)

    def q_block(qb, carry):
        rows = pl.ds(pl.multiple_of(qb * ATT_TQ, ATT_TQ), ATT_TQ)
        q = _rope(q_ref[rows, :].astype(F32), cos_ref[rows, :], sin_ref[rows, :]).astype(BF16)
        outs = _diff_attn_block(q, kall, vall, lam, subln, lam_init)
        for h in range(H_A):
            o_ref[rows, h * LANES:(h + 1) * LANES] = outs[h].astype(BF16)
        return carry

    lax.fori_loop(0, DEC_SEQ // ATT_TQ, q_block, 0)


def _attn_lat(z, cache_k4, cache_v4, layer, cos, sin, lam_a_l, subln_l, lam_init, ya_ctx):
    wb = W_A
    rb0 = N_CTX_TOK // DEC_SEQ
    t_all = DEC_SEQ + PAST_LEN
    return pl.pallas_call(
        functools.partial(_attn_lat_kernel, lam_init),
        grid=(DEC_BATCH,),
        in_specs=[
            pl.BlockSpec((DEC_SEQ, wb), lambda b: (rb0 + b, Z_QA // wb)),
            pl.BlockSpec((DEC_SEQ, wb), lambda b: (rb0 + b, Z_KA // wb)),
            pl.BlockSpec((DEC_SEQ, wb), lambda b: (rb0 + b, Z_VA // wb)),
            pl.BlockSpec((1, 1, PAST_LEN, wb), lambda b: (b, layer, 0, 0)),
            pl.BlockSpec((1, 1, PAST_LEN, wb), lambda b: (b, layer, 0, 0)),
            pl.BlockSpec((DEC_SEQ, wb), lambda b: (0, 0)),
            pl.BlockSpec((DEC_SEQ, wb), lambda b: (0, 0)),
            pl.BlockSpec((4, DH_A), lambda b: (0, 0)),
            pl.BlockSpec((1, DV_A), lambda b: (0, 0)),
            pl.BlockSpec(memory_space=pl.ANY),
        ],
        out_specs=pl.BlockSpec((DEC_SEQ, wb), lambda b: (rb0 + b, 0)),
        out_shape=jax.ShapeDtypeStruct((N_TOK, wb), BF16),
        scratch_shapes=[pltpu.VMEM((t_all, wb), BF16), pltpu.VMEM((t_all, wb), BF16)],
        input_output_aliases={9: 0},
        compiler_params=_cparams(("parallel",), 48),
        name="attn_lat",
    )(z, z, z, cache_k4, cache_v4, cos, sin, lam_a_l, subln_l.reshape(1, DV_A), ya_ctx)


def _rope_tables(s):
    rows = s // GRID_W
    row = jnp.repeat(jnp.arange(rows, dtype=F32), GRID_W)
    col = jnp.tile(jnp.arange(GRID_W, dtype=F32), rows)
    quarter = DH_A // 4
    inv = ROPE_BASE ** (-jnp.arange(quarter, dtype=F32) / quarter)
    ar = row[:, None] * inv
    ac = col[:, None] * inv
    ang = jnp.concatenate([ar, ar, ac, ac], axis=-1)
    reps = (H_A * 2 * DH_A) // DH_A
    return jnp.tile(jnp.cos(ang), (1, reps)), jnp.tile(jnp.sin(ang), (1, reps))


W_B = G_B * DG_B


def _dft_consts(s):
    def cs(n, scale):
        j = np.arange(n, dtype=np.int64)
        ang = 2.0 * np.pi * ((j[:, None] * j[None, :]) % n).astype(np.float64) / n
        return np.cos(ang) * scale, np.sin(ang) * scale
    cc, sc = cs(DG_B, DG_B ** -0.5)
    chan = np.concatenate([cc, sc], axis=1)
    cp, sp = cs(s, s ** -0.5)
    pos = np.concatenate([cp, -sp], axis=1)
    return jnp.asarray(chan, dtype=F32).astype(BF16), jnp.asarray(pos, dtype=F32).astype(BF16)


def _fourier_kernel(s, f_ref, chan_ref, pos_ref, *rest):
    o_ref, p_ref = rest[-2:]
    chan = chan_ref[...]
    for g in range(G_B):
        sl = slice(g * DG_B, (g + 1) * DG_B)
        a = _dot(f_ref[:, sl], chan)
        p_ref[0:s, sl] = a[:, :DG_B].astype(BF16)
        p_ref[s:2 * s, sl] = a[:, DG_B:].astype(BF16)
    o_ref[...] = _dot(pos_ref[...], p_ref[...]).astype(BF16)


def _fourier(z, s, n_batch, row_block0, chan, pos, prev=None):
    in_specs = [
        pl.BlockSpec((s, W_B), lambda b: (row_block0 + b, Z_FB // W_B)),
        pl.BlockSpec((DG_B, 2 * DG_B), lambda b: (0, 0)),
        pl.BlockSpec((s, 2 * s), lambda b: (0, 0)),
    ]
    args = [z, chan, pos]
    aliases = {}
    if prev is not None:
        in_specs.append(pl.BlockSpec(memory_space=pl.ANY))
        args.append(prev)
        aliases = {3: 0}
    return pl.pallas_call(
        functools.partial(_fourier_kernel, s),
        grid=(n_batch,),
        in_specs=in_specs,
        out_specs=pl.BlockSpec((s, W_B), lambda b: (row_block0 + b, 0)),
        out_shape=jax.ShapeDtypeStruct((N_TOK, W_B), BF16),
        scratch_shapes=[pltpu.VMEM((2 * s, W_B), BF16)],
        input_output_aliases=aliases,
        compiler_params=_cparams(("parallel",), 40),
        name="fourier_%d" % s,
    )(*args)


W_C = H_C * DV_C
RET_TQ = 256


def _log_sigmoid(x):
    return jnp.minimum(x, 0.0) - jnp.log1p(jnp.exp(-jnp.abs(x)))


def _head_pair_lanes(vals, pair):
    lane = lax.broadcasted_iota(jnp.int32, (1, LANES), 1)
    return jnp.where(lane < DK_C, vals[:, 2 * pair:2 * pair + 1], vals[:, 2 * pair + 1:2 * pair + 2])


def _retention_rows(s, row0, tq, q, k_ref, v_ref, g, lg, rnorm, s0f_ref, s0b_ref):
    lane = lax.broadcasted_iota(jnp.int32, (1, LANES), 1)
    ii = (row0 + lax.broadcasted_iota(jnp.int32, (tq, s), 0)).astype(F32)
    jj = lax.broadcasted_iota(jnp.int32, (tq, s), 1).astype(F32)
    rel = ii - jj
    pos = (row0 + lax.broadcasted_iota(jnp.int32, (tq, 1), 0)).astype(F32)
    outs = []
    for h in range(H_C):
        pair, half = h // 2, h % 2
        sl = slice(pair * LANES, (pair + 1) * LANES)
        lgf = lg[0:1, h:h + 1]
        lgb = lg[1:2, h:h + 1]
        in_head = (lane < DK_C) if half == 0 else (lane >= DK_C)
        q_pair = q[:, sl]
        qh = jnp.where(in_head, q_pair, jnp.zeros_like(q_pair))
        sc = _dot_nt(qh, k_ref[:, sl]) * (DK_C ** -0.5)
        decay = (jnp.where(rel >= 0, jnp.exp(lgf * jnp.maximum(rel, 0.0)), 0.0)
                 + jnp.where(rel <= 0, jnp.exp(lgb * jnp.maximum(-rel, 0.0)), 0.0))
        o = _dot((sc * decay).astype(BF16), v_ref[:, h * DV_C:(h + 1) * DV_C])
        if s0f_ref is not None:
            qf = qh.astype(F32)
            xf = jnp.exp(lgf * (pos + 1.0))
            xb = jnp.exp(lgb * (s - pos))
            o = o + _dot((qf * xf).astype(BF16), s0f_ref[0, 0, sl, :].astype(BF16))
            o = o + _dot((qf * xb).astype(BF16), s0b_ref[0, 0, sl, :].astype(BF16))
        gh = g[:, h * DV_C:(h + 1) * DV_C].astype(F32)
        outs.append((_rms(o) * rnorm) * (gh * _sigmoid(gh)))
    return outs


def _ret_ctx_kernel(q_ref, k_ref, v_ref, g_ref, dec_ref, rn_ref, *rest):
    o_ref, sf_ref, sb_ref = rest[-3:]
    s = SEQ
    lg = _log_sigmoid(dec_ref[...])
    outs = _retention_rows(s, 0, s, q_ref[...], k_ref, v_ref, g_ref[...], lg, rn_ref[...], None, None)
    for h in range(H_C):
        o_ref[:, h * DV_C:(h + 1) * DV_C] = outs[h].astype(BF16)
    pos = lax.broadcasted_iota(jnp.int32, (s, 1), 0).astype(F32)
    v = v_ref[...]
    for pair in range(H_C // 2):
        sl = slice(pair * LANES, (pair + 1) * LANES)
        kf = k_ref[:, sl].astype(F32) * (DK_C ** -0.5)
        zf = jnp.exp(_head_pair_lanes(lg[0:1, :], pair) * (s - 1.0 - pos))
        zb = jnp.exp(_head_pair_lanes(lg[1:2, :], pair) * pos)
        stf = _dot_tn((kf * zf).astype(BF16), v)
        stb = _dot_tn((kf * zb).astype(BF16), v)
        for half in range(2):
            h = 2 * pair + half
            rs = slice(half * DK_C, (half + 1) * DK_C)
            cs = slice(h * DV_C, (h + 1) * DV_C)
            sf_ref[0, 0, h] = stf[rs, cs]
            sb_ref[0, 0, h] = stb[rs, cs]


def _ret_ctx(z, dec_l, rnorm_l, layer, prev_states):
    wq = H_C * DK_C
    st_shape = jax.ShapeDtypeStruct((BATCH, DEPTH, H_C, DK_C, DV_C), F32)
    st_spec = pl.BlockSpec((1, 1, H_C, DK_C, DV_C), lambda b: (b, layer, 0, 0, 0))
    in_specs = [
        pl.BlockSpec((SEQ, wq), lambda b: (b, Z_QC // wq)),
        pl.BlockSpec((SEQ, wq), lambda b: (b, Z_KC // wq)),
        pl.BlockSpec((SEQ, W_C), lambda b: (b, Z_VC // W_C)),
        pl.BlockSpec((SEQ, W_C), lambda b: (b, Z_GC // W_C)),
        pl.BlockSpec((2, H_C), lambda b: (0, 0)),
        pl.BlockSpec((1, DV_C), lambda b: (0, 0)),
    ]
    args = [z, z, z, z, dec_l, rnorm_l.reshape(1, DV_C)]
    aliases = {}
    if prev_states is not None:
        in_specs += [pl.BlockSpec(memory_space=pl.ANY)] * 2
        args += list(prev_states)
        aliases = {6: 1, 7: 2}
    return pl.pallas_call(
        _ret_ctx_kernel,
        grid=(BATCH,),
        in_specs=in_specs,
        out_specs=[pl.BlockSpec((SEQ, W_C), lambda b: (b, 0)), st_spec, st_spec],
        out_shape=[jax.ShapeDtypeStruct((N_TOK, W_C), BF16), st_shape, st_shape],
        input_output_aliases=aliases,
        compiler_params=_cparams(("parallel",), 40),
        name="ret_ctx",
    )(*args)


def _ret_lat_kernel(q_ref, k_ref, v_ref, g_ref, dec_ref, rn_ref, s0f_ref, s0b_ref, yc_ctx_ref, o_ref):
    del yc_ctx_ref
    lg = _log_sigmoid(dec_ref[...])
    rnorm = rn_ref[...]

    def q_block(qb, carry):
        row0 = pl.multiple_of(qb * RET_TQ, RET_TQ)
        rows = pl.ds(row0, RET_TQ)
        outs = _retention_rows(DEC_SEQ, row0, RET_TQ, q_ref[rows, :], k_ref, v_ref, g_ref[rows, :],
                               lg, rnorm, s0f_ref, s0b_ref)
        for h in range(H_C):
            o_ref[rows, h * DV_C:(h + 1) * DV_C] = outs[h].astype(BF16)
        return carry

    lax.fori_loop(0, DEC_SEQ // RET_TQ, q_block, 0)


def _ret_lat(z, dec_l, rnorm_l, s0f4, s0b4, layer, yc_ctx):
    wq = H_C * DK_C
    rb0 = N_CTX_TOK // DEC_SEQ
    st_spec = pl.BlockSpec((1, 1, H_C * DK_C, DV_C), lambda b: (b, layer, 0, 0))
    return pl.pallas_call(
        _ret_lat_kernel,
        grid=(DEC_BATCH,),
        in_specs=[
            pl.BlockSpec((DEC_SEQ, wq), lambda b: (rb0 + b, Z_QC // wq)),
            pl.BlockSpec((DEC_SEQ, wq), lambda b: (rb0 + b, Z_KC // wq)),
            pl.BlockSpec((DEC_SEQ, W_C), lambda b: (rb0 + b, Z_VC // W_C)),
            pl.BlockSpec((DEC_SEQ, W_C), lambda b: (rb0 + b, Z_GC // W_C)),
            pl.BlockSpec((2, H_C), lambda b: (0, 0)),
            pl.BlockSpec((1, DV_C), lambda b: (0, 0)),
            st_spec, st_spec,
            pl.BlockSpec(memory_space=pl.ANY),
        ],
        out_specs=pl.BlockSpec((DEC_SEQ, W_C), lambda b: (rb0 + b, 0)),
        out_shape=jax.ShapeDtypeStruct((N_TOK, W_C), BF16),
        input_output_aliases={8: 0},
        compiler_params=_cparams(("parallel",), 40),
        name="ret_lat",
    )(z, z, z, z, dec_l, rnorm_l.reshape(1, DV_C), s0f4, s0b4, yc_ctx)


MRG_TM = 256


def _merge_kernel(x_ref, ya_ref, yb_ref, yc_ref, gt_ref, mod_ref, wa_ref, wb_ref, wc_ref, wo_ref,
                  n2_ref, rw_ref, rb_ref, x1_ref, h2_ref, topi_ref, topw_ref, rank_ref,
                  cnt_row_ref, cnt_col_ref, carry_row, carry_col, wa_bf, wb_bf, wc_bf, wo_bf):
    d = D_MODEL
    tm = x_ref.shape[0]

    @pl.when(pl.program_id(0) == 0)
    def _():
        carry_row[...] = jnp.zeros_like(carry_row)
        carry_col[...] = jnp.zeros_like(carry_col)
        wa_bf[...] = wa_ref[0].astype(BF16)
        wb_bf[...] = wb_ref[0].astype(BF16)
        wc_bf[...] = wc_ref[0].astype(BF16)
        wo_bf[...] = wo_ref[0].astype(BF16)

    merged = _sigmoid(gt_ref[:, 0:d].astype(F32)) * _dot(ya_ref[...], wa_bf[...])
    merged += _sigmoid(gt_ref[:, d:2 * d].astype(F32)) * _dot(yb_ref[...], wb_bf[...])
    merged += _sigmoid(gt_ref[:, 2 * d:3 * d].astype(F32)) * _dot(yc_ref[...], wc_bf[...])
    y = _dot(merged.astype(BF16), wo_bf[...])
    x1 = x_ref[...] + mod_ref[0, 2:3, :] * y
    x1_ref[...] = x1
    h2 = _modnorm(x1, n2_ref[...], mod_ref[0, 4:5, :], mod_ref[0, 3:4, :])
    h2_ref[...] = h2
    logits = _dot_f32x3(h2, rw_ref[...]) + rb_ref[...]
    ids = lax.broadcasted_iota(jnp.int32, logits.shape, 1)
    work = logits
    sels, vals, firsts = [], [], []
    for _ in range(TOP_K):
        m = jnp.max(work, axis=-1, keepdims=True)
        first = jnp.min(jnp.where(work == m, ids, N_EXPERTS), axis=-1, keepdims=True)
        sel = ids == first
        sels.append(sel)
        vals.append(m)
        firsts.append(first)
        work = jnp.where(sel, -jnp.inf, work)
    es = [jnp.exp(v - vals[0]) for v in vals]
    inv = 1.0 / (es[0] + es[1] + es[2] + es[3])
    chosen = jnp.zeros_like(logits)
    for sel in sels:
        chosen = chosen + jnp.where(sel, 1.0, 0.0)
    chosen_bf = chosen.astype(BF16)
    rr = lax.broadcasted_iota(jnp.int32, (tm, tm), 0)
    cc = lax.broadcasted_iota(jnp.int32, (tm, tm), 1)
    before = jnp.where(rr > cc, 1.0, 0.0).astype(BF16)
    prefix = _dot(before, chosen_bf) + carry_row[...]
    ranks = [jnp.sum(jnp.where(sel, prefix, 0.0), axis=-1, keepdims=True) for sel in sels]
    carry_row[...] += jnp.sum(chosen, axis=0, keepdims=True)
    carry_col[...] += _dot_tn(chosen_bf, jnp.ones((tm, LANES), BF16))
    cnt_row_ref[...] = carry_row[...]
    cnt_col_ref[...] = carry_col[...]
    topi_ref[...] = jnp.concatenate(firsts, axis=1)
    topw_ref[...] = jnp.concatenate([e * inv for e in es], axis=1)
    rank_ref[...] = jnp.concatenate(ranks, axis=1).astype(jnp.int32)


def _merge(x, ya, yb, yc, z, mod_l, wa, wb, wc, wo, n2, rw, rb, layer):
    tm = MRG_TM
    d = D_MODEL
    full = lambda shape: pl.BlockSpec(shape, lambda i: tuple(0 for _ in shape))
    tok = lambda w: pl.BlockSpec((tm, w), lambda i: (i, 0))
    per_layer = lambda rows: pl.BlockSpec((1, rows, d), lambda i: (layer, 0, 0))
    return pl.pallas_call(
        _merge_kernel,
        grid=(N_TOK // tm,),
        in_specs=[
            tok(d), tok(W_A), tok(W_B), tok(W_C), tok(3 * d),
            pl.BlockSpec((1, 6, d), lambda i: (_mod_row(i, tm), 0, 0)),
            per_layer(W_A), per_layer(W_B), per_layer(W_C), per_layer(d),
            full((1, d)), full((d, N_EXPERTS)), full((1, N_EXPERTS)),
        ],
        out_specs=[tok(d), tok(d), tok(TOP_K), tok(TOP_K), tok(TOP_K),
                   full((1, N_EXPERTS)), full((N_EXPERTS, LANES))],
        out_shape=[
            jax.ShapeDtypeStruct((N_TOK, d), F32),
            jax.ShapeDtypeStruct((N_TOK, d), F32),
            jax.ShapeDtypeStruct((N_TOK, TOP_K), jnp.int32),
            jax.ShapeDtypeStruct((N_TOK, TOP_K), F32),
            jax.ShapeDtypeStruct((N_TOK, TOP_K), jnp.int32),
            jax.ShapeDtypeStruct((1, N_EXPERTS), F32),
            jax.ShapeDtypeStruct((N_EXPERTS, LANES), F32),
        ],
        scratch_shapes=[pltpu.VMEM((1, N_EXPERTS), F32), pltpu.VMEM((N_EXPERTS, LANES), F32),
                        pltpu.VMEM((W_A, d), BF16), pltpu.VMEM((W_B, d), BF16),
                        pltpu.VMEM((W_C, d), BF16), pltpu.VMEM((d, d), BF16)],
        compiler_params=_cparams(("arbitrary",), 56),
        name="merge_router",
    )(x, ya, yb, yc, z, mod_l, wa, wb, wc, wo, n2.reshape(1, d), rw, rb.reshape(1, N_EXPERTS))


MOE_TM = 256
MOE_TILES = (N_TOK * TOP_K) // MOE_TM + N_EXPERTS
MOE_ROWS = MOE_TILES * MOE_TM
PLAN_LANES = 256
PLAN_TM = 1024


def _plan_kernel(topi_ref, rank_ref, cnt_row_ref, cnt_col_ref, pos_ref, te_ref, nt_ref, last_ref,
                 next_ref):
    e_r = lax.broadcasted_iota(jnp.int32, (N_EXPERTS, N_EXPERTS), 0)
    e_c = lax.broadcasted_iota(jnp.int32, (N_EXPERTS, N_EXPERTS), 1)
    cnt_row = cnt_row_ref[...]
    tiles_row = jnp.ceil(cnt_row * (1.0 / MOE_TM))
    earlier = jnp.where(e_r < e_c, 1.0, 0.0).astype(BF16)
    start_tile_row = _dot(tiles_row.astype(BF16), earlier)
    off_row = start_tile_row * MOE_TM
    ids = lax.broadcasted_iota(jnp.int32, (PLAN_TM, N_EXPERTS), 1)
    cols = []
    for k in range(TOP_K):
        hit = ids == topi_ref[:, k:k + 1]
        cols.append(jnp.sum(jnp.where(hit, off_row, 0.0), axis=-1, keepdims=True))
    pos_ref[...] = jnp.concatenate(cols, axis=1).astype(jnp.int32) + rank_ref[...]
    tiles_col = jnp.ceil(cnt_col_ref[...] * (1.0 / MOE_TM))
    upto = jnp.where(e_c <= e_r, 1.0, 0.0).astype(BF16)
    end_col = _dot(upto, tiles_col.astype(BF16))
    end_b = jnp.concatenate([end_col] * (PLAN_LANES // LANES), axis=1)
    r = lax.broadcasted_iota(jnp.int32, (N_EXPERTS, PLAN_LANES), 1).astype(F32)
    total = end_b[N_EXPERTS - 1:N_EXPERTS, :]
    te = jnp.sum(jnp.where(end_b <= jnp.minimum(r, total - 1.0), 1.0, 0.0), axis=0, keepdims=True)
    te_ref[...] = te.astype(jnp.int32)
    nt_ref[...] = total[:, :LANES].astype(jnp.int32)
    last = jnp.where(cnt_row > 0.0, (start_tile_row + tiles_row - 1.0) * MOE_TM, -1.0)
    last_ref[...] = last.astype(jnp.int32)
    later_used = jnp.logical_and(e_r > e_c, cnt_col_ref[:, 0:N_EXPERTS] > 0.0)
    nxt = jnp.min(jnp.where(later_used, e_r, N_EXPERTS), axis=0, keepdims=True)
    next_ref[...] = jnp.where(nxt == N_EXPERTS, -1, nxt)


def _plan(topi, rank, cnt_row, cnt_col):
    tok = pl.BlockSpec((PLAN_TM, TOP_K), lambda i: (i, 0))
    full = lambda shape: pl.BlockSpec(shape, lambda i: tuple(0 for _ in shape))
    return pl.pallas_call(
        _plan_kernel,
        grid=(N_TOK // PLAN_TM,),
        in_specs=[tok, tok, full((1, N_EXPERTS)), full((N_EXPERTS, LANES))],
        out_specs=[tok, full((1, PLAN_LANES)), full((1, LANES)), full((1, N_EXPERTS)),
                   full((1, N_EXPERTS))],
        out_shape=[
            jax.ShapeDtypeStruct((N_TOK, TOP_K), jnp.int32),
            jax.ShapeDtypeStruct((1, PLAN_LANES), jnp.int32),
            jax.ShapeDtypeStruct((1, LANES), jnp.int32),
            jax.ShapeDtypeStruct((1, N_EXPERTS), jnp.int32),
            jax.ShapeDtypeStruct((1, N_EXPERTS), jnp.int32),
        ],
        compiler_params=_cparams(("arbitrary",), 32),
        name="route_plan",
    )(topi, rank, cnt_row, cnt_col)


DSP_TM = 256


def _dispatch_kernel(pos_ref, last_ref, h_ref, xs_ref, zero_buf, zsem, sem):
    i = pl.program_id(0)

    @pl.when(i == 0)
    def _():
        zero_buf[...] = jnp.zeros_like(zero_buf)

        def zero_copy(e):
            return pltpu.make_async_copy(zero_buf, xs_ref.at[pl.ds(pl.multiple_of(last_ref[0, e], MOE_TM), MOE_TM)], zsem)

        for e in range(N_EXPERTS):
            @pl.when(last_ref[0, e] >= 0)
            def _():
                zero_copy(e).start()
        for e in range(N_EXPERTS):
            @pl.when(last_ref[0, e] >= 0)
            def _():
                zero_copy(e).wait()

    def row_copy(t, k):
        return pltpu.make_async_copy(h_ref.at[pl.ds(t, 1)],
                                     xs_ref.at[pl.ds(pos_ref[0, 0, t * TOP_K + k], 1)], sem)

    def issue(t, carry):
        for k in range(TOP_K):
            row_copy(t, k).start()
        return carry

    lax.fori_loop(0, DSP_TM, issue, 0, unroll=8)

    for k in range(TOP_K):
        pltpu.make_async_copy(h_ref, xs_ref.at[pl.ds(0, DSP_TM)], sem).wait()


def _dispatch(pos, last, h2):
    n_tiles = N_TOK // DSP_TM
    return pl.pallas_call(
        _dispatch_kernel,
        grid=(n_tiles,),
        in_specs=[
            pl.BlockSpec((1, 1, DSP_TM * TOP_K), lambda i: (i, 0, 0), memory_space=pltpu.SMEM),
            pl.BlockSpec((1, N_EXPERTS), lambda i: (0, 0), memory_space=pltpu.SMEM),
            pl.BlockSpec((DSP_TM, D_MODEL), lambda i: (i, 0)),
        ],
        out_specs=pl.BlockSpec(memory_space=pl.ANY),
        out_shape=jax.ShapeDtypeStruct((MOE_ROWS, D_MODEL), F32),
        scratch_shapes=[pltpu.VMEM((MOE_TM, D_MODEL), F32), pltpu.SemaphoreType.DMA,
                        pltpu.SemaphoreType.DMA],
        compiler_params=_cparams(("arbitrary",), 32),
        name="dispatch",
    )(pos.reshape(n_tiles, 1, DSP_TM * TOP_K), last, h2)


FFN_ROWS = 256
PAIR_BLOCKS = (2 * D_FF) // (2 * LANES)


def _prep_expert_weights(wgu_ref, wdn_ref, wgu_bf, wdn_perm, wdn_bf):
    wgu_bf[...] = wgu_ref[...].astype(BF16)
    half = LANES // 2
    for cb in range(D_MODEL // LANES):
        cols = slice(cb * LANES, (cb + 1) * LANES)
        for m in range(PAIR_BLOCKS):
            base = m * LANES
            wdn_perm[cb, pl.ds(base, half, stride=2), :] = wdn_ref[base:base + half, cols]
            wdn_perm[cb, pl.ds(base + 1, half, stride=2), :] = wdn_ref[base + half:base + LANES, cols]
        wdn_bf[:, cols] = wdn_perm[cb].astype(BF16)


def _ffn_rows(x, wgu_bf, bgu, wdn_bf, bdn):
    gu = _dot(x, wgu_bf[...]) + bgu
    lane = lax.broadcasted_iota(jnp.int32, (1, LANES), 1)
    even = (lane % 2) == 0

    def act(block):
        t = gu[:, block * LANES:(block + 1) * LANES]
        glu = jnp.minimum(t, SWIGLU_LIMIT)
        glu = glu * _sigmoid(SWIGLU_ALPHA * glu)
        lin = jnp.clip(t, -SWIGLU_LIMIT, SWIGLU_LIMIT) + 1.0
        return glu * pltpu.roll(lin, LANES - 1, 1)

    parts = []
    for m in range(PAIR_BLOCKS):
        parts.append(jnp.where(even, act(2 * m), pltpu.roll(act(2 * m + 1), 1, 1)))
    hidden = jnp.concatenate(parts, axis=1).astype(BF16)
    return _dot(hidden, wdn_bf[...]) + bdn


def _moe_kernel(layer, te_ref, nt_ref, next_ref, x_ref, wgu_hbm, bgu_ref, wdn_hbm, bdn_ref, o_ref,
                wgu_land, wdn_land, wgu_bf, wdn_perm, wdn_bf, sems):
    r = pl.program_id(0)
    valid = r < nt_ref[0]
    e = te_ref[r]
    new_expert = jnp.logical_or(r == 0, e != te_ref[jnp.maximum(r - 1, 0)])

    def fetch(expert):
        return (pltpu.make_async_copy(wgu_hbm.at[layer, expert], wgu_land, sems.at[0]),
                pltpu.make_async_copy(wdn_hbm.at[layer, expert], wdn_land, sems.at[1]))

    @pl.when(r == 0)
    def _():
        for cp in fetch(e):
            cp.start()

    @pl.when(jnp.logical_and(valid, new_expert))
    def _():
        for cp in fetch(e):
            cp.wait()
        _prep_expert_weights(wgu_land, wdn_land, wgu_bf, wdn_perm, wdn_bf)
        nxt = next_ref[e]

        @pl.when(nxt >= 0)
        def _():
            for cp in fetch(nxt):
                cp.start()

    @pl.when(valid)
    def _():
        o_ref[...] = _ffn_rows(x_ref[...].astype(BF16), wgu_bf, bgu_ref[0, 0], wdn_bf, bdn_ref[0, 0])


def _moe_grouped(tile_expert, n_tiles, next_expert, xs, w_gate_up, b_gate_up4, w_down, b_down4, layer):
    d, f2 = D_MODEL, 2 * D_FF
    row_tile = lambda r, te, nt, nx: (jnp.minimum(r, nt[0] - 1), 0)
    expert = lambda r, te, nt, nx: (layer, te[r], 0, 0)
    grid_spec = pltpu.PrefetchScalarGridSpec(
        num_scalar_prefetch=3,
        grid=(MOE_TILES,),
        in_specs=[
            pl.BlockSpec((MOE_TM, d), row_tile),
            pl.BlockSpec(memory_space=pl.ANY),
            pl.BlockSpec((1, 1, 1, f2), expert),
            pl.BlockSpec(memory_space=pl.ANY),
            pl.BlockSpec((1, 1, 1, d), expert),
        ],
        out_specs=pl.BlockSpec((MOE_TM, d), row_tile),
        scratch_shapes=[
            pltpu.VMEM((d, f2), F32),
            pltpu.VMEM((D_FF, d), F32),
            pltpu.VMEM((d, f2), BF16),
            pltpu.VMEM((d // LANES, D_FF, LANES), F32),
            pltpu.VMEM((D_FF, d), BF16),
            pltpu.SemaphoreType.DMA((2,)),
        ],
    )
    return pl.pallas_call(
        functools.partial(_moe_kernel, layer),
        grid_spec=grid_spec,
        out_shape=jax.ShapeDtypeStruct((MOE_ROWS, d), F32),
        compiler_params=_cparams(("arbitrary",), 56),
        name="moe_grouped",
    )(tile_expert, n_tiles, next_expert, xs, w_gate_up, b_gate_up4, w_down, b_down4)


CMB_TM = 256


def _combine_kernel(final, pos_ref, ys_ref, x1_ref, topw_ref, mod_ref, *rest):
    buf, sem = rest[-2:]

    def row_copy(t, k):
        return pltpu.make_async_copy(ys_ref.at[pl.ds(pos_ref[0, 0, t * TOP_K + k], 1)],
                                     buf.at[k, pl.ds(t, 1)], sem)

    def issue(t, carry):
        for k in range(TOP_K):
            row_copy(t, k).start()
        return carry

    lax.fori_loop(0, CMB_TM, issue, 0, unroll=8)
    for k in range(TOP_K):
        pltpu.make_async_copy(ys_ref.at[pl.ds(0, CMB_TM)], buf.at[k], sem).wait()
    y = topw_ref[:, 0:1] * buf[0]
    for k in range(1, TOP_K):
        y = y + topw_ref[:, k:k + 1] * buf[k]
    x = x1_ref[...] + mod_ref[0, 5:6, :] * y
    if not final:
        rest[0][...] = x
    else:
        fg_ref, o_ctx_ref, o_lat_ref = rest[:3]
        out = _rms(x) * fg_ref[...]
        is_ctx = pl.program_id(0) < N_CTX_TOK // CMB_TM

        @pl.when(is_ctx)
        def _():
            o_ctx_ref[...] = out

        @pl.when(jnp.logical_not(is_ctx))
        def _():
            o_lat_ref[...] = out


def _combine(pos, ys, x1, topw, mod_l, final_g=None):
    tm, d = CMB_TM, D_MODEL
    n_tiles = N_TOK // tm
    n_ctx = N_CTX_TOK // tm
    final = final_g is not None
    in_specs = [
        pl.BlockSpec((1, 1, tm * TOP_K), lambda i: (i, 0, 0), memory_space=pltpu.SMEM),
        pl.BlockSpec(memory_space=pl.ANY),
        pl.BlockSpec((tm, d), lambda i: (i, 0)),
        pl.BlockSpec((tm, TOP_K), lambda i: (i, 0)),
        pl.BlockSpec((1, 6, d), lambda i: (_mod_row(i, tm), 0, 0)),
    ]
    args = [pos.reshape(n_tiles, 1, tm * TOP_K), ys, x1, topw, mod_l]
    if final:
        in_specs.append(pl.BlockSpec((1, d), lambda i: (0, 0)))
        args.append(final_g.reshape(1, d))
        out_specs = [pl.BlockSpec((tm, d), lambda i: (jnp.minimum(i, n_ctx - 1), 0)),
                     pl.BlockSpec((tm, d), lambda i: (jnp.maximum(i - n_ctx, 0), 0))]
        out_shape = [jax.ShapeDtypeStruct((N_CTX_TOK, d), F32), jax.ShapeDtypeStruct((N_LAT_TOK, d), F32)]
    else:
        out_specs = pl.BlockSpec((tm, d), lambda i: (i, 0))
        out_shape = jax.ShapeDtypeStruct((N_TOK, d), F32)
    return pl.pallas_call(
        functools.partial(_combine_kernel, final),
        grid=(n_tiles,),
        in_specs=in_specs,
        out_specs=out_specs,
        out_shape=out_shape,
        scratch_shapes=[pltpu.VMEM((TOP_K, tm, d), F32), pltpu.SemaphoreType.DMA],
        compiler_params=_cparams(("arbitrary",), 32),
        name="combine_final" if final else "combine",
    )(*args)


def kernel(x_prompt, x_sample, cache_k, cache_v, state_ret_fwd, state_ret_bwd, c, c_ctx, norm1_g, norm2_g, w_mod, b_mod, w_in, lam_a, subln_a, ret_decay, ret_norm, w_branch_a, w_branch_b, w_branch_c, w_out, router_w, router_b, w_gate_up, b_gate_up, w_down, b_down, final_g):
    d = D_MODEL
    x = jnp.concatenate([x_prompt.reshape(N_CTX_TOK, d), x_sample.reshape(N_LAT_TOK, d)], axis=0)
    cond8 = jnp.concatenate([c_ctx[None, :], c, jnp.zeros((3, d), F32)], axis=0)
    mod = _modulation(cond8, w_mod, b_mod)

    cache_k4 = cache_k.reshape(DEC_BATCH, DEPTH, PAST_LEN, W_A)
    cache_v4 = cache_v.reshape(DEC_BATCH, DEPTH, PAST_LEN, W_A)
    s0f4 = state_ret_fwd.reshape(DEC_BATCH, DEPTH, H_C * DK_C, DV_C)
    s0b4 = state_ret_bwd.reshape(DEC_BATCH, DEPTH, H_C * DK_C, DV_C)
    b_gate_up4 = b_gate_up.reshape(DEPTH, N_EXPERTS, 1, 2 * D_FF)
    b_down4 = b_down.reshape(DEPTH, N_EXPERTS, 1, d)
    cos, sin = _rope_tables(DEC_SEQ)
    chan, pos_ctx = _dft_consts(SEQ)
    _, pos_lat = _dft_consts(DEC_SEQ)

    caches, states = None, None
    for l in range(DEPTH):
        lam_init = 0.8 - 0.6 * math.exp(-0.3 * l)
        z, new_k, new_v = _inproj(x, mod[l], norm1_g[l], w_in, l, caches)
        caches = (new_k, new_v)
        ya = _attn_ctx(z, lam_a[l], subln_a[l], lam_init)
        ya = _attn_lat(z, cache_k4, cache_v4, l, cos, sin, lam_a[l], subln_a[l], lam_init, ya)
        yb = _fourier(z, SEQ, BATCH, 0, chan, pos_ctx)
        yb = _fourier(z, DEC_SEQ, DEC_BATCH, N_CTX_TOK // DEC_SEQ, chan, pos_lat, yb)
        yc, new_sf, new_sb = _ret_ctx(z, ret_decay[l], ret_norm[l], l, states)
        states = (new_sf, new_sb)
        yc = _ret_lat(z, ret_decay[l], ret_norm[l], s0f4, s0b4, l, yc)
        x1, h2, topi, topw, rank, cnt_row, cnt_col = _merge(
            x, ya, yb, yc, z, mod[l], w_branch_a, w_branch_b, w_branch_c, w_out,
            norm2_g[l], router_w[l], router_b[l], l)
        pos, tile_expert, n_tiles, last, next_expert = _plan(topi, rank, cnt_row, cnt_col)
        xs = _dispatch(pos, last, h2)
        ys = _moe_grouped(tile_expert.reshape(PLAN_LANES), n_tiles[0, :1], next_expert.reshape(N_EXPERTS),
                          xs, w_gate_up, b_gate_up4, w_down, b_down4, l)
        if l < DEPTH - 1:
            x = _combine(pos, ys, x1, topw, mod[l])
        else:
            y_ctx, y_lat = _combine(pos, ys, x1, topw, mod[l], final_g)

    return (y_ctx.reshape(BATCH, SEQ, d), y_lat.reshape(DEC_BATCH, DEC_SEQ, d),
            caches[0].reshape(BATCH, DEPTH, SEQ, H_A, 2, DH_A),
            caches[1].reshape(BATCH, DEPTH, SEQ, H_A, DV_A),
            states[0], states[1])
```

```python
import functools
import math

import numpy as np
import jax
import jax.numpy as jnp
from jax import lax
from jax.experimental import pallas as pl
from jax.experimental.pallas import tpu as pltpu
from jax.experimental.pallas import tpu_sc as plsc

F32 = jnp.float32
BF16 = jnp.bfloat16

D_MODEL = 1024
BATCH = 16
SEQ = 256
DEPTH = 2
DEC_BATCH = 4
DEC_SEQ = 1024
PAST_LEN = 512
GRID_W = 64
H_A = 4
DH_A = 64
DV_A = 128
G_B = 4
DG_B = 128
H_C = 4
DK_C = 64
DV_C = 128
N_EXPERTS = 32
TOP_K = 4
D_FF = 1024
SWIGLU_LIMIT = 7.0
SWIGLU_ALPHA = 1.702
ROPE_BASE = 10000.0
EPS = 1e-6

N_CTX_TOK = BATCH * SEQ
N_LAT_TOK = DEC_BATCH * DEC_SEQ
N_TOK = N_CTX_TOK + N_LAT_TOK
IN_TOTAL = 6656
Z_GATES = 0
Z_QA = 3072
Z_KA = 3584
Z_VA = 4096
Z_FB = 4608
Z_QC = 5120
Z_KC = 5376
Z_VC = 5632
Z_GC = 6144
LANES = 128
MIB = 1024 * 1024


def _cparams(semantics, vmem_mib):
    return pltpu.CompilerParams(dimension_semantics=semantics,
                                vmem_limit_bytes=vmem_mib * MIB)


def _dot(a, b):
    return jnp.dot(a, b, preferred_element_type=F32)


def _dot_nt(a, b):
    return lax.dot_general(a, b, (((1,), (1,)), ((), ())), preferred_element_type=F32)


def _dot_tn(a, b):
    return lax.dot_general(a, b, (((0,), (0,)), ((), ())), preferred_element_type=F32)


def _split_bf16(a):
    hi = a.astype(BF16)
    lo = (a - hi.astype(F32)).astype(BF16)
    return hi, lo


def _dot_f32x3(a, b):
    a_hi, a_lo = _split_bf16(a)
    b_hi, b_lo = _split_bf16(b)
    return _dot(a_hi, b_hi) + _dot(a_lo, b_hi) + _dot(a_hi, b_lo)


def _rms(x):
    return x * lax.rsqrt(jnp.mean(x * x, axis=-1, keepdims=True) + EPS)


def _modnorm(x, g, scale, shift):
    return (_rms(x) * g) * (1.0 + scale) + shift


def _sigmoid(x):
    return 1.0 / (1.0 + jnp.exp(-x))


MOD_TN = 1536


def _mod_kernel(cond_ref, w_ref, b_ref, o_ref):
    cnd = cond_ref[...]
    s = cnd * _sigmoid(cnd)
    o_ref[0] = _dot_f32x3(s, w_ref[0]) + b_ref[0]


def _modulation(cond8, w_mod, b_mod):
    n = 6 * D_MODEL
    out = pl.pallas_call(
        _mod_kernel,
        grid=(DEPTH, n // MOD_TN),
        in_specs=[
            pl.BlockSpec((8, D_MODEL), lambda l, j: (0, 0)),
            pl.BlockSpec((1, D_MODEL, MOD_TN), lambda l, j: (l, 0, j)),
            pl.BlockSpec((1, 1, MOD_TN), lambda l, j: (l, 0, j)),
        ],
        out_specs=pl.BlockSpec((1, 8, MOD_TN), lambda l, j: (l, 0, j)),
        out_shape=jax.ShapeDtypeStruct((DEPTH, 8, n), F32),
        compiler_params=_cparams(("parallel", "parallel"), 40),
        name="modulation",
    )(cond8, w_mod, b_mod.reshape(DEPTH, 1, n))
    return out.reshape(DEPTH, 8, 6, D_MODEL)


def _mod_row(tile_idx, tm):
    n_ctx_tiles = N_CTX_TOK // tm
    per_batch = DEC_SEQ // tm
    return jnp.where(tile_idx < n_ctx_tiles, 0, 1 + (tile_idx - n_ctx_tiles) // per_batch)


INP_TM = 1024
INP_TN = 512
INP_NJ = IN_TOTAL // INP_TN
INP_GATE_BLOCKS = (3 * D_MODEL) // INP_TN
CTX_PER_TILE = INP_TM // SEQ


def _inproj_kernel(x_ref, mod_ref, g_ref, w_ref, *rest):
    z_ref, kc_ref, vc_ref, h_scr = rest[-4:]
    i = pl.program_id(0)
    j = pl.program_id(1)

    @pl.when(j == 0)
    def _():
        h = _modnorm(x_ref[...], g_ref[...], mod_ref[0, 1:2, :], mod_ref[0, 0:1, :])
        h_scr[...] = h.astype(BF16)

    acc = _dot(h_scr[...], w_ref[0].astype(BF16))
    z_ref[...] = acc.astype(BF16)
    is_ctx = i < N_CTX_TOK // INP_TM

    @pl.when(jnp.logical_and(is_ctx, j == Z_KA // INP_TN))
    def _():
        for b in range(CTX_PER_TILE):
            kc_ref[b, 0] = acc[b * SEQ:(b + 1) * SEQ, :]

    @pl.when(jnp.logical_and(is_ctx, j == Z_VA // INP_TN))
    def _():
        for b in range(CTX_PER_TILE):
            vc_ref[b, 0] = acc[b * SEQ:(b + 1) * SEQ, :]


def _inproj(x, mod_l, g, w_in, layer, prev_caches):
    ni = N_TOK // INP_TM
    n_ctx_tiles = N_CTX_TOK // INP_TM
    cache_shape = jax.ShapeDtypeStruct((BATCH, DEPTH, SEQ, W_A), F32)
    cache_spec = pl.BlockSpec((CTX_PER_TILE, 1, SEQ, W_A),
                              lambda i, j: (jnp.minimum(i, n_ctx_tiles - 1), layer, 0, 0))
    in_specs = [
        pl.BlockSpec((INP_TM, D_MODEL), lambda i, j: (i, 0)),
        pl.BlockSpec((1, 6, D_MODEL), lambda i, j: (_mod_row(i, INP_TM), 0, 0)),
        pl.BlockSpec((1, D_MODEL), lambda i, j: (0, 0)),
        pl.BlockSpec((1, D_MODEL, INP_TN),
                     lambda i, j: (layer, 0, (j + INP_NJ - INP_GATE_BLOCKS) % INP_NJ)),
    ]
    args = [x, mod_l, g.reshape(1, D_MODEL), w_in]
    aliases = {}
    if prev_caches is not None:
        in_specs += [pl.BlockSpec(memory_space=pl.ANY)] * 2
        args += list(prev_caches)
        aliases = {4: 1, 5: 2}
    return pl.pallas_call(
        _inproj_kernel,
        grid=(ni, INP_NJ),
        in_specs=in_specs,
        out_specs=[pl.BlockSpec((INP_TM, INP_TN), lambda i, j: (i, j)), cache_spec, cache_spec],
        out_shape=[jax.ShapeDtypeStruct((N_TOK, IN_TOTAL), BF16), cache_shape, cache_shape],
        scratch_shapes=[pltpu.VMEM((INP_TM, D_MODEL), BF16)],
        input_output_aliases=aliases,
        compiler_params=_cparams(("arbitrary", "arbitrary"), 40),
        name="inproj",
    )(*args)


ATT_TQ = 256
W_A = H_A * DV_A


def _lambda(lam_ref, lam_init):
    la = lam_ref[...]
    l01 = jnp.sum(la[0:1, :] * la[1:2, :], axis=1, keepdims=True)
    l23 = jnp.sum(la[2:3, :] * la[3:4, :], axis=1, keepdims=True)
    return jnp.exp(l01) - jnp.exp(l23) + lam_init


def _softmax_parts(s):
    m = jnp.max(s, axis=-1, keepdims=True)
    e = jnp.exp(s - m)
    return e, 1.0 / jnp.sum(e, axis=-1, keepdims=True)


def _diff_attn_block(q, k_ref, v_ref, lam, subln, lam_init):
    lane = lax.broadcasted_iota(jnp.int32, (1, LANES), 1)
    first = lane < DH_A
    scale = DH_A ** -0.5
    outs = []
    for h in range(H_A):
        sl = slice(h * LANES, (h + 1) * LANES)
        qh = q[:, sl]
        kh = k_ref[:, sl]
        zero = jnp.zeros_like(qh)
        s0 = _dot_nt(jnp.where(first, qh, zero), kh) * scale
        s1 = _dot_nt(jnp.where(first, zero, qh), kh) * scale
        e0, r0 = _softmax_parts(s0)
        e1, r1 = _softmax_parts(s1)
        a = e0 * r0 - (lam * r1) * e1
        o = _dot(a.astype(BF16), v_ref[:, sl])
        outs.append((_rms(o) * subln) * (1.0 - lam_init))
    return outs


def _rope(x, cos, sin):
    lane = lax.broadcasted_iota(jnp.int32, (1, x.shape[1]), 1)
    first = (lane % 32) < 16
    n = x.shape[1]
    xr = jnp.where(first, -pltpu.roll(x, n - 16, 1), pltpu.roll(x, 16, 1))
    return x * cos + xr * sin


def _attn_ctx_kernel(lam_init, q_ref, k_ref, v_ref, lam_ref, subln_ref, o_ref):
    lam = _lambda(lam_ref, lam_init)
    outs = _diff_attn_block(q_ref[...], k_ref, v_ref, lam, subln_ref[...], lam_init)
    for h in range(H_A):
        o_ref[:, h * LANES:(h + 1) * LANES] = outs[h].astype(BF16)


def _attn_ctx(z, lam_a_l, subln_l, lam_init):
    wb = W_A
    return pl.pallas_call(
        functools.partial(_attn_ctx_kernel, lam_init),
        grid=(BATCH,),
        in_specs=[
            pl.BlockSpec((SEQ, wb), lambda b: (b, Z_QA // wb)),
            pl.BlockSpec((SEQ, wb), lambda b: (b, Z_KA // wb)),
            pl.BlockSpec((SEQ, wb), lambda b: (b, Z_VA // wb)),
            pl.BlockSpec((4, DH_A), lambda b: (0, 0)),
            pl.BlockSpec((1, DV_A), lambda b: (0, 0)),
        ],
        out_specs=pl.BlockSpec((SEQ, wb), lambda b: (b, 0)),
        out_shape=jax.ShapeDtypeStruct((N_TOK, wb), BF16),
        compiler_params=_cparams(("parallel",), 40),
        name="attn_ctx",
    )(z, z, z, lam_a_l, subln_l.reshape(1, DV_A))


def _attn_lat_kernel(lam_init, q_ref, k_ref, v_ref, ck_ref, cv_ref, cos_ref, sin_ref,
                     lam_ref, subln_ref, ya_ctx_ref, o_ref, kall, vall):
    del ya_ctx_ref
    lam = _lambda(lam_ref, lam_init)
    subln = subln_ref[...]
    kall[0:DEC_SEQ, :] = _rope(k_ref[...].astype(F32), cos_ref[...], sin_ref[...]).astype(BF16)
    kall[DEC_SEQ:, :] = ck_ref[0, 0].astype(BF16)
    vall[0:DEC_SEQ, :] = v_ref[...]
    vall[DEC_SEQ:, :] = cv_ref[0, 0].astype(BF16)

    def q_block(qb, carry):
        rows = pl.ds(pl.multiple_of(qb * ATT_TQ, ATT_TQ), ATT_TQ)
        q = _rope(q_ref[rows, :].astype(F32), cos_ref[rows, :], sin_ref[rows, :]).astype(BF16)
        outs = _diff_attn_block(q, kall, vall, lam, subln, lam_init)
        for h in range(H_A):
            o_ref[rows, h * LANES:(h + 1) * LANES] = outs[h].astype(BF16)
        return carry

    lax.fori_loop(0, DEC_SEQ // ATT_TQ, q_block, 0)


def _attn_lat(z, cache_k4, cache_v4, layer, cos, sin, lam_a_l, subln_l, lam_init, ya_ctx):
    wb = W_A
    rb0 = N_CTX_TOK // DEC_SEQ
    t_all = DEC_SEQ + PAST_LEN
    return pl.pallas_call(
        functools.partial(_attn_lat_kernel, lam_init),
        grid=(DEC_BATCH,),
        in_specs=[
            pl.BlockSpec((DEC_SEQ, wb), lambda b: (rb0 + b, Z_QA // wb)),
            pl.BlockSpec((DEC_SEQ, wb), lambda b: (rb0 + b, Z_KA // wb)),
            pl.BlockSpec((DEC_SEQ, wb), lambda b: (rb0 + b, Z_VA // wb)),
            pl.BlockSpec((1, 1, PAST_LEN, wb), lambda b: (b, layer, 0, 0)),
            pl.BlockSpec((1, 1, PAST_LEN, wb), lambda b: (b, layer, 0, 0)),
            pl.BlockSpec((DEC_SEQ, wb), lambda b: (0, 0)),
            pl.BlockSpec((DEC_SEQ, wb), lambda b: (0, 0)),
            pl.BlockSpec((4, DH_A), lambda b: (0, 0)),
            pl.BlockSpec((1, DV_A), lambda b: (0, 0)),
            pl.BlockSpec(memory_space=pl.ANY),
        ],
        out_specs=pl.BlockSpec((DEC_SEQ, wb), lambda b: (rb0 + b, 0)),
        out_shape=jax.ShapeDtypeStruct((N_TOK, wb), BF16),
        scratch_shapes=[pltpu.VMEM((t_all, wb), BF16), pltpu.VMEM((t_all, wb), BF16)],
        input_output_aliases={9: 0},
        compiler_params=_cparams(("parallel",), 48),
        name="attn_lat",
    )(z, z, z, cache_k4, cache_v4, cos, sin, lam_a_l, subln_l.reshape(1, DV_A), ya_ctx)


def _rope_tables(s):
    rows = s // GRID_W
    row = jnp.repeat(jnp.arange(rows, dtype=F32), GRID_W)
    col = jnp.tile(jnp.arange(GRID_W, dtype=F32), rows)
    quarter = DH_A // 4
    inv = ROPE_BASE ** (-jnp.arange(quarter, dtype=F32) / quarter)
    ar = row[:, None] * inv
    ac = col[:, None] * inv
    ang = jnp.concatenate([ar, ar, ac, ac], axis=-1)
    reps = (H_A * 2 * DH_A) // DH_A
    return jnp.tile(jnp.cos(ang), (1, reps)), jnp.tile(jnp.sin(ang), (1, reps))


W_B = G_B * DG_B


def _dft_consts(s):
    def cs(n, scale):
        j = np.arange(n, dtype=np.int64)
        ang = 2.0 * np.pi * ((j[:, None] * j[None, :]) % n).astype(np.float64) / n
        return np.cos(ang) * scale, np.sin(ang) * scale
    cc, sc = cs(DG_B, DG_B ** -0.5)
    chan = np.concatenate([cc, sc], axis=1)
    cp, sp = cs(s, s ** -0.5)
    pos = np.concatenate([cp, -sp], axis=1)
    return jnp.asarray(chan, dtype=F32).astype(BF16), jnp.asarray(pos, dtype=F32).astype(BF16)


def _fourier_kernel(s, f_ref, chan_ref, pos_ref, *rest):
    o_ref, p_ref = rest[-2:]
    chan = chan_ref[...]
    for g in range(G_B):
        sl = slice(g * DG_B, (g + 1) * DG_B)
        a = _dot(f_ref[:, sl], chan)
        p_ref[0:s, sl] = a[:, :DG_B].astype(BF16)
        p_ref[s:2 * s, sl] = a[:, DG_B:].astype(BF16)
    o_ref[...] = _dot(pos_ref[...], p_ref[...]).astype(BF16)


def _fourier(z, s, n_batch, row_block0, chan, pos, prev=None):
    in_specs = [
        pl.BlockSpec((s, W_B), lambda b: (row_block0 + b, Z_FB // W_B)),
        pl.BlockSpec((DG_B, 2 * DG_B), lambda b: (0, 0)),
        pl.BlockSpec((s, 2 * s), lambda b: (0, 0)),
    ]
    args = [z, chan, pos]
    aliases = {}
    if prev is not None:
        in_specs.append(pl.BlockSpec(memory_space=pl.ANY))
        args.append(prev)
        aliases = {3: 0}
    return pl.pallas_call(
        functools.partial(_fourier_kernel, s),
        grid=(n_batch,),
        in_specs=in_specs,
        out_specs=pl.BlockSpec((s, W_B), lambda b: (row_block0 + b, 0)),
        out_shape=jax.ShapeDtypeStruct((N_TOK, W_B), BF16),
        scratch_shapes=[pltpu.VMEM((2 * s, W_B), BF16)],
        input_output_aliases=aliases,
        compiler_params=_cparams(("parallel",), 40),
        name="fourier_%d" % s,
    )(*args)


W_C = H_C * DV_C
RET_TQ = 256


def _log_sigmoid(x):
    return jnp.minimum(x, 0.0) - jnp.log1p(jnp.exp(-jnp.abs(x)))


def _head_pair_lanes(vals, pair):
    lane = lax.broadcasted_iota(jnp.int32, (1, LANES), 1)
    return jnp.where(lane < DK_C, vals[:, 2 * pair:2 * pair + 1], vals[:, 2 * pair + 1:2 * pair + 2])


def _retention_rows(s, row0, tq, q, k_ref, v_ref, g, lg, rnorm, s0f_ref, s0b_ref):
    lane = lax.broadcasted_iota(jnp.int32, (1, LANES), 1)
    ii = (row0 + lax.broadcasted_iota(jnp.int32, (tq, s), 0)).astype(F32)
    jj = lax.broadcasted_iota(jnp.int32, (tq, s), 1).astype(F32)
    rel = ii - jj
    pos = (row0 + lax.broadcasted_iota(jnp.int32, (tq, 1), 0)).astype(F32)
    outs = []
    for h in range(H_C):
        pair, half = h // 2, h % 2
        sl = slice(pair * LANES, (pair + 1) * LANES)
        lgf = lg[0:1, h:h + 1]
        lgb = lg[1:2, h:h + 1]
        in_head = (lane < DK_C) if half == 0 else (lane >= DK_C)
        q_pair = q[:, sl]
        qh = jnp.where(in_head, q_pair, jnp.zeros_like(q_pair))
        sc = _dot_nt(qh, k_ref[:, sl]) * (DK_C ** -0.5)
        decay = (jnp.where(rel >= 0, jnp.exp(lgf * jnp.maximum(rel, 0.0)), 0.0)
                 + jnp.where(rel <= 0, jnp.exp(lgb * jnp.maximum(-rel, 0.0)), 0.0))
        o = _dot((sc * decay).astype(BF16), v_ref[:, h * DV_C:(h + 1) * DV_C])
        if s0f_ref is not None:
            qf = qh.astype(F32)
            xf = jnp.exp(lgf * (pos + 1.0))
            xb = jnp.exp(lgb * (s - pos))
            o = o + _dot((qf * xf).astype(BF16), s0f_ref[0, 0, sl, :].astype(BF16))
            o = o + _dot((qf * xb).astype(BF16), s0b_ref[0, 0, sl, :].astype(BF16))
        gh = g[:, h * DV_C:(h + 1) * DV_C].astype(F32)
        outs.append((_rms(o) * rnorm) * (gh * _sigmoid(gh)))
    return outs


def _ret_ctx_kernel(q_ref, k_ref, v_ref, g_ref, dec_ref, rn_ref, *rest):
    o_ref, sf_ref, sb_ref = rest[-3:]
    s = SEQ
    lg = _log_sigmoid(dec_ref[...])
    outs = _retention_rows(s, 0, s, q_ref[...], k_ref, v_ref, g_ref[...], lg, rn_ref[...], None, None)
    for h in range(H_C):
        o_ref[:, h * DV_C:(h + 1) * DV_C] = outs[h].astype(BF16)
    pos = lax.broadcasted_iota(jnp.int32, (s, 1), 0).astype(F32)
    v = v_ref[...]
    for pair in range(H_C // 2):
        sl = slice(pair * LANES, (pair + 1) * LANES)
        kf = k_ref[:, sl].astype(F32) * (DK_C ** -0.5)
        zf = jnp.exp(_head_pair_lanes(lg[0:1, :], pair) * (s - 1.0 - pos))
        zb = jnp.exp(_head_pair_lanes(lg[1:2, :], pair) * pos)
        stf = _dot_tn((kf * zf).astype(BF16), v)
        stb = _dot_tn((kf * zb).astype(BF16), v)
        for half in range(2):
            h = 2 * pair + half
            rs = slice(half * DK_C, (half + 1) * DK_C)
            cs = slice(h * DV_C, (h + 1) * DV_C)
            sf_ref[0, 0, h] = stf[rs, cs]
            sb_ref[0, 0, h] = stb[rs, cs]


def _ret_ctx(z, dec_l, rnorm_l, layer, prev_states):
    wq = H_C * DK_C
    st_shape = jax.ShapeDtypeStruct((BATCH, DEPTH, H_C, DK_C, DV_C), F32)
    st_spec = pl.BlockSpec((1, 1, H_C, DK_C, DV_C), lambda b: (b, layer, 0, 0, 0))
    in_specs = [
        pl.BlockSpec((SEQ, wq), lambda b: (b, Z_QC // wq)),
        pl.BlockSpec((SEQ, wq), lambda b: (b, Z_KC // wq)),
        pl.BlockSpec((SEQ, W_C), lambda b: (b, Z_VC // W_C)),
        pl.BlockSpec((SEQ, W_C), lambda b: (b, Z_GC // W_C)),
        pl.BlockSpec((2, H_C), lambda b: (0, 0)),
        pl.BlockSpec((1, DV_C), lambda b: (0, 0)),
    ]
    args = [z, z, z, z, dec_l, rnorm_l.reshape(1, DV_C)]
    aliases = {}
    if prev_states is not None:
        in_specs += [pl.BlockSpec(memory_space=pl.ANY)] * 2
        args += list(prev_states)
        aliases = {6: 1, 7: 2}
    return pl.pallas_call(
        _ret_ctx_kernel,
        grid=(BATCH,),
        in_specs=in_specs,
        out_specs=[pl.BlockSpec((SEQ, W_C), lambda b: (b, 0)), st_spec, st_spec],
        out_shape=[jax.ShapeDtypeStruct((N_TOK, W_C), BF16), st_shape, st_shape],
        input_output_aliases=aliases,
        compiler_params=_cparams(("parallel",), 40),
        name="ret_ctx",
    )(*args)


def _ret_lat_kernel(q_ref, k_ref, v_ref, g_ref, dec_ref, rn_ref, s0f_ref, s0b_ref, yc_ctx_ref, o_ref):
    del yc_ctx_ref
    lg = _log_sigmoid(dec_ref[...])
    rnorm = rn_ref[...]

    def q_block(qb, carry):
        row0 = pl.multiple_of(qb * RET_TQ, RET_TQ)
        rows = pl.ds(row0, RET_TQ)
        outs = _retention_rows(DEC_SEQ, row0, RET_TQ, q_ref[rows, :], k_ref, v_ref, g_ref[rows, :],
                               lg, rnorm, s0f_ref, s0b_ref)
        for h in range(H_C):
            o_ref[rows, h * DV_C:(h + 1) * DV_C] = outs[h].astype(BF16)
        return carry

    lax.fori_loop(0, DEC_SEQ // RET_TQ, q_block, 0)


def _ret_lat(z, dec_l, rnorm_l, s0f4, s0b4, layer, yc_ctx):
    wq = H_C * DK_C
    rb0 = N_CTX_TOK // DEC_SEQ
    st_spec = pl.BlockSpec((1, 1, H_C * DK_C, DV_C), lambda b: (b, layer, 0, 0))
    return pl.pallas_call(
        _ret_lat_kernel,
        grid=(DEC_BATCH,),
        in_specs=[
            pl.BlockSpec((DEC_SEQ, wq), lambda b: (rb0 + b, Z_QC // wq)),
            pl.BlockSpec((DEC_SEQ, wq), lambda b: (rb0 + b, Z_KC // wq)),
            pl.BlockSpec((DEC_SEQ, W_C), lambda b: (rb0 + b, Z_VC // W_C)),
            pl.BlockSpec((DEC_SEQ, W_C), lambda b: (rb0 + b, Z_GC // W_C)),
            pl.BlockSpec((2, H_C), lambda b: (0, 0)),
            pl.BlockSpec((1, DV_C), lambda b: (0, 0)),
            st_spec, st_spec,
            pl.BlockSpec(memory_space=pl.ANY),
        ],
        out_specs=pl.BlockSpec((DEC_SEQ, W_C), lambda b: (rb0 + b, 0)),
        out_shape=jax.ShapeDtypeStruct((N_TOK, W_C), BF16),
        input_output_aliases={8: 0},
        compiler_params=_cparams(("parallel",), 40),
        name="ret_lat",
    )(z, z, z, z, dec_l, rnorm_l.reshape(1, DV_C), s0f4, s0b4, yc_ctx)


MRG_TM = 256


def _merge_kernel(x_ref, ya_ref, yb_ref, yc_ref, gt_ref, mod_ref, wa_ref, wb_ref, wc_ref, wo_ref,
                  n2_ref, rw_ref, rb_ref, x1_ref, h2_ref, topi_ref, topw_ref, rank_ref,
                  cnt_row_ref, cnt_col_ref, carry_row, carry_col, wa_bf, wb_bf, wc_bf, wo_bf):
    d = D_MODEL
    tm = x_ref.shape[0]

    @pl.when(pl.program_id(0) == 0)
    def _():
        carry_row[...] = jnp.zeros_like(carry_row)
        carry_col[...] = jnp.zeros_like(carry_col)
        wa_bf[...] = wa_ref[0].astype(BF16)
        wb_bf[...] = wb_ref[0].astype(BF16)
        wc_bf[...] = wc_ref[0].astype(BF16)
        wo_bf[...] = wo_ref[0].astype(BF16)

    merged = _sigmoid(gt_ref[:, 0:d].astype(F32)) * _dot(ya_ref[...], wa_bf[...])
    merged += _sigmoid(gt_ref[:, d:2 * d].astype(F32)) * _dot(yb_ref[...], wb_bf[...])
    merged += _sigmoid(gt_ref[:, 2 * d:3 * d].astype(F32)) * _dot(yc_ref[...], wc_bf[...])
    y = _dot(merged.astype(BF16), wo_bf[...])
    x1 = x_ref[...] + mod_ref[0, 2:3, :] * y
    x1_ref[...] = x1
    h2 = _modnorm(x1, n2_ref[...], mod_ref[0, 4:5, :], mod_ref[0, 3:4, :])
    h2_ref[...] = h2
    logits = _dot_f32x3(h2, rw_ref[...]) + rb_ref[...]
    ids = lax.broadcasted_iota(jnp.int32, logits.shape, 1)
    work = logits
    sels, vals, firsts = [], [], []
    for _ in range(TOP_K):
        m = jnp.max(work, axis=-1, keepdims=True)
        first = jnp.min(jnp.where(work == m, ids, N_EXPERTS), axis=-1, keepdims=True)
        sel = ids == first
        sels.append(sel)
        vals.append(m)
        firsts.append(first)
        work = jnp.where(sel, -jnp.inf, work)
    es = [jnp.exp(v - vals[0]) for v in vals]
    inv = 1.0 / (es[0] + es[1] + es[2] + es[3])
    chosen = jnp.zeros_like(logits)
    for sel in sels:
        chosen = chosen + jnp.where(sel, 1.0, 0.0)
    chosen_bf = chosen.astype(BF16)
    rr = lax.broadcasted_iota(jnp.int32, (tm, tm), 0)
    cc = lax.broadcasted_iota(jnp.int32, (tm, tm), 1)
    before = jnp.where(rr > cc, 1.0, 0.0).astype(BF16)
    prefix = _dot(before, chosen_bf) + carry_row[...]
    ranks = [jnp.sum(jnp.where(sel, prefix, 0.0), axis=-1, keepdims=True) for sel in sels]
    carry_row[...] += jnp.sum(chosen, axis=0, keepdims=True)
    carry_col[...] += _dot_tn(chosen_bf, jnp.ones((tm, LANES), BF16))
    cnt_row_ref[...] = carry_row[...]
    cnt_col_ref[...] = carry_col[...]
    topi_ref[...] = jnp.concatenate(firsts, axis=1)
    topw_ref[...] = jnp.concatenate([e * inv for e in es], axis=1)
    rank_ref[...] = jnp.concatenate(ranks, axis=1).astype(jnp.int32)


def _merge(x, ya, yb, yc, z, mod_l, wa, wb, wc, wo, n2, rw, rb, layer):
    tm = MRG_TM
    d = D_MODEL
    full = lambda shape: pl.BlockSpec(shape, lambda i: tuple(0 for _ in shape))
    tok = lambda w: pl.BlockSpec((tm, w), lambda i: (i, 0))
    per_layer = lambda rows: pl.BlockSpec((1, rows, d), lambda i: (layer, 0, 0))
    return pl.pallas_call(
        _merge_kernel,
        grid=(N_TOK // tm,),
        in_specs=[
            tok(d), tok(W_A), tok(W_B), tok(W_C), tok(3 * d),
            pl.BlockSpec((1, 6, d), lambda i: (_mod_row(i, tm), 0, 0)),
            per_layer(W_A), per_layer(W_B), per_layer(W_C), per_layer(d),
            full((1, d)), full((d, N_EXPERTS)), full((1, N_EXPERTS)),
        ],
        out_specs=[tok(d), tok(d), tok(TOP_K), tok(TOP_K), tok(TOP_K),
                   full((1, N_EXPERTS)), full((N_EXPERTS, LANES))],
        out_shape=[
            jax.ShapeDtypeStruct((N_TOK, d), F32),
            jax.ShapeDtypeStruct((N_TOK, d), F32),
            jax.ShapeDtypeStruct((N_TOK, TOP_K), jnp.int32),
            jax.ShapeDtypeStruct((N_TOK, TOP_K), F32),
            jax.ShapeDtypeStruct((N_TOK, TOP_K), jnp.int32),
            jax.ShapeDtypeStruct((1, N_EXPERTS), F32),
            jax.ShapeDtypeStruct((N_EXPERTS, LANES), F32),
        ],
        scratch_shapes=[pltpu.VMEM((1, N_EXPERTS), F32), pltpu.VMEM((N_EXPERTS, LANES), F32),
                        pltpu.VMEM((W_A, d), BF16), pltpu.VMEM((W_B, d), BF16),
                        pltpu.VMEM((W_C, d), BF16), pltpu.VMEM((d, d), BF16)],
        compiler_params=_cparams(("arbitrary",), 56),
        name="merge_router",
    )(x, ya, yb, yc, z, mod_l, wa, wb, wc, wo, n2.reshape(1, d), rw, rb.reshape(1, N_EXPERTS))


MOE_TM = 256
MOE_TILES = (N_TOK * TOP_K) // MOE_TM + N_EXPERTS
MOE_ROWS = MOE_TILES * MOE_TM
PLAN_LANES = 256
PLAN_TM = 1024


def _plan_kernel(topi_ref, rank_ref, cnt_row_ref, cnt_col_ref, pos_ref, te_ref, nt_ref, pad_ref,
                 next_ref):
    e_r = lax.broadcasted_iota(jnp.int32, (N_EXPERTS, N_EXPERTS), 0)
    e_c = lax.broadcasted_iota(jnp.int32, (N_EXPERTS, N_EXPERTS), 1)
    cnt_row = cnt_row_ref[...]
    tiles_row = jnp.ceil(cnt_row * (1.0 / MOE_TM))
    earlier = jnp.where(e_r < e_c, 1.0, 0.0).astype(BF16)
    start_tile_row = _dot(tiles_row.astype(BF16), earlier)
    off_row = start_tile_row * MOE_TM
    ids = lax.broadcasted_iota(jnp.int32, (PLAN_TM, N_EXPERTS), 1)
    cols = []
    for k in range(TOP_K):
        hit = ids == topi_ref[:, k:k + 1]
        cols.append(jnp.sum(jnp.where(hit, off_row, 0.0), axis=-1, keepdims=True))
    pos_ref[...] = jnp.concatenate(cols, axis=1).astype(jnp.int32) + rank_ref[...]
    tiles_col = jnp.ceil(cnt_col_ref[...] * (1.0 / MOE_TM))
    upto = jnp.where(e_c <= e_r, 1.0, 0.0).astype(BF16)
    end_col = _dot(upto, tiles_col.astype(BF16))
    end_b = jnp.concatenate([end_col] * (PLAN_LANES // LANES), axis=1)
    r = lax.broadcasted_iota(jnp.int32, (N_EXPERTS, PLAN_LANES), 1).astype(F32)
    total = end_b[N_EXPERTS - 1:N_EXPERTS, :]
    te = jnp.sum(jnp.where(end_b <= jnp.minimum(r, total - 1.0), 1.0, 0.0), axis=0, keepdims=True)
    te_ref[...] = te.astype(jnp.int32)
    nt_ref[...] = total[:, :LANES].astype(jnp.int32)
    end_row = end_b * MOE_TM
    first_pad = jnp.concatenate([(end_col - tiles_col) * MOE_TM + cnt_col_ref[...]] * (PLAN_LANES // LANES),
                                axis=1)
    cand = first_pad + r
    pad_ref[...] = jnp.where(cand < end_row, cand, MOE_ROWS - 1.0).astype(jnp.int32)
    later_used = jnp.logical_and(e_r > e_c, cnt_col_ref[:, 0:N_EXPERTS] > 0.0)
    nxt = jnp.min(jnp.where(later_used, e_r, N_EXPERTS), axis=0, keepdims=True)
    next_ref[...] = jnp.where(nxt == N_EXPERTS, -1, nxt)


def _plan(topi, rank, cnt_row, cnt_col):
    tok = pl.BlockSpec((PLAN_TM, TOP_K), lambda i: (i, 0))
    full = lambda shape: pl.BlockSpec(shape, lambda i: tuple(0 for _ in shape))
    return pl.pallas_call(
        _plan_kernel,
        grid=(N_TOK // PLAN_TM,),
        in_specs=[tok, tok, full((1, N_EXPERTS)), full((N_EXPERTS, LANES))],
        out_specs=[tok, full((1, PLAN_LANES)), full((1, LANES)), full((N_EXPERTS, PLAN_LANES)),
                   full((1, N_EXPERTS))],
        out_shape=[
            jax.ShapeDtypeStruct((N_TOK, TOP_K), jnp.int32),
            jax.ShapeDtypeStruct((1, PLAN_LANES), jnp.int32),
            jax.ShapeDtypeStruct((1, LANES), jnp.int32),
            jax.ShapeDtypeStruct((N_EXPERTS, PLAN_LANES), jnp.int32),
            jax.ShapeDtypeStruct((1, N_EXPERTS), jnp.int32),
        ],
        compiler_params=_cparams(("arbitrary",), 32),
        name="route_plan",
    )(topi, rank, cnt_row, cnt_col)


SC_WORKERS = 32
SC_CORES = 2
SC_LANES = 16
DSP_CHUNK = 32
DSP_TOK = N_TOK // SC_WORKERS
DSP_NCHUNK = DSP_TOK // DSP_CHUNK
DSP_PAD_ROWS = MOE_TM // DSP_CHUNK


def _dispatch_body(h_hbm, idx_hbm, pad_hbm, xs_hbm, idx_v, pad_v, rows_a, rows_b, sem_a, sem_b):
    wid = lax.axis_index("s") * SC_CORES + lax.axis_index("c")
    pltpu.sync_copy(idx_hbm.at[wid], idx_v)
    pltpu.sync_copy(pad_hbm.at[wid], pad_v)
    zero = jnp.zeros((SC_LANES,), F32)

    @pl.loop(0, DSP_CHUNK)
    def _(row):
        @pl.loop(0, D_MODEL, step=SC_LANES)
        def _(col):
            rows_a[row, pl.ds(col, SC_LANES)] = zero

    for j in range(DSP_PAD_ROWS):
        pltpu.async_copy(rows_a, xs_hbm.at[pad_v.at[j]], sem_a)
    for j in range(DSP_PAD_ROWS):
        pltpu.make_async_copy(rows_a, xs_hbm.at[pad_v.at[j]], sem_a).wait()

    bufs = (rows_a, rows_b)
    sems = (sem_a, sem_b)

    @pl.loop(0, DSP_NCHUNK, step=2)
    def _(c0):
        for b in range(2):
            chunk = c0 + b
            pltpu.sync_copy(h_hbm.at[pl.ds(wid * DSP_TOK + chunk * DSP_CHUNK, DSP_CHUNK)], bufs[b])
            for k in range(TOP_K):
                pltpu.async_copy(bufs[b], xs_hbm.at[idx_v.at[chunk * TOP_K + k]], sems[b])
        for b in range(2):
            chunk = c0 + b
            for k in range(TOP_K):
                pltpu.make_async_copy(bufs[b], xs_hbm.at[idx_v.at[chunk * TOP_K + k]], sems[b]).wait()


def _dispatch(pos, pad_rows, h2):
    idx = pos.reshape(SC_WORKERS, DSP_NCHUNK, DSP_CHUNK, TOP_K).transpose(0, 1, 3, 2)
    idx = idx.reshape(SC_WORKERS, DSP_NCHUNK * TOP_K, DSP_CHUNK)
    pad = pad_rows.reshape(N_EXPERTS, DSP_PAD_ROWS, DSP_CHUNK)
    mesh = plsc.VectorSubcoreMesh(core_axis_name="c", subcore_axis_name="s")
    call = pl.kernel(
        _dispatch_body,
        out_type=jax.ShapeDtypeStruct((MOE_ROWS, D_MODEL), F32),
        mesh=mesh,
        scratch_types=[
            pltpu.VMEM((DSP_NCHUNK * TOP_K, DSP_CHUNK), jnp.int32),
            pltpu.VMEM((DSP_PAD_ROWS, DSP_CHUNK), jnp.int32),
            pltpu.VMEM((DSP_CHUNK, D_MODEL), F32),
            pltpu.VMEM((DSP_CHUNK, D_MODEL), F32),
            pltpu.SemaphoreType.DMA,
            pltpu.SemaphoreType.DMA,
        ],
        name="dispatch_sc",
    )
    return call(h2, idx, pad)


PAIR_BLOCKS = (2 * D_FF) // (2 * LANES)
WGU_CHUNKS = 4
WDN_CHUNKS = 2


def _prep_expert_weights(wgu_ref, wdn_ref, wgu_bf, wdn_perm, wdn_bf):
    wgu_bf[...] = wgu_ref[...].astype(BF16)
    half = LANES // 2
    for cb in range(D_MODEL // LANES):
        cols = slice(cb * LANES, (cb + 1) * LANES)
        for m in range(PAIR_BLOCKS):
            base = m * LANES
            wdn_perm[cb, pl.ds(base, half, stride=2), :] = wdn_ref[base:base + half, cols]
            wdn_perm[cb, pl.ds(base + 1, half, stride=2), :] = wdn_ref[base + half:base + LANES, cols]
        wdn_bf[:, cols] = wdn_perm[cb].astype(BF16)


def _ffn_rows(x, wgu_bf, bgu, wdn_bf, bdn):
    gu = _dot(x, wgu_bf[...]) + bgu
    lane = lax.broadcasted_iota(jnp.int32, (1, LANES), 1)
    even = (lane % 2) == 0

    def act(block):
        t = gu[:, block * LANES:(block + 1) * LANES]
        glu = jnp.minimum(t, SWIGLU_LIMIT)
        glu = glu * _sigmoid(SWIGLU_ALPHA * glu)
        lin = jnp.clip(t, -SWIGLU_LIMIT, SWIGLU_LIMIT) + 1.0
        return glu * pltpu.roll(lin, LANES - 1, 1)

    parts = []
    for m in range(PAIR_BLOCKS):
        parts.append(jnp.where(even, act(2 * m), pltpu.roll(act(2 * m + 1), 1, 1)))
    hidden = jnp.concatenate(parts, axis=1).astype(BF16)
    return _dot(hidden, wdn_bf[...]) + bdn


def _moe_kernel(layer, te_ref, nt_ref, next_ref, x_ref, wgu_hbm, bgu_ref, wdn_hbm, bdn_ref, o_ref,
                wgu_land, wdn_land, wgu_bf, wdn_perm, wdn_bf, sems):
    r = pl.program_id(0)
    valid = r < nt_ref[0]
    e = te_ref[r]
    new_expert = jnp.logical_or(r == 0, e != te_ref[jnp.maximum(r - 1, 0)])

    def fetch(expert):
        copies = []
        for c in range(WGU_CHUNKS):
            rows = pl.ds(c * (D_MODEL // WGU_CHUNKS), D_MODEL // WGU_CHUNKS)
            copies.append(pltpu.make_async_copy(wgu_hbm.at[layer, expert, rows], wgu_land.at[rows], sems.at[c]))
        for c in range(WDN_CHUNKS):
            rows = pl.ds(c * (D_FF // WDN_CHUNKS), D_FF // WDN_CHUNKS)
            copies.append(pltpu.make_async_copy(wdn_hbm.at[layer, expert, rows], wdn_land.at[rows],
                                                sems.at[WGU_CHUNKS + c]))
        return copies

    @pl.when(r == 0)
    def _():
        for cp in fetch(e):
            cp.start()

    @pl.when(jnp.logical_and(valid, new_expert))
    def _():
        for cp in fetch(e):
            cp.wait()
        _prep_expert_weights(wgu_land, wdn_land, wgu_bf, wdn_perm, wdn_bf)
        nxt = next_ref[e]

        @pl.when(nxt >= 0)
        def _():
            for cp in fetch(nxt):
                cp.start()

    @pl.when(valid)
    def _():
        o_ref[...] = _ffn_rows(x_ref[...].astype(BF16), wgu_bf, bgu_ref[0, 0], wdn_bf, bdn_ref[0, 0])


def _moe_grouped(tile_expert, n_tiles, next_expert, xs, w_gate_up, b_gate_up4, w_down, b_down4, layer):
    d, f2 = D_MODEL, 2 * D_FF
    row_tile = lambda r, te, nt, nx: (jnp.minimum(r, nt[0] - 1), 0)
    expert = lambda r, te, nt, nx: (layer, te[r], 0, 0)
    grid_spec = pltpu.PrefetchScalarGridSpec(
        num_scalar_prefetch=3,
        grid=(MOE_TILES,),
        in_specs=[
            pl.BlockSpec((MOE_TM, d), row_tile),
            pl.BlockSpec(memory_space=pl.ANY),
            pl.BlockSpec((1, 1, 1, f2), expert),
            pl.BlockSpec(memory_space=pl.ANY),
            pl.BlockSpec((1, 1, 1, d), expert),
        ],
        out_specs=pl.BlockSpec((MOE_TM, d), row_tile),
        scratch_shapes=[
            pltpu.VMEM((d, f2), F32),
            pltpu.VMEM((D_FF, d), F32),
            pltpu.VMEM((d, f2), BF16),
            pltpu.VMEM((d // LANES, D_FF, LANES), F32),
            pltpu.VMEM((D_FF, d), BF16),
            pltpu.SemaphoreType.DMA((WGU_CHUNKS + WDN_CHUNKS,)),
        ],
    )
    return pl.pallas_call(
        functools.partial(_moe_kernel, layer),
        grid_spec=grid_spec,
        out_shape=jax.ShapeDtypeStruct((MOE_ROWS, d), F32),
        compiler_params=_cparams(("arbitrary",), 56),
        name="moe_grouped",
    )(tile_expert, n_tiles, next_expert, xs, w_gate_up, b_gate_up4, w_down, b_down4)


CMB_TM = 256


def _combine_kernel(final, pos_ref, ys_ref, x1_ref, topw_ref, mod_ref, *rest):
    buf, sem = rest[-2:]

    def row_copy(t, k):
        return pltpu.make_async_copy(ys_ref.at[pl.ds(pos_ref[0, 0, t * TOP_K + k], 1)],
                                     buf.at[k, pl.ds(t, 1)], sem)

    def issue(t, carry):
        for k in range(TOP_K):
            row_copy(t, k).start()
        return carry

    lax.fori_loop(0, CMB_TM, issue, 0, unroll=8)
    for k in range(TOP_K):
        pltpu.make_async_copy(ys_ref.at[pl.ds(0, CMB_TM)], buf.at[k], sem).wait()
    y = topw_ref[:, 0:1] * buf[0]
    for k in range(1, TOP_K):
        y = y + topw_ref[:, k:k + 1] * buf[k]
    x = x1_ref[...] + mod_ref[0, 5:6, :] * y
    if not final:
        rest[0][...] = x
    else:
        fg_ref, o_ctx_ref, o_lat_ref = rest[:3]
        out = _rms(x) * fg_ref[...]
        is_ctx = pl.program_id(0) < N_CTX_TOK // CMB_TM

        @pl.when(is_ctx)
        def _():
            o_ctx_ref[...] = out

        @pl.when(jnp.logical_not(is_ctx))
        def _():
            o_lat_ref[...] = out


def _combine(pos, ys, x1, topw, mod_l, final_g=None):
    tm, d = CMB_TM, D_MODEL
    n_tiles = N_TOK // tm
    n_ctx = N_CTX_TOK // tm
    final = final_g is not None
    in_specs = [
        pl.BlockSpec((1, 1, tm * TOP_K), lambda i: (i, 0, 0), memory_space=pltpu.SMEM),
        pl.BlockSpec(memory_space=pl.ANY),
        pl.BlockSpec((tm, d), lambda i: (i, 0)),
        pl.BlockSpec((tm, TOP_K), lambda i: (i, 0)),
        pl.BlockSpec((1, 6, d), lambda i: (_mod_row(i, tm), 0, 0)),
    ]
    args = [pos.reshape(n_tiles, 1, tm * TOP_K), ys, x1, topw, mod_l]
    if final:
        in_specs.append(pl.BlockSpec((1, d), lambda i: (0, 0)))
        args.append(final_g.reshape(1, d))
        out_specs = [pl.BlockSpec((tm, d), lambda i: (jnp.minimum(i, n_ctx - 1), 0)),
                     pl.BlockSpec((tm, d), lambda i: (jnp.maximum(i - n_ctx, 0), 0))]
        out_shape = [jax.ShapeDtypeStruct((N_CTX_TOK, d), F32), jax.ShapeDtypeStruct((N_LAT_TOK, d), F32)]
    else:
        out_specs = pl.BlockSpec((tm, d), lambda i: (i, 0))
        out_shape = jax.ShapeDtypeStruct((N_TOK, d), F32)
    return pl.pallas_call(
        functools.partial(_combine_kernel, final),
        grid=(n_tiles,),
        in_specs=in_specs,
        out_specs=out_specs,
        out_shape=out_shape,
        scratch_shapes=[pltpu.VMEM((TOP_K, tm, d), F32), pltpu.SemaphoreType.DMA],
        compiler_params=_cparams(("arbitrary",), 32),
        name="combine_final" if final else "combine",
    )(*args)


def kernel(x_prompt, x_sample, cache_k, cache_v, state_ret_fwd, state_ret_bwd, c, c_ctx, norm1_g, norm2_g, w_mod, b_mod, w_in, lam_a, subln_a, ret_decay, ret_norm, w_branch_a, w_branch_b, w_branch_c, w_out, router_w, router_b, w_gate_up, b_gate_up, w_down, b_down, final_g):
    d = D_MODEL
    x = jnp.concatenate([x_prompt.reshape(N_CTX_TOK, d), x_sample.reshape(N_LAT_TOK, d)], axis=0)
    cond8 = jnp.concatenate([c_ctx[None, :], c, jnp.zeros((3, d), F32)], axis=0)
    mod = _modulation(cond8, w_mod, b_mod)

    cache_k4 = cache_k.reshape(DEC_BATCH, DEPTH, PAST_LEN, W_A)
    cache_v4 = cache_v.reshape(DEC_BATCH, DEPTH, PAST_LEN, W_A)
    s0f4 = state_ret_fwd.reshape(DEC_BATCH, DEPTH, H_C * DK_C, DV_C)
    s0b4 = state_ret_bwd.reshape(DEC_BATCH, DEPTH, H_C * DK_C, DV_C)
    b_gate_up4 = b_gate_up.reshape(DEPTH, N_EXPERTS, 1, 2 * D_FF)
    b_down4 = b_down.reshape(DEPTH, N_EXPERTS, 1, d)
    cos, sin = _rope_tables(DEC_SEQ)
    chan, pos_ctx = _dft_consts(SEQ)
    _, pos_lat = _dft_consts(DEC_SEQ)

    caches, states = None, None
    for l in range(DEPTH):
        lam_init = 0.8 - 0.6 * math.exp(-0.3 * l)
        z, new_k, new_v = _inproj(x, mod[l], norm1_g[l], w_in, l, caches)
        caches = (new_k, new_v)
        ya = _attn_ctx(z, lam_a[l], subln_a[l], lam_init)
        ya = _attn_lat(z, cache_k4, cache_v4, l, cos, sin, lam_a[l], subln_a[l], lam_init, ya)
        yb = _fourier(z, SEQ, BATCH, 0, chan, pos_ctx)
        yb = _fourier(z, DEC_SEQ, DEC_BATCH, N_CTX_TOK // DEC_SEQ, chan, pos_lat, yb)
        yc, new_sf, new_sb = _ret_ctx(z, ret_decay[l], ret_norm[l], l, states)
        states = (new_sf, new_sb)
        yc = _ret_lat(z, ret_decay[l], ret_norm[l], s0f4, s0b4, l, yc)
        x1, h2, topi, topw, rank, cnt_row, cnt_col = _merge(
            x, ya, yb, yc, z, mod[l], w_branch_a, w_branch_b, w_branch_c, w_out,
            norm2_g[l], router_w[l], router_b[l], l)
        pos, tile_expert, n_tiles, pad_rows, next_expert = _plan(topi, rank, cnt_row, cnt_col)
        xs = _dispatch(pos, pad_rows, h2)
        ys = _moe_grouped(tile_expert.reshape(PLAN_LANES), n_tiles[0, :1], next_expert.reshape(N_EXPERTS),
                          xs, w_gate_up, b_gate_up4, w_down, b_down4, l)
        if l < DEPTH - 1:
            x = _combine(pos, ys, x1, topw, mod[l])
        else:
            y_ctx, y_lat = _combine(pos, ys, x1, topw, mod[l], final_g)

    return (y_ctx.reshape(BATCH, SEQ, d), y_lat.reshape(DEC_BATCH, DEC_SEQ, d),
            caches[0].reshape(BATCH, DEPTH, SEQ, H_A, 2, DH_A),
            caches[1].reshape(BATCH, DEPTH, SEQ, H_A, DV_A),
            states[0], states[1])
```

```python
import functools
import math

import numpy as np
import jax
import jax.numpy as jnp
from jax import lax
from jax.experimental import pallas as pl
from jax.experimental.pallas import tpu as pltpu
from jax.experimental.pallas import tpu_sc as plsc

F32 = jnp.float32
BF16 = jnp.bfloat16

D_MODEL = 1024
BATCH = 16
SEQ = 256
DEPTH = 2
DEC_BATCH = 4
DEC_SEQ = 1024
PAST_LEN = 512
GRID_W = 64
H_A = 4
DH_A = 64
DV_A = 128
G_B = 4
DG_B = 128
H_C = 4
DK_C = 64
DV_C = 128
N_EXPERTS = 32
TOP_K = 4
D_FF = 1024
SWIGLU_LIMIT = 7.0
SWIGLU_ALPHA = 1.702
ROPE_BASE = 10000.0
EPS = 1e-6

N_TOK = BATCH * SEQ
assert N_TOK == DEC_BATCH * DEC_SEQ
IN_TOTAL = 6656
Z_GATES = 0
Z_QA = 3072
Z_KA = 3584
Z_VA = 4096
Z_FB = 4608
Z_QC = 5120
Z_KC = 5376
Z_VC = 5632
Z_GC = 6144
LANES = 128
MIB = 1024 * 1024


def _cparams(semantics, vmem_mib):
    return pltpu.CompilerParams(dimension_semantics=semantics,
                                vmem_limit_bytes=vmem_mib * MIB)


def _dot(a, b):
    return jnp.dot(a, b, preferred_element_type=F32)


def _dot_nt(a, b):
    return lax.dot_general(a, b, (((1,), (1,)), ((), ())), preferred_element_type=F32)


def _dot_tn(a, b):
    return lax.dot_general(a, b, (((0,), (0,)), ((), ())), preferred_element_type=F32)


def _split_bf16(a):
    hi = a.astype(BF16)
    lo = (a - hi.astype(F32)).astype(BF16)
    return hi, lo


def _dot_f32x3(a, b):
    a_hi, a_lo = _split_bf16(a)
    b_hi, b_lo = _split_bf16(b)
    return _dot(a_hi, b_hi) + _dot(a_lo, b_hi) + _dot(a_hi, b_lo)


def _rms(x):
    return x * lax.rsqrt(jnp.mean(x * x, axis=-1, keepdims=True) + EPS)


def _modnorm(x, g, scale, shift):
    return (_rms(x) * g) * (1.0 + scale) + shift


def _sigmoid(x):
    return 1.0 / (1.0 + jnp.exp(-x))


MOD_TN = 1536


def _mod_kernel(cond_ref, w_ref, b_ref, o_ref):
    cnd = cond_ref[...]
    s = cnd * _sigmoid(cnd)
    o_ref[0] = _dot_f32x3(s, w_ref[0]) + b_ref[0]


def _modulation(cond8, w_mod, b_mod):
    n = 6 * D_MODEL
    out = pl.pallas_call(
        _mod_kernel,
        grid=(DEPTH, n // MOD_TN),
        in_specs=[
            pl.BlockSpec((8, D_MODEL), lambda l, j: (0, 0)),
            pl.BlockSpec((1, D_MODEL, MOD_TN), lambda l, j: (l, 0, j)),
            pl.BlockSpec((1, 1, MOD_TN), lambda l, j: (l, 0, j)),
        ],
        out_specs=pl.BlockSpec((1, 8, MOD_TN), lambda l, j: (l, 0, j)),
        out_shape=jax.ShapeDtypeStruct((DEPTH, 8, n), F32),
        compiler_params=_cparams(("parallel", "parallel"), 40),
        name="modulation",
    )(cond8, w_mod, b_mod.reshape(DEPTH, 1, n))
    return out.reshape(DEPTH, 8, 6, D_MODEL)


def _mod_row(tile_idx, tm, latent):
    if not latent:
        return 0
    return 1 + tile_idx // (DEC_SEQ // tm)


INP_TM = 1024
INP_TN = 512
INP_NJ = IN_TOTAL // INP_TN
INP_GATE_BLOCKS = (3 * D_MODEL) // INP_TN
CTX_PER_TILE = INP_TM // SEQ


def _inproj_kernel(latent, x_ref, mod_ref, g_ref, w_ref, *rest):
    h_scr = rest[-1]
    z_ref = rest[-2] if latent else rest[-4]
    j = pl.program_id(1)

    @pl.when(j == 0)
    def _():
        h = _modnorm(x_ref[...], g_ref[...], mod_ref[0, 1:2, :], mod_ref[0, 0:1, :])
        h_scr[...] = h.astype(BF16)

    acc = _dot(h_scr[...], w_ref[0].astype(BF16))
    z_ref[...] = acc.astype(BF16)
    if latent:
        return
    kc_ref, vc_ref = rest[-3:-1]

    @pl.when(j == Z_KA // INP_TN)
    def _():
        for b in range(CTX_PER_TILE):
            kc_ref[b, 0] = acc[b * SEQ:(b + 1) * SEQ, :]

    @pl.when(j == Z_VA // INP_TN)
    def _():
        for b in range(CTX_PER_TILE):
            vc_ref[b, 0] = acc[b * SEQ:(b + 1) * SEQ, :]


def _inproj(x, mod_l, g, w_in, layer, latent, prev_caches=None):
    in_specs = [
        pl.BlockSpec((INP_TM, D_MODEL), lambda i, j: (i, 0)),
        pl.BlockSpec((1, 6, D_MODEL), lambda i, j: (_mod_row(i, INP_TM, latent), 0, 0)),
        pl.BlockSpec((1, D_MODEL), lambda i, j: (0, 0)),
        pl.BlockSpec((1, D_MODEL, INP_TN),
                     lambda i, j: (layer, 0, (j + INP_NJ - INP_GATE_BLOCKS) % INP_NJ)),
    ]
    args = [x, mod_l, g.reshape(1, D_MODEL), w_in]
    out_specs = [pl.BlockSpec((INP_TM, INP_TN), lambda i, j: (i, j))]
    out_shape = [jax.ShapeDtypeStruct((N_TOK, IN_TOTAL), BF16)]
    aliases = {}
    if not latent:
        cache_spec = pl.BlockSpec((CTX_PER_TILE, 1, SEQ, W_A), lambda i, j: (i, layer, 0, 0))
        out_specs += [cache_spec, cache_spec]
        out_shape += [jax.ShapeDtypeStruct((BATCH, DEPTH, SEQ, W_A), F32)] * 2
        if prev_caches is not None:
            in_specs += [pl.BlockSpec(memory_space=pl.ANY)] * 2
            args += list(prev_caches)
            aliases = {4: 1, 5: 2}
    return pl.pallas_call(
        functools.partial(_inproj_kernel, latent),
        grid=(N_TOK // INP_TM, INP_NJ),
        in_specs=in_specs,
        out_specs=out_specs,
        out_shape=out_shape,
        scratch_shapes=[pltpu.VMEM((INP_TM, D_MODEL), BF16)],
        input_output_aliases=aliases,
        compiler_params=_cparams(("parallel", "arbitrary"), 40),
        name="inproj_lat" if latent else "inproj_ctx",
    )(*args)


ATT_TQ = 256
W_A = H_A * DV_A


def _lambda(lam_ref, lam_init):
    la = lam_ref[...]
    l01 = jnp.sum(la[0:1, :] * la[1:2, :], axis=1, keepdims=True)
    l23 = jnp.sum(la[2:3, :] * la[3:4, :], axis=1, keepdims=True)
    return jnp.exp(l01) - jnp.exp(l23) + lam_init


def _softmax_parts(s):
    m = jnp.max(s, axis=-1, keepdims=True)
    e = jnp.exp(s - m)
    return e, 1.0 / jnp.sum(e, axis=-1, keepdims=True)


def _diff_attn_block(q, k_ref, v_ref, lam, subln, lam_init):
    lane = lax.broadcasted_iota(jnp.int32, (1, LANES), 1)
    first = lane < DH_A
    scale = DH_A ** -0.5
    outs = []
    for h in range(H_A):
        sl = slice(h * LANES, (h + 1) * LANES)
        qh = q[:, sl]
        kh = k_ref[:, sl]
        zero = jnp.zeros_like(qh)
        s0 = _dot_nt(jnp.where(first, qh, zero), kh) * scale
        s1 = _dot_nt(jnp.where(first, zero, qh), kh) * scale
        e0, r0 = _softmax_parts(s0)
        e1, r1 = _softmax_parts(s1)
        a = e0 * r0 - (lam * r1) * e1
        o = _dot(a.astype(BF16), v_ref[:, sl])
        outs.append((_rms(o) * subln) * (1.0 - lam_init))
    return outs


def _rope(x, cos, sin):
    lane = lax.broadcasted_iota(jnp.int32, (1, x.shape[1]), 1)
    first = (lane % 32) < 16
    n = x.shape[1]
    xr = jnp.where(first, -pltpu.roll(x, n - 16, 1), pltpu.roll(x, 16, 1))
    return x * cos + xr * sin


def _attn_ctx_kernel(lam_init, q_ref, k_ref, v_ref, lam_ref, subln_ref, o_ref):
    lam = _lambda(lam_ref, lam_init)
    outs = _diff_attn_block(q_ref[...], k_ref, v_ref, lam, subln_ref[...], lam_init)
    for h in range(H_A):
        o_ref[:, h * LANES:(h + 1) * LANES] = outs[h].astype(BF16)


def _attn_ctx(z, lam_a_l, subln_l, lam_init):
    wb = W_A
    return pl.pallas_call(
        functools.partial(_attn_ctx_kernel, lam_init),
        grid=(BATCH,),
        in_specs=[
            pl.BlockSpec((SEQ, wb), lambda b: (b, Z_QA // wb)),
            pl.BlockSpec((SEQ, wb), lambda b: (b, Z_KA // wb)),
            pl.BlockSpec((SEQ, wb), lambda b: (b, Z_VA // wb)),
            pl.BlockSpec((4, DH_A), lambda b: (0, 0)),
            pl.BlockSpec((1, DV_A), lambda b: (0, 0)),
        ],
        out_specs=pl.BlockSpec((SEQ, wb), lambda b: (b, 0)),
        out_shape=jax.ShapeDtypeStruct((N_TOK, wb), BF16),
        compiler_params=_cparams(("parallel",), 40),
        name="attn_ctx",
    )(z, z, z, lam_a_l, subln_l.reshape(1, DV_A))


def _attn_lat_kernel(lam_init, q_ref, k_ref, v_ref, ck_ref, cv_ref, cos_ref, sin_ref,
                     lam_ref, subln_ref, o_ref, kall, vall):
    lam = _lambda(lam_ref, lam_init)
    subln = subln_ref[...]
    kall[0:DEC_SEQ, :] = _rope(k_ref[...].astype(F32), cos_ref[...], sin_ref[...]).astype(BF16)
    kall[DEC_SEQ:, :] = ck_ref[0, 0].astype(BF16)
    vall[0:DEC_SEQ, :] = v_ref[...]
    vall[DEC_SEQ:, :] = cv_ref[0, 0].astype(BF16)

    def q_block(qb, carry):
        rows = pl.ds(pl.multiple_of(qb * ATT_TQ, ATT_TQ), ATT_TQ)
        q = _rope(q_ref[rows, :].astype(F32), cos_ref[rows, :], sin_ref[rows, :]).astype(BF16)
        outs = _diff_attn_block(q, kall, vall, lam, subln, lam_init)
        for h in range(H_A):
            o_ref[rows, h * LANES:(h + 1) * LANES] = outs[h].astype(BF16)
        return carry

    lax.fori_loop(0, DEC_SEQ // ATT_TQ, q_block, 0)


def _attn_lat(z, cache_k4, cache_v4, layer, cos, sin, lam_a_l, subln_l, lam_init):
    wb = W_A
    t_all = DEC_SEQ + PAST_LEN
    return pl.pallas_call(
        functools.partial(_attn_lat_kernel, lam_init),
        grid=(DEC_BATCH,),
        in_specs=[
            pl.BlockSpec((DEC_SEQ, wb), lambda b: (b, Z_QA // wb)),
            pl.BlockSpec((DEC_SEQ, wb), lambda b: (b, Z_KA // wb)),
            pl.BlockSpec((DEC_SEQ, wb), lambda b: (b, Z_VA // wb)),
            pl.BlockSpec((1, 1, PAST_LEN, wb), lambda b: (b, layer, 0, 0)),
            pl.BlockSpec((1, 1, PAST_LEN, wb), lambda b: (b, layer, 0, 0)),
            pl.BlockSpec((DEC_SEQ, wb), lambda b: (0, 0)),
            pl.BlockSpec((DEC_SEQ, wb), lambda b: (0, 0)),
            pl.BlockSpec((4, DH_A), lambda b: (0, 0)),
            pl.BlockSpec((1, DV_A), lambda b: (0, 0)),
        ],
        out_specs=pl.BlockSpec((DEC_SEQ, wb), lambda b: (b, 0)),
        out_shape=jax.ShapeDtypeStruct((N_TOK, wb), BF16),
        scratch_shapes=[pltpu.VMEM((t_all, wb), BF16), pltpu.VMEM((t_all, wb), BF16)],
        compiler_params=_cparams(("parallel",), 48),
        name="attn_lat",
    )(z, z, z, cache_k4, cache_v4, cos, sin, lam_a_l, subln_l.reshape(1, DV_A))


def _rope_tables(s):
    rows = s // GRID_W
    row = jnp.repeat(jnp.arange(rows, dtype=F32), GRID_W)
    col = jnp.tile(jnp.arange(GRID_W, dtype=F32), rows)
    quarter = DH_A // 4
    inv = ROPE_BASE ** (-jnp.arange(quarter, dtype=F32) / quarter)
    ar = row[:, None] * inv
    ac = col[:, None] * inv
    ang = jnp.concatenate([ar, ar, ac, ac], axis=-1)
    reps = (H_A * 2 * DH_A) // DH_A
    return jnp.tile(jnp.cos(ang), (1, reps)), jnp.tile(jnp.sin(ang), (1, reps))


W_B = G_B * DG_B


def _dft_consts(s):
    def cs(n, scale):
        j = np.arange(n, dtype=np.int64)
        ang = 2.0 * np.pi * ((j[:, None] * j[None, :]) % n).astype(np.float64) / n
        return np.cos(ang) * scale, np.sin(ang) * scale
    cc, sc = cs(DG_B, DG_B ** -0.5)
    chan = np.concatenate([cc, sc], axis=1)
    cp, sp = cs(s, s ** -0.5)
    pos = np.concatenate([cp, -sp], axis=1)
    return jnp.asarray(chan, dtype=F32).astype(BF16), jnp.asarray(pos, dtype=F32).astype(BF16)


def _fourier_kernel(s, f_ref, chan_ref, pos_ref, o_ref, p_ref):
    chan = chan_ref[...]
    for g in range(G_B):
        sl = slice(g * DG_B, (g + 1) * DG_B)
        a = _dot(f_ref[:, sl], chan)
        p_ref[0:s, sl] = a[:, :DG_B].astype(BF16)
        p_ref[s:2 * s, sl] = a[:, DG_B:].astype(BF16)
    o_ref[...] = _dot(pos_ref[...], p_ref[...]).astype(BF16)


def _fourier(z, s, chan, pos):
    return pl.pallas_call(
        functools.partial(_fourier_kernel, s),
        grid=(N_TOK // s,),
        in_specs=[
            pl.BlockSpec((s, W_B), lambda b: (b, Z_FB // W_B)),
            pl.BlockSpec((DG_B, 2 * DG_B), lambda b: (0, 0)),
            pl.BlockSpec((s, 2 * s), lambda b: (0, 0)),
        ],
        out_specs=pl.BlockSpec((s, W_B), lambda b: (b, 0)),
        out_shape=jax.ShapeDtypeStruct((N_TOK, W_B), BF16),
        scratch_shapes=[pltpu.VMEM((2 * s, W_B), BF16)],
        compiler_params=_cparams(("parallel",), 40),
        name="fourier_%d" % s,
    )(z, chan, pos)


W_C = H_C * DV_C
RET_TQ = 256


def _log_sigmoid(x):
    return jnp.minimum(x, 0.0) - jnp.log1p(jnp.exp(-jnp.abs(x)))


def _head_pair_lanes(vals, pair):
    lane = lax.broadcasted_iota(jnp.int32, (1, LANES), 1)
    return jnp.where(lane < DK_C, vals[:, 2 * pair:2 * pair + 1], vals[:, 2 * pair + 1:2 * pair + 2])


def _retention_rows(s, row0, tq, q, k_ref, v_ref, g, lg, rnorm, s0f_ref, s0b_ref):
    lane = lax.broadcasted_iota(jnp.int32, (1, LANES), 1)
    ii = (row0 + lax.broadcasted_iota(jnp.int32, (tq, s), 0)).astype(F32)
    jj = lax.broadcasted_iota(jnp.int32, (tq, s), 1).astype(F32)
    rel = ii - jj
    pos = (row0 + lax.broadcasted_iota(jnp.int32, (tq, 1), 0)).astype(F32)
    outs = []
    for h in range(H_C):
        pair, half = h // 2, h % 2
        sl = slice(pair * LANES, (pair + 1) * LANES)
        lgf = lg[0:1, h:h + 1]
        lgb = lg[1:2, h:h + 1]
        in_head = (lane < DK_C) if half == 0 else (lane >= DK_C)
        q_pair = q[:, sl]
        qh = jnp.where(in_head, q_pair, jnp.zeros_like(q_pair))
        sc = _dot_nt(qh, k_ref[:, sl]) * (DK_C ** -0.5)
        decay = (jnp.where(rel >= 0, jnp.exp(lgf * jnp.maximum(rel, 0.0)), 0.0)
                 + jnp.where(rel <= 0, jnp.exp(lgb * jnp.maximum(-rel, 0.0)), 0.0))
        o = _dot((sc * decay).astype(BF16), v_ref[:, h * DV_C:(h + 1) * DV_C])
        if s0f_ref is not None:
            qf = qh.astype(F32)
            xf = jnp.exp(lgf * (pos + 1.0))
            xb = jnp.exp(lgb * (s - pos))
            o = o + _dot((qf * xf).astype(BF16), s0f_ref[0, 0, sl, :].astype(BF16))
            o = o + _dot((qf * xb).astype(BF16), s0b_ref[0, 0, sl, :].astype(BF16))
        gh = g[:, h * DV_C:(h + 1) * DV_C].astype(F32)
        outs.append((_rms(o) * rnorm) * (gh * _sigmoid(gh)))
    return outs


def _ret_ctx_kernel(q_ref, k_ref, v_ref, g_ref, dec_ref, rn_ref, *rest):
    o_ref, sf_ref, sb_ref = rest[-3:]
    s = SEQ
    lg = _log_sigmoid(dec_ref[...])
    outs = _retention_rows(s, 0, s, q_ref[...], k_ref, v_ref, g_ref[...], lg, rn_ref[...], None, None)
    for h in range(H_C):
        o_ref[:, h * DV_C:(h + 1) * DV_C] = outs[h].astype(BF16)
    pos = lax.broadcasted_iota(jnp.int32, (s, 1), 0).astype(F32)
    v = v_ref[...]
    for pair in range(H_C // 2):
        sl = slice(pair * LANES, (pair + 1) * LANES)
        kf = k_ref[:, sl].astype(F32) * (DK_C ** -0.5)
        zf = jnp.exp(_head_pair_lanes(lg[0:1, :], pair) * (s - 1.0 - pos))
        zb = jnp.exp(_head_pair_lanes(lg[1:2, :], pair) * pos)
        stf = _dot_tn((kf * zf).astype(BF16), v)
        stb = _dot_tn((kf * zb).astype(BF16), v)
        for half in range(2):
            h = 2 * pair + half
            rs = slice(half * DK_C, (half + 1) * DK_C)
            cs = slice(h * DV_C, (h + 1) * DV_C)
            sf_ref[0, 0, h] = stf[rs, cs]
            sb_ref[0, 0, h] = stb[rs, cs]


def _ret_ctx(z, dec_l, rnorm_l, layer, prev_states):
    wq = H_C * DK_C
    st_shape = jax.ShapeDtypeStruct((BATCH, DEPTH, H_C, DK_C, DV_C), F32)
    st_spec = pl.BlockSpec((1, 1, H_C, DK_C, DV_C), lambda b: (b, layer, 0, 0, 0))
    in_specs = [
        pl.BlockSpec((SEQ, wq), lambda b: (b, Z_QC // wq)),
        pl.BlockSpec((SEQ, wq), lambda b: (b, Z_KC // wq)),
        pl.BlockSpec((SEQ, W_C), lambda b: (b, Z_VC // W_C)),
        pl.BlockSpec((SEQ, W_C), lambda b: (b, Z_GC // W_C)),
        pl.BlockSpec((2, H_C), lambda b: (0, 0)),
        pl.BlockSpec((1, DV_C), lambda b: (0, 0)),
    ]
    args = [z, z, z, z, dec_l, rnorm_l.reshape(1, DV_C)]
    aliases = {}
    if prev_states is not None:
        in_specs += [pl.BlockSpec(memory_space=pl.ANY)] * 2
        args += list(prev_states)
        aliases = {6: 1, 7: 2}
    return pl.pallas_call(
        _ret_ctx_kernel,
        grid=(BATCH,),
        in_specs=in_specs,
        out_specs=[pl.BlockSpec((SEQ, W_C), lambda b: (b, 0)), st_spec, st_spec],
        out_shape=[jax.ShapeDtypeStruct((N_TOK, W_C), BF16), st_shape, st_shape],
        input_output_aliases=aliases,
        compiler_params=_cparams(("parallel",), 40),
        name="ret_ctx",
    )(*args)


def _ret_lat_kernel(q_ref, k_ref, v_ref, g_ref, dec_ref, rn_ref, s0f_ref, s0b_ref, o_ref):
    lg = _log_sigmoid(dec_ref[...])
    rnorm = rn_ref[...]

    def q_block(qb, carry):
        row0 = pl.multiple_of(qb * RET_TQ, RET_TQ)
        rows = pl.ds(row0, RET_TQ)
        outs = _retention_rows(DEC_SEQ, row0, RET_TQ, q_ref[rows, :], k_ref, v_ref, g_ref[rows, :],
                               lg, rnorm, s0f_ref, s0b_ref)
        for h in range(H_C):
            o_ref[rows, h * DV_C:(h + 1) * DV_C] = outs[h].astype(BF16)
        return carry

    lax.fori_loop(0, DEC_SEQ // RET_TQ, q_block, 0)


def _ret_lat(z, dec_l, rnorm_l, s0f4, s0b4, layer):
    wq = H_C * DK_C
    st_spec = pl.BlockSpec((1, 1, H_C * DK_C, DV_C), lambda b: (b, layer, 0, 0))
    return pl.pallas_call(
        _ret_lat_kernel,
        grid=(DEC_BATCH,),
        in_specs=[
            pl.BlockSpec((DEC_SEQ, wq), lambda b: (b, Z_QC // wq)),
            pl.BlockSpec((DEC_SEQ, wq), lambda b: (b, Z_KC // wq)),
            pl.BlockSpec((DEC_SEQ, W_C), lambda b: (b, Z_VC // W_C)),
            pl.BlockSpec((DEC_SEQ, W_C), lambda b: (b, Z_GC // W_C)),
            pl.BlockSpec((2, H_C), lambda b: (0, 0)),
            pl.BlockSpec((1, DV_C), lambda b: (0, 0)),
            st_spec, st_spec,
        ],
        out_specs=pl.BlockSpec((DEC_SEQ, W_C), lambda b: (b, 0)),
        out_shape=jax.ShapeDtypeStruct((N_TOK, W_C), BF16),
        compiler_params=_cparams(("parallel",), 40),
        name="ret_lat",
    )(z, z, z, z, dec_l, rnorm_l.reshape(1, DV_C), s0f4, s0b4)


MRG_TM = 256


def _merge_kernel(x_ref, ya_ref, yb_ref, yc_ref, gt_ref, mod_ref, wa_ref, wb_ref, wc_ref, wo_ref,
                  n2_ref, rw_ref, rb_ref, x1_ref, h2_ref, topi_ref, topw_ref, rank_ref,
                  cnt_row_ref, cnt_col_ref, carry_row, carry_col, wa_bf, wb_bf, wc_bf, wo_bf):
    d = D_MODEL
    tm = x_ref.shape[0]

    @pl.when(pl.program_id(0) == 0)
    def _():
        carry_row[...] = jnp.zeros_like(carry_row)
        carry_col[...] = jnp.zeros_like(carry_col)
        wa_bf[...] = wa_ref[0].astype(BF16)
        wb_bf[...] = wb_ref[0].astype(BF16)
        wc_bf[...] = wc_ref[0].astype(BF16)
        wo_bf[...] = wo_ref[0].astype(BF16)

    merged = _sigmoid(gt_ref[:, 0:d].astype(F32)) * _dot(ya_ref[...], wa_bf[...])
    merged += _sigmoid(gt_ref[:, d:2 * d].astype(F32)) * _dot(yb_ref[...], wb_bf[...])
    merged += _sigmoid(gt_ref[:, 2 * d:3 * d].astype(F32)) * _dot(yc_ref[...], wc_bf[...])
    y = _dot(merged.astype(BF16), wo_bf[...])
    x1 = x_ref[...] + mod_ref[0, 2:3, :] * y
    x1_ref[...] = x1
    h2 = _modnorm(x1, n2_ref[...], mod_ref[0, 4:5, :], mod_ref[0, 3:4, :])
    h2_ref[...] = h2
    logits = _dot_f32x3(h2, rw_ref[...]) + rb_ref[...]
    ids = lax.broadcasted_iota(jnp.int32, logits.shape, 1)
    work = logits
    sels, vals, firsts = [], [], []
    for _ in range(TOP_K):
        m = jnp.max(work, axis=-1, keepdims=True)
        first = jnp.min(jnp.where(work == m, ids, N_EXPERTS), axis=-1, keepdims=True)
        sel = ids == first
        sels.append(sel)
        vals.append(m)
        firsts.append(first)
        work = jnp.where(sel, -jnp.inf, work)
    es = [jnp.exp(v - vals[0]) for v in vals]
    inv = 1.0 / (es[0] + es[1] + es[2] + es[3])
    chosen = jnp.zeros_like(logits)
    for sel in sels:
        chosen = chosen + jnp.where(sel, 1.0, 0.0)
    chosen_bf = chosen.astype(BF16)
    rr = lax.broadcasted_iota(jnp.int32, (tm, tm), 0)
    cc = lax.broadcasted_iota(jnp.int32, (tm, tm), 1)
    before = jnp.where(rr > cc, 1.0, 0.0).astype(BF16)
    prefix = _dot(before, chosen_bf) + carry_row[...]
    ranks = [jnp.sum(jnp.where(sel, prefix, 0.0), axis=-1, keepdims=True) for sel in sels]
    carry_row[...] += jnp.sum(chosen, axis=0, keepdims=True)
    carry_col[...] += _dot_tn(chosen_bf, jnp.ones((tm, LANES), BF16))
    cnt_row_ref[...] = carry_row[...]
    cnt_col_ref[...] = carry_col[...]
    topi_ref[...] = jnp.concatenate(firsts, axis=1)
    topw_ref[...] = jnp.concatenate([e * inv for e in es], axis=1)
    rank_ref[...] = jnp.concatenate(ranks, axis=1).astype(jnp.int32)


def _merge(x, ya, yb, yc, z, mod_l, wa, wb, wc, wo, n2, rw, rb, layer, latent):
    tm = MRG_TM
    d = D_MODEL
    full = lambda shape: pl.BlockSpec(shape, lambda i: tuple(0 for _ in shape))
    tok = lambda w: pl.BlockSpec((tm, w), lambda i: (i, 0))
    per_layer = lambda rows: pl.BlockSpec((1, rows, d), lambda i: (layer, 0, 0))
    return pl.pallas_call(
        _merge_kernel,
        grid=(N_TOK // tm,),
        in_specs=[
            tok(d), tok(W_A), tok(W_B), tok(W_C), tok(3 * d),
            pl.BlockSpec((1, 6, d), lambda i: (_mod_row(i, tm, latent), 0, 0)),
            per_layer(W_A), per_layer(W_B), per_layer(W_C), per_layer(d),
            full((1, d)), full((d, N_EXPERTS)), full((1, N_EXPERTS)),
        ],
        out_specs=[tok(d), tok(d), tok(TOP_K), tok(TOP_K), tok(TOP_K),
                   full((1, N_EXPERTS)), full((N_EXPERTS, LANES))],
        out_shape=[
            jax.ShapeDtypeStruct((N_TOK, d), F32),
            jax.ShapeDtypeStruct((N_TOK, d), F32),
            jax.ShapeDtypeStruct((N_TOK, TOP_K), jnp.int32),
            jax.ShapeDtypeStruct((N_TOK, TOP_K), F32),
            jax.ShapeDtypeStruct((N_TOK, TOP_K), jnp.int32),
            jax.ShapeDtypeStruct((1, N_EXPERTS), F32),
            jax.ShapeDtypeStruct((N_EXPERTS, LANES), F32),
        ],
        scratch_shapes=[pltpu.VMEM((1, N_EXPERTS), F32), pltpu.VMEM((N_EXPERTS, LANES), F32),
                        pltpu.VMEM((W_A, d), BF16), pltpu.VMEM((W_B, d), BF16),
                        pltpu.VMEM((W_C, d), BF16), pltpu.VMEM((d, d), BF16)],
        compiler_params=_cparams(("arbitrary",), 56),
        name="merge_router",
    )(x, ya, yb, yc, z, mod_l, wa, wb, wc, wo, n2.reshape(1, d), rw, rb.reshape(1, N_EXPERTS))


MOE_TM = 256
MOE_TILES = (N_TOK * TOP_K) // MOE_TM + N_EXPERTS
MOE_ROWS = MOE_TILES * MOE_TM
PLAN_LANES = 256
PLAN_TM = 1024


def _plan_kernel(topi_ref, rank_ref, cnt_row_ref, cnt_col_ref, pos_ref, te_ref, nt_ref, pad_ref,
                 next_ref):
    e_r = lax.broadcasted_iota(jnp.int32, (N_EXPERTS, N_EXPERTS), 0)
    e_c = lax.broadcasted_iota(jnp.int32, (N_EXPERTS, N_EXPERTS), 1)
    cnt_row = cnt_row_ref[...]
    tiles_row = jnp.ceil(cnt_row * (1.0 / MOE_TM))
    earlier = jnp.where(e_r < e_c, 1.0, 0.0).astype(BF16)
    start_tile_row = _dot(tiles_row.astype(BF16), earlier)
    off_row = start_tile_row * MOE_TM
    ids = lax.broadcasted_iota(jnp.int32, (PLAN_TM, N_EXPERTS), 1)
    cols = []
    for k in range(TOP_K):
        hit = ids == topi_ref[:, k:k + 1]
        cols.append(jnp.sum(jnp.where(hit, off_row, 0.0), axis=-1, keepdims=True))
    pos_ref[...] = jnp.concatenate(cols, axis=1).astype(jnp.int32) + rank_ref[...]
    tiles_col = jnp.ceil(cnt_col_ref[...] * (1.0 / MOE_TM))
    upto = jnp.where(e_c <= e_r, 1.0, 0.0).astype(BF16)
    end_col = _dot(upto, tiles_col.astype(BF16))
    end_b = jnp.concatenate([end_col] * (PLAN_LANES // LANES), axis=1)
    r = lax.broadcasted_iota(jnp.int32, (N_EXPERTS, PLAN_LANES), 1).astype(F32)
    total = end_b[N_EXPERTS - 1:N_EXPERTS, :]
    te = jnp.sum(jnp.where(end_b <= jnp.minimum(r, total - 1.0), 1.0, 0.0), axis=0, keepdims=True)
    te_ref[...] = te.astype(jnp.int32)
    nt_ref[...] = total[:, :LANES].astype(jnp.int32)
    end_row = end_b * MOE_TM
    first_pad = jnp.concatenate([(end_col - tiles_col) * MOE_TM + cnt_col_ref[...]] * (PLAN_LANES // LANES),
                                axis=1)
    cand = first_pad + r
    pad_ref[...] = jnp.where(cand < end_row, cand, MOE_ROWS - 1.0).astype(jnp.int32)
    later_used = jnp.logical_and(e_r > e_c, cnt_col_ref[:, 0:N_EXPERTS] > 0.0)
    nxt = jnp.min(jnp.where(later_used, e_r, N_EXPERTS), axis=0, keepdims=True)
    next_ref[...] = jnp.where(nxt == N_EXPERTS, -1, nxt)


def _plan(topi, rank, cnt_row, cnt_col):
    tok = pl.BlockSpec((PLAN_TM, TOP_K), lambda i: (i, 0))
    full = lambda shape: pl.BlockSpec(shape, lambda i: tuple(0 for _ in shape))
    return pl.pallas_call(
        _plan_kernel,
        grid=(N_TOK // PLAN_TM,),
        in_specs=[tok, tok, full((1, N_EXPERTS)), full((N_EXPERTS, LANES))],
        out_specs=[tok, full((1, PLAN_LANES)), full((1, LANES)), full((N_EXPERTS, PLAN_LANES)),
                   full((1, N_EXPERTS))],
        out_shape=[
            jax.ShapeDtypeStruct((N_TOK, TOP_K), jnp.int32),
            jax.ShapeDtypeStruct((1, PLAN_LANES), jnp.int32),
            jax.ShapeDtypeStruct((1, LANES), jnp.int32),
            jax.ShapeDtypeStruct((N_EXPERTS, PLAN_LANES), jnp.int32),
            jax.ShapeDtypeStruct((1, N_EXPERTS), jnp.int32),
        ],
        compiler_params=_cparams(("arbitrary",), 32),
        name="route_plan",
    )(topi, rank, cnt_row, cnt_col)


SC_WORKERS = 32
SC_CORES = 2
SC_LANES = 16
DSP_CHUNK = 32
DSP_TOK = N_TOK // SC_WORKERS
DSP_NCHUNK = DSP_TOK // DSP_CHUNK
DSP_PAD_ROWS = MOE_TM // DSP_CHUNK


def _dispatch_body(h_hbm, idx_hbm, pad_hbm, xs_hbm, idx_v, pad_v, rows_a, rows_b, sem_a, sem_b):
    wid = lax.axis_index("s") * SC_CORES + lax.axis_index("c")
    pltpu.sync_copy(idx_hbm.at[wid], idx_v)
    pltpu.sync_copy(pad_hbm.at[wid], pad_v)
    zero = jnp.zeros((SC_LANES,), F32)

    @pl.loop(0, DSP_CHUNK)
    def _(row):
        @pl.loop(0, D_MODEL, step=SC_LANES)
        def _(col):
            rows_a[row, pl.ds(col, SC_LANES)] = zero

    for j in range(DSP_PAD_ROWS):
        pltpu.async_copy(rows_a, xs_hbm.at[pad_v.at[j]], sem_a)
    for j in range(DSP_PAD_ROWS):
        pltpu.make_async_copy(rows_a, xs_hbm.at[pad_v.at[j]], sem_a).wait()

    bufs = (rows_a, rows_b)
    sems = (sem_a, sem_b)

    @pl.loop(0, DSP_NCHUNK, step=2)
    def _(c0):
        for b in range(2):
            chunk = c0 + b
            pltpu.sync_copy(h_hbm.at[pl.ds(wid * DSP_TOK + chunk * DSP_CHUNK, DSP_CHUNK)], bufs[b])
            for k in range(TOP_K):
                pltpu.async_copy(bufs[b], xs_hbm.at[idx_v.at[chunk * TOP_K + k]], sems[b])
        for b in range(2):
            chunk = c0 + b
            for k in range(TOP_K):
                pltpu.make_async_copy(bufs[b], xs_hbm.at[idx_v.at[chunk * TOP_K + k]], sems[b]).wait()


def _sc_index_lists(pos):
    idx = pos.reshape(SC_WORKERS, DSP_NCHUNK, DSP_CHUNK, TOP_K).transpose(0, 1, 3, 2)
    return idx.reshape(SC_WORKERS, DSP_NCHUNK * TOP_K, DSP_CHUNK)


def _dispatch(idx, pad_rows, h2):
    pad = pad_rows.reshape(N_EXPERTS, DSP_PAD_ROWS, DSP_CHUNK)
    mesh = plsc.VectorSubcoreMesh(core_axis_name="c", subcore_axis_name="s")
    call = pl.kernel(
        _dispatch_body,
        out_type=jax.ShapeDtypeStruct((MOE_ROWS, D_MODEL), F32),
        mesh=mesh,
        scratch_types=[
            pltpu.VMEM((DSP_NCHUNK * TOP_K, DSP_CHUNK), jnp.int32),
            pltpu.VMEM((DSP_PAD_ROWS, DSP_CHUNK), jnp.int32),
            pltpu.VMEM((DSP_CHUNK, D_MODEL), F32),
            pltpu.VMEM((DSP_CHUNK, D_MODEL), F32),
            pltpu.SemaphoreType.DMA,
            pltpu.SemaphoreType.DMA,
        ],
        name="dispatch_sc",
    )
    return call(h2, idx, pad)


PAIR_BLOCKS = (2 * D_FF) // (2 * LANES)
WGU_CHUNKS = 4
WDN_CHUNKS = 2


def _prep_expert_weights(wgu_ref, wdn_ref, wgu_bf, wdn_perm, wdn_bf):
    wgu_bf[...] = wgu_ref[...].astype(BF16)
    half = LANES // 2
    for cb in range(D_MODEL // LANES):
        cols = slice(cb * LANES, (cb + 1) * LANES)
        for m in range(PAIR_BLOCKS):
            base = m * LANES
            wdn_perm[cb, pl.ds(base, half, stride=2), :] = wdn_ref[base:base + half, cols]
            wdn_perm[cb, pl.ds(base + 1, half, stride=2), :] = wdn_ref[base + half:base + LANES, cols]
        wdn_bf[:, cols] = wdn_perm[cb].astype(BF16)


def _ffn_rows(x, wgu_bf, bgu, wdn_bf, bdn):
    gu = _dot(x, wgu_bf[...]) + bgu
    lane = lax.broadcasted_iota(jnp.int32, (1, LANES), 1)
    even = (lane % 2) == 0

    def act(block):
        t = gu[:, block * LANES:(block + 1) * LANES]
        glu = jnp.minimum(t, SWIGLU_LIMIT)
        glu = glu * _sigmoid(SWIGLU_ALPHA * glu)
        lin = jnp.clip(t, -SWIGLU_LIMIT, SWIGLU_LIMIT) + 1.0
        return glu * pltpu.roll(lin, LANES - 1, 1)

    parts = []
    for m in range(PAIR_BLOCKS):
        parts.append(jnp.where(even, act(2 * m), pltpu.roll(act(2 * m + 1), 1, 1)))
    hidden = jnp.concatenate(parts, axis=1).astype(BF16)
    return _dot(hidden, wdn_bf[...]) + bdn


def _moe_kernel(layer, te_ref, nt_ref, next_ref, x_ref, wgu_hbm, bgu_ref, wdn_hbm, bdn_ref, o_ref,
                wgu_land, wdn_land, wgu_bf, wdn_perm, wdn_bf, sems):
    r = pl.program_id(0)
    valid = r < nt_ref[0]
    e = te_ref[r]
    new_expert = jnp.logical_or(r == 0, e != te_ref[jnp.maximum(r - 1, 0)])

    def fetch(expert):
        copies = []
        for c in range(WGU_CHUNKS):
            rows = pl.ds(c * (D_MODEL // WGU_CHUNKS), D_MODEL // WGU_CHUNKS)
            copies.append(pltpu.make_async_copy(wgu_hbm.at[layer, expert, rows], wgu_land.at[rows], sems.at[c]))
        for c in range(WDN_CHUNKS):
            rows = pl.ds(c * (D_FF // WDN_CHUNKS), D_FF // WDN_CHUNKS)
            copies.append(pltpu.make_async_copy(wdn_hbm.at[layer, expert, rows], wdn_land.at[rows],
                                                sems.at[WGU_CHUNKS + c]))
        return copies

    @pl.when(r == 0)
    def _():
        for cp in fetch(e):
            cp.start()

    @pl.when(jnp.logical_and(valid, new_expert))
    def _():
        for cp in fetch(e):
            cp.wait()
        _prep_expert_weights(wgu_land, wdn_land, wgu_bf, wdn_perm, wdn_bf)
        nxt = next_ref[e]

        @pl.when(nxt >= 0)
        def _():
            for cp in fetch(nxt):
                cp.start()

    @pl.when(valid)
    def _():
        o_ref[...] = _ffn_rows(x_ref[...].astype(BF16), wgu_bf, bgu_ref[0, 0], wdn_bf, bdn_ref[0, 0])


def _moe_grouped(tile_expert, n_tiles, next_expert, xs, w_gate_up, b_gate_up4, w_down, b_down4, layer):
    d, f2 = D_MODEL, 2 * D_FF
    row_tile = lambda r, te, nt, nx: (jnp.minimum(r, nt[0] - 1), 0)
    expert = lambda r, te, nt, nx: (layer, te[r], 0, 0)
    grid_spec = pltpu.PrefetchScalarGridSpec(
        num_scalar_prefetch=3,
        grid=(MOE_TILES,),
        in_specs=[
            pl.BlockSpec((MOE_TM, d), row_tile),
            pl.BlockSpec(memory_space=pl.ANY),
            pl.BlockSpec((1, 1, 1, f2), expert),
            pl.BlockSpec(memory_space=pl.ANY),
            pl.BlockSpec((1, 1, 1, d), expert),
        ],
        out_specs=pl.BlockSpec((MOE_TM, d), row_tile),
        scratch_shapes=[
            pltpu.VMEM((d, f2), F32),
            pltpu.VMEM((D_FF, d), F32),
            pltpu.VMEM((d, f2), BF16),
            pltpu.VMEM((d // LANES, D_FF, LANES), F32),
            pltpu.VMEM((D_FF, d), BF16),
            pltpu.SemaphoreType.DMA((WGU_CHUNKS + WDN_CHUNKS,)),
        ],
    )
    return pl.pallas_call(
        functools.partial(_moe_kernel, layer),
        grid_spec=grid_spec,
        out_shape=jax.ShapeDtypeStruct((MOE_ROWS, d), F32),
        compiler_params=_cparams(("arbitrary",), 56),
        name="moe_grouped",
    )(tile_expert, n_tiles, next_expert, xs, w_gate_up, b_gate_up4, w_down, b_down4)


def _gather_body(ys_hbm, idx_hbm, planes_hbm, idx_v, rows_a, rows_b, sem_a, sem_b):
    wid = lax.axis_index("s") * SC_CORES + lax.axis_index("c")
    pltpu.sync_copy(idx_hbm.at[wid], idx_v)
    bufs = (rows_a, rows_b)
    sems = (sem_a, sem_b)

    @pl.loop(0, DSP_NCHUNK)
    def _(chunk):
        tok0 = wid * DSP_TOK + chunk * DSP_CHUNK
        for k0 in range(0, TOP_K, 2):
            for b in range(2):
                pltpu.async_copy(ys_hbm.at[idx_v.at[chunk * TOP_K + k0 + b]], bufs[b], sems[b])
            for b in range(2):
                pltpu.make_async_copy(ys_hbm.at[idx_v.at[chunk * TOP_K + k0 + b]], bufs[b], sems[b]).wait()
                pltpu.sync_copy(bufs[b], planes_hbm.at[k0 + b, pl.ds(tok0, DSP_CHUNK)])


def _gather(idx, ys):
    mesh = plsc.VectorSubcoreMesh(core_axis_name="c", subcore_axis_name="s")
    call = pl.kernel(
        _gather_body,
        out_type=jax.ShapeDtypeStruct((TOP_K, N_TOK, D_MODEL), F32),
        mesh=mesh,
        scratch_types=[
            pltpu.VMEM((DSP_NCHUNK * TOP_K, DSP_CHUNK), jnp.int32),
            pltpu.VMEM((DSP_CHUNK, D_MODEL), F32),
            pltpu.VMEM((DSP_CHUNK, D_MODEL), F32),
            pltpu.SemaphoreType.DMA,
            pltpu.SemaphoreType.DMA,
        ],
        name="gather_sc",
    )
    return call(ys, idx)


CMB_TM = 512


def _combine_kernel(final, planes_ref, x1_ref, topw_ref, mod_ref, *rest):
    o_ref = rest[-1]
    y = topw_ref[:, 0:1] * planes_ref[0]
    for k in range(1, TOP_K):
        y = y + topw_ref[:, k:k + 1] * planes_ref[k]
    x = x1_ref[...] + mod_ref[0, 5:6, :] * y
    if final:
        x = _rms(x) * rest[0][...]
    o_ref[...] = x


def _combine(planes, x1, topw, mod_l, latent, final_g=None):
    tm, d = CMB_TM, D_MODEL
    final = final_g is not None
    in_specs = [
        pl.BlockSpec((TOP_K, tm, d), lambda i: (0, i, 0)),
        pl.BlockSpec((tm, d), lambda i: (i, 0)),
        pl.BlockSpec((tm, TOP_K), lambda i: (i, 0)),
        pl.BlockSpec((1, 6, d), lambda i: (_mod_row(i, tm, latent), 0, 0)),
    ]
    args = [planes, x1, topw, mod_l]
    if final:
        in_specs.append(pl.BlockSpec((1, d), lambda i: (0, 0)))
        args.append(final_g.reshape(1, d))
    return pl.pallas_call(
        functools.partial(_combine_kernel, final),
        grid=(N_TOK // tm,),
        in_specs=in_specs,
        out_specs=pl.BlockSpec((tm, d), lambda i: (i, 0)),
        out_shape=jax.ShapeDtypeStruct((N_TOK, d), F32),
        compiler_params=_cparams(("parallel",), 48),
        name="combine_final" if final else "combine",
    )(*args)


def kernel(x_prompt, x_sample, cache_k, cache_v, state_ret_fwd, state_ret_bwd, c, c_ctx, norm1_g, norm2_g, w_mod, b_mod, w_in, lam_a, subln_a, ret_decay, ret_norm, w_branch_a, w_branch_b, w_branch_c, w_out, router_w, router_b, w_gate_up, b_gate_up, w_down, b_down, final_g):
    d = D_MODEL
    cond8 = jnp.concatenate([c_ctx[None, :], c, jnp.zeros((3, d), F32)], axis=0)
    mod = _modulation(cond8, w_mod, b_mod)

    cache_k4 = cache_k.reshape(DEC_BATCH, DEPTH, PAST_LEN, W_A)
    cache_v4 = cache_v.reshape(DEC_BATCH, DEPTH, PAST_LEN, W_A)
    s0f4 = state_ret_fwd.reshape(DEC_BATCH, DEPTH, H_C * DK_C, DV_C)
    s0b4 = state_ret_bwd.reshape(DEC_BATCH, DEPTH, H_C * DK_C, DV_C)
    b_gate_up4 = b_gate_up.reshape(DEPTH, N_EXPERTS, 1, 2 * D_FF)
    b_down4 = b_down.reshape(DEPTH, N_EXPERTS, 1, d)
    cos, sin = _rope_tables(DEC_SEQ)
    chan, pos_ctx = _dft_consts(SEQ)
    _, pos_lat = _dft_consts(DEC_SEQ)

    def channel_mixer(l, latent, x, ya, yb, yc, z, last):
        x1, h2, topi, topw, rank, cnt_row, cnt_col = _merge(
            x, ya, yb, yc, z, mod[l], w_branch_a, w_branch_b, w_branch_c, w_out,
            norm2_g[l], router_w[l], router_b[l], l, latent)
        pos, tile_expert, n_tiles, pad_rows, next_expert = _plan(topi, rank, cnt_row, cnt_col)
        idx = _sc_index_lists(pos)
        xs = _dispatch(idx, pad_rows, h2)
        ys = _moe_grouped(tile_expert.reshape(PLAN_LANES), n_tiles[0, :1], next_expert.reshape(N_EXPERTS),
                          xs, w_gate_up, b_gate_up4, w_down, b_down4, l)
        planes = _gather(idx, ys)
        return _combine(planes, x1, topw, mod[l], latent, final_g if last else None)

    xc = x_prompt.reshape(N_TOK, d)
    xl = x_sample.reshape(N_TOK, d)
    caches, states = None, None
    for l in range(DEPTH):
        lam_init = 0.8 - 0.6 * math.exp(-0.3 * l)
        last = l == DEPTH - 1
        z, new_k, new_v = _inproj(xc, mod[l], norm1_g[l], w_in, l, False, caches)
        caches = (new_k, new_v)
        ya = _attn_ctx(z, lam_a[l], subln_a[l], lam_init)
        yb = _fourier(z, SEQ, chan, pos_ctx)
        yc, new_sf, new_sb = _ret_ctx(z, ret_decay[l], ret_norm[l], l, states)
        states = (new_sf, new_sb)
        xc = channel_mixer(l, False, xc, ya, yb, yc, z, last)
        z = _inproj(xl, mod[l], norm1_g[l], w_in, l, True)[0]
        ya = _attn_lat(z, cache_k4, cache_v4, l, cos, sin, lam_a[l], subln_a[l], lam_init)
        yb = _fourier(z, DEC_SEQ, chan, pos_lat)
        yc = _ret_lat(z, ret_decay[l], ret_norm[l], s0f4, s0b4, l)
        xl = channel_mixer(l, True, xl, ya, yb, yc, z, last)

    return (xc.reshape(BATCH, SEQ, d), xl.reshape(DEC_BATCH, DEC_SEQ, d),
            caches[0].reshape(BATCH, DEPTH, SEQ, H_A, 2, DH_A),
            caches[1].reshape(BATCH, DEPTH, SEQ, H_A, DV_A),
            states[0], states[1])
```

```python
import functools
import math

import numpy as np
import jax
import jax.numpy as jnp
from jax import lax
from jax.experimental import pallas as pl
from jax.experimental.pallas import tpu as pltpu
from jax.experimental.pallas import tpu_sc as plsc

F32 = jnp.float32
BF16 = jnp.bfloat16

D_MODEL = 1024
BATCH = 16
SEQ = 256
DEPTH = 2
DEC_BATCH = 4
DEC_SEQ = 1024
PAST_LEN = 512
GRID_W = 64
H_A = 4
DH_A = 64
DV_A = 128
G_B = 4
DG_B = 128
H_C = 4
DK_C = 64
DV_C = 128
N_EXPERTS = 32
TOP_K = 4
D_FF = 1024
SWIGLU_LIMIT = 7.0
SWIGLU_ALPHA = 1.702
ROPE_BASE = 10000.0
EPS = 1e-6

N_TOK = BATCH * SEQ
assert N_TOK == DEC_BATCH * DEC_SEQ
IN_TOTAL = 6656
Z_GATES = 0
Z_QA = 3072
Z_KA = 3584
Z_VA = 4096
Z_FB = 4608
Z_QC = 5120
Z_KC = 5376
Z_VC = 5632
Z_GC = 6144
LANES = 128
MIB = 1024 * 1024


def _cparams(semantics, vmem_mib):
    return pltpu.CompilerParams(dimension_semantics=semantics,
                                vmem_limit_bytes=vmem_mib * MIB)


def _dot(a, b):
    return jnp.dot(a, b, preferred_element_type=F32)


def _dot_nt(a, b):
    return lax.dot_general(a, b, (((1,), (1,)), ((), ())), preferred_element_type=F32)


def _dot_tn(a, b):
    return lax.dot_general(a, b, (((0,), (0,)), ((), ())), preferred_element_type=F32)


def _split_bf16(a):
    hi = a.astype(BF16)
    lo = (a - hi.astype(F32)).astype(BF16)
    return hi, lo


def _dot_f32x3(a, b):
    a_hi, a_lo = _split_bf16(a)
    b_hi, b_lo = _split_bf16(b)
    return _dot(a_hi, b_hi) + _dot(a_lo, b_hi) + _dot(a_hi, b_lo)


def _rms(x):
    return x * lax.rsqrt(jnp.mean(x * x, axis=-1, keepdims=True) + EPS)


def _modnorm(x, g, scale, shift):
    return (_rms(x) * g) * (1.0 + scale) + shift


def _sigmoid(x):
    return 1.0 / (1.0 + jnp.exp(-x))


MOD_TN = 1536


def _mod_kernel(cond_ref, w_ref, b_ref, o_ref):
    cnd = cond_ref[...]
    s = cnd * _sigmoid(cnd)
    o_ref[0] = _dot_f32x3(s, w_ref[0]) + b_ref[0]


def _modulation(cond8, w_mod, b_mod):
    n = 6 * D_MODEL
    out = pl.pallas_call(
        _mod_kernel,
        grid=(DEPTH, n // MOD_TN),
        in_specs=[
            pl.BlockSpec((8, D_MODEL), lambda l, j: (0, 0)),
            pl.BlockSpec((1, D_MODEL, MOD_TN), lambda l, j: (l, 0, j)),
            pl.BlockSpec((1, 1, MOD_TN), lambda l, j: (l, 0, j)),
        ],
        out_specs=pl.BlockSpec((1, 8, MOD_TN), lambda l, j: (l, 0, j)),
        out_shape=jax.ShapeDtypeStruct((DEPTH, 8, n), F32),
        compiler_params=_cparams(("parallel", "parallel"), 40),
        name="modulation",
    )(cond8, w_mod, b_mod.reshape(DEPTH, 1, n))
    return out.reshape(DEPTH, 8, 6, D_MODEL)


def _mod_row(tile_idx, tm, latent):
    if not latent:
        return 0
    return 1 + tile_idx // (DEC_SEQ // tm)


INP_TM = 1024
INP_TN = 512
INP_NJ = IN_TOTAL // INP_TN
INP_GATE_BLOCKS = (3 * D_MODEL) // INP_TN
CTX_PER_TILE = INP_TM // SEQ


def _inproj_kernel(latent, x_ref, mod_ref, g_ref, w_ref, *rest):
    h_scr = rest[-1]
    z_ref = rest[-2] if latent else rest[-4]
    j = pl.program_id(1)

    @pl.when(j == 0)
    def _():
        h = _modnorm(x_ref[...], g_ref[...], mod_ref[0, 1:2, :], mod_ref[0, 0:1, :])
        h_scr[...] = h.astype(BF16)

    z_ref[...] = _dot(h_scr[...], w_ref[0].astype(BF16)).astype(BF16)
    if latent:
        return
    kc_ref, vc_ref = rest[-3:-1]

    def write_cache(cache_ref):
        w_bf = w_ref[0].astype(BF16)
        for b in range(CTX_PER_TILE):
            cache_ref[b, 0] = _dot(h_scr[b * SEQ:(b + 1) * SEQ, :], w_bf)

    @pl.when(j == Z_KA // INP_TN)
    def _():
        write_cache(kc_ref)

    @pl.when(j == Z_VA // INP_TN)
    def _():
        write_cache(vc_ref)


def _inproj(x, mod_l, g, w_in, layer, latent, prev_caches=None):
    in_specs = [
        pl.BlockSpec((INP_TM, D_MODEL), lambda i, j: (i, 0)),
        pl.BlockSpec((1, 6, D_MODEL), lambda i, j: (_mod_row(i, INP_TM, latent), 0, 0)),
        pl.BlockSpec((1, D_MODEL), lambda i, j: (0, 0)),
        pl.BlockSpec((1, D_MODEL, INP_TN),
                     lambda i, j: (layer, 0, (j + INP_NJ - INP_GATE_BLOCKS) % INP_NJ)),
    ]
    args = [x, mod_l, g.reshape(1, D_MODEL), w_in]
    out_specs = [pl.BlockSpec((INP_TM, INP_TN), lambda i, j: (i, j))]
    out_shape = [jax.ShapeDtypeStruct((N_TOK, IN_TOTAL), BF16)]
    aliases = {}
    if not latent:
        cache_spec = pl.BlockSpec((CTX_PER_TILE, 1, SEQ, W_A), lambda i, j: (i, layer, 0, 0))
        out_specs += [cache_spec, cache_spec]
        out_shape += [jax.ShapeDtypeStruct((BATCH, DEPTH, SEQ, W_A), F32)] * 2
        if prev_caches is not None:
            in_specs += [pl.BlockSpec(memory_space=pl.ANY)] * 2
            args += list(prev_caches)
            aliases = {4: 1, 5: 2}
    return pl.pallas_call(
        functools.partial(_inproj_kernel, latent),
        grid=(N_TOK // INP_TM, INP_NJ),
        in_specs=in_specs,
        out_specs=out_specs,
        out_shape=out_shape,
        scratch_shapes=[pltpu.VMEM((INP_TM, D_MODEL), BF16)],
        input_output_aliases=aliases,
        compiler_params=_cparams(("parallel", "arbitrary"), 40),
        name="inproj_lat" if latent else "inproj_ctx",
    )(*args)


ATT_TQ = 256
W_A = H_A * DV_A


def _lambda(lam_ref, lam_init):
    la = lam_ref[...]
    l01 = jnp.sum(la[0:1, :] * la[1:2, :], axis=1, keepdims=True)
    l23 = jnp.sum(la[2:3, :] * la[3:4, :], axis=1, keepdims=True)
    return jnp.exp(l01) - jnp.exp(l23) + lam_init


def _softmax_parts(s):
    m = jnp.max(s, axis=-1, keepdims=True)
    e = jnp.exp(s - m)
    return e, 1.0 / jnp.sum(e, axis=-1, keepdims=True)


def _diff_attn_block(q, k_ref, v_ref, lam, subln, lam_init):
    lane = lax.broadcasted_iota(jnp.int32, (1, LANES), 1)
    first = lane < DH_A
    scale = DH_A ** -0.5
    outs = []
    for h in range(H_A):
        sl = slice(h * LANES, (h + 1) * LANES)
        qh = q[:, sl]
        kh = k_ref[:, sl]
        zero = jnp.zeros_like(qh)
        s0 = _dot_nt(jnp.where(first, qh, zero), kh) * scale
        s1 = _dot_nt(jnp.where(first, zero, qh), kh) * scale
        e0, r0 = _softmax_parts(s0)
        e1, r1 = _softmax_parts(s1)
        a = e0 * r0 - (lam * r1) * e1
        o = _dot(a.astype(BF16), v_ref[:, sl])
        outs.append((_rms(o) * subln) * (1.0 - lam_init))
    return outs


def _rope(x, cos, sin):
    lane = lax.broadcasted_iota(jnp.int32, (1, x.shape[1]), 1)
    first = (lane % 32) < 16
    n = x.shape[1]
    xr = jnp.where(first, -pltpu.roll(x, n - 16, 1), pltpu.roll(x, 16, 1))
    return x * cos + xr * sin


def _attn_ctx_kernel(lam_init, q_ref, k_ref, v_ref, lam_ref, subln_ref, o_ref):
    lam = _lambda(lam_ref, lam_init)
    outs = _diff_attn_block(q_ref[...], k_ref, v_ref, lam, subln_ref[...], lam_init)
    for h in range(H_A):
        o_ref[:, h * LANES:(h + 1) * LANES] = outs[h].astype(BF16)


def _attn_ctx(z, lam_a_l, subln_l, lam_init):
    wb = W_A
    return pl.pallas_call(
        functools.partial(_attn_ctx_kernel, lam_init),
        grid=(BATCH,),
        in_specs=[
            pl.BlockSpec((SEQ, wb), lambda b: (b, Z_QA // wb)),
            pl.BlockSpec((SEQ, wb), lambda b: (b, Z_KA // wb)),
            pl.BlockSpec((SEQ, wb), lambda b: (b, Z_VA // wb)),
            pl.BlockSpec((4, DH_A), lambda b: (0, 0)),
            pl.BlockSpec((1, DV_A), lambda b: (0, 0)),
        ],
        out_specs=pl.BlockSpec((SEQ, wb), lambda b: (b, 0)),
        out_shape=jax.ShapeDtypeStruct((N_TOK, wb), BF16),
        compiler_params=_cparams(("parallel",), 40),
        name="attn_ctx",
    )(z, z, z, lam_a_l, subln_l.reshape(1, DV_A))


def _attn_lat_kernel(lam_init, q_ref, k_ref, v_ref, ck_ref, cv_ref, cos_ref, sin_ref,
                     lam_ref, subln_ref, o_ref, kall, vall):
    lam = _lambda(lam_ref, lam_init)
    subln = subln_ref[...]
    kall[0:DEC_SEQ, :] = _rope(k_ref[...].astype(F32), cos_ref[...], sin_ref[...]).astype(BF16)
    kall[DEC_SEQ:, :] = ck_ref[0, 0].astype(BF16)
    vall[0:DEC_SEQ, :] = v_ref[...]
    vall[DEC_SEQ:, :] = cv_ref[0, 0].astype(BF16)

    def q_block(qb, carry):
        rows = pl.ds(pl.multiple_of(qb * ATT_TQ, ATT_TQ), ATT_TQ)
        q = _rope(q_ref[rows, :].astype(F32), cos_ref[rows, :], sin_ref[rows, :]).astype(BF16)
        outs = _diff_attn_block(q, kall, vall, lam, subln, lam_init)
        for h in range(H_A):
            o_ref[rows, h * LANES:(h + 1) * LANES] = outs[h].astype(BF16)
        return carry

    lax.fori_loop(0, DEC_SEQ // ATT_TQ, q_block, 0)


def _attn_lat(z, cache_k4, cache_v4, layer, cos, sin, lam_a_l, subln_l, lam_init):
    wb = W_A
    t_all = DEC_SEQ + PAST_LEN
    return pl.pallas_call(
        functools.partial(_attn_lat_kernel, lam_init),
        grid=(DEC_BATCH,),
        in_specs=[
            pl.BlockSpec((DEC_SEQ, wb), lambda b: (b, Z_QA // wb)),
            pl.BlockSpec((DEC_SEQ, wb), lambda b: (b, Z_KA // wb)),
            pl.BlockSpec((DEC_SEQ, wb), lambda b: (b, Z_VA // wb)),
            pl.BlockSpec((1, 1, PAST_LEN, wb), lambda b: (b, layer, 0, 0)),
            pl.BlockSpec((1, 1, PAST_LEN, wb), lambda b: (b, layer, 0, 0)),
            pl.BlockSpec((DEC_SEQ, wb), lambda b: (0, 0)),
            pl.BlockSpec((DEC_SEQ, wb), lambda b: (0, 0)),
            pl.BlockSpec((4, DH_A), lambda b: (0, 0)),
            pl.BlockSpec((1, DV_A), lambda b: (0, 0)),
        ],
        out_specs=pl.BlockSpec((DEC_SEQ, wb), lambda b: (b, 0)),
        out_shape=jax.ShapeDtypeStruct((N_TOK, wb), BF16),
        scratch_shapes=[pltpu.VMEM((t_all, wb), BF16), pltpu.VMEM((t_all, wb), BF16)],
        compiler_params=_cparams(("parallel",), 48),
        name="attn_lat",
    )(z, z, z, cache_k4, cache_v4, cos, sin, lam_a_l, subln_l.reshape(1, DV_A))


def _rope_tables(s):
    rows = s // GRID_W
    row = jnp.repeat(jnp.arange(rows, dtype=F32), GRID_W)
    col = jnp.tile(jnp.arange(GRID_W, dtype=F32), rows)
    quarter = DH_A // 4
    inv = ROPE_BASE ** (-jnp.arange(quarter, dtype=F32) / quarter)
    ar = row[:, None] * inv
    ac = col[:, None] * inv
    ang = jnp.concatenate([ar, ar, ac, ac], axis=-1)
    reps = (H_A * 2 * DH_A) // DH_A
    return jnp.tile(jnp.cos(ang), (1, reps)), jnp.tile(jnp.sin(ang), (1, reps))


W_B = G_B * DG_B


def _dft_consts(s):
    def cs(n, scale):
        j = np.arange(n, dtype=np.int64)
        ang = 2.0 * np.pi * ((j[:, None] * j[None, :]) % n).astype(np.float64) / n
        return np.cos(ang) * scale, np.sin(ang) * scale
    cc, sc = cs(DG_B, DG_B ** -0.5)
    chan = np.concatenate([cc, sc], axis=1)
    cp, sp = cs(s, s ** -0.5)
    pos = np.concatenate([cp, -sp], axis=1)
    return jnp.asarray(chan, dtype=F32).astype(BF16), jnp.asarray(pos, dtype=F32).astype(BF16)


def _fourier_kernel(s, f_ref, chan_ref, pos_ref, o_ref, p_ref):
    chan = chan_ref[...]
    for g in range(G_B):
        sl = slice(g * DG_B, (g + 1) * DG_B)
        a = _dot(f_ref[:, sl], chan)
        p_ref[0:s, sl] = a[:, :DG_B].astype(BF16)
        p_ref[s:2 * s, sl] = a[:, DG_B:].astype(BF16)
    o_ref[...] = _dot(pos_ref[...], p_ref[...]).astype(BF16)


def _fourier(z, s, chan, pos):
    return pl.pallas_call(
        functools.partial(_fourier_kernel, s),
        grid=(N_TOK // s,),
        in_specs=[
            pl.BlockSpec((s, W_B), lambda b: (b, Z_FB // W_B)),
            pl.BlockSpec((DG_B, 2 * DG_B), lambda b: (0, 0)),
            pl.BlockSpec((s, 2 * s), lambda b: (0, 0)),
        ],
        out_specs=pl.BlockSpec((s, W_B), lambda b: (b, 0)),
        out_shape=jax.ShapeDtypeStruct((N_TOK, W_B), BF16),
        scratch_shapes=[pltpu.VMEM((2 * s, W_B), BF16)],
        compiler_params=_cparams(("parallel",), 40),
        name="fourier_%d" % s,
    )(z, chan, pos)


W_C = H_C * DV_C
RET_TQ = 256


def _log_sigmoid(x):
    return jnp.minimum(x, 0.0) - jnp.log1p(jnp.exp(-jnp.abs(x)))


def _head_pair_lanes(vals, pair):
    lane = lax.broadcasted_iota(jnp.int32, (1, LANES), 1)
    return jnp.where(lane < DK_C, vals[:, 2 * pair:2 * pair + 1], vals[:, 2 * pair + 1:2 * pair + 2])


def _retention_rows(s, row0, tq, q, k_ref, v_ref, g, lg, rnorm, s0f_ref, s0b_ref):
    lane = lax.broadcasted_iota(jnp.int32, (1, LANES), 1)
    ii = (row0 + lax.broadcasted_iota(jnp.int32, (tq, s), 0)).astype(F32)
    jj = lax.broadcasted_iota(jnp.int32, (tq, s), 1).astype(F32)
    rel = ii - jj
    pos = (row0 + lax.broadcasted_iota(jnp.int32, (tq, 1), 0)).astype(F32)
    outs = []
    for h in range(H_C):
        pair, half = h // 2, h % 2
        sl = slice(pair * LANES, (pair + 1) * LANES)
        lgf = lg[0:1, h:h + 1]
        lgb = lg[1:2, h:h + 1]
        in_head = (lane < DK_C) if half == 0 else (lane >= DK_C)
        q_pair = q[:, sl]
        qh = jnp.where(in_head, q_pair, jnp.zeros_like(q_pair))
        sc = _dot_nt(qh, k_ref[:, sl]) * (DK_C ** -0.5)
        decay = (jnp.where(rel >= 0, jnp.exp(lgf * jnp.maximum(rel, 0.0)), 0.0)
                 + jnp.where(rel <= 0, jnp.exp(lgb * jnp.maximum(-rel, 0.0)), 0.0))
        o = _dot((sc * decay).astype(BF16), v_ref[:, h * DV_C:(h + 1) * DV_C])
        if s0f_ref is not None:
            qf = qh.astype(F32)
            xf = jnp.exp(lgf * (pos + 1.0))
            xb = jnp.exp(lgb * (s - pos))
            o = o + _dot((qf * xf).astype(BF16), s0f_ref[0, 0, sl, :].astype(BF16))
            o = o + _dot((qf * xb).astype(BF16), s0b_ref[0, 0, sl, :].astype(BF16))
        gh = g[:, h * DV_C:(h + 1) * DV_C].astype(F32)
        outs.append((_rms(o) * rnorm) * (gh * _sigmoid(gh)))
    return outs


def _ret_ctx_kernel(q_ref, k_ref, v_ref, g_ref, dec_ref, rn_ref, *rest):
    o_ref, sf_ref, sb_ref = rest[-3:]
    s = SEQ
    lg = _log_sigmoid(dec_ref[...])
    outs = _retention_rows(s, 0, s, q_ref[...], k_ref, v_ref, g_ref[...], lg, rn_ref[...], None, None)
    for h in range(H_C):
        o_ref[:, h * DV_C:(h + 1) * DV_C] = outs[h].astype(BF16)
    pos = lax.broadcasted_iota(jnp.int32, (s, 1), 0).astype(F32)
    v = v_ref[...]
    for pair in range(H_C // 2):
        sl = slice(pair * LANES, (pair + 1) * LANES)
        kf = k_ref[:, sl].astype(F32) * (DK_C ** -0.5)
        zf = jnp.exp(_head_pair_lanes(lg[0:1, :], pair) * (s - 1.0 - pos))
        zb = jnp.exp(_head_pair_lanes(lg[1:2, :], pair) * pos)
        stf = _dot_tn((kf * zf).astype(BF16), v)
        stb = _dot_tn((kf * zb).astype(BF16), v)
        for half in range(2):
            h = 2 * pair + half
            rs = slice(half * DK_C, (half + 1) * DK_C)
            cs = slice(h * DV_C, (h + 1) * DV_C)
            sf_ref[0, 0, h] = stf[rs, cs]
            sb_ref[0, 0, h] = stb[rs, cs]


def _ret_ctx(z, dec_l, rnorm_l, layer, prev_states):
    wq = H_C * DK_C
    st_shape = jax.ShapeDtypeStruct((BATCH, DEPTH, H_C, DK_C, DV_C), F32)
    st_spec = pl.BlockSpec((1, 1, H_C, DK_C, DV_C), lambda b: (b, layer, 0, 0, 0))
    in_specs = [
        pl.BlockSpec((SEQ, wq), lambda b: (b, Z_QC // wq)),
        pl.BlockSpec((SEQ, wq), lambda b: (b, Z_KC // wq)),
        pl.BlockSpec((SEQ, W_C), lambda b: (b, Z_VC // W_C)),
        pl.BlockSpec((SEQ, W_C), lambda b: (b, Z_GC // W_C)),
        pl.BlockSpec((2, H_C), lambda b: (0, 0)),
        pl.BlockSpec((1, DV_C), lambda b: (0, 0)),
    ]
    args = [z, z, z, z, dec_l, rnorm_l.reshape(1, DV_C)]
    aliases = {}
    if prev_states is not None:
        in_specs += [pl.BlockSpec(memory_space=pl.ANY)] * 2
        args += list(prev_states)
        aliases = {6: 1, 7: 2}
    return pl.pallas_call(
        _ret_ctx_kernel,
        grid=(BATCH,),
        in_specs=in_specs,
        out_specs=[pl.BlockSpec((SEQ, W_C), lambda b: (b, 0)), st_spec, st_spec],
        out_shape=[jax.ShapeDtypeStruct((N_TOK, W_C), BF16), st_shape, st_shape],
        input_output_aliases=aliases,
        compiler_params=_cparams(("parallel",), 40),
        name="ret_ctx",
    )(*args)


def _ret_lat_kernel(q_ref, k_ref, v_ref, g_ref, dec_ref, rn_ref, s0f_ref, s0b_ref, o_ref):
    lg = _log_sigmoid(dec_ref[...])
    rnorm = rn_ref[...]

    def q_block(qb, carry):
        row0 = pl.multiple_of(qb * RET_TQ, RET_TQ)
        rows = pl.ds(row0, RET_TQ)
        outs = _retention_rows(DEC_SEQ, row0, RET_TQ, q_ref[rows, :], k_ref, v_ref, g_ref[rows, :],
                               lg, rnorm, s0f_ref, s0b_ref)
        for h in range(H_C):
            o_ref[rows, h * DV_C:(h + 1) * DV_C] = outs[h].astype(BF16)
        return carry

    lax.fori_loop(0, DEC_SEQ // RET_TQ, q_block, 0)


def _ret_lat(z, dec_l, rnorm_l, s0f4, s0b4, layer):
    wq = H_C * DK_C
    st_spec = pl.BlockSpec((1, 1, H_C * DK_C, DV_C), lambda b: (b, layer, 0, 0))
    return pl.pallas_call(
        _ret_lat_kernel,
        grid=(DEC_BATCH,),
        in_specs=[
            pl.BlockSpec((DEC_SEQ, wq), lambda b: (b, Z_QC // wq)),
            pl.BlockSpec((DEC_SEQ, wq), lambda b: (b, Z_KC // wq)),
            pl.BlockSpec((DEC_SEQ, W_C), lambda b: (b, Z_VC // W_C)),
            pl.BlockSpec((DEC_SEQ, W_C), lambda b: (b, Z_GC // W_C)),
            pl.BlockSpec((2, H_C), lambda b: (0, 0)),
            pl.BlockSpec((1, DV_C), lambda b: (0, 0)),
            st_spec, st_spec,
        ],
        out_specs=pl.BlockSpec((DEC_SEQ, W_C), lambda b: (b, 0)),
        out_shape=jax.ShapeDtypeStruct((N_TOK, W_C), BF16),
        compiler_params=_cparams(("parallel",), 40),
        name="ret_lat",
    )(z, z, z, z, dec_l, rnorm_l.reshape(1, DV_C), s0f4, s0b4)


MRG_TM = 256


def _merge_kernel(x_ref, ya_ref, yb_ref, yc_ref, gt_ref, mod_ref, wa_ref, wb_ref, wc_ref, wo_ref,
                  n2_ref, rw_ref, rb_ref, x1_ref, h2_ref, topi_ref, topw_ref, rank_ref,
                  cnt_row_ref, cnt_col_ref, carry_row, carry_col, wa_bf, wb_bf, wc_bf, wo_bf):
    d = D_MODEL
    tm = x_ref.shape[0]

    @pl.when(pl.program_id(0) == 0)
    def _():
        carry_row[...] = jnp.zeros_like(carry_row)
        carry_col[...] = jnp.zeros_like(carry_col)
        wa_bf[...] = wa_ref[0].astype(BF16)
        wb_bf[...] = wb_ref[0].astype(BF16)
        wc_bf[...] = wc_ref[0].astype(BF16)
        wo_bf[...] = wo_ref[0].astype(BF16)

    merged = _sigmoid(gt_ref[:, 0:d].astype(F32)) * _dot(ya_ref[...], wa_bf[...])
    merged += _sigmoid(gt_ref[:, d:2 * d].astype(F32)) * _dot(yb_ref[...], wb_bf[...])
    merged += _sigmoid(gt_ref[:, 2 * d:3 * d].astype(F32)) * _dot(yc_ref[...], wc_bf[...])
    y = _dot(merged.astype(BF16), wo_bf[...])
    x1 = x_ref[...] + mod_ref[0, 2:3, :] * y
    x1_ref[...] = x1
    h2 = _modnorm(x1, n2_ref[...], mod_ref[0, 4:5, :], mod_ref[0, 3:4, :])
    h2_ref[...] = h2
    logits = _dot_f32x3(h2, rw_ref[...]) + rb_ref[...]
    ids = lax.broadcasted_iota(jnp.int32, logits.shape, 1)
    work = logits
    sels, vals, firsts = [], [], []
    for _ in range(TOP_K):
        m = jnp.max(work, axis=-1, keepdims=True)
        first = jnp.min(jnp.where(work == m, ids, N_EXPERTS), axis=-1, keepdims=True)
        sel = ids == first
        sels.append(sel)
        vals.append(m)
        firsts.append(first)
        work = jnp.where(sel, -jnp.inf, work)
    es = [jnp.exp(v - vals[0]) for v in vals]
    inv = 1.0 / (es[0] + es[1] + es[2] + es[3])
    chosen = jnp.zeros_like(logits)
    for sel in sels:
        chosen = chosen + jnp.where(sel, 1.0, 0.0)
    chosen_bf = chosen.astype(BF16)
    rr = lax.broadcasted_iota(jnp.int32, (tm, tm), 0)
    cc = lax.broadcasted_iota(jnp.int32, (tm, tm), 1)
    before = jnp.where(rr > cc, 1.0, 0.0).astype(BF16)
    prefix = _dot(before, chosen_bf) + carry_row[...]
    ranks = [jnp.sum(jnp.where(sel, prefix, 0.0), axis=-1, keepdims=True) for sel in sels]
    carry_row[...] += jnp.sum(chosen, axis=0, keepdims=True)
    carry_col[...] += _dot_tn(chosen_bf, jnp.ones((tm, LANES), BF16))
    cnt_row_ref[...] = carry_row[...]
    cnt_col_ref[...] = carry_col[...]
    topi_ref[...] = jnp.concatenate(firsts, axis=1)
    topw_ref[...] = jnp.concatenate([e * inv for e in es], axis=1)
    rank_ref[...] = jnp.concatenate(ranks, axis=1).astype(jnp.int32)


def _merge(x, ya, yb, yc, z, mod_l, wa, wb, wc, wo, n2, rw, rb, layer, latent):
    tm = MRG_TM
    d = D_MODEL
    full = lambda shape: pl.BlockSpec(shape, lambda i: tuple(0 for _ in shape))
    tok = lambda w: pl.BlockSpec((tm, w), lambda i: (i, 0))
    per_layer = lambda rows: pl.BlockSpec((1, rows, d), lambda i: (layer, 0, 0))
    return pl.pallas_call(
        _merge_kernel,
        grid=(N_TOK // tm,),
        in_specs=[
            tok(d), tok(W_A), tok(W_B), tok(W_C), tok(3 * d),
            pl.BlockSpec((1, 6, d), lambda i: (_mod_row(i, tm, latent), 0, 0)),
            per_layer(W_A), per_layer(W_B), per_layer(W_C), per_layer(d),
            full((1, d)), full((d, N_EXPERTS)), full((1, N_EXPERTS)),
        ],
        out_specs=[tok(d), tok(d), tok(TOP_K), tok(TOP_K), tok(TOP_K),
                   full((1, N_EXPERTS)), full((N_EXPERTS, LANES))],
        out_shape=[
            jax.ShapeDtypeStruct((N_TOK, d), F32),
            jax.ShapeDtypeStruct((N_TOK, d), F32),
            jax.ShapeDtypeStruct((N_TOK, TOP_K), jnp.int32),
            jax.ShapeDtypeStruct((N_TOK, TOP_K), F32),
            jax.ShapeDtypeStruct((N_TOK, TOP_K), jnp.int32),
            jax.ShapeDtypeStruct((1, N_EXPERTS), F32),
            jax.ShapeDtypeStruct((N_EXPERTS, LANES), F32),
        ],
        scratch_shapes=[pltpu.VMEM((1, N_EXPERTS), F32), pltpu.VMEM((N_EXPERTS, LANES), F32),
                        pltpu.VMEM((W_A, d), BF16), pltpu.VMEM((W_B, d), BF16),
                        pltpu.VMEM((W_C, d), BF16), pltpu.VMEM((d, d), BF16)],
        compiler_params=_cparams(("arbitrary",), 56),
        name="merge_router",
    )(x, ya, yb, yc, z, mod_l, wa, wb, wc, wo, n2.reshape(1, d), rw, rb.reshape(1, N_EXPERTS))


MOE_TM = 256
MOE_TILES = (N_TOK * TOP_K) // MOE_TM + N_EXPERTS
MOE_ROWS = MOE_TILES * MOE_TM
PLAN_LANES = 256
PLAN_TM = 1024


def _plan_kernel(topi_ref, rank_ref, cnt_row_ref, cnt_col_ref, pos_ref, te_ref, nt_ref, pad_ref,
                 next_ref):
    e_r = lax.broadcasted_iota(jnp.int32, (N_EXPERTS, N_EXPERTS), 0)
    e_c = lax.broadcasted_iota(jnp.int32, (N_EXPERTS, N_EXPERTS), 1)
    cnt_row = cnt_row_ref[...]
    tiles_row = jnp.ceil(cnt_row * (1.0 / MOE_TM))
    earlier = jnp.where(e_r < e_c, 1.0, 0.0).astype(BF16)
    start_tile_row = _dot(tiles_row.astype(BF16), earlier)
    off_row = start_tile_row * MOE_TM
    ids = lax.broadcasted_iota(jnp.int32, (PLAN_TM, N_EXPERTS), 1)
    cols = []
    for k in range(TOP_K):
        hit = ids == topi_ref[:, k:k + 1]
        cols.append(jnp.sum(jnp.where(hit, off_row, 0.0), axis=-1, keepdims=True))
    pos_ref[...] = jnp.concatenate(cols, axis=1).astype(jnp.int32) + rank_ref[...]
    tiles_col = jnp.ceil(cnt_col_ref[...] * (1.0 / MOE_TM))
    upto = jnp.where(e_c <= e_r, 1.0, 0.0).astype(BF16)
    end_col = _dot(upto, tiles_col.astype(BF16))
    end_b = jnp.concatenate([end_col] * (PLAN_LANES // LANES), axis=1)
    r = lax.broadcasted_iota(jnp.int32, (N_EXPERTS, PLAN_LANES), 1).astype(F32)
    total = end_b[N_EXPERTS - 1:N_EXPERTS, :]
    te = jnp.sum(jnp.where(end_b <= jnp.minimum(r, total - 1.0), 1.0, 0.0), axis=0, keepdims=True)
    te_ref[...] = te.astype(jnp.int32)
    nt_ref[...] = total[:, :LANES].astype(jnp.int32)
    end_row = end_b * MOE_TM
    first_pad = jnp.concatenate([(end_col - tiles_col) * MOE_TM + cnt_col_ref[...]] * (PLAN_LANES // LANES),
                                axis=1)
    cand = first_pad + r
    pad_ref[...] = jnp.where(cand < end_row, cand, MOE_ROWS - 1.0).astype(jnp.int32)
    later_used = jnp.logical_and(e_r > e_c, cnt_col_ref[:, 0:N_EXPERTS] > 0.0)
    nxt = jnp.min(jnp.where(later_used, e_r, N_EXPERTS), axis=0, keepdims=True)
    next_ref[...] = jnp.where(nxt == N_EXPERTS, -1, nxt)


def _plan(topi, rank, cnt_row, cnt_col):
    tok = pl.BlockSpec((PLAN_TM, TOP_K), lambda i: (i, 0))
    full = lambda shape: pl.BlockSpec(shape, lambda i: tuple(0 for _ in shape))
    return pl.pallas_call(
        _plan_kernel,
        grid=(N_TOK // PLAN_TM,),
        in_specs=[tok, tok, full((1, N_EXPERTS)), full((N_EXPERTS, LANES))],
        out_specs=[tok, full((1, PLAN_LANES)), full((1, LANES)), full((N_EXPERTS, PLAN_LANES)),
                   full((1, N_EXPERTS))],
        out_shape=[
            jax.ShapeDtypeStruct((N_TOK, TOP_K), jnp.int32),
            jax.ShapeDtypeStruct((1, PLAN_LANES), jnp.int32),
            jax.ShapeDtypeStruct((1, LANES), jnp.int32),
            jax.ShapeDtypeStruct((N_EXPERTS, PLAN_LANES), jnp.int32),
            jax.ShapeDtypeStruct((1, N_EXPERTS), jnp.int32),
        ],
        compiler_params=_cparams(("arbitrary",), 32),
        name="route_plan",
    )(topi, rank, cnt_row, cnt_col)


SC_WORKERS = 32
SC_CORES = 2
SC_LANES = 16
DSP_CHUNK = 32
DSP_TOK = N_TOK // SC_WORKERS
DSP_NCHUNK = DSP_TOK // DSP_CHUNK
DSP_PAD_ROWS = MOE_TM // DSP_CHUNK


def _dispatch_body(h_hbm, idx_hbm, pad_hbm, xs_hbm, idx_v, pad_v, rows_a, rows_b, sem_a, sem_b):
    wid = lax.axis_index("s") * SC_CORES + lax.axis_index("c")
    pltpu.sync_copy(idx_hbm.at[wid], idx_v)
    pltpu.sync_copy(pad_hbm.at[wid], pad_v)
    zero = jnp.zeros((SC_LANES,), F32)

    @pl.loop(0, DSP_CHUNK)
    def _(row):
        @pl.loop(0, D_MODEL, step=SC_LANES)
        def _(col):
            rows_a[row, pl.ds(col, SC_LANES)] = zero

    for j in range(DSP_PAD_ROWS):
        pltpu.async_copy(rows_a, xs_hbm.at[pad_v.at[j]], sem_a)
    for j in range(DSP_PAD_ROWS):
        pltpu.make_async_copy(rows_a, xs_hbm.at[pad_v.at[j]], sem_a).wait()

    bufs = (rows_a, rows_b)
    sems = (sem_a, sem_b)

    @pl.loop(0, DSP_NCHUNK, step=2)
    def _(c0):
        for b in range(2):
            chunk = c0 + b
            pltpu.sync_copy(h_hbm.at[pl.ds(wid * DSP_TOK + chunk * DSP_CHUNK, DSP_CHUNK)], bufs[b])
            for k in range(TOP_K):
                pltpu.async_copy(bufs[b], xs_hbm.at[idx_v.at[chunk * TOP_K + k]], sems[b])
        for b in range(2):
            chunk = c0 + b
            for k in range(TOP_K):
                pltpu.make_async_copy(bufs[b], xs_hbm.at[idx_v.at[chunk * TOP_K + k]], sems[b]).wait()


def _sc_index_lists(pos):
    idx = pos.reshape(SC_WORKERS, DSP_NCHUNK, DSP_CHUNK, TOP_K).transpose(0, 1, 3, 2)
    return idx.reshape(SC_WORKERS, DSP_NCHUNK * TOP_K, DSP_CHUNK)


def _dispatch(idx, pad_rows, h2):
    pad = pad_rows.reshape(N_EXPERTS, DSP_PAD_ROWS, DSP_CHUNK)
    mesh = plsc.VectorSubcoreMesh(core_axis_name="c", subcore_axis_name="s")
    call = pl.kernel(
        _dispatch_body,
        out_type=jax.ShapeDtypeStruct((MOE_ROWS, D_MODEL), F32),
        mesh=mesh,
        scratch_types=[
            pltpu.VMEM((DSP_NCHUNK * TOP_K, DSP_CHUNK), jnp.int32),
            pltpu.VMEM((DSP_PAD_ROWS, DSP_CHUNK), jnp.int32),
            pltpu.VMEM((DSP_CHUNK, D_MODEL), F32),
            pltpu.VMEM((DSP_CHUNK, D_MODEL), F32),
            pltpu.SemaphoreType.DMA,
            pltpu.SemaphoreType.DMA,
        ],
        name="dispatch_sc",
    )
    return call(h2, idx, pad)


PAIR_BLOCKS = (2 * D_FF) // (2 * LANES)
WGU_CHUNKS = 4
WDN_CHUNKS = 2


def _prep_expert_weights(wgu_ref, wdn_ref, wgu_bf, wdn_perm, wdn_bf):
    wgu_bf[...] = wgu_ref[...].astype(BF16)
    half = LANES // 2
    for cb in range(D_MODEL // LANES):
        cols = slice(cb * LANES, (cb + 1) * LANES)
        for m in range(PAIR_BLOCKS):
            base = m * LANES
            wdn_perm[cb, pl.ds(base, half, stride=2), :] = wdn_ref[base:base + half, cols]
            wdn_perm[cb, pl.ds(base + 1, half, stride=2), :] = wdn_ref[base + half:base + LANES, cols]
        wdn_bf[:, cols] = wdn_perm[cb].astype(BF16)


def _ffn_rows(x, wgu_bf, bgu, wdn_bf, bdn):
    gu = _dot(x, wgu_bf[...]) + bgu
    lane = lax.broadcasted_iota(jnp.int32, (1, LANES), 1)
    even = (lane % 2) == 0

    def act(block):
        t = gu[:, block * LANES:(block + 1) * LANES]
        glu = jnp.minimum(t, SWIGLU_LIMIT)
        glu = glu * _sigmoid(SWIGLU_ALPHA * glu)
        lin = jnp.clip(t, -SWIGLU_LIMIT, SWIGLU_LIMIT) + 1.0
        return glu * pltpu.roll(lin, LANES - 1, 1)

    parts = []
    for m in range(PAIR_BLOCKS):
        parts.append(jnp.where(even, act(2 * m), pltpu.roll(act(2 * m + 1), 1, 1)))
    hidden = jnp.concatenate(parts, axis=1).astype(BF16)
    return _dot(hidden, wdn_bf[...]) + bdn


def _moe_kernel(layer, te_ref, nt_ref, next_ref, x_ref, wgu_hbm, bgu_ref, wdn_hbm, bdn_ref, o_ref,
                wgu_land, wdn_land, wgu_bf, wdn_perm, wdn_bf, sems):
    r = pl.program_id(0)
    valid = r < nt_ref[0]
    e = te_ref[r]
    new_expert = jnp.logical_or(r == 0, e != te_ref[jnp.maximum(r - 1, 0)])

    def fetch(expert):
        copies = []
        for c in range(WGU_CHUNKS):
            rows = pl.ds(c * (D_MODEL // WGU_CHUNKS), D_MODEL // WGU_CHUNKS)
            copies.append(pltpu.make_async_copy(wgu_hbm.at[layer, expert, rows], wgu_land.at[rows], sems.at[c]))
        for c in range(WDN_CHUNKS):
            rows = pl.ds(c * (D_FF // WDN_CHUNKS), D_FF // WDN_CHUNKS)
            copies.append(pltpu.make_async_copy(wdn_hbm.at[layer, expert, rows], wdn_land.at[rows],
                                                sems.at[WGU_CHUNKS + c]))
        return copies

    @pl.when(r == 0)
    def _():
        for cp in fetch(e):
            cp.start(priority=1)

    @pl.when(jnp.logical_and(valid, new_expert))
    def _():
        for cp in fetch(e):
            cp.wait()
        _prep_expert_weights(wgu_land, wdn_land, wgu_bf, wdn_perm, wdn_bf)
        nxt = next_ref[e]

        @pl.when(nxt >= 0)
        def _():
            for cp in fetch(nxt):
                cp.start(priority=1)

    @pl.when(valid)
    def _():
        o_ref[...] = _ffn_rows(x_ref[...].astype(BF16), wgu_bf, bgu_ref[0, 0], wdn_bf, bdn_ref[0, 0])


def _moe_grouped(tile_expert, n_tiles, next_expert, xs, w_gate_up, b_gate_up4, w_down, b_down4, layer):
    d, f2 = D_MODEL, 2 * D_FF
    row_tile = lambda r, te, nt, nx: (jnp.minimum(r, nt[0] - 1), 0)
    expert = lambda r, te, nt, nx: (layer, te[r], 0, 0)
    grid_spec = pltpu.PrefetchScalarGridSpec(
        num_scalar_prefetch=3,
        grid=(MOE_TILES,),
        in_specs=[
            pl.BlockSpec((MOE_TM, d), row_tile),
            pl.BlockSpec(memory_space=pl.ANY),
            pl.BlockSpec((1, 1, 1, f2), expert),
            pl.BlockSpec(memory_space=pl.ANY),
            pl.BlockSpec((1, 1, 1, d), expert),
        ],
        out_specs=pl.BlockSpec((MOE_TM, d), row_tile),
        scratch_shapes=[
            pltpu.VMEM((d, f2), F32),
            pltpu.VMEM((D_FF, d), F32),
            pltpu.VMEM((d, f2), BF16),
            pltpu.VMEM((d // LANES, D_FF, LANES), F32),
            pltpu.VMEM((D_FF, d), BF16),
            pltpu.SemaphoreType.DMA((WGU_CHUNKS + WDN_CHUNKS,)),
        ],
    )
    return pl.pallas_call(
        functools.partial(_moe_kernel, layer),
        grid_spec=grid_spec,
        out_shape=jax.ShapeDtypeStruct((MOE_ROWS, d), F32),
        compiler_params=_cparams(("arbitrary",), 56),
        name="moe_grouped",
    )(tile_expert, n_tiles, next_expert, xs, w_gate_up, b_gate_up4, w_down, b_down4)


def _gather_body(ys_hbm, idx_hbm, planes_hbm, idx_v, rows_a, rows_b, sem_a, sem_b):
    wid = lax.axis_index("s") * SC_CORES + lax.axis_index("c")
    pltpu.sync_copy(idx_hbm.at[wid], idx_v)
    bufs = (rows_a, rows_b)
    sems = (sem_a, sem_b)

    @pl.loop(0, DSP_NCHUNK)
    def _(chunk):
        tok0 = wid * DSP_TOK + chunk * DSP_CHUNK
        for k0 in range(0, TOP_K, 2):
            for b in range(2):
                pltpu.async_copy(ys_hbm.at[idx_v.at[chunk * TOP_K + k0 + b]], bufs[b], sems[b])
            for b in range(2):
                pltpu.make_async_copy(ys_hbm.at[idx_v.at[chunk * TOP_K + k0 + b]], bufs[b], sems[b]).wait()
                pltpu.sync_copy(bufs[b], planes_hbm.at[k0 + b, pl.ds(tok0, DSP_CHUNK)])


def _gather(idx, ys):
    mesh = plsc.VectorSubcoreMesh(core_axis_name="c", subcore_axis_name="s")
    call = pl.kernel(
        _gather_body,
        out_type=jax.ShapeDtypeStruct((TOP_K, N_TOK, D_MODEL), F32),
        mesh=mesh,
        scratch_types=[
            pltpu.VMEM((DSP_NCHUNK * TOP_K, DSP_CHUNK), jnp.int32),
            pltpu.VMEM((DSP_CHUNK, D_MODEL), F32),
            pltpu.VMEM((DSP_CHUNK, D_MODEL), F32),
            pltpu.SemaphoreType.DMA,
            pltpu.SemaphoreType.DMA,
        ],
        name="gather_sc",
    )
    return call(ys, idx)


CMB_TM = 512


def _combine_kernel(final, planes_ref, x1_ref, topw_ref, mod_ref, *rest):
    o_ref = rest[-1]
    y = topw_ref[:, 0:1] * planes_ref[0]
    for k in range(1, TOP_K):
        y = y + topw_ref[:, k:k + 1] * planes_ref[k]
    x = x1_ref[...] + mod_ref[0, 5:6, :] * y
    if final:
        x = _rms(x) * rest[0][...]
    o_ref[...] = x


def _combine(planes, x1, topw, mod_l, latent, final_g=None):
    tm, d = CMB_TM, D_MODEL
    final = final_g is not None
    in_specs = [
        pl.BlockSpec((TOP_K, tm, d), lambda i: (0, i, 0)),
        pl.BlockSpec((tm, d), lambda i: (i, 0)),
        pl.BlockSpec((tm, TOP_K), lambda i: (i, 0)),
        pl.BlockSpec((1, 6, d), lambda i: (_mod_row(i, tm, latent), 0, 0)),
    ]
    args = [planes, x1, topw, mod_l]
    if final:
        in_specs.append(pl.BlockSpec((1, d), lambda i: (0, 0)))
        args.append(final_g.reshape(1, d))
    return pl.pallas_call(
        functools.partial(_combine_kernel, final),
        grid=(N_TOK // tm,),
        in_specs=in_specs,
        out_specs=pl.BlockSpec((tm, d), lambda i: (i, 0)),
        out_shape=jax.ShapeDtypeStruct((N_TOK, d), F32),
        compiler_params=_cparams(("parallel",), 48),
        name="combine_final" if final else "combine",
    )(*args)


def kernel(x_prompt, x_sample, cache_k, cache_v, state_ret_fwd, state_ret_bwd, c, c_ctx, norm1_g, norm2_g, w_mod, b_mod, w_in, lam_a, subln_a, ret_decay, ret_norm, w_branch_a, w_branch_b, w_branch_c, w_out, router_w, router_b, w_gate_up, b_gate_up, w_down, b_down, final_g):
    d = D_MODEL
    cond8 = jnp.concatenate([c_ctx[None, :], c, jnp.zeros((3, d), F32)], axis=0)
    mod = _modulation(cond8, w_mod, b_mod)

    cache_k4 = cache_k.reshape(DEC_BATCH, DEPTH, PAST_LEN, W_A)
    cache_v4 = cache_v.reshape(DEC_BATCH, DEPTH, PAST_LEN, W_A)
    s0f4 = state_ret_fwd.reshape(DEC_BATCH, DEPTH, H_C * DK_C, DV_C)
    s0b4 = state_ret_bwd.reshape(DEC_BATCH, DEPTH, H_C * DK_C, DV_C)
    b_gate_up4 = b_gate_up.reshape(DEPTH, N_EXPERTS, 1, 2 * D_FF)
    b_down4 = b_down.reshape(DEPTH, N_EXPERTS, 1, d)
    cos, sin = _rope_tables(DEC_SEQ)
    chan, pos_ctx = _dft_consts(SEQ)
    _, pos_lat = _dft_consts(DEC_SEQ)

    def channel_mixer(l, latent, x, ya, yb, yc, z, last):
        x1, h2, topi, topw, rank, cnt_row, cnt_col = _merge(
            x, ya, yb, yc, z, mod[l], w_branch_a, w_branch_b, w_branch_c, w_out,
            norm2_g[l], router_w[l], router_b[l], l, latent)
        pos, tile_expert, n_tiles, pad_rows, next_expert = _plan(topi, rank, cnt_row, cnt_col)
        idx = _sc_index_lists(pos)
        xs = _dispatch(idx, pad_rows, h2)
        ys = _moe_grouped(tile_expert.reshape(PLAN_LANES), n_tiles[0, :1], next_expert.reshape(N_EXPERTS),
                          xs, w_gate_up, b_gate_up4, w_down, b_down4, l)
        planes = _gather(idx, ys)
        return _combine(planes, x1, topw, mod[l], latent, final_g if last else None)

    xc = x_prompt.reshape(N_TOK, d)
    xl = x_sample.reshape(N_TOK, d)
    caches, states = None, None
    for l in range(DEPTH):
        lam_init = 0.8 - 0.6 * math.exp(-0.3 * l)
        last = l == DEPTH - 1
        z, new_k, new_v = _inproj(xc, mod[l], norm1_g[l], w_in, l, False, caches)
        caches = (new_k, new_v)
        ya = _attn_ctx(z, lam_a[l], subln_a[l], lam_init)
        yb = _fourier(z, SEQ, chan, pos_ctx)
        yc, new_sf, new_sb = _ret_ctx(z, ret_decay[l], ret_norm[l], l, states)
        states = (new_sf, new_sb)
        xc = channel_mixer(l, False, xc, ya, yb, yc, z, last)
        z = _inproj(xl, mod[l], norm1_g[l], w_in, l, True)[0]
        ya = _attn_lat(z, cache_k4, cache_v4, l, cos, sin, lam_a[l], subln_a[l], lam_init)
        yb = _fourier(z, DEC_SEQ, chan, pos_lat)
        yc = _ret_lat(z, ret_decay[l], ret_norm[l], s0f4, s0b4, l)
        xl = channel_mixer(l, True, xl, ya, yb, yc, z, last)

    return (xc.reshape(BATCH, SEQ, d), xl.reshape(DEC_BATCH, DEC_SEQ, d),
            caches[0].reshape(BATCH, DEPTH, SEQ, H_A, 2, DH_A),
            caches[1].reshape(BATCH, DEPTH, SEQ, H_A, DV_A),
            states[0], states[1])
```

```python
import functools
import math

import numpy as np
import jax
import jax.numpy as jnp
from jax import lax
from jax.experimental import pallas as pl
from jax.experimental.pallas import tpu as pltpu

F32 = jnp.float32
BF16 = jnp.bfloat16

D_MODEL = 1024
BATCH = 16
SEQ = 256
DEPTH = 2
DEC_BATCH = 4
DEC_SEQ = 1024
PAST_LEN = 512
GRID_W = 64
H_A = 4
DH_A = 64
DV_A = 128
G_B = 4
DG_B = 128
H_C = 4
DK_C = 64
DV_C = 128
N_EXPERTS = 32
TOP_K = 4
D_FF = 1024
SWIGLU_LIMIT = 7.0
SWIGLU_ALPHA = 1.702
ROPE_BASE = 10000.0
EPS = 1e-6

N_CTX_TOK = BATCH * SEQ
N_LAT_TOK = DEC_BATCH * DEC_SEQ
N_TOK = N_CTX_TOK + N_LAT_TOK
IN_TOTAL = 6656
Z_QA = 0
Z_KA = 512
Z_VA = 1024
Z_FB = 1536
Z_QC = 2048
Z_KC = 2304
Z_VC = 2560
Z_GC = 3072
Z_GATES = 3584
LANES = 128
MIB = 1024 * 1024


def _cparams(semantics, vmem_mib):
    return pltpu.CompilerParams(dimension_semantics=semantics,
                                vmem_limit_bytes=vmem_mib * MIB)


def _dot(a, b):
    return jnp.dot(a, b, preferred_element_type=F32)


def _dot_nt(a, b):
    return lax.dot_general(a, b, (((1,), (1,)), ((), ())), preferred_element_type=F32)


def _dot_tn(a, b):
    return lax.dot_general(a, b, (((0,), (0,)), ((), ())), preferred_element_type=F32)


def _split_bf16(a):
    hi = a.astype(BF16)
    lo = (a - hi.astype(F32)).astype(BF16)
    return hi, lo


def _dot_f32x3(a, b):
    a_hi, a_lo = _split_bf16(a)
    b_hi, b_lo = _split_bf16(b)
    return _dot(a_hi, b_hi) + _dot(a_lo, b_hi) + _dot(a_hi, b_lo)


def _rms(x):
    return x * lax.rsqrt(jnp.mean(x * x, axis=-1, keepdims=True) + EPS)


def _modnorm(x, g, scale, shift):
    return (_rms(x) * g) * (1.0 + scale) + shift


def _sigmoid(x):
    return 1.0 / (1.0 + jnp.exp(-x))


MOD_TN = 1536


def _mod_kernel(cond_ref, w_ref, b_ref, o_ref):
    cnd = cond_ref[...]
    s = cnd * _sigmoid(cnd)
    o_ref[0] = _dot_f32x3(s, w_ref[0]) + b_ref[0]


def _modulation(cond8, w_mod, b_mod):
    n = 6 * D_MODEL
    out = pl.pallas_call(
        _mod_kernel,
        grid=(DEPTH, n // MOD_TN),
        in_specs=[
            pl.BlockSpec((8, D_MODEL), lambda l, j: (0, 0)),
            pl.BlockSpec((1, D_MODEL, MOD_TN), lambda l, j: (l, 0, j)),
            pl.BlockSpec((1, 1, MOD_TN), lambda l, j: (l, 0, j)),
        ],
        out_specs=pl.BlockSpec((1, 8, MOD_TN), lambda l, j: (l, 0, j)),
        out_shape=jax.ShapeDtypeStruct((DEPTH, 8, n), F32),
        compiler_params=_cparams(("parallel", "parallel"), 40),
        name="modulation",
    )(cond8, w_mod, b_mod.reshape(DEPTH, 1, n))
    return out.reshape(DEPTH, 8, 6, D_MODEL)


def _mod_row(tile_idx, tm):
    n_ctx_tiles = N_CTX_TOK // tm
    per_batch = DEC_SEQ // tm
    return jnp.where(tile_idx < n_ctx_tiles, 0, 1 + (tile_idx - n_ctx_tiles) // per_batch)


INP_TM = 1024
INP_TN = 1664
INP_NJ = IN_TOTAL // INP_TN
CTX_PER_TILE = INP_TM // SEQ
assert Z_VA + H_A * DV_A <= INP_TN


def _inproj_kernel(x_ref, mod_ref, g_ref, w_ref, *rest):
    z_ref, kc_ref, vc_ref, h_scr = rest[-4:]
    i = pl.program_id(0)
    j = pl.program_id(1)

    @pl.when(j == 0)
    def _():
        h = _modnorm(x_ref[...], g_ref[...], mod_ref[0, 1:2, :], mod_ref[0, 0:1, :])
        h_scr[...] = h.astype(BF16)

    z_ref[...] = _dot(h_scr[...], w_ref[0].astype(BF16)).astype(BF16)

    @pl.when(jnp.logical_and(i < N_CTX_TOK // INP_TM, j == 0))
    def _():
        wk = w_ref[0, :, Z_KA:Z_KA + W_A].astype(BF16)
        wv = w_ref[0, :, Z_VA:Z_VA + W_A].astype(BF16)
        for b in range(CTX_PER_TILE):
            hb = h_scr[b * SEQ:(b + 1) * SEQ, :]
            kc_ref[b, 0] = _dot(hb, wk)
            vc_ref[b, 0] = _dot(hb, wv)


def _inproj(x, mod_l, g, w_in, layer, prev_caches):
    ni = N_TOK // INP_TM
    n_ctx_tiles = N_CTX_TOK // INP_TM
    cache_shape = jax.ShapeDtypeStruct((BATCH, DEPTH, SEQ, W_A), F32)
    cache_spec = pl.BlockSpec((CTX_PER_TILE, 1, SEQ, W_A),
                              lambda i, j: (jnp.minimum(i, n_ctx_tiles - 1), layer, 0, 0))
    in_specs = [
        pl.BlockSpec((INP_TM, D_MODEL), lambda i, j: (i, 0)),
        pl.BlockSpec((1, 6, D_MODEL), lambda i, j: (_mod_row(i, INP_TM), 0, 0)),
        pl.BlockSpec((1, D_MODEL), lambda i, j: (0, 0)),
        pl.BlockSpec((1, D_MODEL, INP_TN), lambda i, j: (layer, 0, j)),
    ]
    args = [x, mod_l, g.reshape(1, D_MODEL), w_in]
    aliases = {}
    if prev_caches is not None:
        in_specs += [pl.BlockSpec(memory_space=pl.ANY)] * 2
        args += list(prev_caches)
        aliases = {4: 1, 5: 2}
    return pl.pallas_call(
        _inproj_kernel,
        grid=(ni, INP_NJ),
        in_specs=in_specs,
        out_specs=[pl.BlockSpec((INP_TM, INP_TN), lambda i, j: (i, j)), cache_spec, cache_spec],
        out_shape=[jax.ShapeDtypeStruct((N_TOK, IN_TOTAL), BF16), cache_shape, cache_shape],
        scratch_shapes=[pltpu.VMEM((INP_TM, D_MODEL), BF16)],
        input_output_aliases=aliases,
        compiler_params=_cparams(("arbitrary", "arbitrary"), 56),
        name="inproj",
    )(*args)


ATT_TQ = 256
W_A = H_A * DV_A


def _lambda(lam_ref, lam_init):
    la = lam_ref[...]
    l01 = jnp.sum(la[0:1, :] * la[1:2, :], axis=1, keepdims=True)
    l23 = jnp.sum(la[2:3, :] * la[3:4, :], axis=1, keepdims=True)
    return jnp.exp(l01) - jnp.exp(l23) + lam_init


def _softmax_parts(s):
    m = jnp.max(s, axis=-1, keepdims=True)
    e = jnp.exp(s - m)
    return e, 1.0 / jnp.sum(e, axis=-1, keepdims=True)


def _diff_attn_block(q, k_ref, v_ref, lam, subln, lam_init):
    lane = lax.broadcasted_iota(jnp.int32, (1, LANES), 1)
    first = lane < DH_A
    scale = jnp.asarray(DH_A ** -0.5, BF16)
    outs = []
    for h in range(H_A):
        sl = slice(h * LANES, (h + 1) * LANES)
        qh = q[:, sl]
        kh = k_ref[:, sl]
        zero = jnp.zeros_like(qh)
        qs = qh * scale
        s0 = _dot_nt(jnp.where(first, qs, zero), kh)
        s1 = _dot_nt(jnp.where(first, zero, qs), kh)
        e0, r0 = _softmax_parts(s0)
        e1, r1 = _softmax_parts(s1)
        a = e0 * r0 - (lam * r1) * e1
        o = _dot(a.astype(BF16), v_ref[:, sl])
        outs.append((_rms(o) * subln) * (1.0 - lam_init))
    return outs


def _rope(x, cos, sin):
    lane = lax.broadcasted_iota(jnp.int32, (1, x.shape[1]), 1)
    first = (lane % 32) < 16
    n = x.shape[1]
    xr = jnp.where(first, -pltpu.roll(x, n - 16, 1), pltpu.roll(x, 16, 1))
    return x * cos + xr * sin


def _attn_ctx_kernel(lam_init, q_ref, k_ref, v_ref, lam_ref, subln_ref, o_ref):
    lam = _lambda(lam_ref, lam_init)
    outs = _diff_attn_block(q_ref[...], k_ref, v_ref, lam, subln_ref[...], lam_init)
    for h in range(H_A):
        o_ref[:, h * LANES:(h + 1) * LANES] = outs[h].astype(BF16)


def _attn_ctx(z, lam_a_l, subln_l, lam_init):
    wb = W_A
    return pl.pallas_call(
        functools.partial(_attn_ctx_kernel, lam_init),
        grid=(BATCH,),
        in_specs=[
            pl.BlockSpec((SEQ, wb), lambda b: (b, Z_QA // wb)),
            pl.BlockSpec((SEQ, wb), lambda b: (b, Z_KA // wb)),
            pl.BlockSpec((SEQ, wb), lambda b: (b, Z_VA // wb)),
            pl.BlockSpec((4, DH_A), lambda b: (0, 0)),
            pl.BlockSpec((1, DV_A), lambda b: (0, 0)),
        ],
        out_specs=pl.BlockSpec((SEQ, wb), lambda b: (b, 0)),
        out_shape=jax.ShapeDtypeStruct((N_TOK, wb), BF16),
        compiler_params=_cparams(("parallel",), 40),
        name="attn_ctx",
    )(z, z, z, lam_a_l, subln_l.reshape(1, DV_A))


def _attn_lat_kernel(lam_init, q_ref, k_ref, v_ref, ck_ref, cv_ref, cos_ref, sin_ref,
                     lam_ref, subln_ref, ya_ctx_ref, o_ref, kall, vall):
    del ya_ctx_ref
    lam = _lambda(lam_ref, lam_init)
    subln = subln_ref[...]
    kall[0:DEC_SEQ, :] = _rope(k_ref[...].astype(F32), cos_ref[...], sin_ref[...]).astype(BF16)
    kall[DEC_SEQ:, :] = ck_ref[0, 0].astype(BF16)
    vall[0:DEC_SEQ, :] = v_ref[...]
    vall[DEC_SEQ:, :] = cv_ref[0, 0].astype(BF16)

    def q_block(qb, carry):
        rows = pl.ds(pl.multiple_of(qb * ATT_TQ, ATT_TQ), ATT_TQ)
        q = _rope(q_ref[rows, :].astype(F32), cos_ref[rows, :], sin_ref[rows, :]).astype(BF16)
        outs = _diff_attn_block(q, kall, vall, lam, subln, lam_init)
        for h in range(H_A):
            o_ref[rows, h * LANES:(h + 1) * LANES] = outs[h].astype(BF16)
        return carry

    lax.fori_loop(0, DEC_SEQ // ATT_TQ, q_block, 0)


def _attn_lat(z, cache_k4, cache_v4, layer, cos, sin, lam_a_l, subln_l, lam_init, ya_ctx):
    wb = W_A
    rb0 = N_CTX_TOK // DEC_SEQ
    t_all = DEC_SEQ + PAST_LEN
    return pl.pallas_call(
        functools.partial(_attn_lat_kernel, lam_init),
        grid=(DEC_BATCH,),
        in_specs=[
            pl.BlockSpec((DEC_SEQ, wb), lambda b: (rb0 + b, Z_QA // wb)),
            pl.BlockSpec((DEC_SEQ, wb), lambda b: (rb0 + b, Z_KA // wb)),
            pl.BlockSpec((DEC_SEQ, wb), lambda b: (rb0 + b, Z_VA // wb)),
            pl.BlockSpec((1, 1, PAST_LEN, wb), lambda b: (b, layer, 0, 0)),
            pl.BlockSpec((1, 1, PAST_LEN, wb), lambda b: (b, layer, 0, 0)),
            pl.BlockSpec((DEC_SEQ, wb), lambda b: (0, 0)),
            pl.BlockSpec((DEC_SEQ, wb), lambda b: (0, 0)),
            pl.BlockSpec((4, DH_A), lambda b: (0, 0)),
            pl.BlockSpec((1, DV_A), lambda b: (0, 0)),
            pl.BlockSpec(memory_space=pl.ANY),
        ],
        out_specs=pl.BlockSpec((DEC_SEQ, wb), lambda b: (rb0 + b, 0)),
        out_shape=jax.ShapeDtypeStruct((N_TOK, wb), BF16),
        scratch_shapes=[pltpu.VMEM((t_all, wb), BF16), pltpu.VMEM((t_all, wb), BF16)],
        input_output_aliases={9: 0},
        compiler_params=_cparams(("parallel",), 48),
        name="attn_lat",
    )(z, z, z, cache_k4, cache_v4, cos, sin, lam_a_l, subln_l.reshape(1, DV_A), ya_ctx)


def _rope_tables(s):
    rows = s // GRID_W
    row = jnp.repeat(jnp.arange(rows, dtype=F32), GRID_W)
    col = jnp.tile(jnp.arange(GRID_W, dtype=F32), rows)
    quarter = DH_A // 4
    inv = ROPE_BASE ** (-jnp.arange(quarter, dtype=F32) / quarter)
    ar = row[:, None] * inv
    ac = col[:, None] * inv
    ang = jnp.concatenate([ar, ar, ac, ac], axis=-1)
    reps = (H_A * 2 * DH_A) // DH_A
    return jnp.tile(jnp.cos(ang), (1, reps)), jnp.tile(jnp.sin(ang), (1, reps))


W_B = G_B * DG_B


def _dft_consts(s):
    def cs(n, scale):
        j = np.arange(n, dtype=np.int64)
        ang = 2.0 * np.pi * ((j[:, None] * j[None, :]) % n).astype(np.float64) / n
        return np.cos(ang) * scale, np.sin(ang) * scale
    cc, sc = cs(DG_B, DG_B ** -0.5)
    chan = np.concatenate([cc, sc], axis=1)
    cp, sp = cs(s, s ** -0.5)
    pos = np.concatenate([cp, -sp], axis=1)
    return jnp.asarray(chan, dtype=F32).astype(BF16), jnp.asarray(pos, dtype=F32).astype(BF16)


def _fourier_kernel(s, f_ref, chan_ref, pos_ref, *rest):
    o_ref, p_ref = rest[-2:]
    chan = chan_ref[...]
    for g in range(G_B):
        sl = slice(g * DG_B, (g + 1) * DG_B)
        a = _dot(f_ref[:, sl], chan)
        p_ref[0:s, sl] = a[:, :DG_B].astype(BF16)
        p_ref[s:2 * s, sl] = a[:, DG_B:].astype(BF16)
    o_ref[...] = _dot(pos_ref[...], p_ref[...]).astype(BF16)


def _fourier(z, s, n_batch, row_block0, chan, pos, prev=None):
    in_specs = [
        pl.BlockSpec((s, W_B), lambda b: (row_block0 + b, Z_FB // W_B)),
        pl.BlockSpec((DG_B, 2 * DG_B), lambda b: (0, 0)),
        pl.BlockSpec((s, 2 * s), lambda b: (0, 0)),
    ]
    args = [z, chan, pos]
    aliases = {}
    if prev is not None:
        in_specs.append(pl.BlockSpec(memory_space=pl.ANY))
        args.append(prev)
        aliases = {3: 0}
    return pl.pallas_call(
        functools.partial(_fourier_kernel, s),
        grid=(n_batch,),
        in_specs=in_specs,
        out_specs=pl.BlockSpec((s, W_B), lambda b: (row_block0 + b, 0)),
        out_shape=jax.ShapeDtypeStruct((N_TOK, W_B), BF16),
        scratch_shapes=[pltpu.VMEM((2 * s, W_B), BF16)],
        input_output_aliases=aliases,
        compiler_params=_cparams(("parallel",), 40),
        name="fourier_%d" % s,
    )(*args)


W_C = H_C * DV_C
RET_TQ = 256


def _log_sigmoid(x):
    return jnp.minimum(x, 0.0) - jnp.log1p(jnp.exp(-jnp.abs(x)))


def _head_pair_lanes(vals, pair):
    lane = lax.broadcasted_iota(jnp.int32, (1, LANES), 1)
    return jnp.where(lane < DK_C, vals[:, 2 * pair:2 * pair + 1], vals[:, 2 * pair + 1:2 * pair + 2])


def _retention_rows(s, row0, tq, q, k_ref, v_ref, g, lg, rnorm, s0f_ref, s0b_ref):
    lane = lax.broadcasted_iota(jnp.int32, (1, LANES), 1)
    ii = (row0 + lax.broadcasted_iota(jnp.int32, (tq, s), 0)).astype(F32)
    jj = lax.broadcasted_iota(jnp.int32, (tq, s), 1).astype(F32)
    rel = ii - jj
    pos = (row0 + lax.broadcasted_iota(jnp.int32, (tq, 1), 0)).astype(F32)
    outs = []
    for h in range(H_C):
        pair, half = h // 2, h % 2
        sl = slice(pair * LANES, (pair + 1) * LANES)
        lgf = lg[0:1, h:h + 1]
        lgb = lg[1:2, h:h + 1]
        in_head = (lane < DK_C) if half == 0 else (lane >= DK_C)
        q_pair = q[:, sl]
        qh = jnp.where(in_head, q_pair, jnp.zeros_like(q_pair))
        sc = _dot_nt(qh, k_ref[:, sl]) * (DK_C ** -0.5)
        decay = (jnp.where(rel >= 0, jnp.exp(lgf * jnp.maximum(rel, 0.0)), 0.0)
                 + jnp.where(rel <= 0, jnp.exp(lgb * jnp.maximum(-rel, 0.0)), 0.0))
        o = _dot((sc * decay).astype(BF16), v_ref[:, h * DV_C:(h + 1) * DV_C])
        if s0f_ref is not None:
            qf = qh.astype(F32)
            xf = jnp.exp(lgf * (pos + 1.0))
            xb = jnp.exp(lgb * (s - pos))
            o = o + _dot((qf * xf).astype(BF16), s0f_ref[0, 0, sl, :].astype(BF16))
            o = o + _dot((qf * xb).astype(BF16), s0b_ref[0, 0, sl, :].astype(BF16))
        gh = g[:, h * DV_C:(h + 1) * DV_C].astype(F32)
        outs.append((_rms(o) * rnorm) * (gh * _sigmoid(gh)))
    return outs


def _ret_ctx_kernel(q_ref, k_ref, v_ref, g_ref, dec_ref, rn_ref, *rest):
    o_ref, sf_ref, sb_ref = rest[-3:]
    s = SEQ
    lg = _log_sigmoid(dec_ref[...])
    outs = _retention_rows(s, 0, s, q_ref[...], k_ref, v_ref, g_ref[...], lg, rn_ref[...], None, None)
    for h in range(H_C):
        o_ref[:, h * DV_C:(h + 1) * DV_C] = outs[h].astype(BF16)
    pos = lax.broadcasted_iota(jnp.int32, (s, 1), 0).astype(F32)
    v = v_ref[...]
    for pair in range(H_C // 2):
        sl = slice(pair * LANES, (pair + 1) * LANES)
        kf = k_ref[:, sl].astype(F32) * (DK_C ** -0.5)
        zf = jnp.exp(_head_pair_lanes(lg[0:1, :], pair) * (s - 1.0 - pos))
        zb = jnp.exp(_head_pair_lanes(lg[1:2, :], pair) * pos)
        stf = _dot_tn((kf * zf).astype(BF16), v)
        stb = _dot_tn((kf * zb).astype(BF16), v)
        for half in range(2):
            h = 2 * pair + half
            rs = slice(half * DK_C, (half + 1) * DK_C)
            cs = slice(h * DV_C, (h + 1) * DV_C)
            sf_ref[0, 0, h] = stf[rs, cs]
            sb_ref[0, 0, h] = stb[rs, cs]


def _ret_ctx(z, dec_l, rnorm_l, layer, prev_states):
    wq = H_C * DK_C
    st_shape = jax.ShapeDtypeStruct((BATCH, DEPTH, H_C, DK_C, DV_C), F32)
    st_spec = pl.BlockSpec((1, 1, H_C, DK_C, DV_C), lambda b: (b, layer, 0, 0, 0))
    in_specs = [
        pl.BlockSpec((SEQ, wq), lambda b: (b, Z_QC // wq)),
        pl.BlockSpec((SEQ, wq), lambda b: (b, Z_KC // wq)),
        pl.BlockSpec((SEQ, W_C), lambda b: (b, Z_VC // W_C)),
        pl.BlockSpec((SEQ, W_C), lambda b: (b, Z_GC // W_C)),
        pl.BlockSpec((2, H_C), lambda b: (0, 0)),
        pl.BlockSpec((1, DV_C), lambda b: (0, 0)),
    ]
    args = [z, z, z, z, dec_l, rnorm_l.reshape(1, DV_C)]
    aliases = {}
    if prev_states is not None:
        in_specs += [pl.BlockSpec(memory_space=pl.ANY)] * 2
        args += list(prev_states)
        aliases = {6: 1, 7: 2}
    return pl.pallas_call(
        _ret_ctx_kernel,
        grid=(BATCH,),
        in_specs=in_specs,
        out_specs=[pl.BlockSpec((SEQ, W_C), lambda b: (b, 0)), st_spec, st_spec],
        out_shape=[jax.ShapeDtypeStruct((N_TOK, W_C), BF16), st_shape, st_shape],
        input_output_aliases=aliases,
        compiler_params=_cparams(("parallel",), 40),
        name="ret_ctx",
    )(*args)


def _ret_lat_kernel(q_ref, k_ref, v_ref, g_ref, dec_ref, rn_ref, s0f_ref, s0b_ref, yc_ctx_ref, o_ref):
    del yc_ctx_ref
    lg = _log_sigmoid(dec_ref[...])
    rnorm = rn_ref[...]

    def q_block(qb, carry):
        row0 = pl.multiple_of(qb * RET_TQ, RET_TQ)
        rows = pl.ds(row0, RET_TQ)
        outs = _retention_rows(DEC_SEQ, row0, RET_TQ, q_ref[rows, :], k_ref, v_ref, g_ref[rows, :],
                               lg, rnorm, s0f_ref, s0b_ref)
        for h in range(H_C):
            o_ref[rows, h * DV_C:(h + 1) * DV_C] = outs[h].astype(BF16)
        return carry

    lax.fori_loop(0, DEC_SEQ // RET_TQ, q_block, 0)


def _ret_lat(z, dec_l, rnorm_l, s0f4, s0b4, layer, yc_ctx):
    wq = H_C * DK_C
    rb0 = N_CTX_TOK // DEC_SEQ
    st_spec = pl.BlockSpec((1, 1, H_C * DK_C, DV_C), lambda b: (b, layer, 0, 0))
    return pl.pallas_call(
        _ret_lat_kernel,
        grid=(DEC_BATCH,),
        in_specs=[
            pl.BlockSpec((DEC_SEQ, wq), lambda b: (rb0 + b, Z_QC // wq)),
            pl.BlockSpec((DEC_SEQ, wq), lambda b: (rb0 + b, Z_KC // wq)),
            pl.BlockSpec((DEC_SEQ, W_C), lambda b: (rb0 + b, Z_VC // W_C)),
            pl.BlockSpec((DEC_SEQ, W_C), lambda b: (rb0 + b, Z_GC // W_C)),
            pl.BlockSpec((2, H_C), lambda b: (0, 0)),
            pl.BlockSpec((1, DV_C), lambda b: (0, 0)),
            st_spec, st_spec,
            pl.BlockSpec(memory_space=pl.ANY),
        ],
        out_specs=pl.BlockSpec((DEC_SEQ, W_C), lambda b: (rb0 + b, 0)),
        out_shape=jax.ShapeDtypeStruct((N_TOK, W_C), BF16),
        input_output_aliases={8: 0},
        compiler_params=_cparams(("parallel",), 40),
        name="ret_lat",
    )(z, z, z, z, dec_l, rnorm_l.reshape(1, DV_C), s0f4, s0b4, yc_ctx)


MRG_TM = 256
GATE_BLOCK = 512


def _merge_kernel(x_ref, ya_ref, yb_ref, yc_ref, g0_ref, g1_ref, g2_ref, g3_ref, g4_ref, g5_ref,
                  mod_ref, wa_ref, wb_ref, wc_ref, wo_ref,
                  n2_ref, rw_ref, rb_ref, x1_ref, h2_ref, topi_ref, topw_ref, rank_ref,
                  cnt_row_ref, cnt_col_ref, carry_row, carry_col, wa_bf, wb_bf, wc_bf, wo_bf):
    d = D_MODEL
    tm = x_ref.shape[0]

    @pl.when(pl.program_id(0) == 0)
    def _():
        carry_row[...] = jnp.zeros_like(carry_row)
        carry_col[...] = jnp.zeros_like(carry_col)
        wa_bf[...] = wa_ref[0].astype(BF16)
        wb_bf[...] = wb_ref[0].astype(BF16)
        wc_bf[...] = wc_ref[0].astype(BF16)
        wo_bf[...] = wo_ref[0].astype(BF16)

    def gate(lo_ref, hi_ref):
        return _sigmoid(jnp.concatenate([lo_ref[...], hi_ref[...]], axis=1).astype(F32))

    merged = gate(g0_ref, g1_ref) * _dot(ya_ref[...], wa_bf[...])
    merged += gate(g2_ref, g3_ref) * _dot(yb_ref[...], wb_bf[...])
    merged += gate(g4_ref, g5_ref) * _dot(yc_ref[...], wc_bf[...])
    y = _dot(merged.astype(BF16), wo_bf[...])
    x1 = x_ref[...] + mod_ref[0, 2:3, :] * y
    x1_ref[...] = x1
    h2 = _modnorm(x1, n2_ref[...], mod_ref[0, 4:5, :], mod_ref[0, 3:4, :])
    h2_ref[...] = h2
    logits = _dot_f32x3(h2, rw_ref[...]) + rb_ref[...]
    ids = lax.broadcasted_iota(jnp.int32, logits.shape, 1)
    work = logits
    sels, vals, firsts = [], [], []
    for _ in range(TOP_K):
        m = jnp.max(work, axis=-1, keepdims=True)
        first = jnp.min(jnp.where(work == m, ids, N_EXPERTS), axis=-1, keepdims=True)
        sel = ids == first
        sels.append(sel)
        vals.append(m)
        firsts.append(first)
        work = jnp.where(sel, -jnp.inf, work)
    es = [jnp.exp(v - vals[0]) for v in vals]
    inv = 1.0 / (es[0] + es[1] + es[2] + es[3])
    chosen = jnp.zeros_like(logits)
    for sel in sels:
        chosen = chosen + jnp.where(sel, 1.0, 0.0)
    chosen_bf = chosen.astype(BF16)
    rr = lax.broadcasted_iota(jnp.int32, (tm, tm), 0)
    cc = lax.broadcasted_iota(jnp.int32, (tm, tm), 1)
    before = jnp.where(rr > cc, 1.0, 0.0).astype(BF16)
    prefix = _dot(before, chosen_bf) + carry_row[...]
    ranks = [jnp.sum(jnp.where(sel, prefix, 0.0), axis=-1, keepdims=True) for sel in sels]
    carry_row[...] += jnp.sum(chosen, axis=0, keepdims=True)
    carry_col[...] += _dot_tn(chosen_bf, jnp.ones((tm, LANES), BF16))
    cnt_row_ref[...] = carry_row[...]
    cnt_col_ref[...] = carry_col[...]
    topi_ref[...] = jnp.concatenate(firsts, axis=1)
    topw_ref[...] = jnp.concatenate([e * inv for e in es], axis=1)
    rank_ref[...] = jnp.concatenate(ranks, axis=1).astype(jnp.int32)


def _merge(x, ya, yb, yc, z, mod_l, wa, wb, wc, wo, n2, rw, rb, layer):
    tm = MRG_TM
    d = D_MODEL
    full = lambda shape: pl.BlockSpec(shape, lambda i: tuple(0 for _ in shape))
    tok = lambda w: pl.BlockSpec((tm, w), lambda i: (i, 0))
    per_layer = lambda rows: pl.BlockSpec((1, rows, d), lambda i: (layer, 0, 0))
    return pl.pallas_call(
        _merge_kernel,
        grid=(N_TOK // tm,),
        in_specs=[
            tok(d), tok(W_A), tok(W_B), tok(W_C),
            *[pl.BlockSpec((tm, GATE_BLOCK), lambda i, c=c: (i, Z_GATES // GATE_BLOCK + c)) for c in range(6)],
            pl.BlockSpec((1, 6, d), lambda i: (_mod_row(i, tm), 0, 0)),
            per_layer(W_A), per_layer(W_B), per_layer(W_C), per_layer(d),
            full((1, d)), full((d, N_EXPERTS)), full((1, N_EXPERTS)),
        ],
        out_specs=[tok(d), tok(d), tok(TOP_K), tok(TOP_K), tok(TOP_K),
                   full((1, N_EXPERTS)), full((N_EXPERTS, LANES))],
        out_shape=[
            jax.ShapeDtypeStruct((N_TOK, d), F32),
            jax.ShapeDtypeStruct((N_TOK, d), F32),
            jax.ShapeDtypeStruct((N_TOK, TOP_K), jnp.int32),
            jax.ShapeDtypeStruct((N_TOK, TOP_K), F32),
            jax.ShapeDtypeStruct((N_TOK, TOP_K), jnp.int32),
            jax.ShapeDtypeStruct((1, N_EXPERTS), F32),
            jax.ShapeDtypeStruct((N_EXPERTS, LANES), F32),
        ],
        scratch_shapes=[pltpu.VMEM((1, N_EXPERTS), F32), pltpu.VMEM((N_EXPERTS, LANES), F32),
                        pltpu.VMEM((W_A, d), BF16), pltpu.VMEM((W_B, d), BF16),
                        pltpu.VMEM((W_C, d), BF16), pltpu.VMEM((d, d), BF16)],
        compiler_params=_cparams(("arbitrary",), 56),
        name="merge_router",
    )(x, ya, yb, yc, z, z, z, z, z, z, mod_l, wa, wb, wc, wo, n2.reshape(1, d), rw, rb.reshape(1, N_EXPERTS))


MOE_TM = 256
MOE_TILES = (N_TOK * TOP_K) // MOE_TM + N_EXPERTS
MOE_ROWS = MOE_TILES * MOE_TM
PLAN_LANES = 256
PLAN_TM = 1024


def _plan_kernel(topi_ref, rank_ref, cnt_row_ref, cnt_col_ref, pos_ref, te_ref, nt_ref, last_ref,
                 next_ref):
    e_r = lax.broadcasted_iota(jnp.int32, (N_EXPERTS, N_EXPERTS), 0)
    e_c = lax.broadcasted_iota(jnp.int32, (N_EXPERTS, N_EXPERTS), 1)
    cnt_row = cnt_row_ref[...]
    tiles_row = jnp.ceil(cnt_row * (1.0 / MOE_TM))
    earlier = jnp.where(e_r < e_c, 1.0, 0.0).astype(BF16)
    start_tile_row = _dot(tiles_row.astype(BF16), earlier)
    off_row = start_tile_row * MOE_TM
    ids = lax.broadcasted_iota(jnp.int32, (PLAN_TM, N_EXPERTS), 1)
    cols = []
    for k in range(TOP_K):
        hit = ids == topi_ref[:, k:k + 1]
        cols.append(jnp.sum(jnp.where(hit, off_row, 0.0), axis=-1, keepdims=True))
    pos_ref[...] = jnp.concatenate(cols, axis=1).astype(jnp.int32) + rank_ref[...]
    tiles_col = jnp.ceil(cnt_col_ref[...] * (1.0 / MOE_TM))
    upto = jnp.where(e_c <= e_r, 1.0, 0.0).astype(BF16)
    end_col = _dot(upto, tiles_col.astype(BF16))
    end_b = jnp.concatenate([end_col] * (PLAN_LANES // LANES), axis=1)
    r = lax.broadcasted_iota(jnp.int32, (N_EXPERTS, PLAN_LANES), 1).astype(F32)
    total = end_b[N_EXPERTS - 1:N_EXPERTS, :]
    te = jnp.sum(jnp.where(end_b <= jnp.minimum(r, total - 1.0), 1.0, 0.0), axis=0, keepdims=True)
    te_ref[...] = te.astype(jnp.int32)
    nt_ref[...] = total[:, :LANES].astype(jnp.int32)
    last = jnp.where(cnt_row > 0.0, (start_tile_row + tiles_row - 1.0) * MOE_TM, -1.0)
    last_ref[...] = last.astype(jnp.int32)
    later_used = jnp.logical_and(e_r > e_c, cnt_col_ref[:, 0:N_EXPERTS] > 0.0)
    nxt = jnp.min(jnp.where(later_used, e_r, N_EXPERTS), axis=0, keepdims=True)
    next_ref[...] = jnp.where(nxt == N_EXPERTS, -1, nxt)


def _plan(topi, rank, cnt_row, cnt_col):
    tok = pl.BlockSpec((PLAN_TM, TOP_K), lambda i: (i, 0))
    full = lambda shape: pl.BlockSpec(shape, lambda i: tuple(0 for _ in shape))
    return pl.pallas_call(
        _plan_kernel,
        grid=(N_TOK // PLAN_TM,),
        in_specs=[tok, tok, full((1, N_EXPERTS)), full((N_EXPERTS, LANES))],
        out_specs=[tok, full((1, PLAN_LANES)), full((1, LANES)), full((1, N_EXPERTS)),
                   full((1, N_EXPERTS))],
        out_shape=[
            jax.ShapeDtypeStruct((N_TOK, TOP_K), jnp.int32),
            jax.ShapeDtypeStruct((1, PLAN_LANES), jnp.int32),
            jax.ShapeDtypeStruct((1, LANES), jnp.int32),
            jax.ShapeDtypeStruct((1, N_EXPERTS), jnp.int32),
            jax.ShapeDtypeStruct((1, N_EXPERTS), jnp.int32),
        ],
        compiler_params=_cparams(("arbitrary",), 32),
        name="route_plan",
    )(topi, rank, cnt_row, cnt_col)


DSP_TM = 256


def _dispatch_kernel(pos_ref, last_ref, h_ref, xs_ref, zero_buf, zsem, sem):
    i = pl.program_id(0)

    @pl.when(i == 0)
    def _():
        zero_buf[...] = jnp.zeros_like(zero_buf)

        def zero_copy(e):
            return pltpu.make_async_copy(zero_buf, xs_ref.at[pl.ds(pl.multiple_of(last_ref[0, e], MOE_TM), MOE_TM)], zsem)

        for e in range(N_EXPERTS):
            @pl.when(last_ref[0, e] >= 0)
            def _():
                zero_copy(e).start()
        for e in range(N_EXPERTS):
            @pl.when(last_ref[0, e] >= 0)
            def _():
                zero_copy(e).wait()

    def row_copy(t, k):
        return pltpu.make_async_copy(h_ref.at[pl.ds(t, 1)],
                                     xs_ref.at[pl.ds(pos_ref[0, 0, t * TOP_K + k], 1)], sem)

    def issue(t, carry):
        for k in range(TOP_K):
            row_copy(t, k).start()
        return carry

    lax.fori_loop(0, DSP_TM, issue, 0, unroll=8)

    for k in range(TOP_K):
        pltpu.make_async_copy(h_ref, xs_ref.at[pl.ds(0, DSP_TM)], sem).wait()


def _dispatch(pos, last, h2):
    n_tiles = N_TOK // DSP_TM
    return pl.pallas_call(
        _dispatch_kernel,
        grid=(n_tiles,),
        in_specs=[
            pl.BlockSpec((1, 1, DSP_TM * TOP_K), lambda i: (i, 0, 0), memory_space=pltpu.SMEM),
            pl.BlockSpec((1, N_EXPERTS), lambda i: (0, 0), memory_space=pltpu.SMEM),
            pl.BlockSpec((DSP_TM, D_MODEL), lambda i: (i, 0)),
        ],
        out_specs=pl.BlockSpec(memory_space=pl.ANY),
        out_shape=jax.ShapeDtypeStruct((MOE_ROWS, D_MODEL), F32),
        scratch_shapes=[pltpu.VMEM((MOE_TM, D_MODEL), F32), pltpu.SemaphoreType.DMA,
                        pltpu.SemaphoreType.DMA],
        compiler_params=_cparams(("arbitrary",), 32),
        name="dispatch",
    )(pos.reshape(n_tiles, 1, DSP_TM * TOP_K), last, h2)


PAIR_BLOCKS = (2 * D_FF) // (2 * LANES)
WEIGHT_PARTS = 3


def _prep_expert_weights(wgu_ref, wdn_ref, wgu_bf, wdn_perm, wdn_bf):
    wgu_bf[...] = wgu_ref[...].astype(BF16)
    half = LANES // 2
    for cb in range(D_MODEL // LANES):
        cols = slice(cb * LANES, (cb + 1) * LANES)
        for m in range(PAIR_BLOCKS):
            base = m * LANES
            wdn_perm[cb, pl.ds(base, half, stride=2), :] = wdn_ref[base:base + half, cols]
            wdn_perm[cb, pl.ds(base + 1, half, stride=2), :] = wdn_ref[base + half:base + LANES, cols]
        wdn_bf[:, cols] = wdn_perm[cb].astype(BF16)


def _ffn_rows(x, wgu_bf, bgu, wdn_bf, bdn):
    gu = _dot(x, wgu_bf[...]) + bgu
    lane = lax.broadcasted_iota(jnp.int32, (1, LANES), 1)
    even = (lane % 2) == 0

    def act(block):
        t = gu[:, block * LANES:(block + 1) * LANES]
        glu = jnp.minimum(t, SWIGLU_LIMIT)
        glu = glu * _sigmoid(SWIGLU_ALPHA * glu)
        lin = jnp.clip(t, -SWIGLU_LIMIT, SWIGLU_LIMIT) + 1.0
        return glu * pltpu.roll(lin, LANES - 1, 1)

    parts = []
    for m in range(PAIR_BLOCKS):
        parts.append(jnp.where(even, act(2 * m), pltpu.roll(act(2 * m + 1), 1, 1)))
    hidden = jnp.concatenate(parts, axis=1).astype(BF16)
    return _dot(hidden, wdn_bf[...]) + bdn


def _moe_kernel(layer, te_ref, nt_ref, next_ref, x_ref, wgu_hbm, bgu_ref, wdn_hbm, bdn_ref, o_ref,
                wgu_land, wdn_land, wgu_bf, wdn_perm, wdn_bf, issued, sems):
    r = pl.program_id(0)
    valid = r < nt_ref[0]
    e = te_ref[r]
    new_expert = jnp.logical_or(r == 0, e != te_ref[jnp.maximum(r - 1, 0)])
    half = D_MODEL // 2

    def part(expert, p):
        if p == 0:
            return pltpu.make_async_copy(wgu_hbm.at[layer, expert, pl.ds(0, half)],
                                         wgu_land.at[pl.ds(0, half)], sems.at[0])
        if p == 1:
            return pltpu.make_async_copy(wgu_hbm.at[layer, expert, pl.ds(half, half)],
                                         wgu_land.at[pl.ds(half, half)], sems.at[1])
        return pltpu.make_async_copy(wdn_hbm.at[layer, expert], wdn_land, sems.at[2])

    @pl.when(r == 0)
    def _():
        issued[0] = 0

    @pl.when(jnp.logical_and(valid, new_expert))
    def _():
        for p in range(WEIGHT_PARTS):
            @pl.when(issued[0] <= p)
            def _():
                part(e, p).start()
        for p in range(WEIGHT_PARTS):
            part(e, p).wait()
        _prep_expert_weights(wgu_land, wdn_land, wgu_bf, wdn_perm, wdn_bf)
        issued[0] = 0

    @pl.when(valid)
    def _():
        nxt = next_ref[e]
        for p in range(WEIGHT_PARTS):
            @pl.when(jnp.logical_and(nxt >= 0, issued[0] == p))
            def _():
                part(nxt, p).start()

        @pl.when(jnp.logical_and(nxt >= 0, issued[0] < WEIGHT_PARTS))
        def _():
            issued[0] = issued[0] + 1

        o_ref[...] = _ffn_rows(x_ref[...].astype(BF16), wgu_bf, bgu_ref[0, 0], wdn_bf, bdn_ref[0, 0])


def _moe_grouped(tile_expert, n_tiles, next_expert, xs, w_gate_up, b_gate_up4, w_down, b_down4, layer):
    d, f2 = D_MODEL, 2 * D_FF
    row_tile = lambda r, te, nt, nx: (jnp.minimum(r, nt[0] - 1), 0)
    expert = lambda r, te, nt, nx: (layer, te[r], 0, 0)
    grid_spec = pltpu.PrefetchScalarGridSpec(
        num_scalar_prefetch=3,
        grid=(MOE_TILES,),
        in_specs=[
            pl.BlockSpec((MOE_TM, d), row_tile),
            pl.BlockSpec(memory_space=pl.ANY),
            pl.BlockSpec((1, 1, 1, f2), expert),
            pl.BlockSpec(memory_space=pl.ANY),
            pl.BlockSpec((1, 1, 1, d), expert),
        ],
        out_specs=pl.BlockSpec((MOE_TM, d), row_tile),
        scratch_shapes=[
            pltpu.VMEM((d, f2), F32),
            pltpu.VMEM((D_FF, d), F32),
            pltpu.VMEM((d, f2), BF16),
            pltpu.VMEM((d // LANES, D_FF, LANES), F32),
            pltpu.VMEM((D_FF, d), BF16),
            pltpu.SMEM((1,), jnp.int32),
            pltpu.SemaphoreType.DMA((WEIGHT_PARTS,)),
        ],
    )
    return pl.pallas_call(
        functools.partial(_moe_kernel, layer),
        grid_spec=grid_spec,
        out_shape=jax.ShapeDtypeStruct((MOE_ROWS, d), F32),
        compiler_params=_cparams(("arbitrary",), 56),
        name="moe_grouped",
    )(tile_expert, n_tiles, next_expert, xs, w_gate_up, b_gate_up4, w_down, b_down4)


CMB_TM = 256


def _combine_kernel(final, pos_ref, ys_ref, x1_ref, topw_ref, mod_ref, *rest):
    buf, sem = rest[-2:]

    def row_copy(t, k):
        return pltpu.make_async_copy(ys_ref.at[pl.ds(pos_ref[0, 0, t * TOP_K + k], 1)],
                                     buf.at[k, pl.ds(t, 1)], sem)

    def issue(t, carry):
        for k in range(TOP_K):
            row_copy(t, k).start()
        return carry

    lax.fori_loop(0, CMB_TM, issue, 0, unroll=8)
    for k in range(TOP_K):
        pltpu.make_async_copy(ys_ref.at[pl.ds(0, CMB_TM)], buf.at[k], sem).wait()
    y = topw_ref[:, 0:1] * buf[0]
    for k in range(1, TOP_K):
        y = y + topw_ref[:, k:k + 1] * buf[k]
    x = x1_ref[...] + mod_ref[0, 5:6, :] * y
    if not final:
        rest[0][...] = x
    else:
        fg_ref, o_ctx_ref, o_lat_ref = rest[:3]
        out = _rms(x) * fg_ref[...]
        is_ctx = pl.program_id(0) < N_CTX_TOK // CMB_TM

        @pl.when(is_ctx)
        def _():
            o_ctx_ref[...] = out

        @pl.when(jnp.logical_not(is_ctx))
        def _():
            o_lat_ref[...] = out


def _combine(pos, ys, x1, topw, mod_l, final_g=None):
    tm, d = CMB_TM, D_MODEL
    n_tiles = N_TOK // tm
    n_ctx = N_CTX_TOK // tm
    final = final_g is not None
    in_specs = [
        pl.BlockSpec((1, 1, tm * TOP_K), lambda i: (i, 0, 0), memory_space=pltpu.SMEM),
        pl.BlockSpec(memory_space=pl.ANY),
        pl.BlockSpec((tm, d), lambda i: (i, 0)),
        pl.BlockSpec((tm, TOP_K), lambda i: (i, 0)),
        pl.BlockSpec((1, 6, d), lambda i: (_mod_row(i, tm), 0, 0)),
    ]
    args = [pos.reshape(n_tiles, 1, tm * TOP_K), ys, x1, topw, mod_l]
    if final:
        in_specs.append(pl.BlockSpec((1, d), lambda i: (0, 0)))
        args.append(final_g.reshape(1, d))
        out_specs = [pl.BlockSpec((tm, d), lambda i: (jnp.minimum(i, n_ctx - 1), 0)),
                     pl.BlockSpec((tm, d), lambda i: (jnp.maximum(i - n_ctx, 0), 0))]
        out_shape = [jax.ShapeDtypeStruct((N_CTX_TOK, d), F32), jax.ShapeDtypeStruct((N_LAT_TOK, d), F32)]
    else:
        out_specs = pl.BlockSpec((tm, d), lambda i: (i, 0))
        out_shape = jax.ShapeDtypeStruct((N_TOK, d), F32)
    return pl.pallas_call(
        functools.partial(_combine_kernel, final),
        grid=(n_tiles,),
        in_specs=in_specs,
        out_specs=out_specs,
        out_shape=out_shape,
        scratch_shapes=[pltpu.VMEM((TOP_K, tm, d), F32), pltpu.SemaphoreType.DMA],
        compiler_params=_cparams(("arbitrary",), 32),
        name="combine_final" if final else "combine",
    )(*args)


def kernel(x_prompt, x_sample, cache_k, cache_v, state_ret_fwd, state_ret_bwd, c, c_ctx, norm1_g, norm2_g, w_mod, b_mod, w_in, lam_a, subln_a, ret_decay, ret_norm, w_branch_a, w_branch_b, w_branch_c, w_out, router_w, router_b, w_gate_up, b_gate_up, w_down, b_down, final_g):
    d = D_MODEL
    x = jnp.concatenate([x_prompt.reshape(N_CTX_TOK, d), x_sample.reshape(N_LAT_TOK, d)], axis=0)
    cond8 = jnp.concatenate([c_ctx[None, :], c, jnp.zeros((3, d), F32)], axis=0)
    mod = _modulation(cond8, w_mod, b_mod)

    cache_k4 = cache_k.reshape(DEC_BATCH, DEPTH, PAST_LEN, W_A)
    cache_v4 = cache_v.reshape(DEC_BATCH, DEPTH, PAST_LEN, W_A)
    s0f4 = state_ret_fwd.reshape(DEC_BATCH, DEPTH, H_C * DK_C, DV_C)
    s0b4 = state_ret_bwd.reshape(DEC_BATCH, DEPTH, H_C * DK_C, DV_C)
    b_gate_up4 = b_gate_up.reshape(DEPTH, N_EXPERTS, 1, 2 * D_FF)
    b_down4 = b_down.reshape(DEPTH, N_EXPERTS, 1, d)
    cos, sin = _rope_tables(DEC_SEQ)
    chan, pos_ctx = _dft_consts(SEQ)
    _, pos_lat = _dft_consts(DEC_SEQ)

    caches, states = None, None
    for l in range(DEPTH):
        lam_init = 0.8 - 0.6 * math.exp(-0.3 * l)
        z, new_k, new_v = _inproj(x, mod[l], norm1_g[l], w_in, l, caches)
        caches = (new_k, new_v)
        ya = _attn_ctx(z, lam_a[l], subln_a[l], lam_init)
        ya = _attn_lat(z, cache_k4, cache_v4, l, cos, sin, lam_a[l], subln_a[l], lam_init, ya)
        yb = _fourier(z, SEQ, BATCH, 0, chan, pos_ctx)
        yb = _fourier(z, DEC_SEQ, DEC_BATCH, N_CTX_TOK // DEC_SEQ, chan, pos_lat, yb)
        yc, new_sf, new_sb = _ret_ctx(z, ret_decay[l], ret_norm[l], l, states)
        states = (new_sf, new_sb)
        yc = _ret_lat(z, ret_decay[l], ret_norm[l], s0f4, s0b4, l, yc)
        x1, h2, topi, topw, rank, cnt_row, cnt_col = _merge(
            x, ya, yb, yc, z, mod[l], w_branch_a, w_branch_b, w_branch_c, w_out,
            norm2_g[l], router_w[l], router_b[l], l)
        pos, tile_expert, n_tiles, last, next_expert = _plan(topi, rank, cnt_row, cnt_col)
        xs = _dispatch(pos, last, h2)
        ys = _moe_grouped(tile_expert.reshape(PLAN_LANES), n_tiles[0, :1], next_expert.reshape(N_EXPERTS),
                          xs, w_gate_up, b_gate_up4, w_down, b_down4, l)
        if l < DEPTH - 1:
            x = _combine(pos, ys, x1, topw, mod[l])
        else:
            y_ctx, y_lat = _combine(pos, ys, x1, topw, mod[l], final_g)

    return (y_ctx.reshape(BATCH, SEQ, d), y_lat.reshape(DEC_BATCH, DEC_SEQ, d),
            caches[0].reshape(BATCH, DEPTH, SEQ, H_A, 2, DH_A),
            caches[1].reshape(BATCH, DEPTH, SEQ, H_A, DV_A),
            states[0], states[1])
```

```python
import functools
import math

import numpy as np
import jax
import jax.numpy as jnp
from jax import lax
from jax.experimental import pallas as pl
from jax.experimental.pallas import tpu as pltpu

F32 = jnp.float32
BF16 = jnp.bfloat16

D_MODEL = 1024
BATCH = 16
SEQ = 256
DEPTH = 2
DEC_BATCH = 4
DEC_SEQ = 1024
PAST_LEN = 512
GRID_W = 64
H_A = 4
DH_A = 64
DV_A = 128
G_B = 4
DG_B = 128
H_C = 4
DK_C = 64
DV_C = 128
N_EXPERTS = 32
TOP_K = 4
D_FF = 1024
SWIGLU_LIMIT = 7.0
SWIGLU_ALPHA = 1.702
ROPE_BASE = 10000.0
EPS = 1e-6

N_CTX_TOK = BATCH * SEQ
N_LAT_TOK = DEC_BATCH * DEC_SEQ
N_TOK = N_CTX_TOK + N_LAT_TOK
IN_TOTAL = 6656
Z_QA = 0
Z_KA = 512
Z_VA = 1024
Z_FB = 1536
Z_QC = 2048
Z_KC = 2304
Z_VC = 2560
Z_GC = 3072
Z_GATES = 3584
LANES = 128
MIB = 1024 * 1024


def _cparams(semantics, vmem_mib):
    return pltpu.CompilerParams(dimension_semantics=semantics,
                                vmem_limit_bytes=vmem_mib * MIB)


def _dot(a, b):
    return jnp.dot(a, b, preferred_element_type=F32)


def _dot_nt(a, b):
    return lax.dot_general(a, b, (((1,), (1,)), ((), ())), preferred_element_type=F32)


def _dot_tn(a, b):
    return lax.dot_general(a, b, (((0,), (0,)), ((), ())), preferred_element_type=F32)


def _split_bf16(a):
    hi = a.astype(BF16)
    lo = (a - hi.astype(F32)).astype(BF16)
    return hi, lo


def _dot_f32x3(a, b):
    a_hi, a_lo = _split_bf16(a)
    b_hi, b_lo = _split_bf16(b)
    return _dot(a_hi, b_hi) + _dot(a_lo, b_hi) + _dot(a_hi, b_lo)


def _rms(x):
    return x * lax.rsqrt(jnp.mean(x * x, axis=-1, keepdims=True) + EPS)


def _modnorm(x, g, scale, shift):
    return (_rms(x) * g) * (1.0 + scale) + shift


def _sigmoid(x):
    return 1.0 / (1.0 + jnp.exp(-x))


MOD_TN = 1536


def _mod_kernel(cond_ref, w_ref, b_ref, o_ref):
    cnd = cond_ref[...]
    s = cnd * _sigmoid(cnd)
    o_ref[0] = _dot_f32x3(s, w_ref[0]) + b_ref[0]


def _modulation(cond8, w_mod, b_mod):
    n = 6 * D_MODEL
    out = pl.pallas_call(
        _mod_kernel,
        grid=(DEPTH, n // MOD_TN),
        in_specs=[
            pl.BlockSpec((8, D_MODEL), lambda l, j: (0, 0)),
            pl.BlockSpec((1, D_MODEL, MOD_TN), lambda l, j: (l, 0, j)),
            pl.BlockSpec((1, 1, MOD_TN), lambda l, j: (l, 0, j)),
        ],
        out_specs=pl.BlockSpec((1, 8, MOD_TN), lambda l, j: (l, 0, j)),
        out_shape=jax.ShapeDtypeStruct((DEPTH, 8, n), F32),
        compiler_params=_cparams(("parallel", "parallel"), 40),
        name="modulation",
    )(cond8, w_mod, b_mod.reshape(DEPTH, 1, n))
    return out.reshape(DEPTH, 8, 6, D_MODEL)


def _mod_row(tile_idx, tm):
    n_ctx_tiles = N_CTX_TOK // tm
    per_batch = DEC_SEQ // tm
    return jnp.where(tile_idx < n_ctx_tiles, 0, 1 + (tile_idx - n_ctx_tiles) // per_batch)


INP_TM = 1024
INP_TN = 1664
INP_NJ = IN_TOTAL // INP_TN
CTX_PER_TILE = INP_TM // SEQ
assert Z_VA + H_A * DV_A <= INP_TN


def _inproj_kernel(x_ref, mod_ref, g_ref, w_ref, *rest):
    z_ref, kc_ref, vc_ref, h_scr = rest[-4:]
    i = pl.program_id(0)
    j = pl.program_id(1)

    @pl.when(j == 0)
    def _():
        h = _modnorm(x_ref[...], g_ref[...], mod_ref[0, 1:2, :], mod_ref[0, 0:1, :])
        h_scr[...] = h.astype(BF16)

    z_ref[...] = _dot(h_scr[...], w_ref[0].astype(BF16)).astype(BF16)

    @pl.when(jnp.logical_and(i < N_CTX_TOK // INP_TM, j == 0))
    def _():
        wk = w_ref[0, :, Z_KA:Z_KA + W_A].astype(BF16)
        wv = w_ref[0, :, Z_VA:Z_VA + W_A].astype(BF16)
        for b in range(CTX_PER_TILE):
            hb = h_scr[b * SEQ:(b + 1) * SEQ, :]
            kc_ref[b, 0] = _dot(hb, wk)
            vc_ref[b, 0] = _dot(hb, wv)


def _inproj(x, mod_l, g, w_in, layer, prev_caches):
    ni = N_TOK // INP_TM
    n_ctx_tiles = N_CTX_TOK // INP_TM
    cache_shape = jax.ShapeDtypeStruct((BATCH, DEPTH, SEQ, W_A), F32)
    cache_spec = pl.BlockSpec((CTX_PER_TILE, 1, SEQ, W_A),
                              lambda i, j: (jnp.minimum(i, n_ctx_tiles - 1), layer, 0, 0))
    in_specs = [
        pl.BlockSpec((INP_TM, D_MODEL), lambda i, j: (i, 0)),
        pl.BlockSpec((1, 6, D_MODEL), lambda i, j: (_mod_row(i, INP_TM), 0, 0)),
        pl.BlockSpec((1, D_MODEL), lambda i, j: (0, 0)),
        pl.BlockSpec((1, D_MODEL, INP_TN), lambda i, j: (layer, 0, j)),
    ]
    args = [x, mod_l, g.reshape(1, D_MODEL), w_in]
    aliases = {}
    if prev_caches is not None:
        in_specs += [pl.BlockSpec(memory_space=pl.ANY)] * 2
        args += list(prev_caches)
        aliases = {4: 1, 5: 2}
    return pl.pallas_call(
        _inproj_kernel,
        grid=(ni, INP_NJ),
        in_specs=in_specs,
        out_specs=[pl.BlockSpec((INP_TM, INP_TN), lambda i, j: (i, j)), cache_spec, cache_spec],
        out_shape=[jax.ShapeDtypeStruct((N_TOK, IN_TOTAL), BF16), cache_shape, cache_shape],
        scratch_shapes=[pltpu.VMEM((INP_TM, D_MODEL), BF16)],
        input_output_aliases=aliases,
        compiler_params=_cparams(("arbitrary", "arbitrary"), 56),
        name="inproj",
    )(*args)


ATT_TQ = 256
W_A = H_A * DV_A


def _lambda(lam_ref, lam_init):
    la = lam_ref[...]
    l01 = jnp.sum(la[0:1, :] * la[1:2, :], axis=1, keepdims=True)
    l23 = jnp.sum(la[2:3, :] * la[3:4, :], axis=1, keepdims=True)
    return jnp.exp(l01) - jnp.exp(l23) + lam_init


def _softmax_parts(s):
    m = jnp.max(s, axis=-1, keepdims=True)
    e = jnp.exp(s - m)
    return e, 1.0 / jnp.sum(e, axis=-1, keepdims=True)


def _diff_attn_block(q, k_ref, v_ref, lam, subln, lam_init):
    lane = lax.broadcasted_iota(jnp.int32, (1, LANES), 1)
    first = lane < DH_A
    scale = jnp.asarray(DH_A ** -0.5, BF16)
    outs = []
    for h in range(H_A):
        sl = slice(h * LANES, (h + 1) * LANES)
        qh = q[:, sl]
        kh = k_ref[:, sl]
        zero = jnp.zeros_like(qh)
        qs = qh * scale
        s0 = _dot_nt(jnp.where(first, qs, zero), kh)
        s1 = _dot_nt(jnp.where(first, zero, qs), kh)
        e0, r0 = _softmax_parts(s0)
        e1, r1 = _softmax_parts(s1)
        a = e0 * r0 - (lam * r1) * e1
        o = _dot(a.astype(BF16), v_ref[:, sl])
        outs.append((_rms(o) * subln) * (1.0 - lam_init))
    return outs


def _rope(x, cos, sin):
    lane = lax.broadcasted_iota(jnp.int32, (1, x.shape[1]), 1)
    first = (lane % 32) < 16
    n = x.shape[1]
    xr = jnp.where(first, -pltpu.roll(x, n - 16, 1), pltpu.roll(x, 16, 1))
    return x * cos + xr * sin


def _attn_ctx_kernel(lam_init, q_ref, k_ref, v_ref, lam_ref, subln_ref, o_ref):
    lam = _lambda(lam_ref, lam_init)
    outs = _diff_attn_block(q_ref[...], k_ref, v_ref, lam, subln_ref[...], lam_init)
    for h in range(H_A):
        o_ref[:, h * LANES:(h + 1) * LANES] = outs[h].astype(BF16)


def _attn_ctx(z, lam_a_l, subln_l, lam_init):
    wb = W_A
    return pl.pallas_call(
        functools.partial(_attn_ctx_kernel, lam_init),
        grid=(BATCH,),
        in_specs=[
            pl.BlockSpec((SEQ, wb), lambda b: (b, Z_QA // wb)),
            pl.BlockSpec((SEQ, wb), lambda b: (b, Z_KA // wb)),
            pl.BlockSpec((SEQ, wb), lambda b: (b, Z_VA // wb)),
            pl.BlockSpec((4, DH_A), lambda b: (0, 0)),
            pl.BlockSpec((1, DV_A), lambda b: (0, 0)),
        ],
        out_specs=pl.BlockSpec((SEQ, wb), lambda b: (b, 0)),
        out_shape=jax.ShapeDtypeStruct((N_TOK, wb), BF16),
        compiler_params=_cparams(("parallel",), 40),
        name="attn_ctx",
    )(z, z, z, lam_a_l, subln_l.reshape(1, DV_A))


def _attn_lat_kernel(lam_init, q_ref, k_ref, v_ref, ck_ref, cv_ref, cos_ref, sin_ref,
                     lam_ref, subln_ref, ya_ctx_ref, o_ref, kall, vall):
    del ya_ctx_ref
    lam = _lambda(lam_ref, lam_init)
    subln = subln_ref[...]
    kall[0:DEC_SEQ, :] = _rope(k_ref[...].astype(F32), cos_ref[...], sin_ref[...]).astype(BF16)
    kall[DEC_SEQ:, :] = ck_ref[0, 0].astype(BF16)
    vall[0:DEC_SEQ, :] = v_ref[...]
    vall[DEC_SEQ:, :] = cv_ref[0, 0].astype(BF16)

    def q_block(qb, carry):
        rows = pl.ds(pl.multiple_of(qb * ATT_TQ, ATT_TQ), ATT_TQ)
        q = _rope(q_ref[rows, :].astype(F32), cos_ref[rows, :], sin_ref[rows, :]).astype(BF16)
        outs = _diff_attn_block(q, kall, vall, lam, subln, lam_init)
        for h in range(H_A):
            o_ref[rows, h * LANES:(h + 1) * LANES] = outs[h].astype(BF16)
        return carry

    lax.fori_loop(0, DEC_SEQ // ATT_TQ, q_block, 0)


def _attn_lat(z, cache_k4, cache_v4, layer, cos, sin, lam_a_l, subln_l, lam_init, ya_ctx):
    wb = W_A
    rb0 = N_CTX_TOK // DEC_SEQ
    t_all = DEC_SEQ + PAST_LEN
    return pl.pallas_call(
        functools.partial(_attn_lat_kernel, lam_init),
        grid=(DEC_BATCH,),
        in_specs=[
            pl.BlockSpec((DEC_SEQ, wb), lambda b: (rb0 + b, Z_QA // wb)),
            pl.BlockSpec((DEC_SEQ, wb), lambda b: (rb0 + b, Z_KA // wb)),
            pl.BlockSpec((DEC_SEQ, wb), lambda b: (rb0 + b, Z_VA // wb)),
            pl.BlockSpec((1, 1, PAST_LEN, wb), lambda b: (b, layer, 0, 0)),
            pl.BlockSpec((1, 1, PAST_LEN, wb), lambda b: (b, layer, 0, 0)),
            pl.BlockSpec((DEC_SEQ, wb), lambda b: (0, 0)),
            pl.BlockSpec((DEC_SEQ, wb), lambda b: (0, 0)),
            pl.BlockSpec((4, DH_A), lambda b: (0, 0)),
            pl.BlockSpec((1, DV_A), lambda b: (0, 0)),
            pl.BlockSpec(memory_space=pl.ANY),
        ],
        out_specs=pl.BlockSpec((DEC_SEQ, wb), lambda b: (rb0 + b, 0)),
        out_shape=jax.ShapeDtypeStruct((N_TOK, wb), BF16),
        scratch_shapes=[pltpu.VMEM((t_all, wb), BF16), pltpu.VMEM((t_all, wb), BF16)],
        input_output_aliases={9: 0},
        compiler_params=_cparams(("parallel",), 48),
        name="attn_lat",
    )(z, z, z, cache_k4, cache_v4, cos, sin, lam_a_l, subln_l.reshape(1, DV_A), ya_ctx)


def _rope_tables(s):
    rows = s // GRID_W
    row = jnp.repeat(jnp.arange(rows, dtype=F32), GRID_W)
    col = jnp.tile(jnp.arange(GRID_W, dtype=F32), rows)
    quarter = DH_A // 4
    inv = ROPE_BASE ** (-jnp.arange(quarter, dtype=F32) / quarter)
    ar = row[:, None] * inv
    ac = col[:, None] * inv
    ang = jnp.concatenate([ar, ar, ac, ac], axis=-1)
    reps = (H_A * 2 * DH_A) // DH_A
    return jnp.tile(jnp.cos(ang), (1, reps)), jnp.tile(jnp.sin(ang), (1, reps))


W_B = G_B * DG_B


def _dft_consts(s):
    def cs(n, scale):
        j = np.arange(n, dtype=np.int64)
        ang = 2.0 * np.pi * ((j[:, None] * j[None, :]) % n).astype(np.float64) / n
        return np.cos(ang) * scale, np.sin(ang) * scale
    cc, sc = cs(DG_B, DG_B ** -0.5)
    chan = np.concatenate([cc, sc], axis=1)
    cp, sp = cs(s, s ** -0.5)
    pos = np.concatenate([cp, -sp], axis=1)
    return jnp.asarray(chan, dtype=F32).astype(BF16), jnp.asarray(pos, dtype=F32).astype(BF16)


def _fourier_kernel(s, f_ref, chan_ref, pos_ref, *rest):
    o_ref, p_ref = rest[-2:]
    chan = chan_ref[...]
    for g in range(G_B):
        sl = slice(g * DG_B, (g + 1) * DG_B)
        a = _dot(f_ref[:, sl], chan)
        p_ref[0:s, sl] = a[:, :DG_B].astype(BF16)
        p_ref[s:2 * s, sl] = a[:, DG_B:].astype(BF16)
    o_ref[...] = _dot(pos_ref[...], p_ref[...]).astype(BF16)


def _fourier(z, s, n_batch, row_block0, chan, pos, prev=None):
    in_specs = [
        pl.BlockSpec((s, W_B), lambda b: (row_block0 + b, Z_FB // W_B)),
        pl.BlockSpec((DG_B, 2 * DG_B), lambda b: (0, 0)),
        pl.BlockSpec((s, 2 * s), lambda b: (0, 0)),
    ]
    args = [z, chan, pos]
    aliases = {}
    if prev is not None:
        in_specs.append(pl.BlockSpec(memory_space=pl.ANY))
        args.append(prev)
        aliases = {3: 0}
    return pl.pallas_call(
        functools.partial(_fourier_kernel, s),
        grid=(n_batch,),
        in_specs=in_specs,
        out_specs=pl.BlockSpec((s, W_B), lambda b: (row_block0 + b, 0)),
        out_shape=jax.ShapeDtypeStruct((N_TOK, W_B), BF16),
        scratch_shapes=[pltpu.VMEM((2 * s, W_B), BF16)],
        input_output_aliases=aliases,
        compiler_params=_cparams(("parallel",), 40),
        name="fourier_%d" % s,
    )(*args)


W_C = H_C * DV_C
RET_TQ = 256


def _log_sigmoid(x):
    return jnp.minimum(x, 0.0) - jnp.log1p(jnp.exp(-jnp.abs(x)))


def _head_pair_lanes(vals, pair):
    lane = lax.broadcasted_iota(jnp.int32, (1, LANES), 1)
    return jnp.where(lane < DK_C, vals[:, 2 * pair:2 * pair + 1], vals[:, 2 * pair + 1:2 * pair + 2])


def _retention_rows(s, row0, tq, q, k_ref, v_ref, g, lg, rnorm, s0f_ref, s0b_ref):
    lane = lax.broadcasted_iota(jnp.int32, (1, LANES), 1)
    ii = (row0 + lax.broadcasted_iota(jnp.int32, (tq, s), 0)).astype(F32)
    jj = lax.broadcasted_iota(jnp.int32, (tq, s), 1).astype(F32)
    rel = ii - jj
    pos = (row0 + lax.broadcasted_iota(jnp.int32, (tq, 1), 0)).astype(F32)
    outs = []
    for h in range(H_C):
        pair, half = h // 2, h % 2
        sl = slice(pair * LANES, (pair + 1) * LANES)
        lgf = lg[0:1, h:h + 1]
        lgb = lg[1:2, h:h + 1]
        in_head = (lane < DK_C) if half == 0 else (lane >= DK_C)
        q_pair = q[:, sl]
        qh = jnp.where(in_head, q_pair, jnp.zeros_like(q_pair))
        sc = _dot_nt(qh, k_ref[:, sl]) * (DK_C ** -0.5)
        decay = (jnp.where(rel >= 0, jnp.exp(lgf * jnp.maximum(rel, 0.0)), 0.0)
                 + jnp.where(rel <= 0, jnp.exp(lgb * jnp.maximum(-rel, 0.0)), 0.0))
        o = _dot((sc * decay).astype(BF16), v_ref[:, h * DV_C:(h + 1) * DV_C])
        if s0f_ref is not None:
            qf = qh.astype(F32)
            xf = jnp.exp(lgf * (pos + 1.0))
            xb = jnp.exp(lgb * (s - pos))
            o = o + _dot((qf * xf).astype(BF16), s0f_ref[0, 0, sl, :].astype(BF16))
            o = o + _dot((qf * xb).astype(BF16), s0b_ref[0, 0, sl, :].astype(BF16))
        gh = g[:, h * DV_C:(h + 1) * DV_C].astype(F32)
        outs.append((_rms(o) * rnorm) * (gh * _sigmoid(gh)))
    return outs


def _ret_ctx_kernel(q_ref, k_ref, v_ref, g_ref, dec_ref, rn_ref, *rest):
    o_ref, sf_ref, sb_ref = rest[-3:]
    s = SEQ
    lg = _log_sigmoid(dec_ref[...])
    outs = _retention_rows(s, 0, s, q_ref[...], k_ref, v_ref, g_ref[...], lg, rn_ref[...], None, None)
    for h in range(H_C):
        o_ref[:, h * DV_C:(h + 1) * DV_C] = outs[h].astype(BF16)
    pos = lax.broadcasted_iota(jnp.int32, (s, 1), 0).astype(F32)
    v = v_ref[...]
    for pair in range(H_C // 2):
        sl = slice(pair * LANES, (pair + 1) * LANES)
        kf = k_ref[:, sl].astype(F32) * (DK_C ** -0.5)
        zf = jnp.exp(_head_pair_lanes(lg[0:1, :], pair) * (s - 1.0 - pos))
        zb = jnp.exp(_head_pair_lanes(lg[1:2, :], pair) * pos)
        stf = _dot_tn((kf * zf).astype(BF16), v)
        stb = _dot_tn((kf * zb).astype(BF16), v)
        for half in range(2):
            h = 2 * pair + half
            rs = slice(half * DK_C, (half + 1) * DK_C)
            cs = slice(h * DV_C, (h + 1) * DV_C)
            sf_ref[0, 0, h] = stf[rs, cs]
            sb_ref[0, 0, h] = stb[rs, cs]


def _ret_ctx(z, dec_l, rnorm_l, layer, prev_states):
    wq = H_C * DK_C
    st_shape = jax.ShapeDtypeStruct((BATCH, DEPTH, H_C, DK_C, DV_C), F32)
    st_spec = pl.BlockSpec((1, 1, H_C, DK_C, DV_C), lambda b: (b, layer, 0, 0, 0))
    in_specs = [
        pl.BlockSpec((SEQ, wq), lambda b: (b, Z_QC // wq)),
        pl.BlockSpec((SEQ, wq), lambda b: (b, Z_KC // wq)),
        pl.BlockSpec((SEQ, W_C), lambda b: (b, Z_VC // W_C)),
        pl.BlockSpec((SEQ, W_C), lambda b: (b, Z_GC // W_C)),
        pl.BlockSpec((2, H_C), lambda b: (0, 0)),
        pl.BlockSpec((1, DV_C), lambda b: (0, 0)),
    ]
    args = [z, z, z, z, dec_l, rnorm_l.reshape(1, DV_C)]
    aliases = {}
    if prev_states is not None:
        in_specs += [pl.BlockSpec(memory_space=pl.ANY)] * 2
        args += list(prev_states)
        aliases = {6: 1, 7: 2}
    return pl.pallas_call(
        _ret_ctx_kernel,
        grid=(BATCH,),
        in_specs=in_specs,
        out_specs=[pl.BlockSpec((SEQ, W_C), lambda b: (b, 0)), st_spec, st_spec],
        out_shape=[jax.ShapeDtypeStruct((N_TOK, W_C), BF16), st_shape, st_shape],
        input_output_aliases=aliases,
        compiler_params=_cparams(("parallel",), 40),
        name="ret_ctx",
    )(*args)


def _ret_lat_kernel(q_ref, k_ref, v_ref, g_ref, dec_ref, rn_ref, s0f_ref, s0b_ref, yc_ctx_ref, o_ref):
    del yc_ctx_ref
    lg = _log_sigmoid(dec_ref[...])
    rnorm = rn_ref[...]

    def q_block(qb, carry):
        row0 = pl.multiple_of(qb * RET_TQ, RET_TQ)
        rows = pl.ds(row0, RET_TQ)
        outs = _retention_rows(DEC_SEQ, row0, RET_TQ, q_ref[rows, :], k_ref, v_ref, g_ref[rows, :],
                               lg, rnorm, s0f_ref, s0b_ref)
        for h in range(H_C):
            o_ref[rows, h * DV_C:(h + 1) * DV_C] = outs[h].astype(BF16)
        return carry

    lax.fori_loop(0, DEC_SEQ // RET_TQ, q_block, 0)


def _ret_lat(z, dec_l, rnorm_l, s0f4, s0b4, layer, yc_ctx):
    wq = H_C * DK_C
    rb0 = N_CTX_TOK // DEC_SEQ
    st_spec = pl.BlockSpec((1, 1, H_C * DK_C, DV_C), lambda b: (b, layer, 0, 0))
    return pl.pallas_call(
        _ret_lat_kernel,
        grid=(DEC_BATCH,),
        in_specs=[
            pl.BlockSpec((DEC_SEQ, wq), lambda b: (rb0 + b, Z_QC // wq)),
            pl.BlockSpec((DEC_SEQ, wq), lambda b: (rb0 + b, Z_KC // wq)),
            pl.BlockSpec((DEC_SEQ, W_C), lambda b: (rb0 + b, Z_VC // W_C)),
            pl.BlockSpec((DEC_SEQ, W_C), lambda b: (rb0 + b, Z_GC // W_C)),
            pl.BlockSpec((2, H_C), lambda b: (0, 0)),
            pl.BlockSpec((1, DV_C), lambda b: (0, 0)),
            st_spec, st_spec,
            pl.BlockSpec(memory_space=pl.ANY),
        ],
        out_specs=pl.BlockSpec((DEC_SEQ, W_C), lambda b: (rb0 + b, 0)),
        out_shape=jax.ShapeDtypeStruct((N_TOK, W_C), BF16),
        input_output_aliases={8: 0},
        compiler_params=_cparams(("parallel",), 40),
        name="ret_lat",
    )(z, z, z, z, dec_l, rnorm_l.reshape(1, DV_C), s0f4, s0b4, yc_ctx)


MRG_TM = 512
MRG_SUBBLOCKS = 2
GATE_BLOCK = 512


def _merge_kernel(x_ref, ya_ref, yb_ref, yc_ref, g0_ref, g1_ref, g2_ref, g3_ref, g4_ref, g5_ref,
                  mod_ref, wa_ref, wb_ref, wc_ref, wo_ref,
                  n2_ref, rw_ref, rb_ref, x1_ref, h2_ref, topi_ref, topw_ref, rank_ref,
                  cnt_row_ref, cnt_col_ref, carry_row, carry_col, wa_bf, wb_bf, wc_bf, wo_bf):
    d = D_MODEL
    tm = x_ref.shape[0]

    @pl.when(pl.program_id(0) == 0)
    def _():
        carry_row[...] = jnp.zeros_like(carry_row)
        carry_col[...] = jnp.zeros_like(carry_col)
        wa_bf[...] = wa_ref[0].astype(BF16)
        wb_bf[...] = wb_ref[0].astype(BF16)
        wc_bf[...] = wc_ref[0].astype(BF16)
        wo_bf[...] = wo_ref[0].astype(BF16)

    def gate(lo_ref, hi_ref, rs):
        return _sigmoid(jnp.concatenate([lo_ref[rs, :], hi_ref[rs, :]], axis=1).astype(F32))

    sub = tm // MRG_SUBBLOCKS
    for sb in range(MRG_SUBBLOCKS):
        rs = slice(sb * sub, (sb + 1) * sub)
        merged = gate(g0_ref, g1_ref, rs) * _dot(ya_ref[rs, :], wa_bf[...])
        merged += gate(g2_ref, g3_ref, rs) * _dot(yb_ref[rs, :], wb_bf[...])
        merged += gate(g4_ref, g5_ref, rs) * _dot(yc_ref[rs, :], wc_bf[...])
        y = _dot(merged.astype(BF16), wo_bf[...])
        x1 = x_ref[rs, :] + mod_ref[0, 2:3, :] * y
        x1_ref[rs, :] = x1
        h2 = _modnorm(x1, n2_ref[...], mod_ref[0, 4:5, :], mod_ref[0, 3:4, :])
        h2_ref[rs, :] = h2
        logits = _dot_f32x3(h2, rw_ref[...]) + rb_ref[...]
        ids = lax.broadcasted_iota(jnp.int32, logits.shape, 1)
        work = logits
        sels, vals, firsts = [], [], []
        for _ in range(TOP_K):
            m = jnp.max(work, axis=-1, keepdims=True)
            first = jnp.min(jnp.where(work == m, ids, N_EXPERTS), axis=-1, keepdims=True)
            sel = ids == first
            sels.append(sel)
            vals.append(m)
            firsts.append(first)
            work = jnp.where(sel, -jnp.inf, work)
        es = [jnp.exp(v - vals[0]) for v in vals]
        inv = 1.0 / (es[0] + es[1] + es[2] + es[3])
        chosen = jnp.zeros_like(logits)
        for sel in sels:
            chosen = chosen + jnp.where(sel, 1.0, 0.0)
        chosen_bf = chosen.astype(BF16)
        rr = lax.broadcasted_iota(jnp.int32, (sub, sub), 0)
        cc = lax.broadcasted_iota(jnp.int32, (sub, sub), 1)
        before = jnp.where(rr > cc, 1.0, 0.0).astype(BF16)
        prefix = _dot(before, chosen_bf) + carry_row[...]
        ranks = [jnp.sum(jnp.where(sel, prefix, 0.0), axis=-1, keepdims=True) for sel in sels]
        carry_row[...] += jnp.sum(chosen, axis=0, keepdims=True)
        carry_col[...] += _dot_tn(chosen_bf, jnp.ones((sub, LANES), BF16))
        topi_ref[rs, :] = jnp.concatenate(firsts, axis=1)
        topw_ref[rs, :] = jnp.concatenate([e * inv for e in es], axis=1)
        rank_ref[rs, :] = jnp.concatenate(ranks, axis=1).astype(jnp.int32)
    cnt_row_ref[...] = carry_row[...]
    cnt_col_ref[...] = carry_col[...]


def _merge(x, ya, yb, yc, z, mod_l, wa, wb, wc, wo, n2, rw, rb, layer):
    tm = MRG_TM
    d = D_MODEL
    full = lambda shape: pl.BlockSpec(shape, lambda i: tuple(0 for _ in shape))
    tok = lambda w: pl.BlockSpec((tm, w), lambda i: (i, 0))
    per_layer = lambda rows: pl.BlockSpec((1, rows, d), lambda i: (layer, 0, 0))
    return pl.pallas_call(
        _merge_kernel,
        grid=(N_TOK // tm,),
        in_specs=[
            tok(d), tok(W_A), tok(W_B), tok(W_C),
            *[pl.BlockSpec((tm, GATE_BLOCK), lambda i, c=c: (i, Z_GATES // GATE_BLOCK + c)) for c in range(6)],
            pl.BlockSpec((1, 6, d), lambda i: (_mod_row(i, tm), 0, 0)),
            per_layer(W_A), per_layer(W_B), per_layer(W_C), per_layer(d),
            full((1, d)), full((d, N_EXPERTS)), full((1, N_EXPERTS)),
        ],
        out_specs=[tok(d), tok(d), tok(TOP_K), tok(TOP_K), tok(TOP_K),
                   full((1, N_EXPERTS)), full((N_EXPERTS, LANES))],
        out_shape=[
            jax.ShapeDtypeStruct((N_TOK, d), F32),
            jax.ShapeDtypeStruct((N_TOK, d), F32),
            jax.ShapeDtypeStruct((N_TOK, TOP_K), jnp.int32),
            jax.ShapeDtypeStruct((N_TOK, TOP_K), F32),
            jax.ShapeDtypeStruct((N_TOK, TOP_K), jnp.int32),
            jax.ShapeDtypeStruct((1, N_EXPERTS), F32),
            jax.ShapeDtypeStruct((N_EXPERTS, LANES), F32),
        ],
        scratch_shapes=[pltpu.VMEM((1, N_EXPERTS), F32), pltpu.VMEM((N_EXPERTS, LANES), F32),
                        pltpu.VMEM((W_A, d), BF16), pltpu.VMEM((W_B, d), BF16),
                        pltpu.VMEM((W_C, d), BF16), pltpu.VMEM((d, d), BF16)],
        compiler_params=_cparams(("arbitrary",), 56),
        name="merge_router",
    )(x, ya, yb, yc, z, z, z, z, z, z, mod_l, wa, wb, wc, wo, n2.reshape(1, d), rw, rb.reshape(1, N_EXPERTS))


MOE_TM = 256
MOE_TILES = (N_TOK * TOP_K) // MOE_TM + N_EXPERTS
MOE_ROWS = MOE_TILES * MOE_TM
PLAN_LANES = 256
PLAN_TM = 1024


def _plan_kernel(topi_ref, rank_ref, cnt_row_ref, cnt_col_ref, pos_ref, te_ref, nt_ref, last_ref,
                 next_ref):
    e_r = lax.broadcasted_iota(jnp.int32, (N_EXPERTS, N_EXPERTS), 0)
    e_c = lax.broadcasted_iota(jnp.int32, (N_EXPERTS, N_EXPERTS), 1)
    cnt_row = cnt_row_ref[...]
    tiles_row = jnp.ceil(cnt_row * (1.0 / MOE_TM))
    earlier = jnp.where(e_r < e_c, 1.0, 0.0).astype(BF16)
    start_tile_row = _dot(tiles_row.astype(BF16), earlier)
    off_row = start_tile_row * MOE_TM
    ids = lax.broadcasted_iota(jnp.int32, (PLAN_TM, N_EXPERTS), 1)
    cols = []
    for k in range(TOP_K):
        hit = ids == topi_ref[:, k:k + 1]
        cols.append(jnp.sum(jnp.where(hit, off_row, 0.0), axis=-1, keepdims=True))
    pos_ref[...] = jnp.concatenate(cols, axis=1).astype(jnp.int32) + rank_ref[...]
    tiles_col = jnp.ceil(cnt_col_ref[...] * (1.0 / MOE_TM))
    upto = jnp.where(e_c <= e_r, 1.0, 0.0).astype(BF16)
    end_col = _dot(upto, tiles_col.astype(BF16))
    end_b = jnp.concatenate([end_col] * (PLAN_LANES // LANES), axis=1)
    r = lax.broadcasted_iota(jnp.int32, (N_EXPERTS, PLAN_LANES), 1).astype(F32)
    total = end_b[N_EXPERTS - 1:N_EXPERTS, :]
    te = jnp.sum(jnp.where(end_b <= jnp.minimum(r, total - 1.0), 1.0, 0.0), axis=0, keepdims=True)
    te_ref[...] = te.astype(jnp.int32)
    nt_ref[...] = total[:, :LANES].astype(jnp.int32)
    last = jnp.where(cnt_row > 0.0, (start_tile_row + tiles_row - 1.0) * MOE_TM, -1.0)
    last_ref[...] = last.astype(jnp.int32)
    later_used = jnp.logical_and(e_r > e_c, cnt_col_ref[:, 0:N_EXPERTS] > 0.0)
    nxt = jnp.min(jnp.where(later_used, e_r, N_EXPERTS), axis=0, keepdims=True)
    next_ref[...] = jnp.where(nxt == N_EXPERTS, -1, nxt)


def _plan(topi, rank, cnt_row, cnt_col):
    tok = pl.BlockSpec((PLAN_TM, TOP_K), lambda i: (i, 0))
    full = lambda shape: pl.BlockSpec(shape, lambda i: tuple(0 for _ in shape))
    return pl.pallas_call(
        _plan_kernel,
        grid=(N_TOK // PLAN_TM,),
        in_specs=[tok, tok, full((1, N_EXPERTS)), full((N_EXPERTS, LANES))],
        out_specs=[tok, full((1, PLAN_LANES)), full((1, LANES)), full((1, N_EXPERTS)),
                   full((1, N_EXPERTS))],
        out_shape=[
            jax.ShapeDtypeStruct((N_TOK, TOP_K), jnp.int32),
            jax.ShapeDtypeStruct((1, PLAN_LANES), jnp.int32),
            jax.ShapeDtypeStruct((1, LANES), jnp.int32),
            jax.ShapeDtypeStruct((1, N_EXPERTS), jnp.int32),
            jax.ShapeDtypeStruct((1, N_EXPERTS), jnp.int32),
        ],
        compiler_params=_cparams(("arbitrary",), 32),
        name="route_plan",
    )(topi, rank, cnt_row, cnt_col)


DSP_TM = 256


def _dispatch_kernel(pos_ref, last_ref, h_ref, xs_ref, zero_buf, zsem, sem):
    i = pl.program_id(0)

    @pl.when(i == 0)
    def _():
        zero_buf[...] = jnp.zeros_like(zero_buf)

        def zero_copy(e):
            return pltpu.make_async_copy(zero_buf, xs_ref.at[pl.ds(pl.multiple_of(last_ref[0, e], MOE_TM), MOE_TM)], zsem)

        for e in range(N_EXPERTS):
            @pl.when(last_ref[0, e] >= 0)
            def _():
                zero_copy(e).start()
        for e in range(N_EXPERTS):
            @pl.when(last_ref[0, e] >= 0)
            def _():
                zero_copy(e).wait()

    def row_copy(t, k):
        return pltpu.make_async_copy(h_ref.at[pl.ds(t, 1)],
                                     xs_ref.at[pl.ds(pos_ref[0, 0, t * TOP_K + k], 1)], sem)

    def issue(t, carry):
        for k in range(TOP_K):
            row_copy(t, k).start()
        return carry

    lax.fori_loop(0, DSP_TM, issue, 0, unroll=8)

    for k in range(TOP_K):
        pltpu.make_async_copy(h_ref, xs_ref.at[pl.ds(0, DSP_TM)], sem).wait()


def _dispatch(pos, last, h2):
    n_tiles = N_TOK // DSP_TM
    return pl.pallas_call(
        _dispatch_kernel,
        grid=(n_tiles,),
        in_specs=[
            pl.BlockSpec((1, 1, DSP_TM * TOP_K), lambda i: (i, 0, 0), memory_space=pltpu.SMEM),
            pl.BlockSpec((1, N_EXPERTS), lambda i: (0, 0), memory_space=pltpu.SMEM),
            pl.BlockSpec((DSP_TM, D_MODEL), lambda i: (i, 0)),
        ],
        out_specs=pl.BlockSpec(memory_space=pl.ANY),
        out_shape=jax.ShapeDtypeStruct((MOE_ROWS, D_MODEL), F32),
        scratch_shapes=[pltpu.VMEM((MOE_TM, D_MODEL), F32), pltpu.SemaphoreType.DMA,
                        pltpu.SemaphoreType.DMA],
        compiler_params=_cparams(("arbitrary",), 32),
        name="dispatch",
    )(pos.reshape(n_tiles, 1, DSP_TM * TOP_K), last, h2)


PAIR_BLOCKS = (2 * D_FF) // (2 * LANES)
WEIGHT_PARTS = 3


def _prep_expert_weights(wgu_ref, wdn_ref, wgu_bf, wdn_perm, wdn_bf):
    wgu_bf[...] = wgu_ref[...].astype(BF16)
    half = LANES // 2
    for cb in range(D_MODEL // LANES):
        cols = slice(cb * LANES, (cb + 1) * LANES)
        for m in range(PAIR_BLOCKS):
            base = m * LANES
            wdn_perm[cb, pl.ds(base, half, stride=2), :] = wdn_ref[base:base + half, cols]
            wdn_perm[cb, pl.ds(base + 1, half, stride=2), :] = wdn_ref[base + half:base + LANES, cols]
        wdn_bf[:, cols] = wdn_perm[cb].astype(BF16)


def _ffn_rows(x, wgu_bf, bgu, wdn_bf, bdn):
    gu = _dot(x, wgu_bf[...]) + bgu
    lane = lax.broadcasted_iota(jnp.int32, (1, LANES), 1)
    even = (lane % 2) == 0

    def act(block):
        t = gu[:, block * LANES:(block + 1) * LANES]
        glu = jnp.minimum(t, SWIGLU_LIMIT)
        glu = glu * _sigmoid(SWIGLU_ALPHA * glu)
        lin = jnp.clip(t, -SWIGLU_LIMIT, SWIGLU_LIMIT) + 1.0
        return glu * pltpu.roll(lin, LANES - 1, 1)

    parts = []
    for m in range(PAIR_BLOCKS):
        parts.append(jnp.where(even, act(2 * m), pltpu.roll(act(2 * m + 1), 1, 1)))
    hidden = jnp.concatenate(parts, axis=1).astype(BF16)
    return _dot(hidden, wdn_bf[...]) + bdn


def _moe_kernel(layer, te_ref, nt_ref, next_ref, x_ref, wgu_hbm, bgu_ref, wdn_hbm, bdn_ref, o_ref,
                wgu_land, wdn_land, wgu_bf, wdn_perm, wdn_bf, issued, sems):
    r = pl.program_id(0)
    valid = r < nt_ref[0]
    e = te_ref[r]
    new_expert = jnp.logical_or(r == 0, e != te_ref[jnp.maximum(r - 1, 0)])
    half = D_MODEL // 2

    def part(expert, p):
        if p == 0:
            return pltpu.make_async_copy(wgu_hbm.at[layer, expert, pl.ds(0, half)],
                                         wgu_land.at[pl.ds(0, half)], sems.at[0])
        if p == 1:
            return pltpu.make_async_copy(wgu_hbm.at[layer, expert, pl.ds(half, half)],
                                         wgu_land.at[pl.ds(half, half)], sems.at[1])
        return pltpu.make_async_copy(wdn_hbm.at[layer, expert], wdn_land, sems.at[2])

    @pl.when(r == 0)
    def _():
        issued[0] = 0

    @pl.when(jnp.logical_and(valid, new_expert))
    def _():
        for p in range(WEIGHT_PARTS):
            @pl.when(issued[0] <= p)
            def _():
                part(e, p).start()
        for p in range(WEIGHT_PARTS):
            part(e, p).wait()
        _prep_expert_weights(wgu_land, wdn_land, wgu_bf, wdn_perm, wdn_bf)
        issued[0] = 0

    @pl.when(valid)
    def _():
        nxt = next_ref[e]
        for p in range(WEIGHT_PARTS):
            @pl.when(jnp.logical_and(nxt >= 0, issued[0] == p))
            def _():
                part(nxt, p).start()

        @pl.when(jnp.logical_and(nxt >= 0, issued[0] < WEIGHT_PARTS))
        def _():
            issued[0] = issued[0] + 1

        o_ref[...] = _ffn_rows(x_ref[...].astype(BF16), wgu_bf, bgu_ref[0, 0], wdn_bf, bdn_ref[0, 0])


def _moe_grouped(tile_expert, n_tiles, next_expert, xs, w_gate_up, b_gate_up4, w_down, b_down4, layer):
    d, f2 = D_MODEL, 2 * D_FF
    row_tile = lambda r, te, nt, nx: (jnp.minimum(r, nt[0] - 1), 0)
    expert = lambda r, te, nt, nx: (layer, te[r], 0, 0)
    grid_spec = pltpu.PrefetchScalarGridSpec(
        num_scalar_prefetch=3,
        grid=(MOE_TILES,),
        in_specs=[
            pl.BlockSpec((MOE_TM, d), row_tile),
            pl.BlockSpec(memory_space=pl.ANY),
            pl.BlockSpec((1, 1, 1, f2), expert),
            pl.BlockSpec(memory_space=pl.ANY),
            pl.BlockSpec((1, 1, 1, d), expert),
        ],
        out_specs=pl.BlockSpec((MOE_TM, d), row_tile),
        scratch_shapes=[
            pltpu.VMEM((d, f2), F32),
            pltpu.VMEM((D_FF, d), F32),
            pltpu.VMEM((d, f2), BF16),
            pltpu.VMEM((d // LANES, D_FF, LANES), F32),
            pltpu.VMEM((D_FF, d), BF16),
            pltpu.SMEM((1,), jnp.int32),
            pltpu.SemaphoreType.DMA((WEIGHT_PARTS,)),
        ],
    )
    return pl.pallas_call(
        functools.partial(_moe_kernel, layer),
        grid_spec=grid_spec,
        out_shape=jax.ShapeDtypeStruct((MOE_ROWS, d), F32),
        compiler_params=_cparams(("arbitrary",), 56),
        name="moe_grouped",
    )(tile_expert, n_tiles, next_expert, xs, w_gate_up, b_gate_up4, w_down, b_down4)


CMB_TM = 256


def _combine_kernel(final, pos_ref, pos_next_ref, ys_ref, x1_ref, topw_ref, mod_ref, *rest):
    buf, sems = rest[-2:]
    i = pl.program_id(0)
    slot = i % 2

    def issue_tile(p_ref, s):
        def issue(t, carry):
            for k in range(TOP_K):
                pltpu.make_async_copy(ys_ref.at[pl.ds(p_ref[0, 0, t * TOP_K + k], 1)],
                                      buf.at[s, k, pl.ds(t, 1)], sems.at[s]).start()
            return carry

        lax.fori_loop(0, CMB_TM, issue, 0, unroll=8)

    @pl.when(i == 0)
    def _():
        issue_tile(pos_ref, 0)

    @pl.when(i + 1 < pl.num_programs(0))
    def _():
        issue_tile(pos_next_ref, 1 - slot)

    for k in range(TOP_K):
        pltpu.make_async_copy(ys_ref.at[pl.ds(0, CMB_TM)], buf.at[slot, k], sems.at[slot]).wait()
    y = topw_ref[:, 0:1] * buf[slot, 0]
    for k in range(1, TOP_K):
        y = y + topw_ref[:, k:k + 1] * buf[slot, k]
    x = x1_ref[...] + mod_ref[0, 5:6, :] * y
    if not final:
        rest[0][...] = x
    else:
        fg_ref, o_ctx_ref, o_lat_ref = rest[:3]
        out = _rms(x) * fg_ref[...]
        is_ctx = pl.program_id(0) < N_CTX_TOK // CMB_TM

        @pl.when(is_ctx)
        def _():
            o_ctx_ref[...] = out

        @pl.when(jnp.logical_not(is_ctx))
        def _():
            o_lat_ref[...] = out


def _combine(pos, ys, x1, topw, mod_l, final_g=None):
    tm, d = CMB_TM, D_MODEL
    n_tiles = N_TOK // tm
    n_ctx = N_CTX_TOK // tm
    final = final_g is not None
    pos_tiles = pos.reshape(n_tiles, 1, tm * TOP_K)
    in_specs = [
        pl.BlockSpec((1, 1, tm * TOP_K), lambda i: (i, 0, 0), memory_space=pltpu.SMEM),
        pl.BlockSpec((1, 1, tm * TOP_K), lambda i: (jnp.minimum(i + 1, n_tiles - 1), 0, 0),
                     memory_space=pltpu.SMEM),
        pl.BlockSpec(memory_space=pl.ANY),
        pl.BlockSpec((tm, d), lambda i: (i, 0)),
        pl.BlockSpec((tm, TOP_K), lambda i: (i, 0)),
        pl.BlockSpec((1, 6, d), lambda i: (_mod_row(i, tm), 0, 0)),
    ]
    args = [pos_tiles, pos_tiles, ys, x1, topw, mod_l]
    if final:
        in_specs.append(pl.BlockSpec((1, d), lambda i: (0, 0)))
        args.append(final_g.reshape(1, d))
        out_specs = [pl.BlockSpec((tm, d), lambda i: (jnp.minimum(i, n_ctx - 1), 0)),
                     pl.BlockSpec((tm, d), lambda i: (jnp.maximum(i - n_ctx, 0), 0))]
        out_shape = [jax.ShapeDtypeStruct((N_CTX_TOK, d), F32), jax.ShapeDtypeStruct((N_LAT_TOK, d), F32)]
    else:
        out_specs = pl.BlockSpec((tm, d), lambda i: (i, 0))
        out_shape = jax.ShapeDtypeStruct((N_TOK, d), F32)
    return pl.pallas_call(
        functools.partial(_combine_kernel, final),
        grid=(n_tiles,),
        in_specs=in_specs,
        out_specs=out_specs,
        out_shape=out_shape,
        scratch_shapes=[pltpu.VMEM((2, TOP_K, tm, d), F32), pltpu.SemaphoreType.DMA((2,))],
        compiler_params=_cparams(("arbitrary",), 40),
        name="combine_final" if final else "combine",
    )(*args)


def kernel(x_prompt, x_sample, cache_k, cache_v, state_ret_fwd, state_ret_bwd, c, c_ctx, norm1_g, norm2_g, w_mod, b_mod, w_in, lam_a, subln_a, ret_decay, ret_norm, w_branch_a, w_branch_b, w_branch_c, w_out, router_w, router_b, w_gate_up, b_gate_up, w_down, b_down, final_g):
    d = D_MODEL
    x = jnp.concatenate([x_prompt.reshape(N_CTX_TOK, d), x_sample.reshape(N_LAT_TOK, d)], axis=0)
    cond8 = jnp.concatenate([c_ctx[None, :], c, jnp.zeros((3, d), F32)], axis=0)
    mod = _modulation(cond8, w_mod, b_mod)

    cache_k4 = cache_k.reshape(DEC_BATCH, DEPTH, PAST_LEN, W_A)
    cache_v4 = cache_v.reshape(DEC_BATCH, DEPTH, PAST_LEN, W_A)
    s0f4 = state_ret_fwd.reshape(DEC_BATCH, DEPTH, H_C * DK_C, DV_C)
    s0b4 = state_ret_bwd.reshape(DEC_BATCH, DEPTH, H_C * DK_C, DV_C)
    b_gate_up4 = b_gate_up.reshape(DEPTH, N_EXPERTS, 1, 2 * D_FF)
    b_down4 = b_down.reshape(DEPTH, N_EXPERTS, 1, d)
    cos, sin = _rope_tables(DEC_SEQ)
    chan, pos_ctx = _dft_consts(SEQ)
    _, pos_lat = _dft_consts(DEC_SEQ)

    caches, states = None, None
    for l in range(DEPTH):
        lam_init = 0.8 - 0.6 * math.exp(-0.3 * l)
        z, new_k, new_v = _inproj(x, mod[l], norm1_g[l], w_in, l, caches)
        caches = (new_k, new_v)
        ya = _attn_ctx(z, lam_a[l], subln_a[l], lam_init)
        ya = _attn_lat(z, cache_k4, cache_v4, l, cos, sin, lam_a[l], subln_a[l], lam_init, ya)
        yb = _fourier(z, SEQ, BATCH, 0, chan, pos_ctx)
        yb = _fourier(z, DEC_SEQ, DEC_BATCH, N_CTX_TOK // DEC_SEQ, chan, pos_lat, yb)
        yc, new_sf, new_sb = _ret_ctx(z, ret_decay[l], ret_norm[l], l, states)
        states = (new_sf, new_sb)
        yc = _ret_lat(z, ret_decay[l], ret_norm[l], s0f4, s0b4, l, yc)
        x1, h2, topi, topw, rank, cnt_row, cnt_col = _merge(
            x, ya, yb, yc, z, mod[l], w_branch_a, w_branch_b, w_branch_c, w_out,
            norm2_g[l], router_w[l], router_b[l], l)
        pos, tile_expert, n_tiles, last, next_expert = _plan(topi, rank, cnt_row, cnt_col)
        xs = _dispatch(pos, last, h2)
        ys = _moe_grouped(tile_expert.reshape(PLAN_LANES), n_tiles[0, :1], next_expert.reshape(N_EXPERTS),
                          xs, w_gate_up, b_gate_up4, w_down, b_down4, l)
        if l < DEPTH - 1:
            x = _combine(pos, ys, x1, topw, mod[l])
        else:
            y_ctx, y_lat = _combine(pos, ys, x1, topw, mod[l], final_g)

    return (y_ctx.reshape(BATCH, SEQ, d), y_lat.reshape(DEC_BATCH, DEC_SEQ, d),
            caches[0].reshape(BATCH, DEPTH, SEQ, H_A, 2, DH_A),
            caches[1].reshape(BATCH, DEPTH, SEQ, H_A, DV_A),
            states[0], states[1])
```

```python
import functools
import math

import numpy as np
import jax
import jax.numpy as jnp
from jax import lax
from jax.experimental import pallas as pl
from jax.experimental.pallas import tpu as pltpu

F32 = jnp.float32
BF16 = jnp.bfloat16

D_MODEL = 1024
BATCH = 16
SEQ = 256
DEPTH = 2
DEC_BATCH = 4
DEC_SEQ = 1024
PAST_LEN = 512
GRID_W = 64
H_A = 4
DH_A = 64
DV_A = 128
G_B = 4
DG_B = 128
H_C = 4
DK_C = 64
DV_C = 128
N_EXPERTS = 32
TOP_K = 4
D_FF = 1024
SWIGLU_LIMIT = 7.0
SWIGLU_ALPHA = 1.702
ROPE_BASE = 10000.0
EPS = 1e-6

N_CTX_TOK = BATCH * SEQ
N_LAT_TOK = DEC_BATCH * DEC_SEQ
N_TOK = N_CTX_TOK + N_LAT_TOK
IN_TOTAL = 6656
Z_QA = 0
Z_KA = 512
Z_VA = 1024
Z_FB = 1536
Z_QC = 2048
Z_KC = 2304
Z_VC = 2560
Z_GC = 3072
Z_GATES = 3584
LANES = 128
MIB = 1024 * 1024


def _cparams(semantics, vmem_mib):
    return pltpu.CompilerParams(dimension_semantics=semantics,
                                vmem_limit_bytes=vmem_mib * MIB)


def _dot(a, b):
    return jnp.dot(a, b, preferred_element_type=F32)


def _dot_nt(a, b):
    return lax.dot_general(a, b, (((1,), (1,)), ((), ())), preferred_element_type=F32)


def _dot_tn(a, b):
    return lax.dot_general(a, b, (((0,), (0,)), ((), ())), preferred_element_type=F32)


def _split_bf16(a):
    hi = a.astype(BF16)
    lo = (a - hi.astype(F32)).astype(BF16)
    return hi, lo


def _dot_f32x3(a, b):
    a_hi, a_lo = _split_bf16(a)
    b_hi, b_lo = _split_bf16(b)
    return _dot(a_hi, b_hi) + _dot(a_lo, b_hi) + _dot(a_hi, b_lo)


def _rms(x):
    return x * lax.rsqrt(jnp.mean(x * x, axis=-1, keepdims=True) + EPS)


def _modnorm(x, g, scale, shift):
    return (_rms(x) * g) * (1.0 + scale) + shift


def _sigmoid(x):
    return 1.0 / (1.0 + jnp.exp(-x))


MOD_TN = 1536


def _mod_kernel(cond_ref, w_ref, b_ref, o_ref):
    cnd = cond_ref[...]
    s = cnd * _sigmoid(cnd)
    o_ref[0] = _dot_f32x3(s, w_ref[0]) + b_ref[0]


def _modulation(cond8, w_mod, b_mod):
    n = 6 * D_MODEL
    out = pl.pallas_call(
        _mod_kernel,
        grid=(DEPTH, n // MOD_TN),
        in_specs=[
            pl.BlockSpec((8, D_MODEL), lambda l, j: (0, 0)),
            pl.BlockSpec((1, D_MODEL, MOD_TN), lambda l, j: (l, 0, j)),
            pl.BlockSpec((1, 1, MOD_TN), lambda l, j: (l, 0, j)),
        ],
        out_specs=pl.BlockSpec((1, 8, MOD_TN), lambda l, j: (l, 0, j)),
        out_shape=jax.ShapeDtypeStruct((DEPTH, 8, n), F32),
        compiler_params=_cparams(("parallel", "parallel"), 40),
        name="modulation",
    )(cond8, w_mod, b_mod.reshape(DEPTH, 1, n))
    return out.reshape(DEPTH, 8, 6, D_MODEL)


def _mod_row(tile_idx, tm):
    n_ctx_tiles = N_CTX_TOK // tm
    per_batch = DEC_SEQ // tm
    return jnp.where(tile_idx < n_ctx_tiles, 0, 1 + (tile_idx - n_ctx_tiles) // per_batch)


INP_TM = 1024
INP_TN = 1664
INP_NJ = IN_TOTAL // INP_TN
CTX_PER_TILE = INP_TM // SEQ
assert Z_VA + H_A * DV_A <= INP_TN


def _inproj_kernel(x_ref, mod_ref, g_ref, w_ref, *rest):
    z_ref, kc_ref, vc_ref, h_scr = rest[-4:]
    i = pl.program_id(0)
    j = pl.program_id(1)

    @pl.when(j == 0)
    def _():
        h = _modnorm(x_ref[...], g_ref[...], mod_ref[0, 1:2, :], mod_ref[0, 0:1, :])
        h_scr[...] = h.astype(BF16)

    z_ref[...] = _dot(h_scr[...], w_ref[0].astype(BF16)).astype(BF16)

    @pl.when(jnp.logical_and(i < N_CTX_TOK // INP_TM, j == 0))
    def _():
        wk = w_ref[0, :, Z_KA:Z_KA + W_A].astype(BF16)
        wv = w_ref[0, :, Z_VA:Z_VA + W_A].astype(BF16)
        for b in range(CTX_PER_TILE):
            hb = h_scr[b * SEQ:(b + 1) * SEQ, :]
            kc_ref[b, 0] = _dot(hb, wk)
            vc_ref[b, 0] = _dot(hb, wv)


def _inproj(x, mod_l, g, w_in, layer, prev_caches):
    ni = N_TOK // INP_TM
    n_ctx_tiles = N_CTX_TOK // INP_TM
    cache_shape = jax.ShapeDtypeStruct((BATCH, DEPTH, SEQ, W_A), F32)
    cache_spec = pl.BlockSpec((CTX_PER_TILE, 1, SEQ, W_A),
                              lambda i, j: (jnp.minimum(i, n_ctx_tiles - 1), layer, 0, 0))
    in_specs = [
        pl.BlockSpec((INP_TM, D_MODEL), lambda i, j: (i, 0)),
        pl.BlockSpec((1, 6, D_MODEL), lambda i, j: (_mod_row(i, INP_TM), 0, 0)),
        pl.BlockSpec((1, D_MODEL), lambda i, j: (0, 0)),
        pl.BlockSpec((1, D_MODEL, INP_TN), lambda i, j: (layer, 0, j)),
    ]
    args = [x, mod_l, g.reshape(1, D_MODEL), w_in]
    aliases = {}
    if prev_caches is not None:
        in_specs += [pl.BlockSpec(memory_space=pl.ANY)] * 2
        args += list(prev_caches)
        aliases = {4: 1, 5: 2}
    return pl.pallas_call(
        _inproj_kernel,
        grid=(ni, INP_NJ),
        in_specs=in_specs,
        out_specs=[pl.BlockSpec((INP_TM, INP_TN), lambda i, j: (i, j)), cache_spec, cache_spec],
        out_shape=[jax.ShapeDtypeStruct((N_TOK, IN_TOTAL), BF16), cache_shape, cache_shape],
        scratch_shapes=[pltpu.VMEM((INP_TM, D_MODEL), BF16)],
        input_output_aliases=aliases,
        compiler_params=_cparams(("arbitrary", "arbitrary"), 56),
        name="inproj",
    )(*args)


ATT_TQ = 256
W_A = H_A * DV_A


def _lambda(lam_ref, lam_init):
    la = lam_ref[...]
    l01 = jnp.sum(la[0:1, :] * la[1:2, :], axis=1, keepdims=True)
    l23 = jnp.sum(la[2:3, :] * la[3:4, :], axis=1, keepdims=True)
    return jnp.exp(l01) - jnp.exp(l23) + lam_init


def _softmax_parts(s):
    m = jnp.max(s, axis=-1, keepdims=True)
    e = jnp.exp(s - m)
    return e, 1.0 / jnp.sum(e, axis=-1, keepdims=True)


def _diff_attn_block(q, k_ref, v_ref, lam, subln, lam_init):
    lane = lax.broadcasted_iota(jnp.int32, (1, LANES), 1)
    first = lane < DH_A
    scale = jnp.asarray(DH_A ** -0.5, BF16)
    outs = []
    for h in range(H_A):
        sl = slice(h * LANES, (h + 1) * LANES)
        qh = q[:, sl]
        kh = k_ref[:, sl]
        zero = jnp.zeros_like(qh)
        qs = qh * scale
        s0 = _dot_nt(jnp.where(first, qs, zero), kh)
        s1 = _dot_nt(jnp.where(first, zero, qs), kh)
        e0, r0 = _softmax_parts(s0)
        e1, r1 = _softmax_parts(s1)
        a = e0 * r0 - (lam * r1) * e1
        o = _dot(a.astype(BF16), v_ref[:, sl])
        outs.append((_rms(o) * subln) * (1.0 - lam_init))
    return outs


def _rope(x, cos, sin):
    lane = lax.broadcasted_iota(jnp.int32, (1, x.shape[1]), 1)
    first = (lane % 32) < 16
    n = x.shape[1]
    xr = jnp.where(first, -pltpu.roll(x, n - 16, 1), pltpu.roll(x, 16, 1))
    return x * cos + xr * sin


def _attn_ctx_kernel(lam_init, q_ref, k_ref, v_ref, lam_ref, subln_ref, o_ref):
    lam = _lambda(lam_ref, lam_init)
    outs = _diff_attn_block(q_ref[...], k_ref, v_ref, lam, subln_ref[...], lam_init)
    for h in range(H_A):
        o_ref[:, h * LANES:(h + 1) * LANES] = outs[h].astype(BF16)


def _attn_ctx(z, lam_a_l, subln_l, lam_init):
    wb = W_A
    return pl.pallas_call(
        functools.partial(_attn_ctx_kernel, lam_init),
        grid=(BATCH,),
        in_specs=[
            pl.BlockSpec((SEQ, wb), lambda b: (b, Z_QA // wb)),
            pl.BlockSpec((SEQ, wb), lambda b: (b, Z_KA // wb)),
            pl.BlockSpec((SEQ, wb), lambda b: (b, Z_VA // wb)),
            pl.BlockSpec((4, DH_A), lambda b: (0, 0)),
            pl.BlockSpec((1, DV_A), lambda b: (0, 0)),
        ],
        out_specs=pl.BlockSpec((SEQ, wb), lambda b: (b, 0)),
        out_shape=jax.ShapeDtypeStruct((N_TOK, wb), BF16),
        compiler_params=_cparams(("parallel",), 40),
        name="attn_ctx",
    )(z, z, z, lam_a_l, subln_l.reshape(1, DV_A))


def _attn_lat_kernel(lam_init, q_ref, k_ref, v_ref, ck_ref, cv_ref, cos_ref, sin_ref,
                     lam_ref, subln_ref, ya_ctx_ref, o_ref, kall, vall):
    del ya_ctx_ref
    lam = _lambda(lam_ref, lam_init)
    subln = subln_ref[...]
    kall[0:DEC_SEQ, :] = _rope(k_ref[...].astype(F32), cos_ref[...], sin_ref[...]).astype(BF16)
    kall[DEC_SEQ:, :] = ck_ref[0, 0].astype(BF16)
    vall[0:DEC_SEQ, :] = v_ref[...]
    vall[DEC_SEQ:, :] = cv_ref[0, 0].astype(BF16)

    def q_block(qb, carry):
        rows = pl.ds(pl.multiple_of(qb * ATT_TQ, ATT_TQ), ATT_TQ)
        q = _rope(q_ref[rows, :].astype(F32), cos_ref[rows, :], sin_ref[rows, :]).astype(BF16)
        outs = _diff_attn_block(q, kall, vall, lam, subln, lam_init)
        for h in range(H_A):
            o_ref[rows, h * LANES:(h + 1) * LANES] = outs[h].astype(BF16)
        return carry

    lax.fori_loop(0, DEC_SEQ // ATT_TQ, q_block, 0)


def _attn_lat(z, cache_k4, cache_v4, layer, cos, sin, lam_a_l, subln_l, lam_init, ya_ctx):
    wb = W_A
    rb0 = N_CTX_TOK // DEC_SEQ
    t_all = DEC_SEQ + PAST_LEN
    return pl.pallas_call(
        functools.partial(_attn_lat_kernel, lam_init),
        grid=(DEC_BATCH,),
        in_specs=[
            pl.BlockSpec((DEC_SEQ, wb), lambda b: (rb0 + b, Z_QA // wb)),
            pl.BlockSpec((DEC_SEQ, wb), lambda b: (rb0 + b, Z_KA // wb)),
            pl.BlockSpec((DEC_SEQ, wb), lambda b: (rb0 + b, Z_VA // wb)),
            pl.BlockSpec((1, 1, PAST_LEN, wb), lambda b: (b, layer, 0, 0)),
            pl.BlockSpec((1, 1, PAST_LEN, wb), lambda b: (b, layer, 0, 0)),
            pl.BlockSpec((DEC_SEQ, wb), lambda b: (0, 0)),
            pl.BlockSpec((DEC_SEQ, wb), lambda b: (0, 0)),
            pl.BlockSpec((4, DH_A), lambda b: (0, 0)),
            pl.BlockSpec((1, DV_A), lambda b: (0, 0)),
            pl.BlockSpec(memory_space=pl.ANY),
        ],
        out_specs=pl.BlockSpec((DEC_SEQ, wb), lambda b: (rb0 + b, 0)),
        out_shape=jax.ShapeDtypeStruct((N_TOK, wb), BF16),
        scratch_shapes=[pltpu.VMEM((t_all, wb), BF16), pltpu.VMEM((t_all, wb), BF16)],
        input_output_aliases={9: 0},
        compiler_params=_cparams(("parallel",), 48),
        name="attn_lat",
    )(z, z, z, cache_k4, cache_v4, cos, sin, lam_a_l, subln_l.reshape(1, DV_A), ya_ctx)


def _rope_tables(s):
    rows = s // GRID_W
    row = jnp.repeat(jnp.arange(rows, dtype=F32), GRID_W)
    col = jnp.tile(jnp.arange(GRID_W, dtype=F32), rows)
    quarter = DH_A // 4
    inv = ROPE_BASE ** (-jnp.arange(quarter, dtype=F32) / quarter)
    ar = row[:, None] * inv
    ac = col[:, None] * inv
    ang = jnp.concatenate([ar, ar, ac, ac], axis=-1)
    reps = (H_A * 2 * DH_A) // DH_A
    return jnp.tile(jnp.cos(ang), (1, reps)), jnp.tile(jnp.sin(ang), (1, reps))


W_B = G_B * DG_B


def _dft_consts(s):
    def cs(n, scale):
        j = np.arange(n, dtype=np.int64)
        ang = 2.0 * np.pi * ((j[:, None] * j[None, :]) % n).astype(np.float64) / n
        return np.cos(ang) * scale, np.sin(ang) * scale
    cc, sc = cs(DG_B, DG_B ** -0.5)
    chan = np.concatenate([cc, sc], axis=1)
    cp, sp = cs(s, s ** -0.5)
    pos = np.concatenate([cp, -sp], axis=1)
    return jnp.asarray(chan, dtype=F32).astype(BF16), jnp.asarray(pos, dtype=F32).astype(BF16)


def _fourier_kernel(s, f_ref, chan_ref, pos_ref, *rest):
    o_ref, p_ref = rest[-2:]
    chan = chan_ref[...]
    for g in range(G_B):
        sl = slice(g * DG_B, (g + 1) * DG_B)
        a = _dot(f_ref[:, sl], chan)
        p_ref[0:s, sl] = a[:, :DG_B].astype(BF16)
        p_ref[s:2 * s, sl] = a[:, DG_B:].astype(BF16)
    o_ref[...] = _dot(pos_ref[...], p_ref[...]).astype(BF16)


def _fourier(z, s, n_batch, row_block0, chan, pos, prev=None):
    in_specs = [
        pl.BlockSpec((s, W_B), lambda b: (row_block0 + b, Z_FB // W_B)),
        pl.BlockSpec((DG_B, 2 * DG_B), lambda b: (0, 0)),
        pl.BlockSpec((s, 2 * s), lambda b: (0, 0)),
    ]
    args = [z, chan, pos]
    aliases = {}
    if prev is not None:
        in_specs.append(pl.BlockSpec(memory_space=pl.ANY))
        args.append(prev)
        aliases = {3: 0}
    return pl.pallas_call(
        functools.partial(_fourier_kernel, s),
        grid=(n_batch,),
        in_specs=in_specs,
        out_specs=pl.BlockSpec((s, W_B), lambda b: (row_block0 + b, 0)),
        out_shape=jax.ShapeDtypeStruct((N_TOK, W_B), BF16),
        scratch_shapes=[pltpu.VMEM((2 * s, W_B), BF16)],
        input_output_aliases=aliases,
        compiler_params=_cparams(("parallel",), 40),
        name="fourier_%d" % s,
    )(*args)


W_C = H_C * DV_C
RET_TQ = 256


def _log_sigmoid(x):
    return jnp.minimum(x, 0.0) - jnp.log1p(jnp.exp(-jnp.abs(x)))


def _head_pair_lanes(vals, pair):
    lane = lax.broadcasted_iota(jnp.int32, (1, LANES), 1)
    return jnp.where(lane < DK_C, vals[:, 2 * pair:2 * pair + 1], vals[:, 2 * pair + 1:2 * pair + 2])


def _retention_rows(s, row0, tq, q, k_ref, v_ref, g, lg, rnorm, s0f_ref, s0b_ref):
    lane = lax.broadcasted_iota(jnp.int32, (1, LANES), 1)
    ii = (row0 + lax.broadcasted_iota(jnp.int32, (tq, s), 0)).astype(F32)
    jj = lax.broadcasted_iota(jnp.int32, (tq, s), 1).astype(F32)
    rel = ii - jj
    pos = (row0 + lax.broadcasted_iota(jnp.int32, (tq, 1), 0)).astype(F32)
    outs = []
    for h in range(H_C):
        pair, half = h // 2, h % 2
        sl = slice(pair * LANES, (pair + 1) * LANES)
        lgf = lg[0:1, h:h + 1]
        lgb = lg[1:2, h:h + 1]
        in_head = (lane < DK_C) if half == 0 else (lane >= DK_C)
        q_pair = q[:, sl]
        qh = jnp.where(in_head, q_pair, jnp.zeros_like(q_pair))
        sc = _dot_nt(qh, k_ref[:, sl]) * (DK_C ** -0.5)
        decay = (jnp.where(rel >= 0, jnp.exp(lgf * jnp.maximum(rel, 0.0)), 0.0)
                 + jnp.where(rel <= 0, jnp.exp(lgb * jnp.maximum(-rel, 0.0)), 0.0))
        o = _dot((sc * decay).astype(BF16), v_ref[:, h * DV_C:(h + 1) * DV_C])
        if s0f_ref is not None:
            qf = qh.astype(F32)
            xf = jnp.exp(lgf * (pos + 1.0))
            xb = jnp.exp(lgb * (s - pos))
            o = o + _dot((qf * xf).astype(BF16), s0f_ref[0, 0, sl, :].astype(BF16))
            o = o + _dot((qf * xb).astype(BF16), s0b_ref[0, 0, sl, :].astype(BF16))
        gh = g[:, h * DV_C:(h + 1) * DV_C].astype(F32)
        outs.append((_rms(o) * rnorm) * (gh * _sigmoid(gh)))
    return outs


def _ret_ctx_kernel(q_ref, k_ref, v_ref, g_ref, dec_ref, rn_ref, *rest):
    o_ref, sf_ref, sb_ref = rest[-3:]
    s = SEQ
    lg = _log_sigmoid(dec_ref[...])
    outs = _retention_rows(s, 0, s, q_ref[...], k_ref, v_ref, g_ref[...], lg, rn_ref[...], None, None)
    for h in range(H_C):
        o_ref[:, h * DV_C:(h + 1) * DV_C] = outs[h].astype(BF16)
    pos = lax.broadcasted_iota(jnp.int32, (s, 1), 0).astype(F32)
    v = v_ref[...]
    for pair in range(H_C // 2):
        sl = slice(pair * LANES, (pair + 1) * LANES)
        kf = k_ref[:, sl].astype(F32) * (DK_C ** -0.5)
        zf = jnp.exp(_head_pair_lanes(lg[0:1, :], pair) * (s - 1.0 - pos))
        zb = jnp.exp(_head_pair_lanes(lg[1:2, :], pair) * pos)
        stf = _dot_tn((kf * zf).astype(BF16), v)
        stb = _dot_tn((kf * zb).astype(BF16), v)
        for half in range(2):
            h = 2 * pair + half
            rs = slice(half * DK_C, (half + 1) * DK_C)
            cs = slice(h * DV_C, (h + 1) * DV_C)
            sf_ref[0, 0, h] = stf[rs, cs]
            sb_ref[0, 0, h] = stb[rs, cs]


def _ret_ctx(z, dec_l, rnorm_l, layer, prev_states):
    wq = H_C * DK_C
    st_shape = jax.ShapeDtypeStruct((BATCH, DEPTH, H_C, DK_C, DV_C), F32)
    st_spec = pl.BlockSpec((1, 1, H_C, DK_C, DV_C), lambda b: (b, layer, 0, 0, 0))
    in_specs = [
        pl.BlockSpec((SEQ, wq), lambda b: (b, Z_QC // wq)),
        pl.BlockSpec((SEQ, wq), lambda b: (b, Z_KC // wq)),
        pl.BlockSpec((SEQ, W_C), lambda b: (b, Z_VC // W_C)),
        pl.BlockSpec((SEQ, W_C), lambda b: (b, Z_GC // W_C)),
        pl.BlockSpec((2, H_C), lambda b: (0, 0)),
        pl.BlockSpec((1, DV_C), lambda b: (0, 0)),
    ]
    args = [z, z, z, z, dec_l, rnorm_l.reshape(1, DV_C)]
    aliases = {}
    if prev_states is not None:
        in_specs += [pl.BlockSpec(memory_space=pl.ANY)] * 2
        args += list(prev_states)
        aliases = {6: 1, 7: 2}
    return pl.pallas_call(
        _ret_ctx_kernel,
        grid=(BATCH,),
        in_specs=in_specs,
        out_specs=[pl.BlockSpec((SEQ, W_C), lambda b: (b, 0)), st_spec, st_spec],
        out_shape=[jax.ShapeDtypeStruct((N_TOK, W_C), BF16), st_shape, st_shape],
        input_output_aliases=aliases,
        compiler_params=_cparams(("parallel",), 40),
        name="ret_ctx",
    )(*args)


def _ret_lat_kernel(q_ref, k_ref, v_ref, g_ref, dec_ref, rn_ref, s0f_ref, s0b_ref, yc_ctx_ref, o_ref):
    del yc_ctx_ref
    lg = _log_sigmoid(dec_ref[...])
    rnorm = rn_ref[...]

    def q_block(qb, carry):
        row0 = pl.multiple_of(qb * RET_TQ, RET_TQ)
        rows = pl.ds(row0, RET_TQ)
        outs = _retention_rows(DEC_SEQ, row0, RET_TQ, q_ref[rows, :], k_ref, v_ref, g_ref[rows, :],
                               lg, rnorm, s0f_ref, s0b_ref)
        for h in range(H_C):
            o_ref[rows, h * DV_C:(h + 1) * DV_C] = outs[h].astype(BF16)
        return carry

    lax.fori_loop(0, DEC_SEQ // RET_TQ, q_block, 0)


def _ret_lat(z, dec_l, rnorm_l, s0f4, s0b4, layer, yc_ctx):
    wq = H_C * DK_C
    rb0 = N_CTX_TOK // DEC_SEQ
    st_spec = pl.BlockSpec((1, 1, H_C * DK_C, DV_C), lambda b: (b, layer, 0, 0))
    return pl.pallas_call(
        _ret_lat_kernel,
        grid=(DEC_BATCH,),
        in_specs=[
            pl.BlockSpec((DEC_SEQ, wq), lambda b: (rb0 + b, Z_QC // wq)),
            pl.BlockSpec((DEC_SEQ, wq), lambda b: (rb0 + b, Z_KC // wq)),
            pl.BlockSpec((DEC_SEQ, W_C), lambda b: (rb0 + b, Z_VC // W_C)),
            pl.BlockSpec((DEC_SEQ, W_C), lambda b: (rb0 + b, Z_GC // W_C)),
            pl.BlockSpec((2, H_C), lambda b: (0, 0)),
            pl.BlockSpec((1, DV_C), lambda b: (0, 0)),
            st_spec, st_spec,
            pl.BlockSpec(memory_space=pl.ANY),
        ],
        out_specs=pl.BlockSpec((DEC_SEQ, W_C), lambda b: (rb0 + b, 0)),
        out_shape=jax.ShapeDtypeStruct((N_TOK, W_C), BF16),
        input_output_aliases={8: 0},
        compiler_params=_cparams(("parallel",), 40),
        name="ret_lat",
    )(z, z, z, z, dec_l, rnorm_l.reshape(1, DV_C), s0f4, s0b4, yc_ctx)


MRG_TM = 512
MRG_SUBBLOCKS = 2
GATE_BLOCK = 512


def _merge_kernel(x_ref, ya_ref, yb_ref, yc_ref, g0_ref, g1_ref, g2_ref, g3_ref, g4_ref, g5_ref,
                  mod_ref, wa_ref, wb_ref, wc_ref, wo_ref,
                  n2_ref, rw_ref, rb_ref, x1_ref, h2_ref, topi_ref, topw_ref, rank_ref,
                  cnt_row_ref, cnt_col_ref, carry_row, carry_col, wa_bf, wb_bf, wc_bf, wo_bf):
    d = D_MODEL
    tm = x_ref.shape[0]

    @pl.when(pl.program_id(0) == 0)
    def _():
        carry_row[...] = jnp.zeros_like(carry_row)
        carry_col[...] = jnp.zeros_like(carry_col)
        wa_bf[...] = wa_ref[0].astype(BF16)
        wb_bf[...] = wb_ref[0].astype(BF16)
        wc_bf[...] = wc_ref[0].astype(BF16)
        wo_bf[...] = wo_ref[0].astype(BF16)

    def gate(lo_ref, hi_ref, rs):
        return _sigmoid(jnp.concatenate([lo_ref[rs, :], hi_ref[rs, :]], axis=1).astype(F32))

    sub = tm // MRG_SUBBLOCKS
    for sb in range(MRG_SUBBLOCKS):
        rs = slice(sb * sub, (sb + 1) * sub)
        merged = gate(g0_ref, g1_ref, rs) * _dot(ya_ref[rs, :], wa_bf[...])
        merged += gate(g2_ref, g3_ref, rs) * _dot(yb_ref[rs, :], wb_bf[...])
        merged += gate(g4_ref, g5_ref, rs) * _dot(yc_ref[rs, :], wc_bf[...])
        y = _dot(merged.astype(BF16), wo_bf[...])
        x1 = x_ref[rs, :] + mod_ref[0, 2:3, :] * y
        x1_ref[rs, :] = x1
        h2 = _modnorm(x1, n2_ref[...], mod_ref[0, 4:5, :], mod_ref[0, 3:4, :])
        h2_ref[rs, :] = h2
        logits = _dot_f32x3(h2, rw_ref[...]) + rb_ref[...]
        ids = lax.broadcasted_iota(jnp.int32, logits.shape, 1)
        work = logits
        sels, vals, firsts = [], [], []
        for _ in range(TOP_K):
            m = jnp.max(work, axis=-1, keepdims=True)
            first = jnp.min(jnp.where(work == m, ids, N_EXPERTS), axis=-1, keepdims=True)
            sel = ids == first
            sels.append(sel)
            vals.append(m)
            firsts.append(first)
            work = jnp.where(sel, -jnp.inf, work)
        es = [jnp.exp(v - vals[0]) for v in vals]
        inv = 1.0 / (es[0] + es[1] + es[2] + es[3])
        chosen = jnp.zeros_like(logits)
        for sel in sels:
            chosen = chosen + jnp.where(sel, 1.0, 0.0)
        chosen_bf = chosen.astype(BF16)
        rr = lax.broadcasted_iota(jnp.int32, (sub, sub), 0)
        cc = lax.broadcasted_iota(jnp.int32, (sub, sub), 1)
        before = jnp.where(rr > cc, 1.0, 0.0).astype(BF16)
        prefix = _dot(before, chosen_bf) + carry_row[...]
        ranks = [jnp.sum(jnp.where(sel, prefix, 0.0), axis=-1, keepdims=True) for sel in sels]
        carry_row[...] += jnp.sum(chosen, axis=0, keepdims=True)
        carry_col[...] += _dot_tn(chosen_bf, jnp.ones((sub, LANES), BF16))
        topi_ref[rs, :] = jnp.concatenate(firsts, axis=1)
        topw_ref[rs, :] = jnp.concatenate([e * inv for e in es], axis=1)
        rank_ref[rs, :] = jnp.concatenate(ranks, axis=1).astype(jnp.int32)
    cnt_row_ref[...] = carry_row[...]
    cnt_col_ref[...] = carry_col[...]


def _merge(x, ya, yb, yc, z, mod_l, wa, wb, wc, wo, n2, rw, rb, layer):
    tm = MRG_TM
    d = D_MODEL
    full = lambda shape: pl.BlockSpec(shape, lambda i: tuple(0 for _ in shape))
    tok = lambda w: pl.BlockSpec((tm, w), lambda i: (i, 0))
    per_layer = lambda rows: pl.BlockSpec((1, rows, d), lambda i: (layer, 0, 0))
    return pl.pallas_call(
        _merge_kernel,
        grid=(N_TOK // tm,),
        in_specs=[
            tok(d), tok(W_A), tok(W_B), tok(W_C),
            *[pl.BlockSpec((tm, GATE_BLOCK), lambda i, c=c: (i, Z_GATES // GATE_BLOCK + c)) for c in range(6)],
            pl.BlockSpec((1, 6, d), lambda i: (_mod_row(i, tm), 0, 0)),
            per_layer(W_A), per_layer(W_B), per_layer(W_C), per_layer(d),
            full((1, d)), full((d, N_EXPERTS)), full((1, N_EXPERTS)),
        ],
        out_specs=[tok(d), tok(d), tok(TOP_K), tok(TOP_K), tok(TOP_K),
                   full((1, N_EXPERTS)), full((N_EXPERTS, LANES))],
        out_shape=[
            jax.ShapeDtypeStruct((N_TOK, d), F32),
            jax.ShapeDtypeStruct((N_TOK, d), F32),
            jax.ShapeDtypeStruct((N_TOK, TOP_K), jnp.int32),
            jax.ShapeDtypeStruct((N_TOK, TOP_K), F32),
            jax.ShapeDtypeStruct((N_TOK, TOP_K), jnp.int32),
            jax.ShapeDtypeStruct((1, N_EXPERTS), F32),
            jax.ShapeDtypeStruct((N_EXPERTS, LANES), F32),
        ],
        scratch_shapes=[pltpu.VMEM((1, N_EXPERTS), F32), pltpu.VMEM((N_EXPERTS, LANES), F32),
                        pltpu.VMEM((W_A, d), BF16), pltpu.VMEM((W_B, d), BF16),
                        pltpu.VMEM((W_C, d), BF16), pltpu.VMEM((d, d), BF16)],
        compiler_params=_cparams(("arbitrary",), 56),
        name="merge_router",
    )(x, ya, yb, yc, z, z, z, z, z, z, mod_l, wa, wb, wc, wo, n2.reshape(1, d), rw, rb.reshape(1, N_EXPERTS))


MOE_TM = 256
MOE_TILES = (N_TOK * TOP_K) // MOE_TM + N_EXPERTS
MOE_ROWS = MOE_TILES * MOE_TM
PLAN_LANES = 256
PLAN_TM = 1024


def _plan_kernel(topi_ref, rank_ref, cnt_row_ref, cnt_col_ref, pos_ref, te_ref, nt_ref, last_ref,
                 next_ref):
    e_r = lax.broadcasted_iota(jnp.int32, (N_EXPERTS, N_EXPERTS), 0)
    e_c = lax.broadcasted_iota(jnp.int32, (N_EXPERTS, N_EXPERTS), 1)
    cnt_row = cnt_row_ref[...]
    tiles_row = jnp.ceil(cnt_row * (1.0 / MOE_TM))
    earlier = jnp.where(e_r < e_c, 1.0, 0.0).astype(BF16)
    start_tile_row = _dot(tiles_row.astype(BF16), earlier)
    off_row = start_tile_row * MOE_TM
    ids = lax.broadcasted_iota(jnp.int32, (PLAN_TM, N_EXPERTS), 1)
    cols = []
    for k in range(TOP_K):
        hit = ids == topi_ref[:, k:k + 1]
        cols.append(jnp.sum(jnp.where(hit, off_row, 0.0), axis=-1, keepdims=True))
    pos_ref[...] = jnp.concatenate(cols, axis=1).astype(jnp.int32) + rank_ref[...]
    tiles_col = jnp.ceil(cnt_col_ref[...] * (1.0 / MOE_TM))
    upto = jnp.where(e_c <= e_r, 1.0, 0.0).astype(BF16)
    end_col = _dot(upto, tiles_col.astype(BF16))
    end_b = jnp.concatenate([end_col] * (PLAN_LANES // LANES), axis=1)
    r = lax.broadcasted_iota(jnp.int32, (N_EXPERTS, PLAN_LANES), 1).astype(F32)
    total = end_b[N_EXPERTS - 1:N_EXPERTS, :]
    te = jnp.sum(jnp.where(end_b <= jnp.minimum(r, total - 1.0), 1.0, 0.0), axis=0, keepdims=True)
    te_ref[...] = te.astype(jnp.int32)
    nt_ref[...] = total[:, :LANES].astype(jnp.int32)
    last = jnp.where(cnt_row > 0.0, (start_tile_row + tiles_row - 1.0) * MOE_TM, -1.0)
    last_ref[...] = last.astype(jnp.int32)
    later_used = jnp.logical_and(e_r > e_c, cnt_col_ref[:, 0:N_EXPERTS] > 0.0)
    nxt = jnp.min(jnp.where(later_used, e_r, N_EXPERTS), axis=0, keepdims=True)
    next_ref[...] = jnp.where(nxt == N_EXPERTS, -1, nxt)


def _plan(topi, rank, cnt_row, cnt_col):
    tok = pl.BlockSpec((PLAN_TM, TOP_K), lambda i: (i, 0))
    full = lambda shape: pl.BlockSpec(shape, lambda i: tuple(0 for _ in shape))
    return pl.pallas_call(
        _plan_kernel,
        grid=(N_TOK // PLAN_TM,),
        in_specs=[tok, tok, full((1, N_EXPERTS)), full((N_EXPERTS, LANES))],
        out_specs=[tok, full((1, PLAN_LANES)), full((1, LANES)), full((1, N_EXPERTS)),
                   full((1, N_EXPERTS))],
        out_shape=[
            jax.ShapeDtypeStruct((N_TOK, TOP_K), jnp.int32),
            jax.ShapeDtypeStruct((1, PLAN_LANES), jnp.int32),
            jax.ShapeDtypeStruct((1, LANES), jnp.int32),
            jax.ShapeDtypeStruct((1, N_EXPERTS), jnp.int32),
            jax.ShapeDtypeStruct((1, N_EXPERTS), jnp.int32),
        ],
        compiler_params=_cparams(("arbitrary",), 32),
        name="route_plan",
    )(topi, rank, cnt_row, cnt_col)


DSP_TM = 1024


def _dispatch_kernel(pos_ref, last_ref, h_ref, xs_ref, zero_buf, zsem, sem):
    i = pl.program_id(0)

    @pl.when(i == 0)
    def _():
        zero_buf[...] = jnp.zeros_like(zero_buf)

        def zero_copy(e):
            return pltpu.make_async_copy(zero_buf, xs_ref.at[pl.ds(pl.multiple_of(last_ref[0, e], MOE_TM), MOE_TM)], zsem)

        for e in range(N_EXPERTS):
            @pl.when(last_ref[0, e] >= 0)
            def _():
                zero_copy(e).start()
        for e in range(N_EXPERTS):
            @pl.when(last_ref[0, e] >= 0)
            def _():
                zero_copy(e).wait()

    def row_copy(t, k):
        return pltpu.make_async_copy(h_ref.at[pl.ds(t, 1)],
                                     xs_ref.at[pl.ds(pos_ref[0, 0, t * TOP_K + k], 1)], sem)

    def issue(t, carry):
        for k in range(TOP_K):
            row_copy(t, k).start()
        return carry

    lax.fori_loop(0, DSP_TM, issue, 0, unroll=8)

    for k in range(TOP_K):
        pltpu.make_async_copy(h_ref, xs_ref.at[pl.ds(0, DSP_TM)], sem).wait()


def _dispatch(pos, last, h2):
    n_tiles = N_TOK // DSP_TM
    return pl.pallas_call(
        _dispatch_kernel,
        grid=(n_tiles,),
        in_specs=[
            pl.BlockSpec((1, 1, DSP_TM * TOP_K), lambda i: (i, 0, 0), memory_space=pltpu.SMEM),
            pl.BlockSpec((1, N_EXPERTS), lambda i: (0, 0), memory_space=pltpu.SMEM),
            pl.BlockSpec((DSP_TM, D_MODEL), lambda i: (i, 0)),
        ],
        out_specs=pl.BlockSpec(memory_space=pl.ANY),
        out_shape=jax.ShapeDtypeStruct((MOE_ROWS, D_MODEL), F32),
        scratch_shapes=[pltpu.VMEM((MOE_TM, D_MODEL), F32), pltpu.SemaphoreType.DMA,
                        pltpu.SemaphoreType.DMA],
        compiler_params=_cparams(("arbitrary",), 32),
        name="dispatch",
    )(pos.reshape(n_tiles, 1, DSP_TM * TOP_K), last, h2)


PAIR_BLOCKS = (2 * D_FF) // (2 * LANES)
WEIGHT_PARTS = 3


def _prep_expert_weights(wgu_ref, wdn_ref, wgu_bf, wdn_perm, wdn_bf):
    wgu_bf[...] = wgu_ref[...].astype(BF16)
    half = LANES // 2
    for cb in range(D_MODEL // LANES):
        cols = slice(cb * LANES, (cb + 1) * LANES)
        for m in range(PAIR_BLOCKS):
            base = m * LANES
            wdn_perm[cb, pl.ds(base, half, stride=2), :] = wdn_ref[base:base + half, cols]
            wdn_perm[cb, pl.ds(base + 1, half, stride=2), :] = wdn_ref[base + half:base + LANES, cols]
        wdn_bf[:, cols] = wdn_perm[cb].astype(BF16)


def _ffn_rows(x, wgu_bf, bgu, wdn_bf, bdn):
    gu = _dot(x, wgu_bf[...]) + bgu
    lane = lax.broadcasted_iota(jnp.int32, (1, LANES), 1)
    even = (lane % 2) == 0

    def act(block):
        t = gu[:, block * LANES:(block + 1) * LANES]
        glu = jnp.minimum(t, SWIGLU_LIMIT)
        glu = glu * _sigmoid(SWIGLU_ALPHA * glu)
        lin = jnp.clip(t, -SWIGLU_LIMIT, SWIGLU_LIMIT) + 1.0
        return glu * pltpu.roll(lin, LANES - 1, 1)

    parts = []
    for m in range(PAIR_BLOCKS):
        parts.append(jnp.where(even, act(2 * m), pltpu.roll(act(2 * m + 1), 1, 1)))
    hidden = jnp.concatenate(parts, axis=1).astype(BF16)
    return _dot(hidden, wdn_bf[...]) + bdn


def _moe_kernel(layer, te_ref, nt_ref, next_ref, x_ref, wgu_hbm, bgu_ref, wdn_hbm, bdn_ref, o_ref,
                wgu_land, wdn_land, wgu_bf, wdn_perm, wdn_bf, issued, sems):
    r = pl.program_id(0)
    valid = r < nt_ref[0]
    e = te_ref[r]
    new_expert = jnp.logical_or(r == 0, e != te_ref[jnp.maximum(r - 1, 0)])
    half = D_MODEL // 2

    def part(expert, p):
        if p == 0:
            return pltpu.make_async_copy(wgu_hbm.at[layer, expert, pl.ds(0, half)],
                                         wgu_land.at[pl.ds(0, half)], sems.at[0])
        if p == 1:
            return pltpu.make_async_copy(wgu_hbm.at[layer, expert, pl.ds(half, half)],
                                         wgu_land.at[pl.ds(half, half)], sems.at[1])
        return pltpu.make_async_copy(wdn_hbm.at[layer, expert], wdn_land, sems.at[2])

    @pl.when(r == 0)
    def _():
        issued[0] = 0

    @pl.when(jnp.logical_and(valid, new_expert))
    def _():
        for p in range(WEIGHT_PARTS):
            @pl.when(issued[0] <= p)
            def _():
                part(e, p).start()
        for p in range(WEIGHT_PARTS):
            part(e, p).wait()
        _prep_expert_weights(wgu_land, wdn_land, wgu_bf, wdn_perm, wdn_bf)
        issued[0] = 0

    @pl.when(valid)
    def _():
        nxt = next_ref[e]
        for p in range(WEIGHT_PARTS):
            @pl.when(jnp.logical_and(nxt >= 0, issued[0] == p))
            def _():
                part(nxt, p).start()

        @pl.when(jnp.logical_and(nxt >= 0, issued[0] < WEIGHT_PARTS))
        def _():
            issued[0] = issued[0] + 1

        o_ref[...] = _ffn_rows(x_ref[...].astype(BF16), wgu_bf, bgu_ref[0, 0], wdn_bf, bdn_ref[0, 0])


def _moe_grouped(tile_expert, n_tiles, next_expert, xs, w_gate_up, b_gate_up4, w_down, b_down4, layer):
    d, f2 = D_MODEL, 2 * D_FF
    row_tile = lambda r, te, nt, nx: (jnp.minimum(r, nt[0] - 1), 0)
    expert = lambda r, te, nt, nx: (layer, te[r], 0, 0)
    grid_spec = pltpu.PrefetchScalarGridSpec(
        num_scalar_prefetch=3,
        grid=(MOE_TILES,),
        in_specs=[
            pl.BlockSpec((MOE_TM, d), row_tile),
            pl.BlockSpec(memory_space=pl.ANY),
            pl.BlockSpec((1, 1, 1, f2), expert),
            pl.BlockSpec(memory_space=pl.ANY),
            pl.BlockSpec((1, 1, 1, d), expert),
        ],
        out_specs=pl.BlockSpec((MOE_TM, d), row_tile),
        scratch_shapes=[
            pltpu.VMEM((d, f2), F32),
            pltpu.VMEM((D_FF, d), F32),
            pltpu.VMEM((d, f2), BF16),
            pltpu.VMEM((d // LANES, D_FF, LANES), F32),
            pltpu.VMEM((D_FF, d), BF16),
            pltpu.SMEM((1,), jnp.int32),
            pltpu.SemaphoreType.DMA((WEIGHT_PARTS,)),
        ],
    )
    return pl.pallas_call(
        functools.partial(_moe_kernel, layer),
        grid_spec=grid_spec,
        out_shape=jax.ShapeDtypeStruct((MOE_ROWS, d), F32),
        compiler_params=_cparams(("arbitrary",), 56),
        name="moe_grouped",
    )(tile_expert, n_tiles, next_expert, xs, w_gate_up, b_gate_up4, w_down, b_down4)


CMB_TM = 256
CMB_STEP = 2 * CMB_TM


def _combine_kernel(final, pos_ref, pos_next_ref, ys_ref, x1_ref, topw_ref, mod_ref, *rest):
    buf, sems = rest[-2:]
    j = pl.program_id(0)
    is_ctx = j < N_CTX_TOK // CMB_STEP

    def issue_tile(p_ref, first, s):
        def issue(t, carry):
            for k in range(TOP_K):
                pltpu.make_async_copy(ys_ref.at[pl.ds(p_ref[0, 0, first + t * TOP_K + k], 1)],
                                      buf.at[s, k, pl.ds(t, 1)], sems.at[s]).start()
            return carry

        lax.fori_loop(0, CMB_TM, issue, 0, unroll=8)

    def finish_tile(s):
        rows = slice(s * CMB_TM, (s + 1) * CMB_TM)
        for k in range(TOP_K):
            pltpu.make_async_copy(ys_ref.at[pl.ds(0, CMB_TM)], buf.at[s, k], sems.at[s]).wait()
        y = topw_ref[rows, 0:1] * buf[s, 0]
        for k in range(1, TOP_K):
            y = y + topw_ref[rows, k:k + 1] * buf[s, k]
        x = x1_ref[rows, :] + mod_ref[0, 5:6, :] * y
        if not final:
            rest[0][rows, :] = x
            return
        fg_ref, o_ctx_ref, o_lat_ref = rest[:3]
        out = _rms(x) * fg_ref[...]

        @pl.when(is_ctx)
        def _():
            o_ctx_ref[rows, :] = out

        @pl.when(jnp.logical_not(is_ctx))
        def _():
            o_lat_ref[rows, :] = out

    @pl.when(j == 0)
    def _():
        issue_tile(pos_ref, 0, 0)

    issue_tile(pos_ref, CMB_TM * TOP_K, 1)
    finish_tile(0)

    @pl.when(j + 1 < pl.num_programs(0))
    def _():
        issue_tile(pos_next_ref, 0, 0)

    finish_tile(1)


def _combine(pos, ys, x1, topw, mod_l, final_g=None):
    tm, d = CMB_STEP, D_MODEL
    n_steps = N_TOK // tm
    n_tiles = N_TOK // CMB_TM
    n_ctx = N_CTX_TOK // tm
    final = final_g is not None
    in_specs = [
        pl.BlockSpec((1, 1, tm * TOP_K), lambda i: (i, 0, 0), memory_space=pltpu.SMEM),
        pl.BlockSpec((1, 1, CMB_TM * TOP_K), lambda i: (jnp.minimum(2 * i + 2, n_tiles - 1), 0, 0),
                     memory_space=pltpu.SMEM),
        pl.BlockSpec(memory_space=pl.ANY),
        pl.BlockSpec((tm, d), lambda i: (i, 0)),
        pl.BlockSpec((tm, TOP_K), lambda i: (i, 0)),
        pl.BlockSpec((1, 6, d), lambda i: (_mod_row(i, tm), 0, 0)),
    ]
    args = [pos.reshape(n_steps, 1, tm * TOP_K), pos.reshape(n_tiles, 1, CMB_TM * TOP_K), ys, x1, topw, mod_l]
    if final:
        in_specs.append(pl.BlockSpec((1, d), lambda i: (0, 0)))
        args.append(final_g.reshape(1, d))
        out_specs = [pl.BlockSpec((tm, d), lambda i: (jnp.minimum(i, n_ctx - 1), 0)),
                     pl.BlockSpec((tm, d), lambda i: (jnp.maximum(i - n_ctx, 0), 0))]
        out_shape = [jax.ShapeDtypeStruct((N_CTX_TOK, d), F32), jax.ShapeDtypeStruct((N_LAT_TOK, d), F32)]
    else:
        out_specs = pl.BlockSpec((tm, d), lambda i: (i, 0))
        out_shape = jax.ShapeDtypeStruct((N_TOK, d), F32)
    return pl.pallas_call(
        functools.partial(_combine_kernel, final),
        grid=(n_steps,),
        in_specs=in_specs,
        out_specs=out_specs,
        out_shape=out_shape,
        scratch_shapes=[pltpu.VMEM((2, TOP_K, CMB_TM, d), F32), pltpu.SemaphoreType.DMA((2,))],
        compiler_params=_cparams(("arbitrary",), 40),
        name="combine_final" if final else "combine",
    )(*args)


def kernel(x_prompt, x_sample, cache_k, cache_v, state_ret_fwd, state_ret_bwd, c, c_ctx, norm1_g, norm2_g, w_mod, b_mod, w_in, lam_a, subln_a, ret_decay, ret_norm, w_branch_a, w_branch_b, w_branch_c, w_out, router_w, router_b, w_gate_up, b_gate_up, w_down, b_down, final_g):
    d = D_MODEL
    x = jnp.concatenate([x_prompt.reshape(N_CTX_TOK, d), x_sample.reshape(N_LAT_TOK, d)], axis=0)
    cond8 = jnp.concatenate([c_ctx[None, :], c, jnp.zeros((3, d), F32)], axis=0)
    mod = _modulation(cond8, w_mod, b_mod)

    cache_k4 = cache_k.reshape(DEC_BATCH, DEPTH, PAST_LEN, W_A)
    cache_v4 = cache_v.reshape(DEC_BATCH, DEPTH, PAST_LEN, W_A)
    s0f4 = state_ret_fwd.reshape(DEC_BATCH, DEPTH, H_C * DK_C, DV_C)
    s0b4 = state_ret_bwd.reshape(DEC_BATCH, DEPTH, H_C * DK_C, DV_C)
    b_gate_up4 = b_gate_up.reshape(DEPTH, N_EXPERTS, 1, 2 * D_FF)
    b_down4 = b_down.reshape(DEPTH, N_EXPERTS, 1, d)
    cos, sin = _rope_tables(DEC_SEQ)
    chan, pos_ctx = _dft_consts(SEQ)
    _, pos_lat = _dft_consts(DEC_SEQ)

    caches, states = None, None
    for l in range(DEPTH):
        lam_init = 0.8 - 0.6 * math.exp(-0.3 * l)
        z, new_k, new_v = _inproj(x, mod[l], norm1_g[l], w_in, l, caches)
        caches = (new_k, new_v)
        ya = _attn_ctx(z, lam_a[l], subln_a[l], lam_init)
        ya = _attn_lat(z, cache_k4, cache_v4, l, cos, sin, lam_a[l], subln_a[l], lam_init, ya)
        yb = _fourier(z, SEQ, BATCH, 0, chan, pos_ctx)
        yb = _fourier(z, DEC_SEQ, DEC_BATCH, N_CTX_TOK // DEC_SEQ, chan, pos_lat, yb)
        yc, new_sf, new_sb = _ret_ctx(z, ret_decay[l], ret_norm[l], l, states)
        states = (new_sf, new_sb)
        yc = _ret_lat(z, ret_decay[l], ret_norm[l], s0f4, s0b4, l, yc)
        x1, h2, topi, topw, rank, cnt_row, cnt_col = _merge(
            x, ya, yb, yc, z, mod[l], w_branch_a, w_branch_b, w_branch_c, w_out,
            norm2_g[l], router_w[l], router_b[l], l)
        pos, tile_expert, n_tiles, last, next_expert = _plan(topi, rank, cnt_row, cnt_col)
        xs = _dispatch(pos, last, h2)
        ys = _moe_grouped(tile_expert.reshape(PLAN_LANES), n_tiles[0, :1], next_expert.reshape(N_EXPERTS),
                          xs, w_gate_up, b_gate_up4, w_down, b_down4, l)
        if l < DEPTH - 1:
            x = _combine(pos, ys, x1, topw, mod[l])
        else:
            y_ctx, y_lat = _combine(pos, ys, x1, topw, mod[l], final_g)

    return (y_ctx.reshape(BATCH, SEQ, d), y_lat.reshape(DEC_BATCH, DEC_SEQ, d),
            caches[0].reshape(BATCH, DEPTH, SEQ, H_A, 2, DH_A),
            caches[1].reshape(BATCH, DEPTH, SEQ, H_A, DV_A),
            states[0], states[1])
```

```python
import functools
import math

import numpy as np
import jax
import jax.numpy as jnp
from jax import lax
from jax.experimental import pallas as pl
from jax.experimental.pallas import tpu as pltpu

F32 = jnp.float32
BF16 = jnp.bfloat16

D_MODEL = 1024
BATCH = 16
SEQ = 256
DEPTH = 2
DEC_BATCH = 4
DEC_SEQ = 1024
PAST_LEN = 512
GRID_W = 64
H_A = 4
DH_A = 64
DV_A = 128
G_B = 4
DG_B = 128
H_C = 4
DK_C = 64
DV_C = 128
N_EXPERTS = 32
TOP_K = 4
D_FF = 1024
SWIGLU_LIMIT = 7.0
SWIGLU_ALPHA = 1.702
ROPE_BASE = 10000.0
EPS = 1e-6

N_CTX_TOK = BATCH * SEQ
N_LAT_TOK = DEC_BATCH * DEC_SEQ
N_TOK = N_CTX_TOK + N_LAT_TOK
IN_TOTAL = 6656
Z_QA = 0
Z_KA = 512
Z_VA = 1024
Z_FB = 1536
Z_QC = 2048
Z_KC = 2304
Z_VC = 2560
Z_GC = 3072
Z_GATES = 3584
LANES = 128
MIB = 1024 * 1024


def _cparams(semantics, vmem_mib):
    return pltpu.CompilerParams(dimension_semantics=semantics,
                                vmem_limit_bytes=vmem_mib * MIB)


def _dot(a, b):
    return jnp.dot(a, b, preferred_element_type=F32)


def _dot_nt(a, b):
    return lax.dot_general(a, b, (((1,), (1,)), ((), ())), preferred_element_type=F32)


def _dot_tn(a, b):
    return lax.dot_general(a, b, (((0,), (0,)), ((), ())), preferred_element_type=F32)


def _split_bf16(a):
    hi = a.astype(BF16)
    lo = (a - hi.astype(F32)).astype(BF16)
    return hi, lo


def _dot_f32x3(a, b):
    a_hi, a_lo = _split_bf16(a)
    b_hi, b_lo = _split_bf16(b)
    return _dot(a_hi, b_hi) + _dot(a_lo, b_hi) + _dot(a_hi, b_lo)


def _rms(x):
    return x * lax.rsqrt(jnp.mean(x * x, axis=-1, keepdims=True) + EPS)


def _modnorm(x, g, scale, shift):
    return (_rms(x) * g) * (1.0 + scale) + shift


def _sigmoid(x):
    return 1.0 / (1.0 + jnp.exp(-x))


MOD_TN = 1536


def _mod_kernel(cond_ref, w_ref, b_ref, o_ref):
    cnd = cond_ref[...]
    s = cnd * _sigmoid(cnd)
    o_ref[0] = _dot_f32x3(s, w_ref[0]) + b_ref[0]


def _modulation(cond8, w_mod, b_mod):
    n = 6 * D_MODEL
    out = pl.pallas_call(
        _mod_kernel,
        grid=(DEPTH, n // MOD_TN),
        in_specs=[
            pl.BlockSpec((8, D_MODEL), lambda l, j: (0, 0)),
            pl.BlockSpec((1, D_MODEL, MOD_TN), lambda l, j: (l, 0, j)),
            pl.BlockSpec((1, 1, MOD_TN), lambda l, j: (l, 0, j)),
        ],
        out_specs=pl.BlockSpec((1, 8, MOD_TN), lambda l, j: (l, 0, j)),
        out_shape=jax.ShapeDtypeStruct((DEPTH, 8, n), F32),
        compiler_params=_cparams(("parallel", "parallel"), 40),
        name="modulation",
    )(cond8, w_mod, b_mod.reshape(DEPTH, 1, n))
    return out.reshape(DEPTH, 8, 6, D_MODEL)


def _mod_row(tile_idx, tm):
    n_ctx_tiles = N_CTX_TOK // tm
    per_batch = DEC_SEQ // tm
    return jnp.where(tile_idx < n_ctx_tiles, 0, 1 + (tile_idx - n_ctx_tiles) // per_batch)


INP_TM = 1024
INP_TN = 1664
INP_NJ = IN_TOTAL // INP_TN
CTX_PER_TILE = INP_TM // SEQ
assert Z_VA + H_A * DV_A <= INP_TN


def _inproj_kernel(x_ref, mod_ref, g_ref, w_ref, *rest):
    z_ref, kc_ref, vc_ref, h_scr = rest[-4:]
    i = pl.program_id(0)
    j = pl.program_id(1)

    @pl.when(j == 0)
    def _():
        h = _modnorm(x_ref[...], g_ref[...], mod_ref[0, 1:2, :], mod_ref[0, 0:1, :])
        h_scr[...] = h.astype(BF16)

    z_ref[...] = _dot(h_scr[...], w_ref[0].astype(BF16)).astype(BF16)

    @pl.when(jnp.logical_and(i < N_CTX_TOK // INP_TM, j == 0))
    def _():
        wk = w_ref[0, :, Z_KA:Z_KA + W_A].astype(BF16)
        wv = w_ref[0, :, Z_VA:Z_VA + W_A].astype(BF16)
        for b in range(CTX_PER_TILE):
            hb = h_scr[b * SEQ:(b + 1) * SEQ, :]
            kc_ref[b, 0] = _dot(hb, wk)
            vc_ref[b, 0] = _dot(hb, wv)


def _inproj(x, mod_l, g, w_in, layer, prev_caches):
    ni = N_TOK // INP_TM
    n_ctx_tiles = N_CTX_TOK // INP_TM
    cache_shape = jax.ShapeDtypeStruct((BATCH, DEPTH, SEQ, W_A), F32)
    cache_spec = pl.BlockSpec((CTX_PER_TILE, 1, SEQ, W_A),
                              lambda i, j: (jnp.minimum(i, n_ctx_tiles - 1), layer, 0, 0))
    in_specs = [
        pl.BlockSpec((INP_TM, D_MODEL), lambda i, j: (i, 0)),
        pl.BlockSpec((1, 6, D_MODEL), lambda i, j: (_mod_row(i, INP_TM), 0, 0)),
        pl.BlockSpec((1, D_MODEL), lambda i, j: (0, 0)),
        pl.BlockSpec((1, D_MODEL, INP_TN), lambda i, j: (layer, 0, j)),
    ]
    args = [x, mod_l, g.reshape(1, D_MODEL), w_in]
    aliases = {}
    if prev_caches is not None:
        in_specs += [pl.BlockSpec(memory_space=pl.ANY)] * 2
        args += list(prev_caches)
        aliases = {4: 1, 5: 2}
    return pl.pallas_call(
        _inproj_kernel,
        grid=(ni, INP_NJ),
        in_specs=in_specs,
        out_specs=[pl.BlockSpec((INP_TM, INP_TN), lambda i, j: (i, j)), cache_spec, cache_spec],
        out_shape=[jax.ShapeDtypeStruct((N_TOK, IN_TOTAL), BF16), cache_shape, cache_shape],
        scratch_shapes=[pltpu.VMEM((INP_TM, D_MODEL), BF16)],
        input_output_aliases=aliases,
        compiler_params=_cparams(("arbitrary", "arbitrary"), 56),
        name="inproj",
    )(*args)


ATT_TQ = 256
W_A = H_A * DV_A


def _lambda(lam_ref, lam_init):
    la = lam_ref[...]
    l01 = jnp.sum(la[0:1, :] * la[1:2, :], axis=1, keepdims=True)
    l23 = jnp.sum(la[2:3, :] * la[3:4, :], axis=1, keepdims=True)
    return jnp.exp(l01) - jnp.exp(l23) + lam_init


def _exp_shifted(s):
    return jnp.exp(s - jnp.max(s, axis=-1, keepdims=True)).astype(BF16)


def _diff_attn_block(q, k_ref, vx_ref, lam, subln, lam_init):
    lane = lax.broadcasted_iota(jnp.int32, (1, LANES), 1)
    first = lane < DH_A
    scale = jnp.asarray(DH_A ** -0.5, BF16)
    outs = []
    for h in range(H_A):
        sl = slice(h * LANES, (h + 1) * LANES)
        qh = q[:, sl]
        kh = k_ref[:, sl]
        vx = vx_ref[:, 2 * h * LANES:2 * (h + 1) * LANES]
        zero = jnp.zeros_like(qh)
        qs = qh * scale
        pv0 = _dot(_exp_shifted(_dot_nt(jnp.where(first, qs, zero), kh)), vx)
        pv1 = _dot(_exp_shifted(_dot_nt(jnp.where(first, zero, qs), kh)), vx)
        r0 = 1.0 / pv0[:, LANES:LANES + 1]
        r1 = 1.0 / pv1[:, LANES:LANES + 1]
        o = pv0[:, :LANES] * r0 - (lam * r1) * pv1[:, :LANES]
        outs.append((_rms(o) * subln) * (1.0 - lam_init))
    return outs


def _store_values_with_ones(vx_ref, rows, v):
    ones = jnp.ones((v.shape[0], LANES), BF16)
    for h in range(H_A):
        vx_ref[rows, 2 * h * LANES:(2 * h + 1) * LANES] = v[:, h * LANES:(h + 1) * LANES]
        vx_ref[rows, (2 * h + 1) * LANES:2 * (h + 1) * LANES] = ones


def _rope(x, cos, sin):
    lane = lax.broadcasted_iota(jnp.int32, (1, x.shape[1]), 1)
    first = (lane % 32) < 16
    n = x.shape[1]
    xr = jnp.where(first, -pltpu.roll(x, n - 16, 1), pltpu.roll(x, 16, 1))
    return x * cos + xr * sin


def _attn_ctx_kernel(lam_init, q_ref, k_ref, v_ref, lam_ref, subln_ref, o_ref, vx):
    lam = _lambda(lam_ref, lam_init)
    _store_values_with_ones(vx, slice(None), v_ref[...])
    outs = _diff_attn_block(q_ref[...], k_ref, vx, lam, subln_ref[...], lam_init)
    for h in range(H_A):
        o_ref[:, h * LANES:(h + 1) * LANES] = outs[h].astype(BF16)


def _attn_ctx(z, lam_a_l, subln_l, lam_init):
    wb = W_A
    return pl.pallas_call(
        functools.partial(_attn_ctx_kernel, lam_init),
        grid=(BATCH,),
        in_specs=[
            pl.BlockSpec((SEQ, wb), lambda b: (b, Z_QA // wb)),
            pl.BlockSpec((SEQ, wb), lambda b: (b, Z_KA // wb)),
            pl.BlockSpec((SEQ, wb), lambda b: (b, Z_VA // wb)),
            pl.BlockSpec((4, DH_A), lambda b: (0, 0)),
            pl.BlockSpec((1, DV_A), lambda b: (0, 0)),
        ],
        out_specs=pl.BlockSpec((SEQ, wb), lambda b: (b, 0)),
        out_shape=jax.ShapeDtypeStruct((N_TOK, wb), BF16),
        scratch_shapes=[pltpu.VMEM((SEQ, 2 * wb), BF16)],
        compiler_params=_cparams(("parallel",), 40),
        name="attn_ctx",
    )(z, z, z, lam_a_l, subln_l.reshape(1, DV_A))


def _attn_lat_kernel(lam_init, q_ref, k_ref, v_ref, ck_ref, cv_ref, cos_ref, sin_ref,
                     lam_ref, subln_ref, ya_ctx_ref, o_ref, kall, vall):
    del ya_ctx_ref
    lam = _lambda(lam_ref, lam_init)
    subln = subln_ref[...]
    kall[0:DEC_SEQ, :] = _rope(k_ref[...].astype(F32), cos_ref[...], sin_ref[...]).astype(BF16)
    kall[DEC_SEQ:, :] = ck_ref[0, 0].astype(BF16)
    _store_values_with_ones(vall, slice(0, DEC_SEQ), v_ref[...])
    _store_values_with_ones(vall, slice(DEC_SEQ, DEC_SEQ + PAST_LEN), cv_ref[0, 0].astype(BF16))

    def q_block(qb, carry):
        rows = pl.ds(pl.multiple_of(qb * ATT_TQ, ATT_TQ), ATT_TQ)
        q = _rope(q_ref[rows, :].astype(F32), cos_ref[rows, :], sin_ref[rows, :]).astype(BF16)
        outs = _diff_attn_block(q, kall, vall, lam, subln, lam_init)
        for h in range(H_A):
            o_ref[rows, h * LANES:(h + 1) * LANES] = outs[h].astype(BF16)
        return carry

    lax.fori_loop(0, DEC_SEQ // ATT_TQ, q_block, 0)


def _attn_lat(z, cache_k4, cache_v4, layer, cos, sin, lam_a_l, subln_l, lam_init, ya_ctx):
    wb = W_A
    rb0 = N_CTX_TOK // DEC_SEQ
    t_all = DEC_SEQ + PAST_LEN
    return pl.pallas_call(
        functools.partial(_attn_lat_kernel, lam_init),
        grid=(DEC_BATCH,),
        in_specs=[
            pl.BlockSpec((DEC_SEQ, wb), lambda b: (rb0 + b, Z_QA // wb)),
            pl.BlockSpec((DEC_SEQ, wb), lambda b: (rb0 + b, Z_KA // wb)),
            pl.BlockSpec((DEC_SEQ, wb), lambda b: (rb0 + b, Z_VA // wb)),
            pl.BlockSpec((1, 1, PAST_LEN, wb), lambda b: (b, layer, 0, 0)),
            pl.BlockSpec((1, 1, PAST_LEN, wb), lambda b: (b, layer, 0, 0)),
            pl.BlockSpec((DEC_SEQ, wb), lambda b: (0, 0)),
            pl.BlockSpec((DEC_SEQ, wb), lambda b: (0, 0)),
            pl.BlockSpec((4, DH_A), lambda b: (0, 0)),
            pl.BlockSpec((1, DV_A), lambda b: (0, 0)),
            pl.BlockSpec(memory_space=pl.ANY),
        ],
        out_specs=pl.BlockSpec((DEC_SEQ, wb), lambda b: (rb0 + b, 0)),
        out_shape=jax.ShapeDtypeStruct((N_TOK, wb), BF16),
        scratch_shapes=[pltpu.VMEM((t_all, wb), BF16), pltpu.VMEM((t_all, 2 * wb), BF16)],
        input_output_aliases={9: 0},
        compiler_params=_cparams(("parallel",), 48),
        name="attn_lat",
    )(z, z, z, cache_k4, cache_v4, cos, sin, lam_a_l, subln_l.reshape(1, DV_A), ya_ctx)


def _rope_tables(s):
    rows = s // GRID_W
    row = jnp.repeat(jnp.arange(rows, dtype=F32), GRID_W)
    col = jnp.tile(jnp.arange(GRID_W, dtype=F32), rows)
    quarter = DH_A // 4
    inv = ROPE_BASE ** (-jnp.arange(quarter, dtype=F32) / quarter)
    ar = row[:, None] * inv
    ac = col[:, None] * inv
    ang = jnp.concatenate([ar, ar, ac, ac], axis=-1)
    reps = (H_A * 2 * DH_A) // DH_A
    return jnp.tile(jnp.cos(ang), (1, reps)), jnp.tile(jnp.sin(ang), (1, reps))


W_B = G_B * DG_B


def _dft_consts(s):
    def cs(n, scale):
        j = np.arange(n, dtype=np.int64)
        ang = 2.0 * np.pi * ((j[:, None] * j[None, :]) % n).astype(np.float64) / n
        return np.cos(ang) * scale, np.sin(ang) * scale
    cc, sc = cs(DG_B, DG_B ** -0.5)
    chan = np.concatenate([cc, sc], axis=1)
    cp, sp = cs(s, s ** -0.5)
    pos = np.concatenate([cp, -sp], axis=1)
    return jnp.asarray(chan, dtype=F32).astype(BF16), jnp.asarray(pos, dtype=F32).astype(BF16)


def _fourier_kernel(s, f_ref, chan_ref, pos_ref, *rest):
    o_ref, p_ref = rest[-2:]
    chan = chan_ref[...]
    for g in range(G_B):
        sl = slice(g * DG_B, (g + 1) * DG_B)
        a = _dot(f_ref[:, sl], chan)
        p_ref[0:s, sl] = a[:, :DG_B].astype(BF16)
        p_ref[s:2 * s, sl] = a[:, DG_B:].astype(BF16)
    o_ref[...] = _dot(pos_ref[...], p_ref[...]).astype(BF16)


def _fourier(z, s, n_batch, row_block0, chan, pos, prev=None):
    in_specs = [
        pl.BlockSpec((s, W_B), lambda b: (row_block0 + b, Z_FB // W_B)),
        pl.BlockSpec((DG_B, 2 * DG_B), lambda b: (0, 0)),
        pl.BlockSpec((s, 2 * s), lambda b: (0, 0)),
    ]
    args = [z, chan, pos]
    aliases = {}
    if prev is not None:
        in_specs.append(pl.BlockSpec(memory_space=pl.ANY))
        args.append(prev)
        aliases = {3: 0}
    return pl.pallas_call(
        functools.partial(_fourier_kernel, s),
        grid=(n_batch,),
        in_specs=in_specs,
        out_specs=pl.BlockSpec((s, W_B), lambda b: (row_block0 + b, 0)),
        out_shape=jax.ShapeDtypeStruct((N_TOK, W_B), BF16),
        scratch_shapes=[pltpu.VMEM((2 * s, W_B), BF16)],
        input_output_aliases=aliases,
        compiler_params=_cparams(("parallel",), 40),
        name="fourier_%d" % s,
    )(*args)


W_C = H_C * DV_C
RET_TQ = 256


def _log_sigmoid(x):
    return jnp.minimum(x, 0.0) - jnp.log1p(jnp.exp(-jnp.abs(x)))


def _head_pair_lanes(vals, pair):
    lane = lax.broadcasted_iota(jnp.int32, (1, LANES), 1)
    return jnp.where(lane < DK_C, vals[:, 2 * pair:2 * pair + 1], vals[:, 2 * pair + 1:2 * pair + 2])


def _fill_decay(dec_ref, lg, s, tq):
    for r0 in range(0, s, tq):
        ii = (r0 + lax.broadcasted_iota(jnp.int32, (tq, s), 0)).astype(F32)
        jj = lax.broadcasted_iota(jnp.int32, (tq, s), 1).astype(F32)
        rel = ii - jj
        for h in range(H_C):
            lgf = lg[0:1, h:h + 1]
            lgb = lg[1:2, h:h + 1]
            dec_ref[h, r0:r0 + tq, :] = (
                jnp.where(rel >= 0, jnp.exp(lgf * jnp.maximum(rel, 0.0)), 0.0)
                + jnp.where(rel <= 0, jnp.exp(lgb * jnp.maximum(-rel, 0.0)), 0.0))


def _retention_rows(s, row0, tq, q, k_ref, v_ref, g, lg, rnorm, s0f_ref, s0b_ref, dec_ref):
    lane = lax.broadcasted_iota(jnp.int32, (1, LANES), 1)
    pos = (row0 + lax.broadcasted_iota(jnp.int32, (tq, 1), 0)).astype(F32)
    outs = []
    for h in range(H_C):
        pair, half = h // 2, h % 2
        sl = slice(pair * LANES, (pair + 1) * LANES)
        lgf = lg[0:1, h:h + 1]
        lgb = lg[1:2, h:h + 1]
        in_head = (lane < DK_C) if half == 0 else (lane >= DK_C)
        q_pair = q[:, sl]
        qh = jnp.where(in_head, q_pair, jnp.zeros_like(q_pair))
        sc = _dot_nt(qh, k_ref[:, sl]) * (DK_C ** -0.5)
        decay = dec_ref[h, pl.ds(row0, tq), :]
        o = _dot((sc * decay).astype(BF16), v_ref[:, h * DV_C:(h + 1) * DV_C])
        if s0f_ref is not None:
            qf = qh.astype(F32)
            xf = jnp.exp(lgf * (pos + 1.0))
            xb = jnp.exp(lgb * (s - pos))
            o = o + _dot((qf * xf).astype(BF16), s0f_ref[0, 0, sl, :].astype(BF16))
            o = o + _dot((qf * xb).astype(BF16), s0b_ref[0, 0, sl, :].astype(BF16))
        gh = g[:, h * DV_C:(h + 1) * DV_C].astype(F32)
        outs.append((_rms(o) * rnorm) * (gh * _sigmoid(gh)))
    return outs


def _ret_ctx_kernel(q_ref, k_ref, v_ref, g_ref, dec_ref, rn_ref, *rest):
    o_ref, sf_ref, sb_ref, decay = rest[-4:]
    s = SEQ
    lg = _log_sigmoid(dec_ref[...])

    @pl.when(pl.program_id(0) == 0)
    def _():
        _fill_decay(decay, lg, s, s)

    outs = _retention_rows(s, 0, s, q_ref[...], k_ref, v_ref, g_ref[...], lg, rn_ref[...], None, None,
                           decay)
    for h in range(H_C):
        o_ref[:, h * DV_C:(h + 1) * DV_C] = outs[h].astype(BF16)
    pos = lax.broadcasted_iota(jnp.int32, (s, 1), 0).astype(F32)
    v = v_ref[...]
    for pair in range(H_C // 2):
        sl = slice(pair * LANES, (pair + 1) * LANES)
        kf = k_ref[:, sl].astype(F32) * (DK_C ** -0.5)
        zf = jnp.exp(_head_pair_lanes(lg[0:1, :], pair) * (s - 1.0 - pos))
        zb = jnp.exp(_head_pair_lanes(lg[1:2, :], pair) * pos)
        stf = _dot_tn((kf * zf).astype(BF16), v)
        stb = _dot_tn((kf * zb).astype(BF16), v)
        for half in range(2):
            h = 2 * pair + half
            rs = slice(half * DK_C, (half + 1) * DK_C)
            cs = slice(h * DV_C, (h + 1) * DV_C)
            sf_ref[0, 0, h] = stf[rs, cs]
            sb_ref[0, 0, h] = stb[rs, cs]


def _ret_ctx(z, dec_l, rnorm_l, layer, prev_states):
    wq = H_C * DK_C
    st_shape = jax.ShapeDtypeStruct((BATCH, DEPTH, H_C, DK_C, DV_C), F32)
    st_spec = pl.BlockSpec((1, 1, H_C, DK_C, DV_C), lambda b: (b, layer, 0, 0, 0))
    in_specs = [
        pl.BlockSpec((SEQ, wq), lambda b: (b, Z_QC // wq)),
        pl.BlockSpec((SEQ, wq), lambda b: (b, Z_KC // wq)),
        pl.BlockSpec((SEQ, W_C), lambda b: (b, Z_VC // W_C)),
        pl.BlockSpec((SEQ, W_C), lambda b: (b, Z_GC // W_C)),
        pl.BlockSpec((2, H_C), lambda b: (0, 0)),
        pl.BlockSpec((1, DV_C), lambda b: (0, 0)),
    ]
    args = [z, z, z, z, dec_l, rnorm_l.reshape(1, DV_C)]
    aliases = {}
    if prev_states is not None:
        in_specs += [pl.BlockSpec(memory_space=pl.ANY)] * 2
        args += list(prev_states)
        aliases = {6: 1, 7: 2}
    return pl.pallas_call(
        _ret_ctx_kernel,
        grid=(BATCH,),
        in_specs=in_specs,
        out_specs=[pl.BlockSpec((SEQ, W_C), lambda b: (b, 0)), st_spec, st_spec],
        out_shape=[jax.ShapeDtypeStruct((N_TOK, W_C), BF16), st_shape, st_shape],
        scratch_shapes=[pltpu.VMEM((H_C, SEQ, SEQ), F32)],
        input_output_aliases=aliases,
        compiler_params=_cparams(("arbitrary",), 40),
        name="ret_ctx",
    )(*args)


def _ret_lat_kernel(q_ref, k_ref, v_ref, g_ref, dec_ref, rn_ref, s0f_ref, s0b_ref, yc_ctx_ref, o_ref,
                    decay):
    del yc_ctx_ref
    lg = _log_sigmoid(dec_ref[...])
    rnorm = rn_ref[...]

    @pl.when(pl.program_id(0) == 0)
    def _():
        _fill_decay(decay, lg, DEC_SEQ, RET_TQ)

    def q_block(qb, carry):
        row0 = pl.multiple_of(qb * RET_TQ, RET_TQ)
        rows = pl.ds(row0, RET_TQ)
        outs = _retention_rows(DEC_SEQ, row0, RET_TQ, q_ref[rows, :], k_ref, v_ref, g_ref[rows, :],
                               lg, rnorm, s0f_ref, s0b_ref, decay)
        for h in range(H_C):
            o_ref[rows, h * DV_C:(h + 1) * DV_C] = outs[h].astype(BF16)
        return carry

    lax.fori_loop(0, DEC_SEQ // RET_TQ, q_block, 0)


def _ret_lat(z, dec_l, rnorm_l, s0f4, s0b4, layer, yc_ctx):
    wq = H_C * DK_C
    rb0 = N_CTX_TOK // DEC_SEQ
    st_spec = pl.BlockSpec((1, 1, H_C * DK_C, DV_C), lambda b: (b, layer, 0, 0))
    return pl.pallas_call(
        _ret_lat_kernel,
        grid=(DEC_BATCH,),
        in_specs=[
            pl.BlockSpec((DEC_SEQ, wq), lambda b: (rb0 + b, Z_QC // wq)),
            pl.BlockSpec((DEC_SEQ, wq), lambda b: (rb0 + b, Z_KC // wq)),
            pl.BlockSpec((DEC_SEQ, W_C), lambda b: (rb0 + b, Z_VC // W_C)),
            pl.BlockSpec((DEC_SEQ, W_C), lambda b: (rb0 + b, Z_GC // W_C)),
            pl.BlockSpec((2, H_C), lambda b: (0, 0)),
            pl.BlockSpec((1, DV_C), lambda b: (0, 0)),
            st_spec, st_spec,
            pl.BlockSpec(memory_space=pl.ANY),
        ],
        out_specs=pl.BlockSpec((DEC_SEQ, W_C), lambda b: (rb0 + b, 0)),
        out_shape=jax.ShapeDtypeStruct((N_TOK, W_C), BF16),
        scratch_shapes=[pltpu.VMEM((H_C, DEC_SEQ, DEC_SEQ), F32)],
        input_output_aliases={8: 0},
        compiler_params=_cparams(("arbitrary",), 48),
        name="ret_lat",
    )(z, z, z, z, dec_l, rnorm_l.reshape(1, DV_C), s0f4, s0b4, yc_ctx)


MRG_TM = 512
MRG_SUBBLOCKS = 2
GATE_BLOCK = 512


def _merge_kernel(x_ref, ya_ref, yb_ref, yc_ref, g0_ref, g1_ref, g2_ref, g3_ref, g4_ref, g5_ref,
                  mod_ref, wa_ref, wb_ref, wc_ref, wo_ref,
                  n2_ref, rw_ref, rb_ref, x1_ref, h2_ref, topi_ref, topw_ref, rank_ref,
                  cnt_row_ref, cnt_col_ref, carry_row, carry_col, wa_bf, wb_bf, wc_bf, wo_bf):
    d = D_MODEL
    tm = x_ref.shape[0]

    @pl.when(pl.program_id(0) == 0)
    def _():
        carry_row[...] = jnp.zeros_like(carry_row)
        carry_col[...] = jnp.zeros_like(carry_col)
        wa_bf[...] = wa_ref[0].astype(BF16)
        wb_bf[...] = wb_ref[0].astype(BF16)
        wc_bf[...] = wc_ref[0].astype(BF16)
        wo_bf[...] = wo_ref[0].astype(BF16)

    def gate(lo_ref, hi_ref, rs):
        return _sigmoid(jnp.concatenate([lo_ref[rs, :], hi_ref[rs, :]], axis=1).astype(F32))

    sub = tm // MRG_SUBBLOCKS
    for sb in range(MRG_SUBBLOCKS):
        rs = slice(sb * sub, (sb + 1) * sub)
        merged = gate(g0_ref, g1_ref, rs) * _dot(ya_ref[rs, :], wa_bf[...])
        merged += gate(g2_ref, g3_ref, rs) * _dot(yb_ref[rs, :], wb_bf[...])
        merged += gate(g4_ref, g5_ref, rs) * _dot(yc_ref[rs, :], wc_bf[...])
        y = _dot(merged.astype(BF16), wo_bf[...])
        x1 = x_ref[rs, :] + mod_ref[0, 2:3, :] * y
        x1_ref[rs, :] = x1
        h2 = _modnorm(x1, n2_ref[...], mod_ref[0, 4:5, :], mod_ref[0, 3:4, :])
        h2_ref[rs, :] = h2
        logits = _dot_f32x3(h2, rw_ref[...]) + rb_ref[...]
        ids = lax.broadcasted_iota(jnp.int32, logits.shape, 1)
        work = logits
        sels, vals, firsts = [], [], []
        for _ in range(TOP_K):
            m = jnp.max(work, axis=-1, keepdims=True)
            first = jnp.min(jnp.where(work == m, ids, N_EXPERTS), axis=-1, keepdims=True)
            sel = ids == first
            sels.append(sel)
            vals.append(m)
            firsts.append(first)
            work = jnp.where(sel, -jnp.inf, work)
        es = [jnp.exp(v - vals[0]) for v in vals]
        inv = 1.0 / (es[0] + es[1] + es[2] + es[3])
        chosen = jnp.zeros_like(logits)
        for sel in sels:
            chosen = chosen + jnp.where(sel, 1.0, 0.0)
        chosen_bf = chosen.astype(BF16)
        rr = lax.broadcasted_iota(jnp.int32, (sub, sub), 0)
        cc = lax.broadcasted_iota(jnp.int32, (sub, sub), 1)
        before = jnp.where(rr > cc, 1.0, 0.0).astype(BF16)
        prefix = _dot(before, chosen_bf) + carry_row[...]
        ranks = [jnp.sum(jnp.where(sel, prefix, 0.0), axis=-1, keepdims=True) for sel in sels]
        carry_row[...] += jnp.sum(chosen, axis=0, keepdims=True)
        carry_col[...] += _dot_tn(chosen_bf, jnp.ones((sub, LANES), BF16))
        topi_ref[rs, :] = jnp.concatenate(firsts, axis=1)
        topw_ref[rs, :] = jnp.concatenate([e * inv for e in es], axis=1)
        rank_ref[rs, :] = jnp.concatenate(ranks, axis=1).astype(jnp.int32)
    cnt_row_ref[...] = carry_row[...]
    cnt_col_ref[...] = carry_col[...]


def _merge(x, ya, yb, yc, z, mod_l, wa, wb, wc, wo, n2, rw, rb, layer):
    tm = MRG_TM
    d = D_MODEL
    full = lambda shape: pl.BlockSpec(shape, lambda i: tuple(0 for _ in shape))
    tok = lambda w: pl.BlockSpec((tm, w), lambda i: (i, 0))
    per_layer = lambda rows: pl.BlockSpec((1, rows, d), lambda i: (layer, 0, 0))
    return pl.pallas_call(
        _merge_kernel,
        grid=(N_TOK // tm,),
        in_specs=[
            tok(d), tok(W_A), tok(W_B), tok(W_C),
            *[pl.BlockSpec((tm, GATE_BLOCK), lambda i, c=c: (i, Z_GATES // GATE_BLOCK + c)) for c in range(6)],
            pl.BlockSpec((1, 6, d), lambda i: (_mod_row(i, tm), 0, 0)),
            per_layer(W_A), per_layer(W_B), per_layer(W_C), per_layer(d),
            full((1, d)), full((d, N_EXPERTS)), full((1, N_EXPERTS)),
        ],
        out_specs=[tok(d), tok(d), tok(TOP_K), tok(TOP_K), tok(TOP_K),
                   full((1, N_EXPERTS)), full((N_EXPERTS, LANES))],
        out_shape=[
            jax.ShapeDtypeStruct((N_TOK, d), F32),
            jax.ShapeDtypeStruct((N_TOK, d), F32),
            jax.ShapeDtypeStruct((N_TOK, TOP_K), jnp.int32),
            jax.ShapeDtypeStruct((N_TOK, TOP_K), F32),
            jax.ShapeDtypeStruct((N_TOK, TOP_K), jnp.int32),
            jax.ShapeDtypeStruct((1, N_EXPERTS), F32),
            jax.ShapeDtypeStruct((N_EXPERTS, LANES), F32),
        ],
        scratch_shapes=[pltpu.VMEM((1, N_EXPERTS), F32), pltpu.VMEM((N_EXPERTS, LANES), F32),
                        pltpu.VMEM((W_A, d), BF16), pltpu.VMEM((W_B, d), BF16),
                        pltpu.VMEM((W_C, d), BF16), pltpu.VMEM((d, d), BF16)],
        compiler_params=_cparams(("arbitrary",), 56),
        name="merge_router",
    )(x, ya, yb, yc, z, z, z, z, z, z, mod_l, wa, wb, wc, wo, n2.reshape(1, d), rw, rb.reshape(1, N_EXPERTS))


MOE_TM = 256
MOE_TILES = (N_TOK * TOP_K) // MOE_TM + N_EXPERTS
MOE_ROWS = MOE_TILES * MOE_TM
PLAN_LANES = 256
PLAN_TM = 1024


def _plan_kernel(topi_ref, rank_ref, cnt_row_ref, cnt_col_ref, pos_ref, te_ref, nt_ref, last_ref,
                 next_ref):
    e_r = lax.broadcasted_iota(jnp.int32, (N_EXPERTS, N_EXPERTS), 0)
    e_c = lax.broadcasted_iota(jnp.int32, (N_EXPERTS, N_EXPERTS), 1)
    cnt_row = cnt_row_ref[...]
    tiles_row = jnp.ceil(cnt_row * (1.0 / MOE_TM))
    earlier = jnp.where(e_r < e_c, 1.0, 0.0).astype(BF16)
    start_tile_row = _dot(tiles_row.astype(BF16), earlier)
    off_row = start_tile_row * MOE_TM
    ids = lax.broadcasted_iota(jnp.int32, (PLAN_TM, N_EXPERTS), 1)
    cols = []
    for k in range(TOP_K):
        hit = ids == topi_ref[:, k:k + 1]
        cols.append(jnp.sum(jnp.where(hit, off_row, 0.0), axis=-1, keepdims=True))
    pos_ref[...] = jnp.concatenate(cols, axis=1).astype(jnp.int32) + rank_ref[...]
    tiles_col = jnp.ceil(cnt_col_ref[...] * (1.0 / MOE_TM))
    upto = jnp.where(e_c <= e_r, 1.0, 0.0).astype(BF16)
    end_col = _dot(upto, tiles_col.astype(BF16))
    end_b = jnp.concatenate([end_col] * (PLAN_LANES // LANES), axis=1)
    r = lax.broadcasted_iota(jnp.int32, (N_EXPERTS, PLAN_LANES), 1).astype(F32)
    total = end_b[N_EXPERTS - 1:N_EXPERTS, :]
    te = jnp.sum(jnp.where(end_b <= jnp.minimum(r, total - 1.0), 1.0, 0.0), axis=0, keepdims=True)
    te_ref[...] = te.astype(jnp.int32)
    nt_ref[...] = total[:, :LANES].astype(jnp.int32)
    last = jnp.where(cnt_row > 0.0, (start_tile_row + tiles_row - 1.0) * MOE_TM, -1.0)
    last_ref[...] = last.astype(jnp.int32)
    later_used = jnp.logical_and(e_r > e_c, cnt_col_ref[:, 0:N_EXPERTS] > 0.0)
    nxt = jnp.min(jnp.where(later_used, e_r, N_EXPERTS), axis=0, keepdims=True)
    next_ref[...] = jnp.where(nxt == N_EXPERTS, -1, nxt)


def _plan(topi, rank, cnt_row, cnt_col):
    tok = pl.BlockSpec((PLAN_TM, TOP_K), lambda i: (i, 0))
    full = lambda shape: pl.BlockSpec(shape, lambda i: tuple(0 for _ in shape))
    return pl.pallas_call(
        _plan_kernel,
        grid=(N_TOK // PLAN_TM,),
        in_specs=[tok, tok, full((1, N_EXPERTS)), full((N_EXPERTS, LANES))],
        out_specs=[tok, full((1, PLAN_LANES)), full((1, LANES)), full((1, N_EXPERTS)),
                   full((1, N_EXPERTS))],
        out_shape=[
            jax.ShapeDtypeStruct((N_TOK, TOP_K), jnp.int32),
            jax.ShapeDtypeStruct((1, PLAN_LANES), jnp.int32),
            jax.ShapeDtypeStruct((1, LANES), jnp.int32),
            jax.ShapeDtypeStruct((1, N_EXPERTS), jnp.int32),
            jax.ShapeDtypeStruct((1, N_EXPERTS), jnp.int32),
        ],
        compiler_params=_cparams(("arbitrary",), 32),
        name="route_plan",
    )(topi, rank, cnt_row, cnt_col)


DSP_TM = 1024


def _dispatch_kernel(pos_ref, last_ref, h_ref, xs_ref, zero_buf, zsem, sem):
    i = pl.program_id(0)

    @pl.when(i == 0)
    def _():
        zero_buf[...] = jnp.zeros_like(zero_buf)

        def zero_copy(e):
            return pltpu.make_async_copy(zero_buf, xs_ref.at[pl.ds(pl.multiple_of(last_ref[0, e], MOE_TM), MOE_TM)], zsem)

        for e in range(N_EXPERTS):
            @pl.when(last_ref[0, e] >= 0)
            def _():
                zero_copy(e).start()
        for e in range(N_EXPERTS):
            @pl.when(last_ref[0, e] >= 0)
            def _():
                zero_copy(e).wait()

    def row_copy(t, k):
        return pltpu.make_async_copy(h_ref.at[pl.ds(t, 1)],
                                     xs_ref.at[pl.ds(pos_ref[0, 0, t * TOP_K + k], 1)], sem)

    def issue(t, carry):
        for k in range(TOP_K):
            row_copy(t, k).start()
        return carry

    lax.fori_loop(0, DSP_TM, issue, 0, unroll=8)

    for k in range(TOP_K):
        pltpu.make_async_copy(h_ref, xs_ref.at[pl.ds(0, DSP_TM)], sem).wait()


def _dispatch(pos, last, h2):
    n_tiles = N_TOK // DSP_TM
    return pl.pallas_call(
        _dispatch_kernel,
        grid=(n_tiles,),
        in_specs=[
            pl.BlockSpec((1, 1, DSP_TM * TOP_K), lambda i: (i, 0, 0), memory_space=pltpu.SMEM),
            pl.BlockSpec((1, N_EXPERTS), lambda i: (0, 0), memory_space=pltpu.SMEM),
            pl.BlockSpec((DSP_TM, D_MODEL), lambda i: (i, 0)),
        ],
        out_specs=pl.BlockSpec(memory_space=pl.ANY),
        out_shape=jax.ShapeDtypeStruct((MOE_ROWS, D_MODEL), F32),
        scratch_shapes=[pltpu.VMEM((MOE_TM, D_MODEL), F32), pltpu.SemaphoreType.DMA,
                        pltpu.SemaphoreType.DMA],
        compiler_params=_cparams(("arbitrary",), 32),
        name="dispatch",
    )(pos.reshape(n_tiles, 1, DSP_TM * TOP_K), last, h2)


PAIR_BLOCKS = (2 * D_FF) // (2 * LANES)
WEIGHT_PARTS = 3


def _prep_expert_weights(wgu_ref, wdn_ref, wgu_bf, wdn_perm, wdn_bf):
    wgu_bf[...] = wgu_ref[...].astype(BF16)
    half = LANES // 2
    for cb in range(D_MODEL // LANES):
        cols = slice(cb * LANES, (cb + 1) * LANES)
        for m in range(PAIR_BLOCKS):
            base = m * LANES
            wdn_perm[cb, pl.ds(base, half, stride=2), :] = wdn_ref[base:base + half, cols]
            wdn_perm[cb, pl.ds(base + 1, half, stride=2), :] = wdn_ref[base + half:base + LANES, cols]
        wdn_bf[:, cols] = wdn_perm[cb].astype(BF16)


def _ffn_rows(x, wgu_bf, bgu, wdn_bf, bdn):
    gu = _dot(x, wgu_bf[...]) + bgu
    lane = lax.broadcasted_iota(jnp.int32, (1, LANES), 1)
    even = (lane % 2) == 0

    def act(block):
        t = gu[:, block * LANES:(block + 1) * LANES]
        glu = jnp.minimum(t, SWIGLU_LIMIT)
        glu = glu * _sigmoid(SWIGLU_ALPHA * glu)
        lin = jnp.clip(t, -SWIGLU_LIMIT, SWIGLU_LIMIT) + 1.0
        return glu * pltpu.roll(lin, LANES - 1, 1)

    parts = []
    for m in range(PAIR_BLOCKS):
        parts.append(jnp.where(even, act(2 * m), pltpu.roll(act(2 * m + 1), 1, 1)))
    hidden = jnp.concatenate(parts, axis=1).astype(BF16)
    return _dot(hidden, wdn_bf[...]) + bdn


def _moe_kernel(layer, te_ref, nt_ref, next_ref, x_ref, wgu_hbm, bgu_ref, wdn_hbm, bdn_ref, o_ref,
                wgu_land, wdn_land, wgu_bf, wdn_perm, wdn_bf, issued, sems):
    r = pl.program_id(0)
    valid = r < nt_ref[0]
    e = te_ref[r]
    new_expert = jnp.logical_or(r == 0, e != te_ref[jnp.maximum(r - 1, 0)])
    half = D_MODEL // 2

    def part(expert, p):
        if p == 0:
            return pltpu.make_async_copy(wgu_hbm.at[layer, expert, pl.ds(0, half)],
                                         wgu_land.at[pl.ds(0, half)], sems.at[0])
        if p == 1:
            return pltpu.make_async_copy(wgu_hbm.at[layer, expert, pl.ds(half, half)],
                                         wgu_land.at[pl.ds(half, half)], sems.at[1])
        return pltpu.make_async_copy(wdn_hbm.at[layer, expert], wdn_land, sems.at[2])

    @pl.when(r == 0)
    def _():
        issued[0] = 0

    @pl.when(jnp.logical_and(valid, new_expert))
    def _():
        for p in range(WEIGHT_PARTS):
            @pl.when(issued[0] <= p)
            def _():
                part(e, p).start()
        for p in range(WEIGHT_PARTS):
            part(e, p).wait()
        _prep_expert_weights(wgu_land, wdn_land, wgu_bf, wdn_perm, wdn_bf)
        issued[0] = 0

    @pl.when(valid)
    def _():
        nxt = next_ref[e]
        for p in range(WEIGHT_PARTS):
            @pl.when(jnp.logical_and(nxt >= 0, issued[0] == p))
            def _():
                part(nxt, p).start()

        @pl.when(jnp.logical_and(nxt >= 0, issued[0] < WEIGHT_PARTS))
        def _():
            issued[0] = issued[0] + 1

        o_ref[...] = _ffn_rows(x_ref[...].astype(BF16), wgu_bf, bgu_ref[0, 0], wdn_bf, bdn_ref[0, 0])


def _moe_grouped(tile_expert, n_tiles, next_expert, xs, w_gate_up, b_gate_up4, w_down, b_down4, layer):
    d, f2 = D_MODEL, 2 * D_FF
    row_tile = lambda r, te, nt, nx: (jnp.minimum(r, nt[0] - 1), 0)
    expert = lambda r, te, nt, nx: (layer, te[r], 0, 0)
    grid_spec = pltpu.PrefetchScalarGridSpec(
        num_scalar_prefetch=3,
        grid=(MOE_TILES,),
        in_specs=[
            pl.BlockSpec((MOE_TM, d), row_tile),
            pl.BlockSpec(memory_space=pl.ANY),
            pl.BlockSpec((1, 1, 1, f2), expert),
            pl.BlockSpec(memory_space=pl.ANY),
            pl.BlockSpec((1, 1, 1, d), expert),
        ],
        out_specs=pl.BlockSpec((MOE_TM, d), row_tile),
        scratch_shapes=[
            pltpu.VMEM((d, f2), F32),
            pltpu.VMEM((D_FF, d), F32),
            pltpu.VMEM((d, f2), BF16),
            pltpu.VMEM((d // LANES, D_FF, LANES), F32),
            pltpu.VMEM((D_FF, d), BF16),
            pltpu.SMEM((1,), jnp.int32),
            pltpu.SemaphoreType.DMA((WEIGHT_PARTS,)),
        ],
    )
    return pl.pallas_call(
        functools.partial(_moe_kernel, layer),
        grid_spec=grid_spec,
        out_shape=jax.ShapeDtypeStruct((MOE_ROWS, d), F32),
        compiler_params=_cparams(("arbitrary",), 56),
        name="moe_grouped",
    )(tile_expert, n_tiles, next_expert, xs, w_gate_up, b_gate_up4, w_down, b_down4)


CMB_TM = 256
CMB_STEP = 2 * CMB_TM


def _combine_kernel(final, pos_ref, pos_next_ref, ys_ref, x1_ref, topw_ref, mod_ref, *rest):
    buf, sems = rest[-2:]
    j = pl.program_id(0)
    is_ctx = j < N_CTX_TOK // CMB_STEP

    def issue_tile(p_ref, first, s):
        def issue(t, carry):
            for k in range(TOP_K):
                pltpu.make_async_copy(ys_ref.at[pl.ds(p_ref[0, 0, first + t * TOP_K + k], 1)],
                                      buf.at[s, k, pl.ds(t, 1)], sems.at[s]).start()
            return carry

        lax.fori_loop(0, CMB_TM, issue, 0, unroll=8)

    def finish_tile(s):
        rows = slice(s * CMB_TM, (s + 1) * CMB_TM)
        for k in range(TOP_K):
            pltpu.make_async_copy(ys_ref.at[pl.ds(0, CMB_TM)], buf.at[s, k], sems.at[s]).wait()
        y = topw_ref[rows, 0:1] * buf[s, 0]
        for k in range(1, TOP_K):
            y = y + topw_ref[rows, k:k + 1] * buf[s, k]
        x = x1_ref[rows, :] + mod_ref[0, 5:6, :] * y
        if not final:
            rest[0][rows, :] = x
            return
        fg_ref, o_ctx_ref, o_lat_ref = rest[:3]
        out = _rms(x) * fg_ref[...]

        @pl.when(is_ctx)
        def _():
            o_ctx_ref[rows, :] = out

        @pl.when(jnp.logical_not(is_ctx))
        def _():
            o_lat_ref[rows, :] = out

    @pl.when(j == 0)
    def _():
        issue_tile(pos_ref, 0, 0)

    issue_tile(pos_ref, CMB_TM * TOP_K, 1)
    finish_tile(0)

    @pl.when(j + 1 < pl.num_programs(0))
    def _():
        issue_tile(pos_next_ref, 0, 0)

    finish_tile(1)


def _combine(pos, ys, x1, topw, mod_l, final_g=None):
    tm, d = CMB_STEP, D_MODEL
    n_steps = N_TOK // tm
    n_tiles = N_TOK // CMB_TM
    n_ctx = N_CTX_TOK // tm
    final = final_g is not None
    in_specs = [
        pl.BlockSpec((1, 1, tm * TOP_K), lambda i: (i, 0, 0), memory_space=pltpu.SMEM),
        pl.BlockSpec((1, 1, CMB_TM * TOP_K), lambda i: (jnp.minimum(2 * i + 2, n_tiles - 1), 0, 0),
                     memory_space=pltpu.SMEM),
        pl.BlockSpec(memory_space=pl.ANY),
        pl.BlockSpec((tm, d), lambda i: (i, 0)),
        pl.BlockSpec((tm, TOP_K), lambda i: (i, 0)),
        pl.BlockSpec((1, 6, d), lambda i: (_mod_row(i, tm), 0, 0)),
    ]
    args = [pos.reshape(n_steps, 1, tm * TOP_K), pos.reshape(n_tiles, 1, CMB_TM * TOP_K), ys, x1, topw, mod_l]
    if final:
        in_specs.append(pl.BlockSpec((1, d), lambda i: (0, 0)))
        args.append(final_g.reshape(1, d))
        out_specs = [pl.BlockSpec((tm, d), lambda i: (jnp.minimum(i, n_ctx - 1), 0)),
                     pl.BlockSpec((tm, d), lambda i: (jnp.maximum(i - n_ctx, 0), 0))]
        out_shape = [jax.ShapeDtypeStruct((N_CTX_TOK, d), F32), jax.ShapeDtypeStruct((N_LAT_TOK, d), F32)]
    else:
        out_specs = pl.BlockSpec((tm, d), lambda i: (i, 0))
        out_shape = jax.ShapeDtypeStruct((N_TOK, d), F32)
    return pl.pallas_call(
        functools.partial(_combine_kernel, final),
        grid=(n_steps,),
        in_specs=in_specs,
        out_specs=out_specs,
        out_shape=out_shape,
        scratch_shapes=[pltpu.VMEM((2, TOP_K, CMB_TM, d), F32), pltpu.SemaphoreType.DMA((2,))],
        compiler_params=_cparams(("arbitrary",), 40),
        name="combine_final" if final else "combine",
    )(*args)


def kernel(x_prompt, x_sample, cache_k, cache_v, state_ret_fwd, state_ret_bwd, c, c_ctx, norm1_g, norm2_g, w_mod, b_mod, w_in, lam_a, subln_a, ret_decay, ret_norm, w_branch_a, w_branch_b, w_branch_c, w_out, router_w, router_b, w_gate_up, b_gate_up, w_down, b_down, final_g):
    d = D_MODEL
    x = jnp.concatenate([x_prompt.reshape(N_CTX_TOK, d), x_sample.reshape(N_LAT_TOK, d)], axis=0)
    cond8 = jnp.concatenate([c_ctx[None, :], c, jnp.zeros((3, d), F32)], axis=0)
    mod = _modulation(cond8, w_mod, b_mod)

    cache_k4 = cache_k.reshape(DEC_BATCH, DEPTH, PAST_LEN, W_A)
    cache_v4 = cache_v.reshape(DEC_BATCH, DEPTH, PAST_LEN, W_A)
    s0f4 = state_ret_fwd.reshape(DEC_BATCH, DEPTH, H_C * DK_C, DV_C)
    s0b4 = state_ret_bwd.reshape(DEC_BATCH, DEPTH, H_C * DK_C, DV_C)
    b_gate_up4 = b_gate_up.reshape(DEPTH, N_EXPERTS, 1, 2 * D_FF)
    b_down4 = b_down.reshape(DEPTH, N_EXPERTS, 1, d)
    cos, sin = _rope_tables(DEC_SEQ)
    chan, pos_ctx = _dft_consts(SEQ)
    _, pos_lat = _dft_consts(DEC_SEQ)

    caches, states = None, None
    for l in range(DEPTH):
        lam_init = 0.8 - 0.6 * math.exp(-0.3 * l)
        z, new_k, new_v = _inproj(x, mod[l], norm1_g[l], w_in, l, caches)
        caches = (new_k, new_v)
        ya = _attn_ctx(z, lam_a[l], subln_a[l], lam_init)
        ya = _attn_lat(z, cache_k4, cache_v4, l, cos, sin, lam_a[l], subln_a[l], lam_init, ya)
        yb = _fourier(z, SEQ, BATCH, 0, chan, pos_ctx)
        yb = _fourier(z, DEC_SEQ, DEC_BATCH, N_CTX_TOK // DEC_SEQ, chan, pos_lat, yb)
        yc, new_sf, new_sb = _ret_ctx(z, ret_decay[l], ret_norm[l], l, states)
        states = (new_sf, new_sb)
        yc = _ret_lat(z, ret_decay[l], ret_norm[l], s0f4, s0b4, l, yc)
        x1, h2, topi, topw, rank, cnt_row, cnt_col = _merge(
            x, ya, yb, yc, z, mod[l], w_branch_a, w_branch_b, w_branch_c, w_out,
            norm2_g[l], router_w[l], router_b[l], l)
        pos, tile_expert, n_tiles, last, next_expert = _plan(topi, rank, cnt_row, cnt_col)
        xs = _dispatch(pos, last, h2)
        ys = _moe_grouped(tile_expert.reshape(PLAN_LANES), n_tiles[0, :1], next_expert.reshape(N_EXPERTS),
                          xs, w_gate_up, b_gate_up4, w_down, b_down4, l)
        if l < DEPTH - 1:
            x = _combine(pos, ys, x1, topw, mod[l])
        else:
            y_ctx, y_lat = _combine(pos, ys, x1, topw, mod[l], final_g)

    return (y_ctx.reshape(BATCH, SEQ, d), y_lat.reshape(DEC_BATCH, DEC_SEQ, d),
            caches[0].reshape(BATCH, DEPTH, SEQ, H_A, 2, DH_A),
            caches[1].reshape(BATCH, DEPTH, SEQ, H_A, DV_A),
            states[0], states[1])
```

```python
import functools
import math

import numpy as np
import jax
import jax.numpy as jnp
from jax import lax
from jax.experimental import pallas as pl
from jax.experimental.pallas import tpu as pltpu

F32 = jnp.float32
BF16 = jnp.bfloat16

D_MODEL = 1024
BATCH = 16
SEQ = 256
DEPTH = 2
DEC_BATCH = 4
DEC_SEQ = 1024
PAST_LEN = 512
GRID_W = 64
H_A = 4
DH_A = 64
DV_A = 128
G_B = 4
DG_B = 128
H_C = 4
DK_C = 64
DV_C = 128
N_EXPERTS = 32
TOP_K = 4
D_FF = 1024
SWIGLU_LIMIT = 7.0
SWIGLU_ALPHA = 1.702
ROPE_BASE = 10000.0
EPS = 1e-6

N_CTX_TOK = BATCH * SEQ
N_LAT_TOK = DEC_BATCH * DEC_SEQ
N_TOK = N_CTX_TOK + N_LAT_TOK
IN_TOTAL = 6656
Z_QA = 0
Z_KA = 512
Z_VA = 1024
Z_FB = 1536
Z_QC = 2048
Z_KC = 2304
Z_VC = 2560
Z_GC = 3072
Z_GATES = 3584
LANES = 128
MIB = 1024 * 1024


def _cparams(semantics, vmem_mib):
    return pltpu.CompilerParams(dimension_semantics=semantics,
                                vmem_limit_bytes=vmem_mib * MIB)


def _dot(a, b):
    return jnp.dot(a, b, preferred_element_type=F32)


def _dot_nt(a, b):
    return lax.dot_general(a, b, (((1,), (1,)), ((), ())), preferred_element_type=F32)


def _dot_tn(a, b):
    return lax.dot_general(a, b, (((0,), (0,)), ((), ())), preferred_element_type=F32)


def _split_bf16(a):
    hi = a.astype(BF16)
    lo = (a - hi.astype(F32)).astype(BF16)
    return hi, lo


def _dot_f32x3(a, b):
    a_hi, a_lo = _split_bf16(a)
    b_hi, b_lo = _split_bf16(b)
    return _dot(a_hi, b_hi) + _dot(a_lo, b_hi) + _dot(a_hi, b_lo)


def _rms(x):
    return x * lax.rsqrt(jnp.mean(x * x, axis=-1, keepdims=True) + EPS)


def _modnorm(x, g, scale, shift):
    return (_rms(x) * g) * (1.0 + scale) + shift


def _sigmoid(x):
    return 1.0 / (1.0 + jnp.exp(-x))


MOD_TN = 1536


def _mod_kernel(cond_ref, w_ref, b_ref, o_ref):
    cnd = cond_ref[...]
    s = cnd * _sigmoid(cnd)
    o_ref[0] = _dot_f32x3(s, w_ref[0]) + b_ref[0]


def _modulation(cond8, w_mod, b_mod):
    n = 6 * D_MODEL
    out = pl.pallas_call(
        _mod_kernel,
        grid=(DEPTH, n // MOD_TN),
        in_specs=[
            pl.BlockSpec((8, D_MODEL), lambda l, j: (0, 0)),
            pl.BlockSpec((1, D_MODEL, MOD_TN), lambda l, j: (l, 0, j)),
            pl.BlockSpec((1, 1, MOD_TN), lambda l, j: (l, 0, j)),
        ],
        out_specs=pl.BlockSpec((1, 8, MOD_TN), lambda l, j: (l, 0, j)),
        out_shape=jax.ShapeDtypeStruct((DEPTH, 8, n), F32),
        compiler_params=_cparams(("parallel", "parallel"), 40),
        name="modulation",
    )(cond8, w_mod, b_mod.reshape(DEPTH, 1, n))
    return out.reshape(DEPTH, 8, 6, D_MODEL)


def _mod_row(tile_idx, tm):
    n_ctx_tiles = N_CTX_TOK // tm
    per_batch = DEC_SEQ // tm
    return jnp.where(tile_idx < n_ctx_tiles, 0, 1 + (tile_idx - n_ctx_tiles) // per_batch)


INP_TM = 1024
INP_TN = 1664
INP_NJ = IN_TOTAL // INP_TN
CTX_PER_TILE = INP_TM // SEQ
assert Z_VA + H_A * DV_A <= INP_TN


def _inproj_kernel(x_ref, mod_ref, g_ref, w_ref, *rest):
    z_ref, kc_ref, vc_ref, h_scr = rest[-4:]
    i = pl.program_id(0)
    j = pl.program_id(1)

    @pl.when(j == 0)
    def _():
        h = _modnorm(x_ref[...], g_ref[...], mod_ref[0, 1:2, :], mod_ref[0, 0:1, :])
        h_scr[...] = h.astype(BF16)

    z_ref[...] = _dot(h_scr[...], w_ref[0].astype(BF16)).astype(BF16)

    @pl.when(jnp.logical_and(i < N_CTX_TOK // INP_TM, j == 0))
    def _():
        wk = w_ref[0, :, Z_KA:Z_KA + W_A].astype(BF16)
        wv = w_ref[0, :, Z_VA:Z_VA + W_A].astype(BF16)
        for b in range(CTX_PER_TILE):
            hb = h_scr[b * SEQ:(b + 1) * SEQ, :]
            kc_ref[b, 0] = _dot(hb, wk)
            vc_ref[b, 0] = _dot(hb, wv)


def _inproj(x, mod_l, g, w_in, layer, prev_caches):
    ni = N_TOK // INP_TM
    n_ctx_tiles = N_CTX_TOK // INP_TM
    cache_shape = jax.ShapeDtypeStruct((BATCH, DEPTH, SEQ, W_A), F32)
    cache_spec = pl.BlockSpec((CTX_PER_TILE, 1, SEQ, W_A),
                              lambda i, j: (jnp.minimum(i, n_ctx_tiles - 1), layer, 0, 0))
    in_specs = [
        pl.BlockSpec((INP_TM, D_MODEL), lambda i, j: (i, 0)),
        pl.BlockSpec((1, 6, D_MODEL), lambda i, j: (_mod_row(i, INP_TM), 0, 0)),
        pl.BlockSpec((1, D_MODEL), lambda i, j: (0, 0)),
        pl.BlockSpec((1, D_MODEL, INP_TN), lambda i, j: (layer, 0, j)),
    ]
    args = [x, mod_l, g.reshape(1, D_MODEL), w_in]
    aliases = {}
    if prev_caches is not None:
        in_specs += [pl.BlockSpec(memory_space=pl.ANY)] * 2
        args += list(prev_caches)
        aliases = {4: 1, 5: 2}
    return pl.pallas_call(
        _inproj_kernel,
        grid=(ni, INP_NJ),
        in_specs=in_specs,
        out_specs=[pl.BlockSpec((INP_TM, INP_TN), lambda i, j: (i, j)), cache_spec, cache_spec],
        out_shape=[jax.ShapeDtypeStruct((N_TOK, IN_TOTAL), BF16), cache_shape, cache_shape],
        scratch_shapes=[pltpu.VMEM((INP_TM, D_MODEL), BF16)],
        input_output_aliases=aliases,
        compiler_params=_cparams(("arbitrary", "arbitrary"), 56),
        name="inproj",
    )(*args)


ATT_TQ = 256
W_A = H_A * DV_A


def _lambda(lam_ref, lam_init):
    la = lam_ref[...]
    l01 = jnp.sum(la[0:1, :] * la[1:2, :], axis=1, keepdims=True)
    l23 = jnp.sum(la[2:3, :] * la[3:4, :], axis=1, keepdims=True)
    return jnp.exp(l01) - jnp.exp(l23) + lam_init


def _exp_shifted(s):
    return jnp.exp(s - jnp.max(s, axis=-1, keepdims=True)).astype(BF16)


def _diff_attn_block(q, k_ref, v_ref, lam, subln, lam_init, ones_beside_v):
    lane = lax.broadcasted_iota(jnp.int32, (1, LANES), 1)
    first = lane < DH_A
    scale = jnp.asarray(DH_A ** -0.5, BF16)
    outs = []
    for h in range(H_A):
        sl = slice(h * LANES, (h + 1) * LANES)
        qh = q[:, sl]
        kh = k_ref[:, sl]
        zero = jnp.zeros_like(qh)
        qs = qh * scale
        s0 = _dot_nt(jnp.where(first, qs, zero), kh)
        s1 = _dot_nt(jnp.where(first, zero, qs), kh)
        if ones_beside_v:
            vx = v_ref[:, 2 * h * LANES:2 * (h + 1) * LANES]
            pv0 = _dot(_exp_shifted(s0), vx)
            pv1 = _dot(_exp_shifted(s1), vx)
            r0 = 1.0 / pv0[:, LANES:LANES + 1]
            r1 = 1.0 / pv1[:, LANES:LANES + 1]
            o = pv0[:, :LANES] * r0 - (lam * r1) * pv1[:, :LANES]
        else:
            e0 = jnp.exp(s0 - jnp.max(s0, axis=-1, keepdims=True))
            e1 = jnp.exp(s1 - jnp.max(s1, axis=-1, keepdims=True))
            r0 = 1.0 / jnp.sum(e0, axis=-1, keepdims=True)
            r1 = 1.0 / jnp.sum(e1, axis=-1, keepdims=True)
            o = _dot((e0 * r0 - (lam * r1) * e1).astype(BF16), v_ref[:, sl])
        outs.append((_rms(o) * subln) * (1.0 - lam_init))
    return outs


def _store_values_with_ones(vx_ref, rows, v):
    ones = jnp.ones((v.shape[0], LANES), BF16)
    for h in range(H_A):
        vx_ref[rows, 2 * h * LANES:(2 * h + 1) * LANES] = v[:, h * LANES:(h + 1) * LANES]
        vx_ref[rows, (2 * h + 1) * LANES:2 * (h + 1) * LANES] = ones


def _rope(x, cos, sin):
    lane = lax.broadcasted_iota(jnp.int32, (1, x.shape[1]), 1)
    first = (lane % 32) < 16
    n = x.shape[1]
    xr = jnp.where(first, -pltpu.roll(x, n - 16, 1), pltpu.roll(x, 16, 1))
    return x * cos + xr * sin


def _attn_ctx_kernel(lam_init, q_ref, k_ref, v_ref, lam_ref, subln_ref, o_ref):
    lam = _lambda(lam_ref, lam_init)
    outs = _diff_attn_block(q_ref[...], k_ref, v_ref, lam, subln_ref[...], lam_init, False)
    for h in range(H_A):
        o_ref[:, h * LANES:(h + 1) * LANES] = outs[h].astype(BF16)


def _attn_ctx(z, lam_a_l, subln_l, lam_init):
    wb = W_A
    return pl.pallas_call(
        functools.partial(_attn_ctx_kernel, lam_init),
        grid=(BATCH,),
        in_specs=[
            pl.BlockSpec((SEQ, wb), lambda b: (b, Z_QA // wb)),
            pl.BlockSpec((SEQ, wb), lambda b: (b, Z_KA // wb)),
            pl.BlockSpec((SEQ, wb), lambda b: (b, Z_VA // wb)),
            pl.BlockSpec((4, DH_A), lambda b: (0, 0)),
            pl.BlockSpec((1, DV_A), lambda b: (0, 0)),
        ],
        out_specs=pl.BlockSpec((SEQ, wb), lambda b: (b, 0)),
        out_shape=jax.ShapeDtypeStruct((N_TOK, wb), BF16),
        compiler_params=_cparams(("parallel",), 40),
        name="attn_ctx",
    )(z, z, z, lam_a_l, subln_l.reshape(1, DV_A))


def _attn_lat_kernel(lam_init, q_ref, k_ref, v_ref, ck_ref, cv_ref, cos_ref, sin_ref,
                     lam_ref, subln_ref, ya_ctx_ref, o_ref, kall, vall):
    del ya_ctx_ref
    lam = _lambda(lam_ref, lam_init)
    subln = subln_ref[...]
    kall[0:DEC_SEQ, :] = _rope(k_ref[...].astype(F32), cos_ref[...], sin_ref[...]).astype(BF16)
    kall[DEC_SEQ:, :] = ck_ref[0, 0].astype(BF16)
    _store_values_with_ones(vall, slice(0, DEC_SEQ), v_ref[...])
    _store_values_with_ones(vall, slice(DEC_SEQ, DEC_SEQ + PAST_LEN), cv_ref[0, 0].astype(BF16))

    def q_block(qb, carry):
        rows = pl.ds(pl.multiple_of(qb * ATT_TQ, ATT_TQ), ATT_TQ)
        q = _rope(q_ref[rows, :].astype(F32), cos_ref[rows, :], sin_ref[rows, :]).astype(BF16)
        outs = _diff_attn_block(q, kall, vall, lam, subln, lam_init, True)
        for h in range(H_A):
            o_ref[rows, h * LANES:(h + 1) * LANES] = outs[h].astype(BF16)
        return carry

    lax.fori_loop(0, DEC_SEQ // ATT_TQ, q_block, 0)


def _attn_lat(z, cache_k4, cache_v4, layer, cos, sin, lam_a_l, subln_l, lam_init, ya_ctx):
    wb = W_A
    rb0 = N_CTX_TOK // DEC_SEQ
    t_all = DEC_SEQ + PAST_LEN
    return pl.pallas_call(
        functools.partial(_attn_lat_kernel, lam_init),
        grid=(DEC_BATCH,),
        in_specs=[
            pl.BlockSpec((DEC_SEQ, wb), lambda b: (rb0 + b, Z_QA // wb)),
            pl.BlockSpec((DEC_SEQ, wb), lambda b: (rb0 + b, Z_KA // wb)),
            pl.BlockSpec((DEC_SEQ, wb), lambda b: (rb0 + b, Z_VA // wb)),
            pl.BlockSpec((1, 1, PAST_LEN, wb), lambda b: (b, layer, 0, 0)),
            pl.BlockSpec((1, 1, PAST_LEN, wb), lambda b: (b, layer, 0, 0)),
            pl.BlockSpec((DEC_SEQ, wb), lambda b: (0, 0)),
            pl.BlockSpec((DEC_SEQ, wb), lambda b: (0, 0)),
            pl.BlockSpec((4, DH_A), lambda b: (0, 0)),
            pl.BlockSpec((1, DV_A), lambda b: (0, 0)),
            pl.BlockSpec(memory_space=pl.ANY),
        ],
        out_specs=pl.BlockSpec((DEC_SEQ, wb), lambda b: (rb0 + b, 0)),
        out_shape=jax.ShapeDtypeStruct((N_TOK, wb), BF16),
        scratch_shapes=[pltpu.VMEM((t_all, wb), BF16), pltpu.VMEM((t_all, 2 * wb), BF16)],
        input_output_aliases={9: 0},
        compiler_params=_cparams(("parallel",), 48),
        name="attn_lat",
    )(z, z, z, cache_k4, cache_v4, cos, sin, lam_a_l, subln_l.reshape(1, DV_A), ya_ctx)


def _rope_tables(s):
    rows = s // GRID_W
    row = jnp.repeat(jnp.arange(rows, dtype=F32), GRID_W)
    col = jnp.tile(jnp.arange(GRID_W, dtype=F32), rows)
    quarter = DH_A // 4
    inv = ROPE_BASE ** (-jnp.arange(quarter, dtype=F32) / quarter)
    ar = row[:, None] * inv
    ac = col[:, None] * inv
    ang = jnp.concatenate([ar, ar, ac, ac], axis=-1)
    reps = (H_A * 2 * DH_A) // DH_A
    return jnp.tile(jnp.cos(ang), (1, reps)), jnp.tile(jnp.sin(ang), (1, reps))


W_B = G_B * DG_B


def _dft_consts(s):
    def cs(n, scale):
        j = np.arange(n, dtype=np.int64)
        ang = 2.0 * np.pi * ((j[:, None] * j[None, :]) % n).astype(np.float64) / n
        return np.cos(ang) * scale, np.sin(ang) * scale
    cc, sc = cs(DG_B, DG_B ** -0.5)
    chan = np.concatenate([cc, sc], axis=1)
    cp, sp = cs(s, s ** -0.5)
    pos = np.concatenate([cp, -sp], axis=1)
    return jnp.asarray(chan, dtype=F32).astype(BF16), jnp.asarray(pos, dtype=F32).astype(BF16)


def _fourier_kernel(s, f_ref, chan_ref, pos_ref, *rest):
    o_ref, p_ref = rest[-2:]
    chan = chan_ref[...]
    for g in range(G_B):
        sl = slice(g * DG_B, (g + 1) * DG_B)
        a = _dot(f_ref[:, sl], chan)
        p_ref[0:s, sl] = a[:, :DG_B].astype(BF16)
        p_ref[s:2 * s, sl] = a[:, DG_B:].astype(BF16)
    o_ref[...] = _dot(pos_ref[...], p_ref[...]).astype(BF16)


def _fourier(z, s, n_batch, row_block0, chan, pos, prev=None):
    in_specs = [
        pl.BlockSpec((s, W_B), lambda b: (row_block0 + b, Z_FB // W_B)),
        pl.BlockSpec((DG_B, 2 * DG_B), lambda b: (0, 0)),
        pl.BlockSpec((s, 2 * s), lambda b: (0, 0)),
    ]
    args = [z, chan, pos]
    aliases = {}
    if prev is not None:
        in_specs.append(pl.BlockSpec(memory_space=pl.ANY))
        args.append(prev)
        aliases = {3: 0}
    return pl.pallas_call(
        functools.partial(_fourier_kernel, s),
        grid=(n_batch,),
        in_specs=in_specs,
        out_specs=pl.BlockSpec((s, W_B), lambda b: (row_block0 + b, 0)),
        out_shape=jax.ShapeDtypeStruct((N_TOK, W_B), BF16),
        scratch_shapes=[pltpu.VMEM((2 * s, W_B), BF16)],
        input_output_aliases=aliases,
        compiler_params=_cparams(("parallel",), 40),
        name="fourier_%d" % s,
    )(*args)


W_C = H_C * DV_C
RET_TQ = 256


def _log_sigmoid(x):
    return jnp.minimum(x, 0.0) - jnp.log1p(jnp.exp(-jnp.abs(x)))


def _head_pair_lanes(vals, pair):
    lane = lax.broadcasted_iota(jnp.int32, (1, LANES), 1)
    return jnp.where(lane < DK_C, vals[:, 2 * pair:2 * pair + 1], vals[:, 2 * pair + 1:2 * pair + 2])


def _fill_decay(dec_ref, lg, s, tq):
    for r0 in range(0, s, tq):
        ii = (r0 + lax.broadcasted_iota(jnp.int32, (tq, s), 0)).astype(F32)
        jj = lax.broadcasted_iota(jnp.int32, (tq, s), 1).astype(F32)
        rel = ii - jj
        for h in range(H_C):
            lgf = lg[0:1, h:h + 1]
            lgb = lg[1:2, h:h + 1]
            dec_ref[h, r0:r0 + tq, :] = (
                jnp.where(rel >= 0, jnp.exp(lgf * jnp.maximum(rel, 0.0)), 0.0)
                + jnp.where(rel <= 0, jnp.exp(lgb * jnp.maximum(-rel, 0.0)), 0.0))


def _retention_rows(s, row0, tq, q, k_ref, v_ref, g, lg, rnorm, s0f_ref, s0b_ref, dec_ref):
    lane = lax.broadcasted_iota(jnp.int32, (1, LANES), 1)
    pos = (row0 + lax.broadcasted_iota(jnp.int32, (tq, 1), 0)).astype(F32)
    outs = []
    for h in range(H_C):
        pair, half = h // 2, h % 2
        sl = slice(pair * LANES, (pair + 1) * LANES)
        lgf = lg[0:1, h:h + 1]
        lgb = lg[1:2, h:h + 1]
        in_head = (lane < DK_C) if half == 0 else (lane >= DK_C)
        q_pair = q[:, sl]
        qh = jnp.where(in_head, q_pair, jnp.zeros_like(q_pair))
        sc = _dot_nt(qh, k_ref[:, sl]) * (DK_C ** -0.5)
        decay = dec_ref[h, pl.ds(row0, tq), :]
        o = _dot((sc * decay).astype(BF16), v_ref[:, h * DV_C:(h + 1) * DV_C])
        if s0f_ref is not None:
            qf = qh.astype(F32)
            xf = jnp.exp(lgf * (pos + 1.0))
            xb = jnp.exp(lgb * (s - pos))
            o = o + _dot((qf * xf).astype(BF16), s0f_ref[0, 0, sl, :].astype(BF16))
            o = o + _dot((qf * xb).astype(BF16), s0b_ref[0, 0, sl, :].astype(BF16))
        gh = g[:, h * DV_C:(h + 1) * DV_C].astype(F32)
        outs.append((_rms(o) * rnorm) * (gh * _sigmoid(gh)))
    return outs


def _ret_ctx_kernel(q_ref, k_ref, v_ref, g_ref, dec_ref, rn_ref, *rest):
    o_ref, sf_ref, sb_ref, decay = rest[-4:]
    s = SEQ
    lg = _log_sigmoid(dec_ref[...])

    @pl.when(pl.program_id(0) == 0)
    def _():
        _fill_decay(decay, lg, s, s)

    outs = _retention_rows(s, 0, s, q_ref[...], k_ref, v_ref, g_ref[...], lg, rn_ref[...], None, None,
                           decay)
    for h in range(H_C):
        o_ref[:, h * DV_C:(h + 1) * DV_C] = outs[h].astype(BF16)
    pos = lax.broadcasted_iota(jnp.int32, (s, 1), 0).astype(F32)
    v = v_ref[...]
    for pair in range(H_C // 2):
        sl = slice(pair * LANES, (pair + 1) * LANES)
        kf = k_ref[:, sl].astype(F32) * (DK_C ** -0.5)
        zf = jnp.exp(_head_pair_lanes(lg[0:1, :], pair) * (s - 1.0 - pos))
        zb = jnp.exp(_head_pair_lanes(lg[1:2, :], pair) * pos)
        stf = _dot_tn((kf * zf).astype(BF16), v)
        stb = _dot_tn((kf * zb).astype(BF16), v)
        for half in range(2):
            h = 2 * pair + half
            rs = slice(half * DK_C, (half + 1) * DK_C)
            cs = slice(h * DV_C, (h + 1) * DV_C)
            sf_ref[0, 0, h] = stf[rs, cs]
            sb_ref[0, 0, h] = stb[rs, cs]


def _ret_ctx(z, dec_l, rnorm_l, layer, prev_states):
    wq = H_C * DK_C
    st_shape = jax.ShapeDtypeStruct((BATCH, DEPTH, H_C, DK_C, DV_C), F32)
    st_spec = pl.BlockSpec((1, 1, H_C, DK_C, DV_C), lambda b: (b, layer, 0, 0, 0))
    in_specs = [
        pl.BlockSpec((SEQ, wq), lambda b: (b, Z_QC // wq)),
        pl.BlockSpec((SEQ, wq), lambda b: (b, Z_KC // wq)),
        pl.BlockSpec((SEQ, W_C), lambda b: (b, Z_VC // W_C)),
        pl.BlockSpec((SEQ, W_C), lambda b: (b, Z_GC // W_C)),
        pl.BlockSpec((2, H_C), lambda b: (0, 0)),
        pl.BlockSpec((1, DV_C), lambda b: (0, 0)),
    ]
    args = [z, z, z, z, dec_l, rnorm_l.reshape(1, DV_C)]
    aliases = {}
    if prev_states is not None:
        in_specs += [pl.BlockSpec(memory_space=pl.ANY)] * 2
        args += list(prev_states)
        aliases = {6: 1, 7: 2}
    return pl.pallas_call(
        _ret_ctx_kernel,
        grid=(BATCH,),
        in_specs=in_specs,
        out_specs=[pl.BlockSpec((SEQ, W_C), lambda b: (b, 0)), st_spec, st_spec],
        out_shape=[jax.ShapeDtypeStruct((N_TOK, W_C), BF16), st_shape, st_shape],
        scratch_shapes=[pltpu.VMEM((H_C, SEQ, SEQ), F32)],
        input_output_aliases=aliases,
        compiler_params=_cparams(("arbitrary",), 40),
        name="ret_ctx",
    )(*args)


def _ret_lat_kernel(q_ref, k_ref, v_ref, g_ref, dec_ref, rn_ref, s0f_ref, s0b_ref, yc_ctx_ref, o_ref,
                    decay):
    del yc_ctx_ref
    lg = _log_sigmoid(dec_ref[...])
    rnorm = rn_ref[...]

    @pl.when(pl.program_id(0) == 0)
    def _():
        _fill_decay(decay, lg, DEC_SEQ, RET_TQ)

    def q_block(qb, carry):
        row0 = pl.multiple_of(qb * RET_TQ, RET_TQ)
        rows = pl.ds(row0, RET_TQ)
        outs = _retention_rows(DEC_SEQ, row0, RET_TQ, q_ref[rows, :], k_ref, v_ref, g_ref[rows, :],
                               lg, rnorm, s0f_ref, s0b_ref, decay)
        for h in range(H_C):
            o_ref[rows, h * DV_C:(h + 1) * DV_C] = outs[h].astype(BF16)
        return carry

    lax.fori_loop(0, DEC_SEQ // RET_TQ, q_block, 0)


def _ret_lat(z, dec_l, rnorm_l, s0f4, s0b4, layer, yc_ctx):
    wq = H_C * DK_C
    rb0 = N_CTX_TOK // DEC_SEQ
    st_spec = pl.BlockSpec((1, 1, H_C * DK_C, DV_C), lambda b: (b, layer, 0, 0))
    return pl.pallas_call(
        _ret_lat_kernel,
        grid=(DEC_BATCH,),
        in_specs=[
            pl.BlockSpec((DEC_SEQ, wq), lambda b: (rb0 + b, Z_QC // wq)),
            pl.BlockSpec((DEC_SEQ, wq), lambda b: (rb0 + b, Z_KC // wq)),
            pl.BlockSpec((DEC_SEQ, W_C), lambda b: (rb0 + b, Z_VC // W_C)),
            pl.BlockSpec((DEC_SEQ, W_C), lambda b: (rb0 + b, Z_GC // W_C)),
            pl.BlockSpec((2, H_C), lambda b: (0, 0)),
            pl.BlockSpec((1, DV_C), lambda b: (0, 0)),
            st_spec, st_spec,
            pl.BlockSpec(memory_space=pl.ANY),
        ],
        out_specs=pl.BlockSpec((DEC_SEQ, W_C), lambda b: (rb0 + b, 0)),
        out_shape=jax.ShapeDtypeStruct((N_TOK, W_C), BF16),
        scratch_shapes=[pltpu.VMEM((H_C, DEC_SEQ, DEC_SEQ), F32)],
        input_output_aliases={8: 0},
        compiler_params=_cparams(("arbitrary",), 48),
        name="ret_lat",
    )(z, z, z, z, dec_l, rnorm_l.reshape(1, DV_C), s0f4, s0b4, yc_ctx)


MRG_TM = 512
MRG_SUBBLOCKS = 2
GATE_BLOCK = 512


def _merge_kernel(x_ref, ya_ref, yb_ref, yc_ref, g0_ref, g1_ref, g2_ref, g3_ref, g4_ref, g5_ref,
                  mod_ref, wa_ref, wb_ref, wc_ref, wo_ref,
                  n2_ref, rw_ref, rb_ref, x1_ref, h2_ref, topi_ref, topw_ref, rank_ref,
                  cnt_row_ref, cnt_col_ref, carry_row, carry_col, wa_bf, wb_bf, wc_bf, wo_bf):
    d = D_MODEL
    tm = x_ref.shape[0]

    @pl.when(pl.program_id(0) == 0)
    def _():
        carry_row[...] = jnp.zeros_like(carry_row)
        carry_col[...] = jnp.zeros_like(carry_col)
        wa_bf[...] = wa_ref[0].astype(BF16)
        wb_bf[...] = wb_ref[0].astype(BF16)
        wc_bf[...] = wc_ref[0].astype(BF16)
        wo_bf[...] = wo_ref[0].astype(BF16)

    def gate(lo_ref, hi_ref, rs):
        return _sigmoid(jnp.concatenate([lo_ref[rs, :], hi_ref[rs, :]], axis=1).astype(F32))

    sub = tm // MRG_SUBBLOCKS
    for sb in range(MRG_SUBBLOCKS):
        rs = slice(sb * sub, (sb + 1) * sub)
        merged = gate(g0_ref, g1_ref, rs) * _dot(ya_ref[rs, :], wa_bf[...])
        merged += gate(g2_ref, g3_ref, rs) * _dot(yb_ref[rs, :], wb_bf[...])
        merged += gate(g4_ref, g5_ref, rs) * _dot(yc_ref[rs, :], wc_bf[...])
        y = _dot(merged.astype(BF16), wo_bf[...])
        x1 = x_ref[rs, :] + mod_ref[0, 2:3, :] * y
        x1_ref[rs, :] = x1
        h2 = _modnorm(x1, n2_ref[...], mod_ref[0, 4:5, :], mod_ref[0, 3:4, :])
        h2_ref[rs, :] = h2
        logits = _dot_f32x3(h2, rw_ref[...]) + rb_ref[...]
        ids = lax.broadcasted_iota(jnp.int32, logits.shape, 1)
        work = logits
        sels, vals, firsts = [], [], []
        for _ in range(TOP_K):
            m = jnp.max(work, axis=-1, keepdims=True)
            first = jnp.min(jnp.where(work == m, ids, N_EXPERTS), axis=-1, keepdims=True)
            sel = ids == first
            sels.append(sel)
            vals.append(m)
            firsts.append(first)
            work = jnp.where(sel, -jnp.inf, work)
        es = [jnp.exp(v - vals[0]) for v in vals]
        inv = 1.0 / (es[0] + es[1] + es[2] + es[3])
        chosen = jnp.zeros_like(logits)
        for sel in sels:
            chosen = chosen + jnp.where(sel, 1.0, 0.0)
        chosen_bf = chosen.astype(BF16)
        rr = lax.broadcasted_iota(jnp.int32, (sub, sub), 0)
        cc = lax.broadcasted_iota(jnp.int32, (sub, sub), 1)
        before = jnp.where(rr > cc, 1.0, 0.0).astype(BF16)
        prefix = _dot(before, chosen_bf) + carry_row[...]
        ranks = [jnp.sum(jnp.where(sel, prefix, 0.0), axis=-1, keepdims=True) for sel in sels]
        carry_row[...] += jnp.sum(chosen, axis=0, keepdims=True)
        carry_col[...] += _dot_tn(chosen_bf, jnp.ones((sub, LANES), BF16))
        topi_ref[rs, :] = jnp.concatenate(firsts, axis=1)
        topw_ref[rs, :] = jnp.concatenate([e * inv for e in es], axis=1)
        rank_ref[rs, :] = jnp.concatenate(ranks, axis=1).astype(jnp.int32)
    cnt_row_ref[...] = carry_row[...]
    cnt_col_ref[...] = carry_col[...]


def _merge(x, ya, yb, yc, z, mod_l, wa, wb, wc, wo, n2, rw, rb, layer):
    tm = MRG_TM
    d = D_MODEL
    full = lambda shape: pl.BlockSpec(shape, lambda i: tuple(0 for _ in shape))
    tok = lambda w: pl.BlockSpec((tm, w), lambda i: (i, 0))
    per_layer = lambda rows: pl.BlockSpec((1, rows, d), lambda i: (layer, 0, 0))
    return pl.pallas_call(
        _merge_kernel,
        grid=(N_TOK // tm,),
        in_specs=[
            tok(d), tok(W_A), tok(W_B), tok(W_C),
            *[pl.BlockSpec((tm, GATE_BLOCK), lambda i, c=c: (i, Z_GATES // GATE_BLOCK + c)) for c in range(6)],
            pl.BlockSpec((1, 6, d), lambda i: (_mod_row(i, tm), 0, 0)),
            per_layer(W_A), per_layer(W_B), per_layer(W_C), per_layer(d),
            full((1, d)), full((d, N_EXPERTS)), full((1, N_EXPERTS)),
        ],
        out_specs=[tok(d), tok(d), tok(TOP_K), tok(TOP_K), tok(TOP_K),
                   full((1, N_EXPERTS)), full((N_EXPERTS, LANES))],
        out_shape=[
            jax.ShapeDtypeStruct((N_TOK, d), F32),
            jax.ShapeDtypeStruct((N_TOK, d), F32),
            jax.ShapeDtypeStruct((N_TOK, TOP_K), jnp.int32),
            jax.ShapeDtypeStruct((N_TOK, TOP_K), F32),
            jax.ShapeDtypeStruct((N_TOK, TOP_K), jnp.int32),
            jax.ShapeDtypeStruct((1, N_EXPERTS), F32),
            jax.ShapeDtypeStruct((N_EXPERTS, LANES), F32),
        ],
        scratch_shapes=[pltpu.VMEM((1, N_EXPERTS), F32), pltpu.VMEM((N_EXPERTS, LANES), F32),
                        pltpu.VMEM((W_A, d), BF16), pltpu.VMEM((W_B, d), BF16),
                        pltpu.VMEM((W_C, d), BF16), pltpu.VMEM((d, d), BF16)],
        compiler_params=_cparams(("arbitrary",), 56),
        name="merge_router",
    )(x, ya, yb, yc, z, z, z, z, z, z, mod_l, wa, wb, wc, wo, n2.reshape(1, d), rw, rb.reshape(1, N_EXPERTS))


MOE_TM = 256
MOE_TILES = (N_TOK * TOP_K) // MOE_TM + N_EXPERTS
MOE_ROWS = MOE_TILES * MOE_TM
PLAN_LANES = 256
PLAN_TM = 1024


def _plan_kernel(topi_ref, rank_ref, cnt_row_ref, cnt_col_ref, pos_ref, te_ref, nt_ref, last_ref,
                 next_ref):
    e_r = lax.broadcasted_iota(jnp.int32, (N_EXPERTS, N_EXPERTS), 0)
    e_c = lax.broadcasted_iota(jnp.int32, (N_EXPERTS, N_EXPERTS), 1)
    cnt_row = cnt_row_ref[...]
    tiles_row = jnp.ceil(cnt_row * (1.0 / MOE_TM))
    earlier = jnp.where(e_r < e_c, 1.0, 0.0).astype(BF16)
    start_tile_row = _dot(tiles_row.astype(BF16), earlier)
    off_row = start_tile_row * MOE_TM
    ids = lax.broadcasted_iota(jnp.int32, (PLAN_TM, N_EXPERTS), 1)
    cols = []
    for k in range(TOP_K):
        hit = ids == topi_ref[:, k:k + 1]
        cols.append(jnp.sum(jnp.where(hit, off_row, 0.0), axis=-1, keepdims=True))
    pos_ref[...] = jnp.concatenate(cols, axis=1).astype(jnp.int32) + rank_ref[...]
    tiles_col = jnp.ceil(cnt_col_ref[...] * (1.0 / MOE_TM))
    upto = jnp.where(e_c <= e_r, 1.0, 0.0).astype(BF16)
    end_col = _dot(upto, tiles_col.astype(BF16))
    end_b = jnp.concatenate([end_col] * (PLAN_LANES // LANES), axis=1)
    r = lax.broadcasted_iota(jnp.int32, (N_EXPERTS, PLAN_LANES), 1).astype(F32)
    total = end_b[N_EXPERTS - 1:N_EXPERTS, :]
    te = jnp.sum(jnp.where(end_b <= jnp.minimum(r, total - 1.0), 1.0, 0.0), axis=0, keepdims=True)
    te_ref[...] = te.astype(jnp.int32)
    nt_ref[...] = total[:, :LANES].astype(jnp.int32)
    last = jnp.where(cnt_row > 0.0, (start_tile_row + tiles_row - 1.0) * MOE_TM, -1.0)
    last_ref[...] = last.astype(jnp.int32)
    later_used = jnp.logical_and(e_r > e_c, cnt_col_ref[:, 0:N_EXPERTS] > 0.0)
    nxt = jnp.min(jnp.where(later_used, e_r, N_EXPERTS), axis=0, keepdims=True)
    next_ref[...] = jnp.where(nxt == N_EXPERTS, -1, nxt)


def _plan(topi, rank, cnt_row, cnt_col):
    tok = pl.BlockSpec((PLAN_TM, TOP_K), lambda i: (i, 0))
    full = lambda shape: pl.BlockSpec(shape, lambda i: tuple(0 for _ in shape))
    return pl.pallas_call(
        _plan_kernel,
        grid=(N_TOK // PLAN_TM,),
        in_specs=[tok, tok, full((1, N_EXPERTS)), full((N_EXPERTS, LANES))],
        out_specs=[tok, full((1, PLAN_LANES)), full((1, LANES)), full((1, N_EXPERTS)),
                   full((1, N_EXPERTS))],
        out_shape=[
            jax.ShapeDtypeStruct((N_TOK, TOP_K), jnp.int32),
            jax.ShapeDtypeStruct((1, PLAN_LANES), jnp.int32),
            jax.ShapeDtypeStruct((1, LANES), jnp.int32),
            jax.ShapeDtypeStruct((1, N_EXPERTS), jnp.int32),
            jax.ShapeDtypeStruct((1, N_EXPERTS), jnp.int32),
        ],
        compiler_params=_cparams(("arbitrary",), 32),
        name="route_plan",
    )(topi, rank, cnt_row, cnt_col)


DSP_TM = 1024


def _dispatch_kernel(pos_ref, last_ref, h_ref, xs_ref, zero_buf, zsem, sem):
    i = pl.program_id(0)

    @pl.when(i == 0)
    def _():
        zero_buf[...] = jnp.zeros_like(zero_buf)

        def zero_copy(e):
            return pltpu.make_async_copy(zero_buf, xs_ref.at[pl.ds(pl.multiple_of(last_ref[0, e], MOE_TM), MOE_TM)], zsem)

        for e in range(N_EXPERTS):
            @pl.when(last_ref[0, e] >= 0)
            def _():
                zero_copy(e).start()
        for e in range(N_EXPERTS):
            @pl.when(last_ref[0, e] >= 0)
            def _():
                zero_copy(e).wait()

    def row_copy(t, k):
        return pltpu.make_async_copy(h_ref.at[pl.ds(t, 1)],
                                     xs_ref.at[pl.ds(pos_ref[0, 0, t * TOP_K + k], 1)], sem)

    def issue(t, carry):
        for k in range(TOP_K):
            row_copy(t, k).start()
        return carry

    lax.fori_loop(0, DSP_TM, issue, 0, unroll=8)

    for k in range(TOP_K):
        pltpu.make_async_copy(h_ref, xs_ref.at[pl.ds(0, DSP_TM)], sem).wait()


def _dispatch(pos, last, h2):
    n_tiles = N_TOK // DSP_TM
    return pl.pallas_call(
        _dispatch_kernel,
        grid=(n_tiles,),
        in_specs=[
            pl.BlockSpec((1, 1, DSP_TM * TOP_K), lambda i: (i, 0, 0), memory_space=pltpu.SMEM),
            pl.BlockSpec((1, N_EXPERTS), lambda i: (0, 0), memory_space=pltpu.SMEM),
            pl.BlockSpec((DSP_TM, D_MODEL), lambda i: (i, 0)),
        ],
        out_specs=pl.BlockSpec(memory_space=pl.ANY),
        out_shape=jax.ShapeDtypeStruct((MOE_ROWS, D_MODEL), F32),
        scratch_shapes=[pltpu.VMEM((MOE_TM, D_MODEL), F32), pltpu.SemaphoreType.DMA,
                        pltpu.SemaphoreType.DMA],
        compiler_params=_cparams(("arbitrary",), 32),
        name="dispatch",
    )(pos.reshape(n_tiles, 1, DSP_TM * TOP_K), last, h2)


PAIR_BLOCKS = (2 * D_FF) // (2 * LANES)
WEIGHT_PARTS = 3


def _prep_expert_weights(wgu_ref, wdn_ref, wgu_bf, wdn_perm, wdn_bf):
    wgu_bf[...] = wgu_ref[...].astype(BF16)
    half = LANES // 2
    for cb in range(D_MODEL // LANES):
        cols = slice(cb * LANES, (cb + 1) * LANES)
        for m in range(PAIR_BLOCKS):
            base = m * LANES
            wdn_perm[cb, pl.ds(base, half, stride=2), :] = wdn_ref[base:base + half, cols]
            wdn_perm[cb, pl.ds(base + 1, half, stride=2), :] = wdn_ref[base + half:base + LANES, cols]
        wdn_bf[:, cols] = wdn_perm[cb].astype(BF16)


def _ffn_rows(x, wgu_bf, bgu, wdn_bf, bdn):
    gu = _dot(x, wgu_bf[...]) + bgu
    lane = lax.broadcasted_iota(jnp.int32, (1, LANES), 1)
    even = (lane % 2) == 0

    def act(block):
        t = gu[:, block * LANES:(block + 1) * LANES]
        glu = jnp.minimum(t, SWIGLU_LIMIT)
        glu = glu * _sigmoid(SWIGLU_ALPHA * glu)
        lin = jnp.clip(t, -SWIGLU_LIMIT, SWIGLU_LIMIT) + 1.0
        return glu * pltpu.roll(lin, LANES - 1, 1)

    parts = []
    for m in range(PAIR_BLOCKS):
        parts.append(jnp.where(even, act(2 * m), pltpu.roll(act(2 * m + 1), 1, 1)))
    hidden = jnp.concatenate(parts, axis=1).astype(BF16)
    return _dot(hidden, wdn_bf[...]) + bdn


def _moe_kernel(layer, te_ref, nt_ref, next_ref, x_ref, wgu_hbm, bgu_ref, wdn_hbm, bdn_ref, o_ref,
                wgu_land, wdn_land, wgu_bf, wdn_perm, wdn_bf, issued, sems):
    r = pl.program_id(0)
    valid = r < nt_ref[0]
    e = te_ref[r]
    new_expert = jnp.logical_or(r == 0, e != te_ref[jnp.maximum(r - 1, 0)])
    half = D_MODEL // 2

    def part(expert, p):
        if p == 0:
            return pltpu.make_async_copy(wgu_hbm.at[layer, expert, pl.ds(0, half)],
                                         wgu_land.at[pl.ds(0, half)], sems.at[0])
        if p == 1:
            return pltpu.make_async_copy(wgu_hbm.at[layer, expert, pl.ds(half, half)],
                                         wgu_land.at[pl.ds(half, half)], sems.at[1])
        return pltpu.make_async_copy(wdn_hbm.at[layer, expert], wdn_land, sems.at[2])

    @pl.when(r == 0)
    def _():
        issued[0] = 0

    @pl.when(jnp.logical_and(valid, new_expert))
    def _():
        for p in range(WEIGHT_PARTS):
            @pl.when(issued[0] <= p)
            def _():
                part(e, p).start()
        for p in range(WEIGHT_PARTS):
            part(e, p).wait()
        _prep_expert_weights(wgu_land, wdn_land, wgu_bf, wdn_perm, wdn_bf)
        issued[0] = 0

    @pl.when(valid)
    def _():
        nxt = next_ref[e]
        for p in range(WEIGHT_PARTS):
            @pl.when(jnp.logical_and(nxt >= 0, issued[0] == p))
            def _():
                part(nxt, p).start()

        @pl.when(jnp.logical_and(nxt >= 0, issued[0] < WEIGHT_PARTS))
        def _():
            issued[0] = issued[0] + 1

        o_ref[...] = _ffn_rows(x_ref[...].astype(BF16), wgu_bf, bgu_ref[0, 0], wdn_bf, bdn_ref[0, 0])


def _moe_grouped(tile_expert, n_tiles, next_expert, xs, w_gate_up, b_gate_up4, w_down, b_down4, layer):
    d, f2 = D_MODEL, 2 * D_FF
    row_tile = lambda r, te, nt, nx: (jnp.minimum(r, nt[0] - 1), 0)
    expert = lambda r, te, nt, nx: (layer, te[r], 0, 0)
    grid_spec = pltpu.PrefetchScalarGridSpec(
        num_scalar_prefetch=3,
        grid=(MOE_TILES,),
        in_specs=[
            pl.BlockSpec((MOE_TM, d), row_tile),
            pl.BlockSpec(memory_space=pl.ANY),
            pl.BlockSpec((1, 1, 1, f2), expert),
            pl.BlockSpec(memory_space=pl.ANY),
            pl.BlockSpec((1, 1, 1, d), expert),
        ],
        out_specs=pl.BlockSpec((MOE_TM, d), row_tile),
        scratch_shapes=[
            pltpu.VMEM((d, f2), F32),
            pltpu.VMEM((D_FF, d), F32),
            pltpu.VMEM((d, f2), BF16),
            pltpu.VMEM((d // LANES, D_FF, LANES), F32),
            pltpu.VMEM((D_FF, d), BF16),
            pltpu.SMEM((1,), jnp.int32),
            pltpu.SemaphoreType.DMA((WEIGHT_PARTS,)),
        ],
    )
    return pl.pallas_call(
        functools.partial(_moe_kernel, layer),
        grid_spec=grid_spec,
        out_shape=jax.ShapeDtypeStruct((MOE_ROWS, d), F32),
        compiler_params=_cparams(("arbitrary",), 56),
        name="moe_grouped",
    )(tile_expert, n_tiles, next_expert, xs, w_gate_up, b_gate_up4, w_down, b_down4)


CMB_TM = 256
CMB_STEP = 2 * CMB_TM


def _combine_kernel(final, pos_ref, pos_next_ref, ys_ref, x1_ref, topw_ref, mod_ref, *rest):
    buf, sems = rest[-2:]
    j = pl.program_id(0)
    is_ctx = j < N_CTX_TOK // CMB_STEP

    def issue_tile(p_ref, first, s):
        def issue(t, carry):
            for k in range(TOP_K):
                pltpu.make_async_copy(ys_ref.at[pl.ds(p_ref[0, 0, first + t * TOP_K + k], 1)],
                                      buf.at[s, k, pl.ds(t, 1)], sems.at[s]).start()
            return carry

        lax.fori_loop(0, CMB_TM, issue, 0, unroll=8)

    def finish_tile(s):
        rows = slice(s * CMB_TM, (s + 1) * CMB_TM)
        for k in range(TOP_K):
            pltpu.make_async_copy(ys_ref.at[pl.ds(0, CMB_TM)], buf.at[s, k], sems.at[s]).wait()
        y = topw_ref[rows, 0:1] * buf[s, 0]
        for k in range(1, TOP_K):
            y = y + topw_ref[rows, k:k + 1] * buf[s, k]
        x = x1_ref[rows, :] + mod_ref[0, 5:6, :] * y
        if not final:
            rest[0][rows, :] = x
            return
        fg_ref, o_ctx_ref, o_lat_ref = rest[:3]
        out = _rms(x) * fg_ref[...]

        @pl.when(is_ctx)
        def _():
            o_ctx_ref[rows, :] = out

        @pl.when(jnp.logical_not(is_ctx))
        def _():
            o_lat_ref[rows, :] = out

    @pl.when(j == 0)
    def _():
        issue_tile(pos_ref, 0, 0)

    issue_tile(pos_ref, CMB_TM * TOP_K, 1)
    finish_tile(0)

    @pl.when(j + 1 < pl.num_programs(0))
    def _():
        issue_tile(pos_next_ref, 0, 0)

    finish_tile(1)


def _combine(pos, ys, x1, topw, mod_l, final_g=None):
    tm, d = CMB_STEP, D_MODEL
    n_steps = N_TOK // tm
    n_tiles = N_TOK // CMB_TM
    n_ctx = N_CTX_TOK // tm
    final = final_g is not None
    in_specs = [
        pl.BlockSpec((1, 1, tm * TOP_K), lambda i: (i, 0, 0), memory_space=pltpu.SMEM),
        pl.BlockSpec((1, 1, CMB_TM * TOP_K), lambda i: (jnp.minimum(2 * i + 2, n_tiles - 1), 0, 0),
                     memory_space=pltpu.SMEM),
        pl.BlockSpec(memory_space=pl.ANY),
        pl.BlockSpec((tm, d), lambda i: (i, 0)),
        pl.BlockSpec((tm, TOP_K), lambda i: (i, 0)),
        pl.BlockSpec((1, 6, d), lambda i: (_mod_row(i, tm), 0, 0)),
    ]
    args = [pos.reshape(n_steps, 1, tm * TOP_K), pos.reshape(n_tiles, 1, CMB_TM * TOP_K), ys, x1, topw, mod_l]
    if final:
        in_specs.append(pl.BlockSpec((1, d), lambda i: (0, 0)))
        args.append(final_g.reshape(1, d))
        out_specs = [pl.BlockSpec((tm, d), lambda i: (jnp.minimum(i, n_ctx - 1), 0)),
                     pl.BlockSpec((tm, d), lambda i: (jnp.maximum(i - n_ctx, 0), 0))]
        out_shape = [jax.ShapeDtypeStruct((N_CTX_TOK, d), F32), jax.ShapeDtypeStruct((N_LAT_TOK, d), F32)]
    else:
        out_specs = pl.BlockSpec((tm, d), lambda i: (i, 0))
        out_shape = jax.ShapeDtypeStruct((N_TOK, d), F32)
    return pl.pallas_call(
        functools.partial(_combine_kernel, final),
        grid=(n_steps,),
        in_specs=in_specs,
        out_specs=out_specs,
        out_shape=out_shape,
        scratch_shapes=[pltpu.VMEM((2, TOP_K, CMB_TM, d), F32), pltpu.SemaphoreType.DMA((2,))],
        compiler_params=_cparams(("arbitrary",), 40),
        name="combine_final" if final else "combine",
    )(*args)


def kernel(x_prompt, x_sample, cache_k, cache_v, state_ret_fwd, state_ret_bwd, c, c_ctx, norm1_g, norm2_g, w_mod, b_mod, w_in, lam_a, subln_a, ret_decay, ret_norm, w_branch_a, w_branch_b, w_branch_c, w_out, router_w, router_b, w_gate_up, b_gate_up, w_down, b_down, final_g):
    d = D_MODEL
    x = jnp.concatenate([x_prompt.reshape(N_CTX_TOK, d), x_sample.reshape(N_LAT_TOK, d)], axis=0)
    cond8 = jnp.concatenate([c_ctx[None, :], c, jnp.zeros((3, d), F32)], axis=0)
    mod = _modulation(cond8, w_mod, b_mod)

    cache_k4 = cache_k.reshape(DEC_BATCH, DEPTH, PAST_LEN, W_A)
    cache_v4 = cache_v.reshape(DEC_BATCH, DEPTH, PAST_LEN, W_A)
    s0f4 = state_ret_fwd.reshape(DEC_BATCH, DEPTH, H_C * DK_C, DV_C)
    s0b4 = state_ret_bwd.reshape(DEC_BATCH, DEPTH, H_C * DK_C, DV_C)
    b_gate_up4 = b_gate_up.reshape(DEPTH, N_EXPERTS, 1, 2 * D_FF)
    b_down4 = b_down.reshape(DEPTH, N_EXPERTS, 1, d)
    cos, sin = _rope_tables(DEC_SEQ)
    chan, pos_ctx = _dft_consts(SEQ)
    _, pos_lat = _dft_consts(DEC_SEQ)

    caches, states = None, None
    for l in range(DEPTH):
        lam_init = 0.8 - 0.6 * math.exp(-0.3 * l)
        z, new_k, new_v = _inproj(x, mod[l], norm1_g[l], w_in, l, caches)
        caches = (new_k, new_v)
        ya = _attn_ctx(z, lam_a[l], subln_a[l], lam_init)
        ya = _attn_lat(z, cache_k4, cache_v4, l, cos, sin, lam_a[l], subln_a[l], lam_init, ya)
        yb = _fourier(z, SEQ, BATCH, 0, chan, pos_ctx)
        yb = _fourier(z, DEC_SEQ, DEC_BATCH, N_CTX_TOK // DEC_SEQ, chan, pos_lat, yb)
        yc, new_sf, new_sb = _ret_ctx(z, ret_decay[l], ret_norm[l], l, states)
        states = (new_sf, new_sb)
        yc = _ret_lat(z, ret_decay[l], ret_norm[l], s0f4, s0b4, l, yc)
        x1, h2, topi, topw, rank, cnt_row, cnt_col = _merge(
            x, ya, yb, yc, z, mod[l], w_branch_a, w_branch_b, w_branch_c, w_out,
            norm2_g[l], router_w[l], router_b[l], l)
        pos, tile_expert, n_tiles, last, next_expert = _plan(topi, rank, cnt_row, cnt_col)
        xs = _dispatch(pos, last, h2)
        ys = _moe_grouped(tile_expert.reshape(PLAN_LANES), n_tiles[0, :1], next_expert.reshape(N_EXPERTS),
                          xs, w_gate_up, b_gate_up4, w_down, b_down4, l)
        if l < DEPTH - 1:
            x = _combine(pos, ys, x1, topw, mod[l])
        else:
            y_ctx, y_lat = _combine(pos, ys, x1, topw, mod[l], final_g)

    return (y_ctx.reshape(BATCH, SEQ, d), y_lat.reshape(DEC_BATCH, DEC_SEQ, d),
            caches[0].reshape(BATCH, DEPTH, SEQ, H_A, 2, DH_A),
            caches[1].reshape(BATCH, DEPTH, SEQ, H_A, DV_A),
            states[0], states[1])
```

```python
import functools
import math

import numpy as np
import jax
import jax.numpy as jnp
from jax import lax
from jax.experimental import pallas as pl
from jax.experimental.pallas import tpu as pltpu

F32 = jnp.float32
BF16 = jnp.bfloat16

D_MODEL = 1024
BATCH = 16
SEQ = 256
DEPTH = 2
DEC_BATCH = 4
DEC_SEQ = 1024
PAST_LEN = 512
GRID_W = 64
H_A = 4
DH_A = 64
DV_A = 128
G_B = 4
DG_B = 128
H_C = 4
DK_C = 64
DV_C = 128
N_EXPERTS = 32
TOP_K = 4
D_FF = 1024
SWIGLU_LIMIT = 7.0
SWIGLU_ALPHA = 1.702
ROPE_BASE = 10000.0
EPS = 1e-6

N_CTX_TOK = BATCH * SEQ
N_LAT_TOK = DEC_BATCH * DEC_SEQ
N_TOK = N_CTX_TOK + N_LAT_TOK
IN_TOTAL = 6656
Z_QA = 0
Z_KA = 512
Z_VA = 1024
Z_FB = 1536
Z_QC = 2048
Z_KC = 2304
Z_VC = 2560
Z_GC = 3072
Z_GATES = 3584
LANES = 128
MIB = 1024 * 1024


def _cparams(semantics, vmem_mib):
    return pltpu.CompilerParams(dimension_semantics=semantics,
                                vmem_limit_bytes=vmem_mib * MIB)


def _dot(a, b):
    return jnp.dot(a, b, preferred_element_type=F32)


def _dot_nt(a, b):
    return lax.dot_general(a, b, (((1,), (1,)), ((), ())), preferred_element_type=F32)


def _dot_tn(a, b):
    return lax.dot_general(a, b, (((0,), (0,)), ((), ())), preferred_element_type=F32)


def _split_bf16(a):
    hi = a.astype(BF16)
    lo = (a - hi.astype(F32)).astype(BF16)
    return hi, lo


def _dot_f32x3(a, b):
    a_hi, a_lo = _split_bf16(a)
    b_hi, b_lo = _split_bf16(b)
    return _dot(a_hi, b_hi) + _dot(a_lo, b_hi) + _dot(a_hi, b_lo)


def _rms(x):
    return x * lax.rsqrt(jnp.mean(x * x, axis=-1, keepdims=True) + EPS)


def _modnorm(x, g, scale, shift):
    return (_rms(x) * g) * (1.0 + scale) + shift


def _sigmoid(x):
    return 1.0 / (1.0 + jnp.exp(-x))


MOD_TN = 1536


def _mod_kernel(cond_ref, w_ref, b_ref, o_ref):
    cnd = cond_ref[...]
    s = cnd * _sigmoid(cnd)
    o_ref[0] = _dot_f32x3(s, w_ref[0]) + b_ref[0]


def _modulation(cond8, w_mod, b_mod):
    n = 6 * D_MODEL
    out = pl.pallas_call(
        _mod_kernel,
        grid=(DEPTH, n // MOD_TN),
        in_specs=[
            pl.BlockSpec((8, D_MODEL), lambda l, j: (0, 0)),
            pl.BlockSpec((1, D_MODEL, MOD_TN), lambda l, j: (l, 0, j)),
            pl.BlockSpec((1, 1, MOD_TN), lambda l, j: (l, 0, j)),
        ],
        out_specs=pl.BlockSpec((1, 8, MOD_TN), lambda l, j: (l, 0, j)),
        out_shape=jax.ShapeDtypeStruct((DEPTH, 8, n), F32),
        compiler_params=_cparams(("parallel", "parallel"), 40),
        name="modulation",
    )(cond8, w_mod, b_mod.reshape(DEPTH, 1, n))
    return out.reshape(DEPTH, 8, 6, D_MODEL)


def _mod_row(tile_idx, tm):
    n_ctx_tiles = N_CTX_TOK // tm
    per_batch = DEC_SEQ // tm
    return jnp.where(tile_idx < n_ctx_tiles, 0, 1 + (tile_idx - n_ctx_tiles) // per_batch)


INP_TM = 1024
INP_TN = 1664
INP_NJ = IN_TOTAL // INP_TN
CTX_PER_TILE = INP_TM // SEQ
assert Z_VA + H_A * DV_A <= INP_TN


def _token_rows(x_refs, is_ctx, rows=slice(None)):
    if len(x_refs) == 1:
        return x_refs[0][rows, :]
    return jnp.where(is_ctx, x_refs[0][rows, :], x_refs[1][rows, :])


def _token_specs(x, tm):
    n_ctx_tiles = N_CTX_TOK // tm
    first = lambda idx: idx[0]
    if not isinstance(x, tuple):
        return [pl.BlockSpec((tm, D_MODEL), lambda *idx: (first(idx), 0))], [x]
    return ([pl.BlockSpec((tm, D_MODEL), lambda *idx: (jnp.minimum(first(idx), n_ctx_tiles - 1), 0)),
             pl.BlockSpec((tm, D_MODEL), lambda *idx: (jnp.maximum(first(idx) - n_ctx_tiles, 0), 0))],
            list(x))


def _inproj_kernel(n_x, *refs):
    x_refs = refs[:n_x]
    mod_ref, g_ref, w_ref = refs[n_x:n_x + 3]
    z_ref, kc_ref, vc_ref, h_scr = refs[-4:]
    i = pl.program_id(0)
    j = pl.program_id(1)

    @pl.when(j == 0)
    def _():
        x = _token_rows(x_refs, i < N_CTX_TOK // INP_TM)
        h = _modnorm(x, g_ref[...], mod_ref[0, 1:2, :], mod_ref[0, 0:1, :])
        h_scr[...] = h.astype(BF16)

    z_ref[...] = _dot(h_scr[...], w_ref[0].astype(BF16)).astype(BF16)

    @pl.when(jnp.logical_and(i < N_CTX_TOK // INP_TM, j == 0))
    def _():
        wk = w_ref[0, :, Z_KA:Z_KA + W_A].astype(BF16)
        wv = w_ref[0, :, Z_VA:Z_VA + W_A].astype(BF16)
        for b in range(CTX_PER_TILE):
            hb = h_scr[b * SEQ:(b + 1) * SEQ, :]
            kc_ref[b, 0] = _dot(hb, wk)
            vc_ref[b, 0] = _dot(hb, wv)


def _inproj(x, mod_l, g, w_in, layer, prev_caches):
    ni = N_TOK // INP_TM
    n_ctx_tiles = N_CTX_TOK // INP_TM
    cache_shape = jax.ShapeDtypeStruct((BATCH, DEPTH, SEQ, W_A), F32)
    cache_spec = pl.BlockSpec((CTX_PER_TILE, 1, SEQ, W_A),
                              lambda i, j: (jnp.minimum(i, n_ctx_tiles - 1), layer, 0, 0))
    x_specs, x_args = _token_specs(x, INP_TM)
    in_specs = x_specs + [
        pl.BlockSpec((1, 6, D_MODEL), lambda i, j: (_mod_row(i, INP_TM), 0, 0)),
        pl.BlockSpec((1, D_MODEL), lambda i, j: (0, 0)),
        pl.BlockSpec((1, D_MODEL, INP_TN), lambda i, j: (layer, 0, j)),
    ]
    args = x_args + [mod_l, g.reshape(1, D_MODEL), w_in]
    aliases = {}
    if prev_caches is not None:
        aliases = {len(args): 1, len(args) + 1: 2}
        in_specs += [pl.BlockSpec(memory_space=pl.ANY)] * 2
        args += list(prev_caches)
    return pl.pallas_call(
        functools.partial(_inproj_kernel, len(x_args)),
        grid=(ni, INP_NJ),
        in_specs=in_specs,
        out_specs=[pl.BlockSpec((INP_TM, INP_TN), lambda i, j: (i, j)), cache_spec, cache_spec],
        out_shape=[jax.ShapeDtypeStruct((N_TOK, IN_TOTAL), BF16), cache_shape, cache_shape],
        scratch_shapes=[pltpu.VMEM((INP_TM, D_MODEL), BF16)],
        input_output_aliases=aliases,
        compiler_params=_cparams(("arbitrary", "arbitrary"), 56),
        name="inproj",
    )(*args)


ATT_TQ = 256
W_A = H_A * DV_A


def _lambda(lam_ref, lam_init):
    la = lam_ref[...]
    l01 = jnp.sum(la[0:1, :] * la[1:2, :], axis=1, keepdims=True)
    l23 = jnp.sum(la[2:3, :] * la[3:4, :], axis=1, keepdims=True)
    return jnp.exp(l01) - jnp.exp(l23) + lam_init


def _exp_shifted(s):
    return jnp.exp(s - jnp.max(s, axis=-1, keepdims=True)).astype(BF16)


def _diff_attn_block(q, k_ref, v_ref, lam, subln, lam_init, ones_beside_v):
    lane = lax.broadcasted_iota(jnp.int32, (1, LANES), 1)
    first = lane < DH_A
    scale = jnp.asarray(DH_A ** -0.5, BF16)
    outs = []
    for h in range(H_A):
        sl = slice(h * LANES, (h + 1) * LANES)
        qh = q[:, sl]
        kh = k_ref[:, sl]
        zero = jnp.zeros_like(qh)
        qs = qh * scale
        s0 = _dot_nt(jnp.where(first, qs, zero), kh)
        s1 = _dot_nt(jnp.where(first, zero, qs), kh)
        if ones_beside_v:
            vx = v_ref[:, 2 * h * LANES:2 * (h + 1) * LANES]
            pv0 = _dot(_exp_shifted(s0), vx)
            pv1 = _dot(_exp_shifted(s1), vx)
            r0 = 1.0 / pv0[:, LANES:LANES + 1]
            r1 = 1.0 / pv1[:, LANES:LANES + 1]
            o = pv0[:, :LANES] * r0 - (lam * r1) * pv1[:, :LANES]
        else:
            e0 = jnp.exp(s0 - jnp.max(s0, axis=-1, keepdims=True))
            e1 = jnp.exp(s1 - jnp.max(s1, axis=-1, keepdims=True))
            r0 = 1.0 / jnp.sum(e0, axis=-1, keepdims=True)
            r1 = 1.0 / jnp.sum(e1, axis=-1, keepdims=True)
            o = _dot((e0 * r0 - (lam * r1) * e1).astype(BF16), v_ref[:, sl])
        outs.append((_rms(o) * subln) * (1.0 - lam_init))
    return outs


def _store_values_with_ones(vx_ref, rows, v):
    ones = jnp.ones((v.shape[0], LANES), BF16)
    for h in range(H_A):
        vx_ref[rows, 2 * h * LANES:(2 * h + 1) * LANES] = v[:, h * LANES:(h + 1) * LANES]
        vx_ref[rows, (2 * h + 1) * LANES:2 * (h + 1) * LANES] = ones


def _rope(x, cos, sin):
    lane = lax.broadcasted_iota(jnp.int32, (1, x.shape[1]), 1)
    first = (lane % 32) < 16
    n = x.shape[1]
    xr = jnp.where(first, -pltpu.roll(x, n - 16, 1), pltpu.roll(x, 16, 1))
    return x * cos + xr * sin


def _attn_ctx_kernel(lam_init, q_ref, k_ref, v_ref, lam_ref, subln_ref, o_ref):
    lam = _lambda(lam_ref, lam_init)
    outs = _diff_attn_block(q_ref[...], k_ref, v_ref, lam, subln_ref[...], lam_init, False)
    for h in range(H_A):
        o_ref[:, h * LANES:(h + 1) * LANES] = outs[h].astype(BF16)


def _attn_ctx(z, lam_a_l, subln_l, lam_init):
    wb = W_A
    return pl.pallas_call(
        functools.partial(_attn_ctx_kernel, lam_init),
        grid=(BATCH,),
        in_specs=[
            pl.BlockSpec((SEQ, wb), lambda b: (b, Z_QA // wb)),
            pl.BlockSpec((SEQ, wb), lambda b: (b, Z_KA // wb)),
            pl.BlockSpec((SEQ, wb), lambda b: (b, Z_VA // wb)),
            pl.BlockSpec((4, DH_A), lambda b: (0, 0)),
            pl.BlockSpec((1, DV_A), lambda b: (0, 0)),
        ],
        out_specs=pl.BlockSpec((SEQ, wb), lambda b: (b, 0)),
        out_shape=jax.ShapeDtypeStruct((N_TOK, wb), BF16),
        compiler_params=_cparams(("parallel",), 40),
        name="attn_ctx",
    )(z, z, z, lam_a_l, subln_l.reshape(1, DV_A))


def _attn_lat_kernel(lam_init, q_ref, k_ref, v_ref, ck_ref, cv_ref, cos_ref, sin_ref,
                     lam_ref, subln_ref, ya_ctx_ref, o_ref, kall, vall):
    del ya_ctx_ref
    lam = _lambda(lam_ref, lam_init)
    subln = subln_ref[...]
    kall[0:DEC_SEQ, :] = _rope(k_ref[...].astype(F32), cos_ref[...], sin_ref[...]).astype(BF16)
    kall[DEC_SEQ:, :] = ck_ref[0, 0].astype(BF16)
    _store_values_with_ones(vall, slice(0, DEC_SEQ), v_ref[...])
    _store_values_with_ones(vall, slice(DEC_SEQ, DEC_SEQ + PAST_LEN), cv_ref[0, 0].astype(BF16))

    def q_block(qb, carry):
        rows = pl.ds(pl.multiple_of(qb * ATT_TQ, ATT_TQ), ATT_TQ)
        q = _rope(q_ref[rows, :].astype(F32), cos_ref[rows, :], sin_ref[rows, :]).astype(BF16)
        outs = _diff_attn_block(q, kall, vall, lam, subln, lam_init, True)
        for h in range(H_A):
            o_ref[rows, h * LANES:(h + 1) * LANES] = outs[h].astype(BF16)
        return carry

    lax.fori_loop(0, DEC_SEQ // ATT_TQ, q_block, 0)


def _attn_lat(z, cache_k4, cache_v4, layer, cos, sin, lam_a_l, subln_l, lam_init, ya_ctx):
    wb = W_A
    rb0 = N_CTX_TOK // DEC_SEQ
    t_all = DEC_SEQ + PAST_LEN
    return pl.pallas_call(
        functools.partial(_attn_lat_kernel, lam_init),
        grid=(DEC_BATCH,),
        in_specs=[
            pl.BlockSpec((DEC_SEQ, wb), lambda b: (rb0 + b, Z_QA // wb)),
            pl.BlockSpec((DEC_SEQ, wb), lambda b: (rb0 + b, Z_KA // wb)),
            pl.BlockSpec((DEC_SEQ, wb), lambda b: (rb0 + b, Z_VA // wb)),
            pl.BlockSpec((1, 1, PAST_LEN, wb), lambda b: (b, layer, 0, 0)),
            pl.BlockSpec((1, 1, PAST_LEN, wb), lambda b: (b, layer, 0, 0)),
            pl.BlockSpec((DEC_SEQ, wb), lambda b: (0, 0)),
            pl.BlockSpec((DEC_SEQ, wb), lambda b: (0, 0)),
            pl.BlockSpec((4, DH_A), lambda b: (0, 0)),
            pl.BlockSpec((1, DV_A), lambda b: (0, 0)),
            pl.BlockSpec(memory_space=pl.ANY),
        ],
        out_specs=pl.BlockSpec((DEC_SEQ, wb), lambda b: (rb0 + b, 0)),
        out_shape=jax.ShapeDtypeStruct((N_TOK, wb), BF16),
        scratch_shapes=[pltpu.VMEM((t_all, wb), BF16), pltpu.VMEM((t_all, 2 * wb), BF16)],
        input_output_aliases={9: 0},
        compiler_params=_cparams(("parallel",), 48),
        name="attn_lat",
    )(z, z, z, cache_k4, cache_v4, cos, sin, lam_a_l, subln_l.reshape(1, DV_A), ya_ctx)


def _rope_tables(s):
    rows = s // GRID_W
    row = jnp.repeat(jnp.arange(rows, dtype=F32), GRID_W)
    col = jnp.tile(jnp.arange(GRID_W, dtype=F32), rows)
    quarter = DH_A // 4
    inv = ROPE_BASE ** (-jnp.arange(quarter, dtype=F32) / quarter)
    ar = row[:, None] * inv
    ac = col[:, None] * inv
    ang = jnp.concatenate([ar, ar, ac, ac], axis=-1)
    reps = (H_A * 2 * DH_A) // DH_A
    return jnp.tile(jnp.cos(ang), (1, reps)), jnp.tile(jnp.sin(ang), (1, reps))


W_B = G_B * DG_B


def _dft_consts(s):
    def cs(n, scale):
        j = np.arange(n, dtype=np.int64)
        ang = 2.0 * np.pi * ((j[:, None] * j[None, :]) % n).astype(np.float64) / n
        return np.cos(ang) * scale, np.sin(ang) * scale
    cc, sc = cs(DG_B, DG_B ** -0.5)
    chan = np.concatenate([cc, sc], axis=1)
    cp, sp = cs(s, s ** -0.5)
    pos = np.concatenate([cp, -sp], axis=1)
    return jnp.asarray(chan, dtype=F32).astype(BF16), jnp.asarray(pos, dtype=F32).astype(BF16)


def _fourier_kernel(s, f_ref, chan_ref, pos_ref, *rest):
    o_ref, p_ref = rest[-2:]
    chan = chan_ref[...]
    for g in range(G_B):
        sl = slice(g * DG_B, (g + 1) * DG_B)
        a = _dot(f_ref[:, sl], chan)
        p_ref[0:s, sl] = a[:, :DG_B].astype(BF16)
        p_ref[s:2 * s, sl] = a[:, DG_B:].astype(BF16)
    o_ref[...] = _dot(pos_ref[...], p_ref[...]).astype(BF16)


def _fourier(z, s, n_batch, row_block0, chan, pos, prev=None):
    in_specs = [
        pl.BlockSpec((s, W_B), lambda b: (row_block0 + b, Z_FB // W_B)),
        pl.BlockSpec((DG_B, 2 * DG_B), lambda b: (0, 0)),
        pl.BlockSpec((s, 2 * s), lambda b: (0, 0)),
    ]
    args = [z, chan, pos]
    aliases = {}
    if prev is not None:
        in_specs.append(pl.BlockSpec(memory_space=pl.ANY))
        args.append(prev)
        aliases = {3: 0}
    return pl.pallas_call(
        functools.partial(_fourier_kernel, s),
        grid=(n_batch,),
        in_specs=in_specs,
        out_specs=pl.BlockSpec((s, W_B), lambda b: (row_block0 + b, 0)),
        out_shape=jax.ShapeDtypeStruct((N_TOK, W_B), BF16),
        scratch_shapes=[pltpu.VMEM((2 * s, W_B), BF16)],
        input_output_aliases=aliases,
        compiler_params=_cparams(("parallel",), 40),
        name="fourier_%d" % s,
    )(*args)


W_C = H_C * DV_C
RET_TQ = 256


def _log_sigmoid(x):
    return jnp.minimum(x, 0.0) - jnp.log1p(jnp.exp(-jnp.abs(x)))


def _head_pair_lanes(vals, pair):
    lane = lax.broadcasted_iota(jnp.int32, (1, LANES), 1)
    return jnp.where(lane < DK_C, vals[:, 2 * pair:2 * pair + 1], vals[:, 2 * pair + 1:2 * pair + 2])


def _fill_decay(dec_ref, lg, s, tq):
    for r0 in range(0, s, tq):
        ii = (r0 + lax.broadcasted_iota(jnp.int32, (tq, s), 0)).astype(F32)
        jj = lax.broadcasted_iota(jnp.int32, (tq, s), 1).astype(F32)
        rel = ii - jj
        for h in range(H_C):
            lgf = lg[0:1, h:h + 1]
            lgb = lg[1:2, h:h + 1]
            dec_ref[h, r0:r0 + tq, :] = (
                jnp.where(rel >= 0, jnp.exp(lgf * jnp.maximum(rel, 0.0)), 0.0)
                + jnp.where(rel <= 0, jnp.exp(lgb * jnp.maximum(-rel, 0.0)), 0.0))


def _retention_rows(s, row0, tq, q, k_ref, v_ref, g, lg, rnorm, s0f_ref, s0b_ref, dec_ref):
    lane = lax.broadcasted_iota(jnp.int32, (1, LANES), 1)
    pos = (row0 + lax.broadcasted_iota(jnp.int32, (tq, 1), 0)).astype(F32)
    outs = []
    for h in range(H_C):
        pair, half = h // 2, h % 2
        sl = slice(pair * LANES, (pair + 1) * LANES)
        lgf = lg[0:1, h:h + 1]
        lgb = lg[1:2, h:h + 1]
        in_head = (lane < DK_C) if half == 0 else (lane >= DK_C)
        q_pair = q[:, sl]
        qh = jnp.where(in_head, q_pair, jnp.zeros_like(q_pair))
        sc = _dot_nt(qh, k_ref[:, sl]) * (DK_C ** -0.5)
        decay = dec_ref[h, pl.ds(row0, tq), :]
        o = _dot((sc * decay).astype(BF16), v_ref[:, h * DV_C:(h + 1) * DV_C])
        if s0f_ref is not None:
            qf = qh.astype(F32)
            xf = jnp.exp(lgf * (pos + 1.0))
            xb = jnp.exp(lgb * (s - pos))
            o = o + _dot((qf * xf).astype(BF16), s0f_ref[0, 0, sl, :].astype(BF16))
            o = o + _dot((qf * xb).astype(BF16), s0b_ref[0, 0, sl, :].astype(BF16))
        gh = g[:, h * DV_C:(h + 1) * DV_C].astype(F32)
        outs.append((_rms(o) * rnorm) * (gh * _sigmoid(gh)))
    return outs


def _ret_ctx_kernel(q_ref, k_ref, v_ref, g_ref, dec_ref, rn_ref, *rest):
    o_ref, sf_ref, sb_ref, decay = rest[-4:]
    s = SEQ
    lg = _log_sigmoid(dec_ref[...])

    @pl.when(pl.program_id(0) == 0)
    def _():
        _fill_decay(decay, lg, s, s)

    outs = _retention_rows(s, 0, s, q_ref[...], k_ref, v_ref, g_ref[...], lg, rn_ref[...], None, None,
                           decay)
    for h in range(H_C):
        o_ref[:, h * DV_C:(h + 1) * DV_C] = outs[h].astype(BF16)
    pos = lax.broadcasted_iota(jnp.int32, (s, 1), 0).astype(F32)
    v = v_ref[...]
    for pair in range(H_C // 2):
        sl = slice(pair * LANES, (pair + 1) * LANES)
        kf = k_ref[:, sl].astype(F32) * (DK_C ** -0.5)
        zf = jnp.exp(_head_pair_lanes(lg[0:1, :], pair) * (s - 1.0 - pos))
        zb = jnp.exp(_head_pair_lanes(lg[1:2, :], pair) * pos)
        stf = _dot_tn((kf * zf).astype(BF16), v)
        stb = _dot_tn((kf * zb).astype(BF16), v)
        for half in range(2):
            h = 2 * pair + half
            rs = slice(half * DK_C, (half + 1) * DK_C)
            cs = slice(h * DV_C, (h + 1) * DV_C)
            sf_ref[0, 0, h] = stf[rs, cs]
            sb_ref[0, 0, h] = stb[rs, cs]


def _ret_ctx(z, dec_l, rnorm_l, layer, prev_states):
    wq = H_C * DK_C
    st_shape = jax.ShapeDtypeStruct((BATCH, DEPTH, H_C, DK_C, DV_C), F32)
    st_spec = pl.BlockSpec((1, 1, H_C, DK_C, DV_C), lambda b: (b, layer, 0, 0, 0))
    in_specs = [
        pl.BlockSpec((SEQ, wq), lambda b: (b, Z_QC // wq)),
        pl.BlockSpec((SEQ, wq), lambda b: (b, Z_KC // wq)),
        pl.BlockSpec((SEQ, W_C), lambda b: (b, Z_VC // W_C)),
        pl.BlockSpec((SEQ, W_C), lambda b: (b, Z_GC // W_C)),
        pl.BlockSpec((2, H_C), lambda b: (0, 0)),
        pl.BlockSpec((1, DV_C), lambda b: (0, 0)),
    ]
    args = [z, z, z, z, dec_l, rnorm_l.reshape(1, DV_C)]
    aliases = {}
    if prev_states is not None:
        in_specs += [pl.BlockSpec(memory_space=pl.ANY)] * 2
        args += list(prev_states)
        aliases = {6: 1, 7: 2}
    return pl.pallas_call(
        _ret_ctx_kernel,
        grid=(BATCH,),
        in_specs=in_specs,
        out_specs=[pl.BlockSpec((SEQ, W_C), lambda b: (b, 0)), st_spec, st_spec],
        out_shape=[jax.ShapeDtypeStruct((N_TOK, W_C), BF16), st_shape, st_shape],
        scratch_shapes=[pltpu.VMEM((H_C, SEQ, SEQ), F32)],
        input_output_aliases=aliases,
        compiler_params=_cparams(("arbitrary",), 40),
        name="ret_ctx",
    )(*args)


def _ret_lat_kernel(q_ref, k_ref, v_ref, g_ref, dec_ref, rn_ref, s0f_ref, s0b_ref, yc_ctx_ref, o_ref,
                    decay):
    del yc_ctx_ref
    lg = _log_sigmoid(dec_ref[...])
    rnorm = rn_ref[...]

    @pl.when(pl.program_id(0) == 0)
    def _():
        _fill_decay(decay, lg, DEC_SEQ, RET_TQ)

    def q_block(qb, carry):
        row0 = pl.multiple_of(qb * RET_TQ, RET_TQ)
        rows = pl.ds(row0, RET_TQ)
        outs = _retention_rows(DEC_SEQ, row0, RET_TQ, q_ref[rows, :], k_ref, v_ref, g_ref[rows, :],
                               lg, rnorm, s0f_ref, s0b_ref, decay)
        for h in range(H_C):
            o_ref[rows, h * DV_C:(h + 1) * DV_C] = outs[h].astype(BF16)
        return carry

    lax.fori_loop(0, DEC_SEQ // RET_TQ, q_block, 0)


def _ret_lat(z, dec_l, rnorm_l, s0f4, s0b4, layer, yc_ctx):
    wq = H_C * DK_C
    rb0 = N_CTX_TOK // DEC_SEQ
    st_spec = pl.BlockSpec((1, 1, H_C * DK_C, DV_C), lambda b: (b, layer, 0, 0))
    return pl.pallas_call(
        _ret_lat_kernel,
        grid=(DEC_BATCH,),
        in_specs=[
            pl.BlockSpec((DEC_SEQ, wq), lambda b: (rb0 + b, Z_QC // wq)),
            pl.BlockSpec((DEC_SEQ, wq), lambda b: (rb0 + b, Z_KC // wq)),
            pl.BlockSpec((DEC_SEQ, W_C), lambda b: (rb0 + b, Z_VC // W_C)),
            pl.BlockSpec((DEC_SEQ, W_C), lambda b: (rb0 + b, Z_GC // W_C)),
            pl.BlockSpec((2, H_C), lambda b: (0, 0)),
            pl.BlockSpec((1, DV_C), lambda b: (0, 0)),
            st_spec, st_spec,
            pl.BlockSpec(memory_space=pl.ANY),
        ],
        out_specs=pl.BlockSpec((DEC_SEQ, W_C), lambda b: (rb0 + b, 0)),
        out_shape=jax.ShapeDtypeStruct((N_TOK, W_C), BF16),
        scratch_shapes=[pltpu.VMEM((H_C, DEC_SEQ, DEC_SEQ), F32)],
        input_output_aliases={8: 0},
        compiler_params=_cparams(("arbitrary",), 48),
        name="ret_lat",
    )(z, z, z, z, dec_l, rnorm_l.reshape(1, DV_C), s0f4, s0b4, yc_ctx)


MRG_TM = 512
MRG_SUBBLOCKS = 2
GATE_BLOCK = 512


def _merge_kernel(n_x, *refs):
    x_refs = refs[:n_x]
    (ya_ref, yb_ref, yc_ref, g0_ref, g1_ref, g2_ref, g3_ref, g4_ref, g5_ref,
     mod_ref, wa_ref, wb_ref, wc_ref, wo_ref,
     n2_ref, rw_ref, rb_ref, x1_ref, h2_ref, topi_ref, topw_ref, rank_ref,
     cnt_row_ref, cnt_col_ref, carry_row, carry_col, wa_bf, wb_bf, wc_bf, wo_bf) = refs[n_x:]
    tm = x1_ref.shape[0]
    is_ctx = pl.program_id(0) < N_CTX_TOK // tm

    @pl.when(pl.program_id(0) == 0)
    def _():
        carry_row[...] = jnp.zeros_like(carry_row)
        carry_col[...] = jnp.zeros_like(carry_col)
        wa_bf[...] = wa_ref[0].astype(BF16)
        wb_bf[...] = wb_ref[0].astype(BF16)
        wc_bf[...] = wc_ref[0].astype(BF16)
        wo_bf[...] = wo_ref[0].astype(BF16)

    def gate(lo_ref, hi_ref, rs):
        return _sigmoid(jnp.concatenate([lo_ref[rs, :], hi_ref[rs, :]], axis=1).astype(F32))

    sub = tm // MRG_SUBBLOCKS
    for sb in range(MRG_SUBBLOCKS):
        rs = slice(sb * sub, (sb + 1) * sub)
        merged = gate(g0_ref, g1_ref, rs) * _dot(ya_ref[rs, :], wa_bf[...])
        merged += gate(g2_ref, g3_ref, rs) * _dot(yb_ref[rs, :], wb_bf[...])
        merged += gate(g4_ref, g5_ref, rs) * _dot(yc_ref[rs, :], wc_bf[...])
        y = _dot(merged.astype(BF16), wo_bf[...])
        x1 = _token_rows(x_refs, is_ctx, rs) + mod_ref[0, 2:3, :] * y
        x1_ref[rs, :] = x1
        h2 = _modnorm(x1, n2_ref[...], mod_ref[0, 4:5, :], mod_ref[0, 3:4, :])
        h2_ref[rs, :] = h2
        logits = _dot_f32x3(h2, rw_ref[...]) + rb_ref[...]
        ids = lax.broadcasted_iota(jnp.int32, logits.shape, 1)
        work = logits
        sels, vals, firsts = [], [], []
        for _ in range(TOP_K):
            m = jnp.max(work, axis=-1, keepdims=True)
            first = jnp.min(jnp.where(work == m, ids, N_EXPERTS), axis=-1, keepdims=True)
            sel = ids == first
            sels.append(sel)
            vals.append(m)
            firsts.append(first)
            work = jnp.where(sel, -jnp.inf, work)
        es = [jnp.exp(v - vals[0]) for v in vals]
        inv = 1.0 / (es[0] + es[1] + es[2] + es[3])
        chosen = jnp.zeros_like(logits)
        for sel in sels:
            chosen = chosen + jnp.where(sel, 1.0, 0.0)
        chosen_bf = chosen.astype(BF16)
        rr = lax.broadcasted_iota(jnp.int32, (sub, sub), 0)
        cc = lax.broadcasted_iota(jnp.int32, (sub, sub), 1)
        before = jnp.where(rr > cc, 1.0, 0.0).astype(BF16)
        prefix = _dot(before, chosen_bf) + carry_row[...]
        ranks = [jnp.sum(jnp.where(sel, prefix, 0.0), axis=-1, keepdims=True) for sel in sels]
        carry_row[...] += jnp.sum(chosen, axis=0, keepdims=True)
        carry_col[...] += _dot_tn(chosen_bf, jnp.ones((sub, LANES), BF16))
        topi_ref[rs, :] = jnp.concatenate(firsts, axis=1)
        topw_ref[rs, :] = jnp.concatenate([e * inv for e in es], axis=1)
        rank_ref[rs, :] = jnp.concatenate(ranks, axis=1).astype(jnp.int32)
    cnt_row_ref[...] = carry_row[...]
    cnt_col_ref[...] = carry_col[...]


def _merge(x, ya, yb, yc, z, mod_l, wa, wb, wc, wo, n2, rw, rb, layer):
    tm = MRG_TM
    d = D_MODEL
    full = lambda shape: pl.BlockSpec(shape, lambda i: tuple(0 for _ in shape))
    tok = lambda w: pl.BlockSpec((tm, w), lambda i: (i, 0))
    per_layer = lambda rows: pl.BlockSpec((1, rows, d), lambda i: (layer, 0, 0))
    x_specs, x_args = _token_specs(x, tm)
    return pl.pallas_call(
        functools.partial(_merge_kernel, len(x_args)),
        grid=(N_TOK // tm,),
        in_specs=x_specs + [
            tok(W_A), tok(W_B), tok(W_C),
            *[pl.BlockSpec((tm, GATE_BLOCK), lambda i, c=c: (i, Z_GATES // GATE_BLOCK + c)) for c in range(6)],
            pl.BlockSpec((1, 6, d), lambda i: (_mod_row(i, tm), 0, 0)),
            per_layer(W_A), per_layer(W_B), per_layer(W_C), per_layer(d),
            full((1, d)), full((d, N_EXPERTS)), full((1, N_EXPERTS)),
        ],
        out_specs=[tok(d), tok(d), tok(TOP_K), tok(TOP_K), tok(TOP_K),
                   full((1, N_EXPERTS)), full((N_EXPERTS, LANES))],
        out_shape=[
            jax.ShapeDtypeStruct((N_TOK, d), F32),
            jax.ShapeDtypeStruct((N_TOK, d), F32),
            jax.ShapeDtypeStruct((N_TOK, TOP_K), jnp.int32),
            jax.ShapeDtypeStruct((N_TOK, TOP_K), F32),
            jax.ShapeDtypeStruct((N_TOK, TOP_K), jnp.int32),
            jax.ShapeDtypeStruct((1, N_EXPERTS), F32),
            jax.ShapeDtypeStruct((N_EXPERTS, LANES), F32),
        ],
        scratch_shapes=[pltpu.VMEM((1, N_EXPERTS), F32), pltpu.VMEM((N_EXPERTS, LANES), F32),
                        pltpu.VMEM((W_A, d), BF16), pltpu.VMEM((W_B, d), BF16),
                        pltpu.VMEM((W_C, d), BF16), pltpu.VMEM((d, d), BF16)],
        compiler_params=_cparams(("arbitrary",), 56),
        name="merge_router",
    )(*x_args, ya, yb, yc, z, z, z, z, z, z, mod_l, wa, wb, wc, wo, n2.reshape(1, d), rw,
      rb.reshape(1, N_EXPERTS))


MOE_TM = 256
MOE_TILES = (N_TOK * TOP_K) // MOE_TM + N_EXPERTS
MOE_ROWS = MOE_TILES * MOE_TM
PLAN_LANES = 256
PLAN_TM = 1024


def _plan_kernel(topi_ref, rank_ref, cnt_row_ref, cnt_col_ref, pos_ref, te_ref, nt_ref, last_ref,
                 next_ref):
    e_r = lax.broadcasted_iota(jnp.int32, (N_EXPERTS, N_EXPERTS), 0)
    e_c = lax.broadcasted_iota(jnp.int32, (N_EXPERTS, N_EXPERTS), 1)
    cnt_row = cnt_row_ref[...]
    tiles_row = jnp.ceil(cnt_row * (1.0 / MOE_TM))
    earlier = jnp.where(e_r < e_c, 1.0, 0.0).astype(BF16)
    start_tile_row = _dot(tiles_row.astype(BF16), earlier)
    off_row = start_tile_row * MOE_TM
    ids = lax.broadcasted_iota(jnp.int32, (PLAN_TM, N_EXPERTS), 1)
    cols = []
    for k in range(TOP_K):
        hit = ids == topi_ref[:, k:k + 1]
        cols.append(jnp.sum(jnp.where(hit, off_row, 0.0), axis=-1, keepdims=True))
    pos_ref[...] = jnp.concatenate(cols, axis=1).astype(jnp.int32) + rank_ref[...]
    tiles_col = jnp.ceil(cnt_col_ref[...] * (1.0 / MOE_TM))
    upto = jnp.where(e_c <= e_r, 1.0, 0.0).astype(BF16)
    end_col = _dot(upto, tiles_col.astype(BF16))
    end_b = jnp.concatenate([end_col] * (PLAN_LANES // LANES), axis=1)
    r = lax.broadcasted_iota(jnp.int32, (N_EXPERTS, PLAN_LANES), 1).astype(F32)
    total = end_b[N_EXPERTS - 1:N_EXPERTS, :]
    te = jnp.sum(jnp.where(end_b <= jnp.minimum(r, total - 1.0), 1.0, 0.0), axis=0, keepdims=True)
    te_ref[...] = te.astype(jnp.int32)
    nt_ref[...] = total[:, :LANES].astype(jnp.int32)
    last = jnp.where(cnt_row > 0.0, (start_tile_row + tiles_row - 1.0) * MOE_TM, -1.0)
    last_ref[...] = last.astype(jnp.int32)
    later_used = jnp.logical_and(e_r > e_c, cnt_col_ref[:, 0:N_EXPERTS] > 0.0)
    nxt = jnp.min(jnp.where(later_used, e_r, N_EXPERTS), axis=0, keepdims=True)
    next_ref[...] = jnp.where(nxt == N_EXPERTS, -1, nxt)


def _plan(topi, rank, cnt_row, cnt_col):
    tok = pl.BlockSpec((PLAN_TM, TOP_K), lambda i: (i, 0))
    full = lambda shape: pl.BlockSpec(shape, lambda i: tuple(0 for _ in shape))
    return pl.pallas_call(
        _plan_kernel,
        grid=(N_TOK // PLAN_TM,),
        in_specs=[tok, tok, full((1, N_EXPERTS)), full((N_EXPERTS, LANES))],
        out_specs=[tok, full((1, PLAN_LANES)), full((1, LANES)), full((1, N_EXPERTS)),
                   full((1, N_EXPERTS))],
        out_shape=[
            jax.ShapeDtypeStruct((N_TOK, TOP_K), jnp.int32),
            jax.ShapeDtypeStruct((1, PLAN_LANES), jnp.int32),
            jax.ShapeDtypeStruct((1, LANES), jnp.int32),
            jax.ShapeDtypeStruct((1, N_EXPERTS), jnp.int32),
            jax.ShapeDtypeStruct((1, N_EXPERTS), jnp.int32),
        ],
        compiler_params=_cparams(("arbitrary",), 32),
        name="route_plan",
    )(topi, rank, cnt_row, cnt_col)


DSP_TM = 1024


def _dispatch_kernel(pos_ref, last_ref, h_ref, xs_ref, zero_buf, zsem, sem):
    i = pl.program_id(0)

    @pl.when(i == 0)
    def _():
        zero_buf[...] = jnp.zeros_like(zero_buf)

        def zero_copy(e):
            return pltpu.make_async_copy(zero_buf, xs_ref.at[pl.ds(pl.multiple_of(last_ref[0, e], MOE_TM), MOE_TM)], zsem)

        for e in range(N_EXPERTS):
            @pl.when(last_ref[0, e] >= 0)
            def _():
                zero_copy(e).start()
        for e in range(N_EXPERTS):
            @pl.when(last_ref[0, e] >= 0)
            def _():
                zero_copy(e).wait()

    def row_copy(t, k):
        return pltpu.make_async_copy(h_ref.at[pl.ds(t, 1)],
                                     xs_ref.at[pl.ds(pos_ref[0, 0, t * TOP_K + k], 1)], sem)

    def issue(t, carry):
        for k in range(TOP_K):
            row_copy(t, k).start()
        return carry

    lax.fori_loop(0, DSP_TM, issue, 0, unroll=8)

    for k in range(TOP_K):
        pltpu.make_async_copy(h_ref, xs_ref.at[pl.ds(0, DSP_TM)], sem).wait()


def _dispatch(pos, last, h2):
    n_tiles = N_TOK // DSP_TM
    return pl.pallas_call(
        _dispatch_kernel,
        grid=(n_tiles,),
        in_specs=[
            pl.BlockSpec((1, 1, DSP_TM * TOP_K), lambda i: (i, 0, 0), memory_space=pltpu.SMEM),
            pl.BlockSpec((1, N_EXPERTS), lambda i: (0, 0), memory_space=pltpu.SMEM),
            pl.BlockSpec((DSP_TM, D_MODEL), lambda i: (i, 0)),
        ],
        out_specs=pl.BlockSpec(memory_space=pl.ANY),
        out_shape=jax.ShapeDtypeStruct((MOE_ROWS, D_MODEL), F32),
        scratch_shapes=[pltpu.VMEM((MOE_TM, D_MODEL), F32), pltpu.SemaphoreType.DMA,
                        pltpu.SemaphoreType.DMA],
        compiler_params=_cparams(("arbitrary",), 32),
        name="dispatch",
    )(pos.reshape(n_tiles, 1, DSP_TM * TOP_K), last, h2)


PAIR_BLOCKS = (2 * D_FF) // (2 * LANES)
WEIGHT_PARTS = 3


def _prep_expert_weights(wgu_ref, wdn_ref, wgu_bf, wdn_perm, wdn_bf):
    wgu_bf[...] = wgu_ref[...].astype(BF16)
    half = LANES // 2
    for cb in range(D_MODEL // LANES):
        cols = slice(cb * LANES, (cb + 1) * LANES)
        for m in range(PAIR_BLOCKS):
            base = m * LANES
            wdn_perm[cb, pl.ds(base, half, stride=2), :] = wdn_ref[base:base + half, cols]
            wdn_perm[cb, pl.ds(base + 1, half, stride=2), :] = wdn_ref[base + half:base + LANES, cols]
        wdn_bf[:, cols] = wdn_perm[cb].astype(BF16)


def _ffn_rows(x, wgu_bf, bgu, wdn_bf, bdn):
    gu = _dot(x, wgu_bf[...]) + bgu
    lane = lax.broadcasted_iota(jnp.int32, (1, LANES), 1)
    even = (lane % 2) == 0

    def act(block):
        t = gu[:, block * LANES:(block + 1) * LANES]
        glu = jnp.minimum(t, SWIGLU_LIMIT)
        glu = glu * _sigmoid(SWIGLU_ALPHA * glu)
        lin = jnp.clip(t, -SWIGLU_LIMIT, SWIGLU_LIMIT) + 1.0
        return glu * pltpu.roll(lin, LANES - 1, 1)

    parts = []
    for m in range(PAIR_BLOCKS):
        parts.append(jnp.where(even, act(2 * m), pltpu.roll(act(2 * m + 1), 1, 1)))
    hidden = jnp.concatenate(parts, axis=1).astype(BF16)
    return _dot(hidden, wdn_bf[...]) + bdn


def _moe_kernel(layer, te_ref, nt_ref, next_ref, x_ref, wgu_hbm, bgu_ref, wdn_hbm, bdn_ref, o_ref,
                wgu_land, wdn_land, wgu_bf, wdn_perm, wdn_bf, issued, sems):
    r = pl.program_id(0)
    valid = r < nt_ref[0]
    e = te_ref[r]
    new_expert = jnp.logical_or(r == 0, e != te_ref[jnp.maximum(r - 1, 0)])
    half = D_MODEL // 2

    def part(expert, p):
        if p == 0:
            return pltpu.make_async_copy(wgu_hbm.at[layer, expert, pl.ds(0, half)],
                                         wgu_land.at[pl.ds(0, half)], sems.at[0])
        if p == 1:
            return pltpu.make_async_copy(wgu_hbm.at[layer, expert, pl.ds(half, half)],
                                         wgu_land.at[pl.ds(half, half)], sems.at[1])
        return pltpu.make_async_copy(wdn_hbm.at[layer, expert], wdn_land, sems.at[2])

    @pl.when(r == 0)
    def _():
        issued[0] = 0

    @pl.when(jnp.logical_and(valid, new_expert))
    def _():
        for p in range(WEIGHT_PARTS):
            @pl.when(issued[0] <= p)
            def _():
                part(e, p).start()
        for p in range(WEIGHT_PARTS):
            part(e, p).wait()
        _prep_expert_weights(wgu_land, wdn_land, wgu_bf, wdn_perm, wdn_bf)
        issued[0] = 0

    @pl.when(valid)
    def _():
        nxt = next_ref[e]
        for p in range(WEIGHT_PARTS):
            @pl.when(jnp.logical_and(nxt >= 0, issued[0] == p))
            def _():
                part(nxt, p).start()

        @pl.when(jnp.logical_and(nxt >= 0, issued[0] < WEIGHT_PARTS))
        def _():
            issued[0] = issued[0] + 1

        o_ref[...] = _ffn_rows(x_ref[...].astype(BF16), wgu_bf, bgu_ref[0, 0], wdn_bf, bdn_ref[0, 0])


def _moe_grouped(tile_expert, n_tiles, next_expert, xs, w_gate_up, b_gate_up4, w_down, b_down4, layer):
    d, f2 = D_MODEL, 2 * D_FF
    row_tile = lambda r, te, nt, nx: (jnp.minimum(r, nt[0] - 1), 0)
    expert = lambda r, te, nt, nx: (layer, te[r], 0, 0)
    grid_spec = pltpu.PrefetchScalarGridSpec(
        num_scalar_prefetch=3,
        grid=(MOE_TILES,),
        in_specs=[
            pl.BlockSpec((MOE_TM, d), row_tile),
            pl.BlockSpec(memory_space=pl.ANY),
            pl.BlockSpec((1, 1, 1, f2), expert),
            pl.BlockSpec(memory_space=pl.ANY),
            pl.BlockSpec((1, 1, 1, d), expert),
        ],
        out_specs=pl.BlockSpec((MOE_TM, d), row_tile),
        scratch_shapes=[
            pltpu.VMEM((d, f2), F32),
            pltpu.VMEM((D_FF, d), F32),
            pltpu.VMEM((d, f2), BF16),
            pltpu.VMEM((d // LANES, D_FF, LANES), F32),
            pltpu.VMEM((D_FF, d), BF16),
            pltpu.SMEM((1,), jnp.int32),
            pltpu.SemaphoreType.DMA((WEIGHT_PARTS,)),
        ],
    )
    return pl.pallas_call(
        functools.partial(_moe_kernel, layer),
        grid_spec=grid_spec,
        out_shape=jax.ShapeDtypeStruct((MOE_ROWS, d), F32),
        compiler_params=_cparams(("arbitrary",), 56),
        name="moe_grouped",
    )(tile_expert, n_tiles, next_expert, xs, w_gate_up, b_gate_up4, w_down, b_down4)


CMB_TM = 256
CMB_STEP = 2 * CMB_TM


def _combine_kernel(final, pos_ref, pos_next_ref, ys_ref, x1_ref, topw_ref, mod_ref, *rest):
    buf, sems = rest[-2:]
    j = pl.program_id(0)
    is_ctx = j < N_CTX_TOK // CMB_STEP

    def issue_tile(p_ref, first, s):
        def issue(t, carry):
            for k in range(TOP_K):
                pltpu.make_async_copy(ys_ref.at[pl.ds(p_ref[0, 0, first + t * TOP_K + k], 1)],
                                      buf.at[s, k, pl.ds(t, 1)], sems.at[s]).start()
            return carry

        lax.fori_loop(0, CMB_TM, issue, 0, unroll=8)

    def finish_tile(s):
        rows = slice(s * CMB_TM, (s + 1) * CMB_TM)
        for k in range(TOP_K):
            pltpu.make_async_copy(ys_ref.at[pl.ds(0, CMB_TM)], buf.at[s, k], sems.at[s]).wait()
        y = topw_ref[rows, 0:1] * buf[s, 0]
        for k in range(1, TOP_K):
            y = y + topw_ref[rows, k:k + 1] * buf[s, k]
        x = x1_ref[rows, :] + mod_ref[0, 5:6, :] * y
        if not final:
            rest[0][rows, :] = x
            return
        fg_ref, o_ctx_ref, o_lat_ref = rest[:3]
        out = _rms(x) * fg_ref[...]

        @pl.when(is_ctx)
        def _():
            o_ctx_ref[rows, :] = out

        @pl.when(jnp.logical_not(is_ctx))
        def _():
            o_lat_ref[rows, :] = out

    @pl.when(j == 0)
    def _():
        issue_tile(pos_ref, 0, 0)

    issue_tile(pos_ref, CMB_TM * TOP_K, 1)
    finish_tile(0)

    @pl.when(j + 1 < pl.num_programs(0))
    def _():
        issue_tile(pos_next_ref, 0, 0)

    finish_tile(1)


def _combine(pos, ys, x1, topw, mod_l, final_g=None):
    tm, d = CMB_STEP, D_MODEL
    n_steps = N_TOK // tm
    n_tiles = N_TOK // CMB_TM
    n_ctx = N_CTX_TOK // tm
    final = final_g is not None
    in_specs = [
        pl.BlockSpec((1, 1, tm * TOP_K), lambda i: (i, 0, 0), memory_space=pltpu.SMEM),
        pl.BlockSpec((1, 1, CMB_TM * TOP_K), lambda i: (jnp.minimum(2 * i + 2, n_tiles - 1), 0, 0),
                     memory_space=pltpu.SMEM),
        pl.BlockSpec(memory_space=pl.ANY),
        pl.BlockSpec((tm, d), lambda i: (i, 0)),
        pl.BlockSpec((tm, TOP_K), lambda i: (i, 0)),
        pl.BlockSpec((1, 6, d), lambda i: (_mod_row(i, tm), 0, 0)),
    ]
    args = [pos.reshape(n_steps, 1, tm * TOP_K), pos.reshape(n_tiles, 1, CMB_TM * TOP_K), ys, x1, topw, mod_l]
    if final:
        in_specs.append(pl.BlockSpec((1, d), lambda i: (0, 0)))
        args.append(final_g.reshape(1, d))
        out_specs = [pl.BlockSpec((tm, d), lambda i: (jnp.minimum(i, n_ctx - 1), 0)),
                     pl.BlockSpec((tm, d), lambda i: (jnp.maximum(i - n_ctx, 0), 0))]
        out_shape = [jax.ShapeDtypeStruct((N_CTX_TOK, d), F32), jax.ShapeDtypeStruct((N_LAT_TOK, d), F32)]
    else:
        out_specs = pl.BlockSpec((tm, d), lambda i: (i, 0))
        out_shape = jax.ShapeDtypeStruct((N_TOK, d), F32)
    return pl.pallas_call(
        functools.partial(_combine_kernel, final),
        grid=(n_steps,),
        in_specs=in_specs,
        out_specs=out_specs,
        out_shape=out_shape,
        scratch_shapes=[pltpu.VMEM((2, TOP_K, CMB_TM, d), F32), pltpu.SemaphoreType.DMA((2,))],
        compiler_params=_cparams(("arbitrary",), 40),
        name="combine_final" if final else "combine",
    )(*args)


def kernel(x_prompt, x_sample, cache_k, cache_v, state_ret_fwd, state_ret_bwd, c, c_ctx, norm1_g, norm2_g, w_mod, b_mod, w_in, lam_a, subln_a, ret_decay, ret_norm, w_branch_a, w_branch_b, w_branch_c, w_out, router_w, router_b, w_gate_up, b_gate_up, w_down, b_down, final_g):
    d = D_MODEL
    x = (x_prompt.reshape(N_CTX_TOK, d), x_sample.reshape(N_LAT_TOK, d))
    cond8 = jnp.concatenate([c_ctx[None, :], c, jnp.zeros((3, d), F32)], axis=0)
    mod = _modulation(cond8, w_mod, b_mod)

    cache_k4 = cache_k.reshape(DEC_BATCH, DEPTH, PAST_LEN, W_A)
    cache_v4 = cache_v.reshape(DEC_BATCH, DEPTH, PAST_LEN, W_A)
    s0f4 = state_ret_fwd.reshape(DEC_BATCH, DEPTH, H_C * DK_C, DV_C)
    s0b4 = state_ret_bwd.reshape(DEC_BATCH, DEPTH, H_C * DK_C, DV_C)
    b_gate_up4 = b_gate_up.reshape(DEPTH, N_EXPERTS, 1, 2 * D_FF)
    b_down4 = b_down.reshape(DEPTH, N_EXPERTS, 1, d)
    cos, sin = _rope_tables(DEC_SEQ)
    chan, pos_ctx = _dft_consts(SEQ)
    _, pos_lat = _dft_consts(DEC_SEQ)

    caches, states = None, None
    for l in range(DEPTH):
        lam_init = 0.8 - 0.6 * math.exp(-0.3 * l)
        z, new_k, new_v = _inproj(x, mod[l], norm1_g[l], w_in, l, caches)
        caches = (new_k, new_v)
        ya = _attn_ctx(z, lam_a[l], subln_a[l], lam_init)
        ya = _attn_lat(z, cache_k4, cache_v4, l, cos, sin, lam_a[l], subln_a[l], lam_init, ya)
        yb = _fourier(z, SEQ, BATCH, 0, chan, pos_ctx)
        yb = _fourier(z, DEC_SEQ, DEC_BATCH, N_CTX_TOK // DEC_SEQ, chan, pos_lat, yb)
        yc, new_sf, new_sb = _ret_ctx(z, ret_decay[l], ret_norm[l], l, states)
        states = (new_sf, new_sb)
        yc = _ret_lat(z, ret_decay[l], ret_norm[l], s0f4, s0b4, l, yc)
        x1, h2, topi, topw, rank, cnt_row, cnt_col = _merge(
            x, ya, yb, yc, z, mod[l], w_branch_a, w_branch_b, w_branch_c, w_out,
            norm2_g[l], router_w[l], router_b[l], l)
        pos, tile_expert, n_tiles, last, next_expert = _plan(topi, rank, cnt_row, cnt_col)
        xs = _dispatch(pos, last, h2)
        ys = _moe_grouped(tile_expert.reshape(PLAN_LANES), n_tiles[0, :1], next_expert.reshape(N_EXPERTS),
                          xs, w_gate_up, b_gate_up4, w_down, b_down4, l)
        if l < DEPTH - 1:
            x = _combine(pos, ys, x1, topw, mod[l])
        else:
            y_ctx, y_lat = _combine(pos, ys, x1, topw, mod[l], final_g)

    return (y_ctx.reshape(BATCH, SEQ, d), y_lat.reshape(DEC_BATCH, DEC_SEQ, d),
            caches[0].reshape(BATCH, DEPTH, SEQ, H_A, 2, DH_A),
            caches[1].reshape(BATCH, DEPTH, SEQ, H_A, DV_A),
            states[0], states[1])
```

```python
import functools
import math

import numpy as np
import jax
import jax.numpy as jnp
from jax import lax
from jax.experimental import pallas as pl
from jax.experimental.pallas import tpu as pltpu

F32 = jnp.float32
BF16 = jnp.bfloat16

D_MODEL = 1024
BATCH = 16
SEQ = 256
DEPTH = 2
DEC_BATCH = 4
DEC_SEQ = 1024
PAST_LEN = 512
GRID_W = 64
H_A = 4
DH_A = 64
DV_A = 128
G_B = 4
DG_B = 128
H_C = 4
DK_C = 64
DV_C = 128
N_EXPERTS = 32
TOP_K = 4
D_FF = 1024
SWIGLU_LIMIT = 7.0
SWIGLU_ALPHA = 1.702
ROPE_BASE = 10000.0
EPS = 1e-6

N_CTX_TOK = BATCH * SEQ
N_LAT_TOK = DEC_BATCH * DEC_SEQ
N_TOK = N_CTX_TOK + N_LAT_TOK
IN_TOTAL = 6656
Z_QA = 0
Z_KA = 512
Z_VA = 1024
Z_FB = 1536
Z_QC = 2048
Z_KC = 2304
Z_VC = 2560
Z_GC = 3072
Z_GATES = 3584
LANES = 128
MIB = 1024 * 1024


def _cparams(semantics, vmem_mib):
    return pltpu.CompilerParams(dimension_semantics=semantics,
                                vmem_limit_bytes=vmem_mib * MIB)


def _dot(a, b):
    return jnp.dot(a, b, preferred_element_type=F32)


def _dot_nt(a, b):
    return lax.dot_general(a, b, (((1,), (1,)), ((), ())), preferred_element_type=F32)


def _dot_tn(a, b):
    return lax.dot_general(a, b, (((0,), (0,)), ((), ())), preferred_element_type=F32)


def _split_bf16(a):
    hi = a.astype(BF16)
    lo = (a - hi.astype(F32)).astype(BF16)
    return hi, lo


def _dot_f32x3(a, b):
    a_hi, a_lo = _split_bf16(a)
    b_hi, b_lo = _split_bf16(b)
    return _dot(a_hi, b_hi) + _dot(a_lo, b_hi) + _dot(a_hi, b_lo)


def _rms(x):
    return x * lax.rsqrt(jnp.mean(x * x, axis=-1, keepdims=True) + EPS)


def _modnorm(x, g, scale, shift):
    return (_rms(x) * g) * (1.0 + scale) + shift


def _sigmoid(x):
    return 1.0 / (1.0 + jnp.exp(-x))


MOD_TN = 1536


def _mod_kernel(cond_ref, w_ref, b_ref, o_ref):
    cnd = cond_ref[...]
    s = cnd * _sigmoid(cnd)
    o_ref[0] = _dot_f32x3(s, w_ref[0]) + b_ref[0]


def _modulation(cond8, w_mod, b_mod):
    n = 6 * D_MODEL
    out = pl.pallas_call(
        _mod_kernel,
        grid=(DEPTH, n // MOD_TN),
        in_specs=[
            pl.BlockSpec((8, D_MODEL), lambda l, j: (0, 0)),
            pl.BlockSpec((1, D_MODEL, MOD_TN), lambda l, j: (l, 0, j)),
            pl.BlockSpec((1, 1, MOD_TN), lambda l, j: (l, 0, j)),
        ],
        out_specs=pl.BlockSpec((1, 8, MOD_TN), lambda l, j: (l, 0, j)),
        out_shape=jax.ShapeDtypeStruct((DEPTH, 8, n), F32),
        compiler_params=_cparams(("parallel", "parallel"), 40),
        name="modulation",
    )(cond8, w_mod, b_mod.reshape(DEPTH, 1, n))
    return out.reshape(DEPTH, 8, 6, D_MODEL)


def _mod_row(tile_idx, tm):
    n_ctx_tiles = N_CTX_TOK // tm
    per_batch = DEC_SEQ // tm
    return jnp.where(tile_idx < n_ctx_tiles, 0, 1 + (tile_idx - n_ctx_tiles) // per_batch)


INP_TM = 1024
INP_TN = 1664
INP_NJ = IN_TOTAL // INP_TN
CTX_PER_TILE = INP_TM // SEQ
assert Z_VA + H_A * DV_A <= INP_TN


def _token_rows(x_refs, is_ctx, rows=slice(None)):
    if len(x_refs) == 1:
        return x_refs[0][rows, :]
    return jnp.where(is_ctx, x_refs[0][rows, :], x_refs[1][rows, :])


def _token_specs(x, tm):
    n_ctx_tiles = N_CTX_TOK // tm
    first = lambda idx: idx[0]
    if not isinstance(x, tuple):
        return [pl.BlockSpec((tm, D_MODEL), lambda *idx: (first(idx), 0))], [x]
    return ([pl.BlockSpec((tm, D_MODEL), lambda *idx: (jnp.minimum(first(idx), n_ctx_tiles - 1), 0)),
             pl.BlockSpec((tm, D_MODEL), lambda *idx: (jnp.maximum(first(idx) - n_ctx_tiles, 0), 0))],
            list(x))


def _inproj_kernel(n_x, *refs):
    x_refs = refs[:n_x]
    mod_ref, g_ref, w_ref = refs[n_x:n_x + 3]
    z_ref, kc_ref, vc_ref, h_scr = refs[-4:]
    i = pl.program_id(0)
    j = pl.program_id(1)

    @pl.when(j == 0)
    def _():
        x = _token_rows(x_refs, i < N_CTX_TOK // INP_TM)
        h = _modnorm(x, g_ref[...], mod_ref[0, 1:2, :], mod_ref[0, 0:1, :])
        h_scr[...] = h.astype(BF16)

    z_ref[...] = _dot(h_scr[...], w_ref[0].astype(BF16)).astype(BF16)

    @pl.when(jnp.logical_and(i < N_CTX_TOK // INP_TM, j == 0))
    def _():
        wk = w_ref[0, :, Z_KA:Z_KA + W_A].astype(BF16)
        wv = w_ref[0, :, Z_VA:Z_VA + W_A].astype(BF16)
        for b in range(CTX_PER_TILE):
            hb = h_scr[b * SEQ:(b + 1) * SEQ, :]
            kc_ref[b, 0] = _dot(hb, wk)
            vc_ref[b, 0] = _dot(hb, wv)


def _inproj(x, mod_l, g, w_in, layer, prev_caches):
    ni = N_TOK // INP_TM
    n_ctx_tiles = N_CTX_TOK // INP_TM
    cache_shape = jax.ShapeDtypeStruct((BATCH, DEPTH, SEQ, W_A), F32)
    cache_spec = pl.BlockSpec((CTX_PER_TILE, 1, SEQ, W_A),
                              lambda i, j: (jnp.minimum(i, n_ctx_tiles - 1), layer, 0, 0))
    x_specs, x_args = _token_specs(x, INP_TM)
    in_specs = x_specs + [
        pl.BlockSpec((1, 6, D_MODEL), lambda i, j: (_mod_row(i, INP_TM), 0, 0)),
        pl.BlockSpec((1, D_MODEL), lambda i, j: (0, 0)),
        pl.BlockSpec((1, D_MODEL, INP_TN), lambda i, j: (layer, 0, j)),
    ]
    args = x_args + [mod_l, g.reshape(1, D_MODEL), w_in]
    aliases = {}
    if prev_caches is not None:
        aliases = {len(args): 1, len(args) + 1: 2}
        in_specs += [pl.BlockSpec(memory_space=pl.ANY)] * 2
        args += list(prev_caches)
    return pl.pallas_call(
        functools.partial(_inproj_kernel, len(x_args)),
        grid=(ni, INP_NJ),
        in_specs=in_specs,
        out_specs=[pl.BlockSpec((INP_TM, INP_TN), lambda i, j: (i, j)), cache_spec, cache_spec],
        out_shape=[jax.ShapeDtypeStruct((N_TOK, IN_TOTAL), BF16), cache_shape, cache_shape],
        scratch_shapes=[pltpu.VMEM((INP_TM, D_MODEL), BF16)],
        input_output_aliases=aliases,
        compiler_params=_cparams(("arbitrary", "arbitrary"), 56),
        name="inproj",
    )(*args)


ATT_TQ = 512
W_A = H_A * DV_A


def _lambda(lam_ref, lam_init):
    la = lam_ref[...]
    l01 = jnp.sum(la[0:1, :] * la[1:2, :], axis=1, keepdims=True)
    l23 = jnp.sum(la[2:3, :] * la[3:4, :], axis=1, keepdims=True)
    return jnp.exp(l01) - jnp.exp(l23) + lam_init


def _exp_shifted(s):
    return jnp.exp(s - jnp.max(s, axis=-1, keepdims=True)).astype(BF16)


def _diff_attn_block(q, k_ref, v_ref, lam, subln, lam_init, ones_beside_v):
    lane = lax.broadcasted_iota(jnp.int32, (1, LANES), 1)
    first = lane < DH_A
    scale = jnp.asarray(DH_A ** -0.5, BF16)
    outs = []
    for h in range(H_A):
        sl = slice(h * LANES, (h + 1) * LANES)
        qh = q[:, sl]
        kh = k_ref[:, sl]
        zero = jnp.zeros_like(qh)
        qs = qh * scale
        s0 = _dot_nt(jnp.where(first, qs, zero), kh)
        s1 = _dot_nt(jnp.where(first, zero, qs), kh)
        if ones_beside_v:
            vx = v_ref[:, 2 * h * LANES:2 * (h + 1) * LANES]
            pv0 = _dot(_exp_shifted(s0), vx)
            pv1 = _dot(_exp_shifted(s1), vx)
            r0 = 1.0 / pv0[:, LANES:LANES + 1]
            r1 = 1.0 / pv1[:, LANES:LANES + 1]
            o = pv0[:, :LANES] * r0 - (lam * r1) * pv1[:, :LANES]
        else:
            e0 = jnp.exp(s0 - jnp.max(s0, axis=-1, keepdims=True))
            e1 = jnp.exp(s1 - jnp.max(s1, axis=-1, keepdims=True))
            r0 = 1.0 / jnp.sum(e0, axis=-1, keepdims=True)
            r1 = 1.0 / jnp.sum(e1, axis=-1, keepdims=True)
            o = _dot((e0 * r0 - (lam * r1) * e1).astype(BF16), v_ref[:, sl])
        outs.append((_rms(o) * subln) * (1.0 - lam_init))
    return outs


def _store_values_with_ones(vx_ref, rows, v):
    ones = jnp.ones((v.shape[0], LANES), BF16)
    for h in range(H_A):
        vx_ref[rows, 2 * h * LANES:(2 * h + 1) * LANES] = v[:, h * LANES:(h + 1) * LANES]
        vx_ref[rows, (2 * h + 1) * LANES:2 * (h + 1) * LANES] = ones


def _rope(x, cos, sin):
    lane = lax.broadcasted_iota(jnp.int32, (1, x.shape[1]), 1)
    first = (lane % 32) < 16
    n = x.shape[1]
    xr = jnp.where(first, -pltpu.roll(x, n - 16, 1), pltpu.roll(x, 16, 1))
    return x * cos + xr * sin


def _attn_ctx_kernel(lam_init, q_ref, k_ref, v_ref, lam_ref, subln_ref, o_ref):
    lam = _lambda(lam_ref, lam_init)
    outs = _diff_attn_block(q_ref[...], k_ref, v_ref, lam, subln_ref[...], lam_init, False)
    for h in range(H_A):
        o_ref[:, h * LANES:(h + 1) * LANES] = outs[h].astype(BF16)


def _attn_ctx(z, lam_a_l, subln_l, lam_init):
    wb = W_A
    return pl.pallas_call(
        functools.partial(_attn_ctx_kernel, lam_init),
        grid=(BATCH,),
        in_specs=[
            pl.BlockSpec((SEQ, wb), lambda b: (b, Z_QA // wb)),
            pl.BlockSpec((SEQ, wb), lambda b: (b, Z_KA // wb)),
            pl.BlockSpec((SEQ, wb), lambda b: (b, Z_VA // wb)),
            pl.BlockSpec((4, DH_A), lambda b: (0, 0)),
            pl.BlockSpec((1, DV_A), lambda b: (0, 0)),
        ],
        out_specs=pl.BlockSpec((SEQ, wb), lambda b: (b, 0)),
        out_shape=jax.ShapeDtypeStruct((N_TOK, wb), BF16),
        compiler_params=_cparams(("parallel",), 40),
        name="attn_ctx",
    )(z, z, z, lam_a_l, subln_l.reshape(1, DV_A))


def _attn_lat_kernel(lam_init, q_ref, k_ref, v_ref, ck_ref, cv_ref, cos_ref, sin_ref,
                     lam_ref, subln_ref, ya_ctx_ref, o_ref, kall, vall):
    del ya_ctx_ref
    lam = _lambda(lam_ref, lam_init)
    subln = subln_ref[...]
    kall[0:DEC_SEQ, :] = _rope(k_ref[...].astype(F32), cos_ref[...], sin_ref[...]).astype(BF16)
    kall[DEC_SEQ:, :] = ck_ref[0, 0].astype(BF16)
    _store_values_with_ones(vall, slice(0, DEC_SEQ), v_ref[...])
    _store_values_with_ones(vall, slice(DEC_SEQ, DEC_SEQ + PAST_LEN), cv_ref[0, 0].astype(BF16))

    def q_block(qb, carry):
        rows = pl.ds(pl.multiple_of(qb * ATT_TQ, ATT_TQ), ATT_TQ)
        q = _rope(q_ref[rows, :].astype(F32), cos_ref[rows, :], sin_ref[rows, :]).astype(BF16)
        outs = _diff_attn_block(q, kall, vall, lam, subln, lam_init, True)
        for h in range(H_A):
            o_ref[rows, h * LANES:(h + 1) * LANES] = outs[h].astype(BF16)
        return carry

    lax.fori_loop(0, DEC_SEQ // ATT_TQ, q_block, 0)


def _attn_lat(z, cache_k4, cache_v4, layer, cos, sin, lam_a_l, subln_l, lam_init, ya_ctx):
    wb = W_A
    rb0 = N_CTX_TOK // DEC_SEQ
    t_all = DEC_SEQ + PAST_LEN
    return pl.pallas_call(
        functools.partial(_attn_lat_kernel, lam_init),
        grid=(DEC_BATCH,),
        in_specs=[
            pl.BlockSpec((DEC_SEQ, wb), lambda b: (rb0 + b, Z_QA // wb)),
            pl.BlockSpec((DEC_SEQ, wb), lambda b: (rb0 + b, Z_KA // wb)),
            pl.BlockSpec((DEC_SEQ, wb), lambda b: (rb0 + b, Z_VA // wb)),
            pl.BlockSpec((1, 1, PAST_LEN, wb), lambda b: (b, layer, 0, 0)),
            pl.BlockSpec((1, 1, PAST_LEN, wb), lambda b: (b, layer, 0, 0)),
            pl.BlockSpec((DEC_SEQ, wb), lambda b: (0, 0)),
            pl.BlockSpec((DEC_SEQ, wb), lambda b: (0, 0)),
            pl.BlockSpec((4, DH_A), lambda b: (0, 0)),
            pl.BlockSpec((1, DV_A), lambda b: (0, 0)),
            pl.BlockSpec(memory_space=pl.ANY),
        ],
        out_specs=pl.BlockSpec((DEC_SEQ, wb), lambda b: (rb0 + b, 0)),
        out_shape=jax.ShapeDtypeStruct((N_TOK, wb), BF16),
        scratch_shapes=[pltpu.VMEM((t_all, wb), BF16), pltpu.VMEM((t_all, 2 * wb), BF16)],
        input_output_aliases={9: 0},
        compiler_params=_cparams(("parallel",), 48),
        name="attn_lat",
    )(z, z, z, cache_k4, cache_v4, cos, sin, lam_a_l, subln_l.reshape(1, DV_A), ya_ctx)


def _rope_tables(s):
    rows = s // GRID_W
    row = jnp.repeat(jnp.arange(rows, dtype=F32), GRID_W)
    col = jnp.tile(jnp.arange(GRID_W, dtype=F32), rows)
    quarter = DH_A // 4
    inv = ROPE_BASE ** (-jnp.arange(quarter, dtype=F32) / quarter)
    ar = row[:, None] * inv
    ac = col[:, None] * inv
    ang = jnp.concatenate([ar, ar, ac, ac], axis=-1)
    reps = (H_A * 2 * DH_A) // DH_A
    return jnp.tile(jnp.cos(ang), (1, reps)), jnp.tile(jnp.sin(ang), (1, reps))


W_B = G_B * DG_B


def _dft_consts(s):
    def cs(n, scale):
        j = np.arange(n, dtype=np.int64)
        ang = 2.0 * np.pi * ((j[:, None] * j[None, :]) % n).astype(np.float64) / n
        return np.cos(ang) * scale, np.sin(ang) * scale
    cc, sc = cs(DG_B, DG_B ** -0.5)
    chan = np.concatenate([cc, sc], axis=1)
    cp, sp = cs(s, s ** -0.5)
    pos = np.concatenate([cp, -sp], axis=1)
    return jnp.asarray(chan, dtype=F32).astype(BF16), jnp.asarray(pos, dtype=F32).astype(BF16)


def _fourier_kernel(s, f_ref, chan_ref, pos_ref, *rest):
    o_ref, p_ref = rest[-2:]
    chan = chan_ref[...]
    for g in range(G_B):
        sl = slice(g * DG_B, (g + 1) * DG_B)
        a = _dot(f_ref[:, sl], chan)
        p_ref[0:s, sl] = a[:, :DG_B].astype(BF16)
        p_ref[s:2 * s, sl] = a[:, DG_B:].astype(BF16)
    o_ref[...] = _dot(pos_ref[...], p_ref[...]).astype(BF16)


def _fourier(z, s, n_batch, row_block0, chan, pos, prev=None):
    in_specs = [
        pl.BlockSpec((s, W_B), lambda b: (row_block0 + b, Z_FB // W_B)),
        pl.BlockSpec((DG_B, 2 * DG_B), lambda b: (0, 0)),
        pl.BlockSpec((s, 2 * s), lambda b: (0, 0)),
    ]
    args = [z, chan, pos]
    aliases = {}
    if prev is not None:
        in_specs.append(pl.BlockSpec(memory_space=pl.ANY))
        args.append(prev)
        aliases = {3: 0}
    return pl.pallas_call(
        functools.partial(_fourier_kernel, s),
        grid=(n_batch,),
        in_specs=in_specs,
        out_specs=pl.BlockSpec((s, W_B), lambda b: (row_block0 + b, 0)),
        out_shape=jax.ShapeDtypeStruct((N_TOK, W_B), BF16),
        scratch_shapes=[pltpu.VMEM((2 * s, W_B), BF16)],
        input_output_aliases=aliases,
        compiler_params=_cparams(("parallel",), 40),
        name="fourier_%d" % s,
    )(*args)


W_C = H_C * DV_C
RET_TQ = 256


def _log_sigmoid(x):
    return jnp.minimum(x, 0.0) - jnp.log1p(jnp.exp(-jnp.abs(x)))


def _head_pair_lanes(vals, pair):
    lane = lax.broadcasted_iota(jnp.int32, (1, LANES), 1)
    return jnp.where(lane < DK_C, vals[:, 2 * pair:2 * pair + 1], vals[:, 2 * pair + 1:2 * pair + 2])


def _fill_decay(dec_ref, lg, s, tq):
    for r0 in range(0, s, tq):
        ii = (r0 + lax.broadcasted_iota(jnp.int32, (tq, s), 0)).astype(F32)
        jj = lax.broadcasted_iota(jnp.int32, (tq, s), 1).astype(F32)
        rel = ii - jj
        for h in range(H_C):
            lgf = lg[0:1, h:h + 1]
            lgb = lg[1:2, h:h + 1]
            dec_ref[h, r0:r0 + tq, :] = (
                jnp.where(rel >= 0, jnp.exp(lgf * jnp.maximum(rel, 0.0)), 0.0)
                + jnp.where(rel <= 0, jnp.exp(lgb * jnp.maximum(-rel, 0.0)), 0.0))


def _retention_rows(s, row0, tq, q, k_ref, v_ref, g, lg, rnorm, s0f_ref, s0b_ref, dec_ref):
    lane = lax.broadcasted_iota(jnp.int32, (1, LANES), 1)
    pos = (row0 + lax.broadcasted_iota(jnp.int32, (tq, 1), 0)).astype(F32)
    outs = []
    for h in range(H_C):
        pair, half = h // 2, h % 2
        sl = slice(pair * LANES, (pair + 1) * LANES)
        lgf = lg[0:1, h:h + 1]
        lgb = lg[1:2, h:h + 1]
        in_head = (lane < DK_C) if half == 0 else (lane >= DK_C)
        q_pair = q[:, sl]
        qh = jnp.where(in_head, q_pair, jnp.zeros_like(q_pair))
        sc = _dot_nt(qh, k_ref[:, sl]) * (DK_C ** -0.5)
        decay = dec_ref[h, pl.ds(row0, tq), :]
        o = _dot((sc * decay).astype(BF16), v_ref[:, h * DV_C:(h + 1) * DV_C])
        if s0f_ref is not None:
            qf = qh.astype(F32)
            xf = jnp.exp(lgf * (pos + 1.0))
            xb = jnp.exp(lgb * (s - pos))
            o = o + _dot((qf * xf).astype(BF16), s0f_ref[0, 0, sl, :].astype(BF16))
            o = o + _dot((qf * xb).astype(BF16), s0b_ref[0, 0, sl, :].astype(BF16))
        gh = g[:, h * DV_C:(h + 1) * DV_C].astype(F32)
        outs.append((_rms(o) * rnorm) * (gh * _sigmoid(gh)))
    return outs


def _ret_ctx_kernel(q_ref, k_ref, v_ref, g_ref, dec_ref, rn_ref, *rest):
    o_ref, sf_ref, sb_ref, decay = rest[-4:]
    s = SEQ
    lg = _log_sigmoid(dec_ref[...])

    @pl.when(pl.program_id(0) == 0)
    def _():
        _fill_decay(decay, lg, s, s)

    outs = _retention_rows(s, 0, s, q_ref[...], k_ref, v_ref, g_ref[...], lg, rn_ref[...], None, None,
                           decay)
    for h in range(H_C):
        o_ref[:, h * DV_C:(h + 1) * DV_C] = outs[h].astype(BF16)
    pos = lax.broadcasted_iota(jnp.int32, (s, 1), 0).astype(F32)
    v = v_ref[...]
    for pair in range(H_C // 2):
        sl = slice(pair * LANES, (pair + 1) * LANES)
        kf = k_ref[:, sl].astype(F32) * (DK_C ** -0.5)
        zf = jnp.exp(_head_pair_lanes(lg[0:1, :], pair) * (s - 1.0 - pos))
        zb = jnp.exp(_head_pair_lanes(lg[1:2, :], pair) * pos)
        stf = _dot_tn((kf * zf).astype(BF16), v)
        stb = _dot_tn((kf * zb).astype(BF16), v)
        for half in range(2):
            h = 2 * pair + half
            rs = slice(half * DK_C, (half + 1) * DK_C)
            cs = slice(h * DV_C, (h + 1) * DV_C)
            sf_ref[0, 0, h] = stf[rs, cs]
            sb_ref[0, 0, h] = stb[rs, cs]


def _ret_ctx(z, dec_l, rnorm_l, layer, prev_states):
    wq = H_C * DK_C
    st_shape = jax.ShapeDtypeStruct((BATCH, DEPTH, H_C, DK_C, DV_C), F32)
    st_spec = pl.BlockSpec((1, 1, H_C, DK_C, DV_C), lambda b: (b, layer, 0, 0, 0))
    in_specs = [
        pl.BlockSpec((SEQ, wq), lambda b: (b, Z_QC // wq)),
        pl.BlockSpec((SEQ, wq), lambda b: (b, Z_KC // wq)),
        pl.BlockSpec((SEQ, W_C), lambda b: (b, Z_VC // W_C)),
        pl.BlockSpec((SEQ, W_C), lambda b: (b, Z_GC // W_C)),
        pl.BlockSpec((2, H_C), lambda b: (0, 0)),
        pl.BlockSpec((1, DV_C), lambda b: (0, 0)),
    ]
    args = [z, z, z, z, dec_l, rnorm_l.reshape(1, DV_C)]
    aliases = {}
    if prev_states is not None:
        in_specs += [pl.BlockSpec(memory_space=pl.ANY)] * 2
        args += list(prev_states)
        aliases = {6: 1, 7: 2}
    return pl.pallas_call(
        _ret_ctx_kernel,
        grid=(BATCH,),
        in_specs=in_specs,
        out_specs=[pl.BlockSpec((SEQ, W_C), lambda b: (b, 0)), st_spec, st_spec],
        out_shape=[jax.ShapeDtypeStruct((N_TOK, W_C), BF16), st_shape, st_shape],
        scratch_shapes=[pltpu.VMEM((H_C, SEQ, SEQ), F32)],
        input_output_aliases=aliases,
        compiler_params=_cparams(("arbitrary",), 40),
        name="ret_ctx",
    )(*args)


def _ret_lat_kernel(q_ref, k_ref, v_ref, g_ref, dec_ref, rn_ref, s0f_ref, s0b_ref, yc_ctx_ref, o_ref,
                    decay):
    del yc_ctx_ref
    lg = _log_sigmoid(dec_ref[...])
    rnorm = rn_ref[...]

    @pl.when(pl.program_id(0) == 0)
    def _():
        _fill_decay(decay, lg, DEC_SEQ, RET_TQ)

    def q_block(qb, carry):
        row0 = pl.multiple_of(qb * RET_TQ, RET_TQ)
        rows = pl.ds(row0, RET_TQ)
        outs = _retention_rows(DEC_SEQ, row0, RET_TQ, q_ref[rows, :], k_ref, v_ref, g_ref[rows, :],
                               lg, rnorm, s0f_ref, s0b_ref, decay)
        for h in range(H_C):
            o_ref[rows, h * DV_C:(h + 1) * DV_C] = outs[h].astype(BF16)
        return carry

    lax.fori_loop(0, DEC_SEQ // RET_TQ, q_block, 0)


def _ret_lat(z, dec_l, rnorm_l, s0f4, s0b4, layer, yc_ctx):
    wq = H_C * DK_C
    rb0 = N_CTX_TOK // DEC_SEQ
    st_spec = pl.BlockSpec((1, 1, H_C * DK_C, DV_C), lambda b: (b, layer, 0, 0))
    return pl.pallas_call(
        _ret_lat_kernel,
        grid=(DEC_BATCH,),
        in_specs=[
            pl.BlockSpec((DEC_SEQ, wq), lambda b: (rb0 + b, Z_QC // wq)),
            pl.BlockSpec((DEC_SEQ, wq), lambda b: (rb0 + b, Z_KC // wq)),
            pl.BlockSpec((DEC_SEQ, W_C), lambda b: (rb0 + b, Z_VC // W_C)),
            pl.BlockSpec((DEC_SEQ, W_C), lambda b: (rb0 + b, Z_GC // W_C)),
            pl.BlockSpec((2, H_C), lambda b: (0, 0)),
            pl.BlockSpec((1, DV_C), lambda b: (0, 0)),
            st_spec, st_spec,
            pl.BlockSpec(memory_space=pl.ANY),
        ],
        out_specs=pl.BlockSpec((DEC_SEQ, W_C), lambda b: (rb0 + b, 0)),
        out_shape=jax.ShapeDtypeStruct((N_TOK, W_C), BF16),
        scratch_shapes=[pltpu.VMEM((H_C, DEC_SEQ, DEC_SEQ), F32)],
        input_output_aliases={8: 0},
        compiler_params=_cparams(("arbitrary",), 48),
        name="ret_lat",
    )(z, z, z, z, dec_l, rnorm_l.reshape(1, DV_C), s0f4, s0b4, yc_ctx)


MRG_TM = 512
MRG_SUBBLOCKS = 2
GATE_BLOCK = 512


def _merge_kernel(n_x, *refs):
    x_refs = refs[:n_x]
    (ya_ref, yb_ref, yc_ref, g0_ref, g1_ref, g2_ref, g3_ref, g4_ref, g5_ref,
     mod_ref, wa_ref, wb_ref, wc_ref, wo_ref,
     n2_ref, rw_ref, rb_ref, x1_ref, h2_ref, topi_ref, topw_ref, rank_ref,
     cnt_row_ref, cnt_col_ref, carry_row, carry_col, wa_bf, wb_bf, wc_bf, wo_bf) = refs[n_x:]
    tm = x1_ref.shape[0]
    is_ctx = pl.program_id(0) < N_CTX_TOK // tm

    @pl.when(pl.program_id(0) == 0)
    def _():
        carry_row[...] = jnp.zeros_like(carry_row)
        carry_col[...] = jnp.zeros_like(carry_col)
        wa_bf[...] = wa_ref[0].astype(BF16)
        wb_bf[...] = wb_ref[0].astype(BF16)
        wc_bf[...] = wc_ref[0].astype(BF16)
        wo_bf[...] = wo_ref[0].astype(BF16)

    def gate(lo_ref, hi_ref, rs):
        return _sigmoid(jnp.concatenate([lo_ref[rs, :], hi_ref[rs, :]], axis=1).astype(F32))

    sub = tm // MRG_SUBBLOCKS
    for sb in range(MRG_SUBBLOCKS):
        rs = slice(sb * sub, (sb + 1) * sub)
        merged = gate(g0_ref, g1_ref, rs) * _dot(ya_ref[rs, :], wa_bf[...])
        merged += gate(g2_ref, g3_ref, rs) * _dot(yb_ref[rs, :], wb_bf[...])
        merged += gate(g4_ref, g5_ref, rs) * _dot(yc_ref[rs, :], wc_bf[...])
        y = _dot(merged.astype(BF16), wo_bf[...])
        x1 = _token_rows(x_refs, is_ctx, rs) + mod_ref[0, 2:3, :] * y
        x1_ref[rs, :] = x1
        h2 = _modnorm(x1, n2_ref[...], mod_ref[0, 4:5, :], mod_ref[0, 3:4, :])
        h2_ref[rs, :] = h2
        logits = _dot_f32x3(h2, rw_ref[...]) + rb_ref[...]
        ids = lax.broadcasted_iota(jnp.int32, logits.shape, 1)
        work = logits
        sels, vals, firsts = [], [], []
        for _ in range(TOP_K):
            m = jnp.max(work, axis=-1, keepdims=True)
            first = jnp.min(jnp.where(work == m, ids, N_EXPERTS), axis=-1, keepdims=True)
            sel = ids == first
            sels.append(sel)
            vals.append(m)
            firsts.append(first)
            work = jnp.where(sel, -jnp.inf, work)
        es = [jnp.exp(v - vals[0]) for v in vals]
        inv = 1.0 / (es[0] + es[1] + es[2] + es[3])
        chosen = jnp.zeros_like(logits)
        for sel in sels:
            chosen = chosen + jnp.where(sel, 1.0, 0.0)
        chosen_bf = chosen.astype(BF16)
        rr = lax.broadcasted_iota(jnp.int32, (sub, sub), 0)
        cc = lax.broadcasted_iota(jnp.int32, (sub, sub), 1)
        before = jnp.where(rr > cc, 1.0, 0.0).astype(BF16)
        prefix = _dot(before, chosen_bf) + carry_row[...]
        ranks = [jnp.sum(jnp.where(sel, prefix, 0.0), axis=-1, keepdims=True) for sel in sels]
        carry_row[...] += jnp.sum(chosen, axis=0, keepdims=True)
        carry_col[...] += _dot_tn(chosen_bf, jnp.ones((sub, LANES), BF16))
        topi_ref[rs, :] = jnp.concatenate(firsts, axis=1)
        topw_ref[rs, :] = jnp.concatenate([e * inv for e in es], axis=1)
        rank_ref[rs, :] = jnp.concatenate(ranks, axis=1).astype(jnp.int32)
    cnt_row_ref[...] = carry_row[...]
    cnt_col_ref[...] = carry_col[...]


def _merge(x, ya, yb, yc, z, mod_l, wa, wb, wc, wo, n2, rw, rb, layer):
    tm = MRG_TM
    d = D_MODEL
    full = lambda shape: pl.BlockSpec(shape, lambda i: tuple(0 for _ in shape))
    tok = lambda w: pl.BlockSpec((tm, w), lambda i: (i, 0))
    per_layer = lambda rows: pl.BlockSpec((1, rows, d), lambda i: (layer, 0, 0))
    x_specs, x_args = _token_specs(x, tm)
    return pl.pallas_call(
        functools.partial(_merge_kernel, len(x_args)),
        grid=(N_TOK // tm,),
        in_specs=x_specs + [
            tok(W_A), tok(W_B), tok(W_C),
            *[pl.BlockSpec((tm, GATE_BLOCK), lambda i, c=c: (i, Z_GATES // GATE_BLOCK + c)) for c in range(6)],
            pl.BlockSpec((1, 6, d), lambda i: (_mod_row(i, tm), 0, 0)),
            per_layer(W_A), per_layer(W_B), per_layer(W_C), per_layer(d),
            full((1, d)), full((d, N_EXPERTS)), full((1, N_EXPERTS)),
        ],
        out_specs=[tok(d), tok(d), tok(TOP_K), tok(TOP_K), tok(TOP_K),
                   full((1, N_EXPERTS)), full((N_EXPERTS, LANES))],
        out_shape=[
            jax.ShapeDtypeStruct((N_TOK, d), F32),
            jax.ShapeDtypeStruct((N_TOK, d), F32),
            jax.ShapeDtypeStruct((N_TOK, TOP_K), jnp.int32),
            jax.ShapeDtypeStruct((N_TOK, TOP_K), F32),
            jax.ShapeDtypeStruct((N_TOK, TOP_K), jnp.int32),
            jax.ShapeDtypeStruct((1, N_EXPERTS), F32),
            jax.ShapeDtypeStruct((N_EXPERTS, LANES), F32),
        ],
        scratch_shapes=[pltpu.VMEM((1, N_EXPERTS), F32), pltpu.VMEM((N_EXPERTS, LANES), F32),
                        pltpu.VMEM((W_A, d), BF16), pltpu.VMEM((W_B, d), BF16),
                        pltpu.VMEM((W_C, d), BF16), pltpu.VMEM((d, d), BF16)],
        compiler_params=_cparams(("arbitrary",), 56),
        name="merge_router",
    )(*x_args, ya, yb, yc, z, z, z, z, z, z, mod_l, wa, wb, wc, wo, n2.reshape(1, d), rw,
      rb.reshape(1, N_EXPERTS))


MOE_TM = 256
MOE_TILES = (N_TOK * TOP_K) // MOE_TM + N_EXPERTS
MOE_ROWS = MOE_TILES * MOE_TM
PLAN_LANES = 256
PLAN_TM = 1024


def _plan_kernel(topi_ref, rank_ref, cnt_row_ref, cnt_col_ref, pos_ref, te_ref, nt_ref, last_ref,
                 next_ref):
    e_r = lax.broadcasted_iota(jnp.int32, (N_EXPERTS, N_EXPERTS), 0)
    e_c = lax.broadcasted_iota(jnp.int32, (N_EXPERTS, N_EXPERTS), 1)
    cnt_row = cnt_row_ref[...]
    tiles_row = jnp.ceil(cnt_row * (1.0 / MOE_TM))
    earlier = jnp.where(e_r < e_c, 1.0, 0.0).astype(BF16)
    start_tile_row = _dot(tiles_row.astype(BF16), earlier)
    off_row = start_tile_row * MOE_TM
    ids = lax.broadcasted_iota(jnp.int32, (PLAN_TM, N_EXPERTS), 1)
    cols = []
    for k in range(TOP_K):
        hit = ids == topi_ref[:, k:k + 1]
        cols.append(jnp.sum(jnp.where(hit, off_row, 0.0), axis=-1, keepdims=True))
    pos_ref[...] = jnp.concatenate(cols, axis=1).astype(jnp.int32) + rank_ref[...]
    tiles_col = jnp.ceil(cnt_col_ref[...] * (1.0 / MOE_TM))
    upto = jnp.where(e_c <= e_r, 1.0, 0.0).astype(BF16)
    end_col = _dot(upto, tiles_col.astype(BF16))
    end_b = jnp.concatenate([end_col] * (PLAN_LANES // LANES), axis=1)
    r = lax.broadcasted_iota(jnp.int32, (N_EXPERTS, PLAN_LANES), 1).astype(F32)
    total = end_b[N_EXPERTS - 1:N_EXPERTS, :]
    te = jnp.sum(jnp.where(end_b <= jnp.minimum(r, total - 1.0), 1.0, 0.0), axis=0, keepdims=True)
    te_ref[...] = te.astype(jnp.int32)
    nt_ref[...] = total[:, :LANES].astype(jnp.int32)
    last = jnp.where(cnt_row > 0.0, (start_tile_row + tiles_row - 1.0) * MOE_TM, -1.0)
    last_ref[...] = last.astype(jnp.int32)
    later_used = jnp.logical_and(e_r > e_c, cnt_col_ref[:, 0:N_EXPERTS] > 0.0)
    nxt = jnp.min(jnp.where(later_used, e_r, N_EXPERTS), axis=0, keepdims=True)
    next_ref[...] = jnp.where(nxt == N_EXPERTS, -1, nxt)


def _plan(topi, rank, cnt_row, cnt_col):
    tok = pl.BlockSpec((PLAN_TM, TOP_K), lambda i: (i, 0))
    full = lambda shape: pl.BlockSpec(shape, lambda i: tuple(0 for _ in shape))
    return pl.pallas_call(
        _plan_kernel,
        grid=(N_TOK // PLAN_TM,),
        in_specs=[tok, tok, full((1, N_EXPERTS)), full((N_EXPERTS, LANES))],
        out_specs=[tok, full((1, PLAN_LANES)), full((1, LANES)), full((1, N_EXPERTS)),
                   full((1, N_EXPERTS))],
        out_shape=[
            jax.ShapeDtypeStruct((N_TOK, TOP_K), jnp.int32),
            jax.ShapeDtypeStruct((1, PLAN_LANES), jnp.int32),
            jax.ShapeDtypeStruct((1, LANES), jnp.int32),
            jax.ShapeDtypeStruct((1, N_EXPERTS), jnp.int32),
            jax.ShapeDtypeStruct((1, N_EXPERTS), jnp.int32),
        ],
        compiler_params=_cparams(("arbitrary",), 32),
        name="route_plan",
    )(topi, rank, cnt_row, cnt_col)


DSP_TM = 1024


def _dispatch_kernel(pos_ref, last_ref, h_ref, xs_ref, zero_buf, zsem, sem):
    i = pl.program_id(0)

    @pl.when(i == 0)
    def _():
        zero_buf[...] = jnp.zeros_like(zero_buf)

        def zero_copy(e):
            return pltpu.make_async_copy(zero_buf, xs_ref.at[pl.ds(pl.multiple_of(last_ref[0, e], MOE_TM), MOE_TM)], zsem)

        for e in range(N_EXPERTS):
            @pl.when(last_ref[0, e] >= 0)
            def _():
                zero_copy(e).start()
        for e in range(N_EXPERTS):
            @pl.when(last_ref[0, e] >= 0)
            def _():
                zero_copy(e).wait()

    def row_copy(t, k):
        return pltpu.make_async_copy(h_ref.at[pl.ds(t, 1)],
                                     xs_ref.at[pl.ds(pos_ref[0, 0, t * TOP_K + k], 1)], sem)

    def issue(t, carry):
        for k in range(TOP_K):
            row_copy(t, k).start()
        return carry

    lax.fori_loop(0, DSP_TM, issue, 0, unroll=8)

    for k in range(TOP_K):
        pltpu.make_async_copy(h_ref, xs_ref.at[pl.ds(0, DSP_TM)], sem).wait()


def _dispatch(pos, last, h2):
    n_tiles = N_TOK // DSP_TM
    return pl.pallas_call(
        _dispatch_kernel,
        grid=(n_tiles,),
        in_specs=[
            pl.BlockSpec((1, 1, DSP_TM * TOP_K), lambda i: (i, 0, 0), memory_space=pltpu.SMEM),
            pl.BlockSpec((1, N_EXPERTS), lambda i: (0, 0), memory_space=pltpu.SMEM),
            pl.BlockSpec((DSP_TM, D_MODEL), lambda i: (i, 0)),
        ],
        out_specs=pl.BlockSpec(memory_space=pl.ANY),
        out_shape=jax.ShapeDtypeStruct((MOE_ROWS, D_MODEL), F32),
        scratch_shapes=[pltpu.VMEM((MOE_TM, D_MODEL), F32), pltpu.SemaphoreType.DMA,
                        pltpu.SemaphoreType.DMA],
        compiler_params=_cparams(("arbitrary",), 32),
        name="dispatch",
    )(pos.reshape(n_tiles, 1, DSP_TM * TOP_K), last, h2)


PAIR_BLOCKS = (2 * D_FF) // (2 * LANES)
WEIGHT_PARTS = 3


def _prep_expert_weights(wgu_ref, wdn_ref, wgu_bf, wdn_perm, wdn_bf):
    wgu_bf[...] = wgu_ref[...].astype(BF16)
    half = LANES // 2
    for cb in range(D_MODEL // LANES):
        cols = slice(cb * LANES, (cb + 1) * LANES)
        for m in range(PAIR_BLOCKS):
            base = m * LANES
            wdn_perm[cb, pl.ds(base, half, stride=2), :] = wdn_ref[base:base + half, cols]
            wdn_perm[cb, pl.ds(base + 1, half, stride=2), :] = wdn_ref[base + half:base + LANES, cols]
        wdn_bf[:, cols] = wdn_perm[cb].astype(BF16)


def _ffn_rows(x, wgu_bf, bgu, wdn_bf, bdn):
    gu = _dot(x, wgu_bf[...]) + bgu
    lane = lax.broadcasted_iota(jnp.int32, (1, LANES), 1)
    even = (lane % 2) == 0

    def act(block):
        t = gu[:, block * LANES:(block + 1) * LANES]
        glu = jnp.minimum(t, SWIGLU_LIMIT)
        glu = glu * _sigmoid(SWIGLU_ALPHA * glu)
        lin = jnp.clip(t, -SWIGLU_LIMIT, SWIGLU_LIMIT) + 1.0
        return glu * pltpu.roll(lin, LANES - 1, 1)

    parts = []
    for m in range(PAIR_BLOCKS):
        parts.append(jnp.where(even, act(2 * m), pltpu.roll(act(2 * m + 1), 1, 1)))
    hidden = jnp.concatenate(parts, axis=1).astype(BF16)
    return _dot(hidden, wdn_bf[...]) + bdn


def _moe_kernel(layer, te_ref, nt_ref, next_ref, x_ref, wgu_hbm, bgu_ref, wdn_hbm, bdn_ref, o_ref,
                wgu_land, wdn_land, wgu_bf, wdn_perm, wdn_bf, issued, sems):
    r = pl.program_id(0)
    valid = r < nt_ref[0]
    e = te_ref[r]
    new_expert = jnp.logical_or(r == 0, e != te_ref[jnp.maximum(r - 1, 0)])
    half = D_MODEL // 2

    def part(expert, p):
        if p == 0:
            return pltpu.make_async_copy(wgu_hbm.at[layer, expert, pl.ds(0, half)],
                                         wgu_land.at[pl.ds(0, half)], sems.at[0])
        if p == 1:
            return pltpu.make_async_copy(wgu_hbm.at[layer, expert, pl.ds(half, half)],
                                         wgu_land.at[pl.ds(half, half)], sems.at[1])
        return pltpu.make_async_copy(wdn_hbm.at[layer, expert], wdn_land, sems.at[2])

    @pl.when(r == 0)
    def _():
        issued[0] = 0

    @pl.when(jnp.logical_and(valid, new_expert))
    def _():
        for p in range(WEIGHT_PARTS):
            @pl.when(issued[0] <= p)
            def _():
                part(e, p).start()
        for p in range(WEIGHT_PARTS):
            part(e, p).wait()
        _prep_expert_weights(wgu_land, wdn_land, wgu_bf, wdn_perm, wdn_bf)
        issued[0] = 0

    @pl.when(valid)
    def _():
        nxt = next_ref[e]
        for p in range(WEIGHT_PARTS):
            @pl.when(jnp.logical_and(nxt >= 0, issued[0] == p))
            def _():
                part(nxt, p).start()

        @pl.when(jnp.logical_and(nxt >= 0, issued[0] < WEIGHT_PARTS))
        def _():
            issued[0] = issued[0] + 1

        o_ref[...] = _ffn_rows(x_ref[...].astype(BF16), wgu_bf, bgu_ref[0, 0], wdn_bf, bdn_ref[0, 0])


def _moe_grouped(tile_expert, n_tiles, next_expert, xs, w_gate_up, b_gate_up4, w_down, b_down4, layer):
    d, f2 = D_MODEL, 2 * D_FF
    row_tile = lambda r, te, nt, nx: (jnp.minimum(r, nt[0] - 1), 0)
    expert = lambda r, te, nt, nx: (layer, te[r], 0, 0)
    grid_spec = pltpu.PrefetchScalarGridSpec(
        num_scalar_prefetch=3,
        grid=(MOE_TILES,),
        in_specs=[
            pl.BlockSpec((MOE_TM, d), row_tile),
            pl.BlockSpec(memory_space=pl.ANY),
            pl.BlockSpec((1, 1, 1, f2), expert),
            pl.BlockSpec(memory_space=pl.ANY),
            pl.BlockSpec((1, 1, 1, d), expert),
        ],
        out_specs=pl.BlockSpec((MOE_TM, d), row_tile),
        scratch_shapes=[
            pltpu.VMEM((d, f2), F32),
            pltpu.VMEM((D_FF, d), F32),
            pltpu.VMEM((d, f2), BF16),
            pltpu.VMEM((d // LANES, D_FF, LANES), F32),
            pltpu.VMEM((D_FF, d), BF16),
            pltpu.SMEM((1,), jnp.int32),
            pltpu.SemaphoreType.DMA((WEIGHT_PARTS,)),
        ],
    )
    return pl.pallas_call(
        functools.partial(_moe_kernel, layer),
        grid_spec=grid_spec,
        out_shape=jax.ShapeDtypeStruct((MOE_ROWS, d), F32),
        compiler_params=_cparams(("arbitrary",), 56),
        name="moe_grouped",
    )(tile_expert, n_tiles, next_expert, xs, w_gate_up, b_gate_up4, w_down, b_down4)


CMB_TM = 256
CMB_STEP = 2 * CMB_TM


def _combine_kernel(final, pos_ref, pos_next_ref, ys_ref, x1_ref, topw_ref, mod_ref, *rest):
    buf, sems = rest[-2:]
    j = pl.program_id(0)
    is_ctx = j < N_CTX_TOK // CMB_STEP

    def issue_tile(p_ref, first, s):
        def issue(t, carry):
            for k in range(TOP_K):
                pltpu.make_async_copy(ys_ref.at[pl.ds(p_ref[0, 0, first + t * TOP_K + k], 1)],
                                      buf.at[s, k, pl.ds(t, 1)], sems.at[s]).start()
            return carry

        lax.fori_loop(0, CMB_TM, issue, 0, unroll=8)

    def finish_tile(s):
        rows = slice(s * CMB_TM, (s + 1) * CMB_TM)
        for k in range(TOP_K):
            pltpu.make_async_copy(ys_ref.at[pl.ds(0, CMB_TM)], buf.at[s, k], sems.at[s]).wait()
        y = topw_ref[rows, 0:1] * buf[s, 0]
        for k in range(1, TOP_K):
            y = y + topw_ref[rows, k:k + 1] * buf[s, k]
        x = x1_ref[rows, :] + mod_ref[0, 5:6, :] * y
        if not final:
            rest[0][rows, :] = x
            return
        fg_ref, o_ctx_ref, o_lat_ref = rest[:3]
        out = _rms(x) * fg_ref[...]

        @pl.when(is_ctx)
        def _():
            o_ctx_ref[rows, :] = out

        @pl.when(jnp.logical_not(is_ctx))
        def _():
            o_lat_ref[rows, :] = out

    @pl.when(j == 0)
    def _():
        issue_tile(pos_ref, 0, 0)

    issue_tile(pos_ref, CMB_TM * TOP_K, 1)
    finish_tile(0)

    @pl.when(j + 1 < pl.num_programs(0))
    def _():
        issue_tile(pos_next_ref, 0, 0)

    finish_tile(1)


def _combine(pos, ys, x1, topw, mod_l, final_g=None):
    tm, d = CMB_STEP, D_MODEL
    n_steps = N_TOK // tm
    n_tiles = N_TOK // CMB_TM
    n_ctx = N_CTX_TOK // tm
    final = final_g is not None
    in_specs = [
        pl.BlockSpec((1, 1, tm * TOP_K), lambda i: (i, 0, 0), memory_space=pltpu.SMEM),
        pl.BlockSpec((1, 1, CMB_TM * TOP_K), lambda i: (jnp.minimum(2 * i + 2, n_tiles - 1), 0, 0),
                     memory_space=pltpu.SMEM),
        pl.BlockSpec(memory_space=pl.ANY),
        pl.BlockSpec((tm, d), lambda i: (i, 0)),
        pl.BlockSpec((tm, TOP_K), lambda i: (i, 0)),
        pl.BlockSpec((1, 6, d), lambda i: (_mod_row(i, tm), 0, 0)),
    ]
    args = [pos.reshape(n_steps, 1, tm * TOP_K), pos.reshape(n_tiles, 1, CMB_TM * TOP_K), ys, x1, topw, mod_l]
    if final:
        in_specs.append(pl.BlockSpec((1, d), lambda i: (0, 0)))
        args.append(final_g.reshape(1, d))
        out_specs = [pl.BlockSpec((tm, d), lambda i: (jnp.minimum(i, n_ctx - 1), 0)),
                     pl.BlockSpec((tm, d), lambda i: (jnp.maximum(i - n_ctx, 0), 0))]
        out_shape = [jax.ShapeDtypeStruct((N_CTX_TOK, d), F32), jax.ShapeDtypeStruct((N_LAT_TOK, d), F32)]
    else:
        out_specs = pl.BlockSpec((tm, d), lambda i: (i, 0))
        out_shape = jax.ShapeDtypeStruct((N_TOK, d), F32)
    return pl.pallas_call(
        functools.partial(_combine_kernel, final),
        grid=(n_steps,),
        in_specs=in_specs,
        out_specs=out_specs,
        out_shape=out_shape,
        scratch_shapes=[pltpu.VMEM((2, TOP_K, CMB_TM, d), F32), pltpu.SemaphoreType.DMA((2,))],
        compiler_params=_cparams(("arbitrary",), 40),
        name="combine_final" if final else "combine",
    )(*args)


def kernel(x_prompt, x_sample, cache_k, cache_v, state_ret_fwd, state_ret_bwd, c, c_ctx, norm1_g, norm2_g, w_mod, b_mod, w_in, lam_a, subln_a, ret_decay, ret_norm, w_branch_a, w_branch_b, w_branch_c, w_out, router_w, router_b, w_gate_up, b_gate_up, w_down, b_down, final_g):
    d = D_MODEL
    x = (x_prompt.reshape(N_CTX_TOK, d), x_sample.reshape(N_LAT_TOK, d))
    cond8 = jnp.concatenate([c_ctx[None, :], c, jnp.zeros((3, d), F32)], axis=0)
    mod = _modulation(cond8, w_mod, b_mod)

    cache_k4 = cache_k.reshape(DEC_BATCH, DEPTH, PAST_LEN, W_A)
    cache_v4 = cache_v.reshape(DEC_BATCH, DEPTH, PAST_LEN, W_A)
    s0f4 = state_ret_fwd.reshape(DEC_BATCH, DEPTH, H_C * DK_C, DV_C)
    s0b4 = state_ret_bwd.reshape(DEC_BATCH, DEPTH, H_C * DK_C, DV_C)
    b_gate_up4 = b_gate_up.reshape(DEPTH, N_EXPERTS, 1, 2 * D_FF)
    b_down4 = b_down.reshape(DEPTH, N_EXPERTS, 1, d)
    cos, sin = _rope_tables(DEC_SEQ)
    chan, pos_ctx = _dft_consts(SEQ)
    _, pos_lat = _dft_consts(DEC_SEQ)

    caches, states = None, None
    for l in range(DEPTH):
        lam_init = 0.8 - 0.6 * math.exp(-0.3 * l)
        z, new_k, new_v = _inproj(x, mod[l], norm1_g[l], w_in, l, caches)
        caches = (new_k, new_v)
        ya = _attn_ctx(z, lam_a[l], subln_a[l], lam_init)
        ya = _attn_lat(z, cache_k4, cache_v4, l, cos, sin, lam_a[l], subln_a[l], lam_init, ya)
        yb = _fourier(z, SEQ, BATCH, 0, chan, pos_ctx)
        yb = _fourier(z, DEC_SEQ, DEC_BATCH, N_CTX_TOK // DEC_SEQ, chan, pos_lat, yb)
        yc, new_sf, new_sb = _ret_ctx(z, ret_decay[l], ret_norm[l], l, states)
        states = (new_sf, new_sb)
        yc = _ret_lat(z, ret_decay[l], ret_norm[l], s0f4, s0b4, l, yc)
        x1, h2, topi, topw, rank, cnt_row, cnt_col = _merge(
            x, ya, yb, yc, z, mod[l], w_branch_a, w_branch_b, w_branch_c, w_out,
            norm2_g[l], router_w[l], router_b[l], l)
        pos, tile_expert, n_tiles, last, next_expert = _plan(topi, rank, cnt_row, cnt_col)
        xs = _dispatch(pos, last, h2)
        ys = _moe_grouped(tile_expert.reshape(PLAN_LANES), n_tiles[0, :1], next_expert.reshape(N_EXPERTS),
                          xs, w_gate_up, b_gate_up4, w_down, b_down4, l)
        if l < DEPTH - 1:
            x = _combine(pos, ys, x1, topw, mod[l])
        else:
            y_ctx, y_lat = _combine(pos, ys, x1, topw, mod[l], final_g)

    return (y_ctx.reshape(BATCH, SEQ, d), y_lat.reshape(DEC_BATCH, DEC_SEQ, d),
            caches[0].reshape(BATCH, DEPTH, SEQ, H_A, 2, DH_A),
            caches[1].reshape(BATCH, DEPTH, SEQ, H_A, DV_A),
            states[0], states[1])
```
